```python
import math
import jax, jax.numpy as jnp
from jax import lax
import numpy as np

D_MODEL = 1024
BATCH = 2
SEQ = 8192
DEPTH = 1

CHUNK = 64
Q_BLOCK = 128
N_HEADS_A = 4
HEAD_DIM = 64
D_ATTN = N_HEADS_A * 2 * HEAD_DIM
POOL_WINDOWS = (2, 4, 8, 16)
N_POOL_GROUPS = 4
POOL_GROUP_DIM = 128
D_POOL = N_POOL_GROUPS * POOL_GROUP_DIM
D_IN = 3 * D_ATTN + D_POOL + 2 * D_MODEL
N_BUCKETS = 32
MAX_DISTANCE = 128
N_EXPERT_GROUPS = 4
EXPERTS_PER_GROUP = 4
N_EXPERTS = N_EXPERT_GROUPS * EXPERTS_PER_GROUP
TOP_K_INNER = 2
D_EXPERT = 512
RMS_EPS = 1e-6

kernel_name = "hybrid_diffattn_pool_hmoe_block"


def rms_norm(x, g):
    xf = x.astype(jnp.float32)
    y = xf * lax.rsqrt(jnp.mean(xf * xf, axis=-1, keepdims=True) + RMS_EPS)
    return (y * g.astype(jnp.float32)).astype(x.dtype)


def t5_bucket(rel):
    nb = N_BUCKETS // 2
    max_exact = nb // 2
    bucket = jnp.where(rel > 0, nb, 0)
    n = jnp.abs(rel)
    n_f = jnp.maximum(n, max_exact).astype(jnp.float32)
    large = max_exact + (jnp.log(n_f / max_exact) / math.log(MAX_DISTANCE / max_exact)
                         * (nb - max_exact)).astype(jnp.int32)
    large = jnp.minimum(large, nb - 1)
    return bucket + jnp.where(n < max_exact, n, large)


def diff_attention(q, k, v, rel_bias, lam):
    s = q.shape[1]
    scale = HEAD_DIM ** -0.5
    outs = []
    for i in range(s // Q_BLOCK):
        q0 = i * Q_BLOCK
        kend = q0 + Q_BLOCK
        qb = q[:, q0:kend]
        kb = k[:, :kend]
        vb = v[:, :kend]
        qpos = jnp.arange(q0, kend)
        kpos = jnp.arange(kend)
        rel = kpos[None, :] - qpos[:, None]
        bias = jnp.transpose(rel_bias[t5_bucket(rel)], (2, 0, 1)).astype(jnp.float32)
        mask = (kpos[None, :] // CHUNK) <= (qpos[:, None] // CHUNK)
        logits = jnp.einsum('bqhmd,bkhmd->bhmqk', qb, kb,
                            preferred_element_type=jnp.float32) * scale + bias[None, :, None]
        logits = jnp.where(mask, logits, -jnp.inf)
        p = jax.nn.softmax(logits, axis=-1)
        a = p[:, :, 0] - lam * p[:, :, 1]
        outs.append(jnp.einsum('bhqk,bkhe->bqhe', a.astype(v.dtype), vb))
    return jnp.concatenate(outs, axis=1)


def multiscale_pool(u, pool_w, pool_scale):
    b, s, _ = u.shape
    ug = u.reshape(b, s, N_POOL_GROUPS, POOL_GROUP_DIM)
    t = jnp.arange(s)
    pooled = []
    for g, w in enumerate(POOL_WINDOWS):
        ch = ug[:, :, g].astype(jnp.float32)
        cs = jnp.concatenate([jnp.zeros((b, 1, POOL_GROUP_DIM), jnp.float32),
                              jnp.cumsum(ch, axis=1)], axis=1)
        lo = jnp.maximum(t + 1 - w, 0)
        win_sum = cs[:, 1:] - cs[:, lo]
        count = jnp.minimum(t + 1, w).astype(jnp.float32)
        pooled.append(win_sum / count[None, :, None] - ch)
    m = jnp.stack(pooled, axis=2).astype(u.dtype)
    y = jnp.einsum('bsgc,gcd->bsgd', m, pool_w).reshape(b, s, D_POOL)
    return y * pool_scale


def hierarchical_moe(h, wg_r, bg_r, we_r, be_r, w_gate, w_up, w_down):
    b, s, d = h.shape
    t = h.reshape(-1, d)
    g_prob = jax.nn.softmax((t @ wg_r + bg_r).astype(jnp.float32), axis=-1)
    g_val, g_idx = lax.top_k(g_prob, 1)
    e_logits = (t @ we_r + be_r).astype(jnp.float32).reshape(-1, N_EXPERT_GROUPS, EXPERTS_PER_GROUP)
    e_sel = jnp.take_along_axis(e_logits, g_idx[:, :, None], axis=1)[:, 0]
    e_prob = jax.nn.softmax(e_sel, axis=-1)
    e_val, e_idx = lax.top_k(e_prob, TOP_K_INNER)
    e_val = e_val / jnp.sum(e_val, axis=-1, keepdims=True)
    weights = g_val * e_val
    expert_id = g_idx * EXPERTS_PER_GROUP + e_idx
    combine = jnp.sum(jax.nn.one_hot(expert_id, N_EXPERTS, dtype=jnp.float32)
                      * weights[..., None], axis=1)
    y = jnp.zeros(t.shape, jnp.float32)
    for e in range(N_EXPERTS):
        hid = jax.nn.silu(t @ w_gate[e]) * (t @ w_up[e])
        y = y + combine[:, e:e + 1] * (hid @ w_down[e]).astype(jnp.float32)
    return y.astype(h.dtype).reshape(b, s, d)


def setup_inputs(seed: int = 0) -> dict:
    key = jax.random.key(seed)
    ks = jax.random.split(key, 32)
    L, D = DEPTH, D_MODEL
    nrm = lambda k, shape, sc: jax.random.normal(k, shape, jnp.float32) * sc
    return {
        "x": nrm(ks[0], (BATCH, SEQ, D), 1.0),
        "c": nrm(ks[1], (BATCH, D), 1.0),
        "rel_bias": nrm(ks[2], (N_BUCKETS, N_HEADS_A), 0.5),
        "ada_w": nrm(ks[3], (L, D, 6 * D), 0.5 * D ** -0.5),
        "ada_b": nrm(ks[4], (L, 6 * D), 0.02),
        "norm1_g": 1.0 + nrm(ks[5], (L, D), 0.02),
        "w_in": nrm(ks[6], (L, D, D_IN), D ** -0.5),
        "q_norm_g": 1.0 + nrm(ks[7], (L, HEAD_DIM), 0.02),
        "k_norm_g": 1.0 + nrm(ks[8], (L, HEAD_DIM), 0.02),
        "lambda_q1": nrm(ks[9], (L, HEAD_DIM), 0.1),
        "lambda_k1": nrm(ks[10], (L, HEAD_DIM), 0.1),
        "lambda_q2": nrm(ks[11], (L, HEAD_DIM), 0.1),
        "lambda_k2": nrm(ks[12], (L, HEAD_DIM), 0.1),
        "subln_g": 1.0 + nrm(ks[13], (L, 2 * HEAD_DIM), 0.02),
        "w_branch_attn": nrm(ks[14], (L, D_ATTN, D), D_ATTN ** -0.5),
        "pool_w": nrm(ks[15], (L, N_POOL_GROUPS, POOL_GROUP_DIM, POOL_GROUP_DIM), POOL_GROUP_DIM ** -0.5),
        "pool_scale": 1.0 + nrm(ks[16], (L, D_POOL), 0.1),
        "w_branch_pool": nrm(ks[17], (L, D_POOL, D), D_POOL ** -0.5),
        "w_out": nrm(ks[18], (L, D, D), D ** -0.5),
        "norm2_g": 1.0 + nrm(ks[19], (L, D), 0.02),
        "router_group_w": nrm(ks[20], (L, D, N_EXPERT_GROUPS), D ** -0.5),
        "router_group_b": nrm(ks[21], (L, N_EXPERT_GROUPS), 0.01),
        "router_expert_w": nrm(ks[22], (L, D, N_EXPERTS), D ** -0.5),
        "router_expert_b": nrm(ks[23], (L, N_EXPERTS), 0.01),
        "expert_w_gate": nrm(ks[24], (L, N_EXPERTS, D, D_EXPERT), D ** -0.5),
        "expert_w_up": nrm(ks[25], (L, N_EXPERTS, D, D_EXPERT), D ** -0.5),
        "expert_w_down": nrm(ks[26], (L, N_EXPERTS, D_EXPERT, D), D_EXPERT ** -0.5),
    }


def reference(x, c, rel_bias, ada_w, ada_b, norm1_g, w_in, q_norm_g, k_norm_g,
              lambda_q1, lambda_k1, lambda_q2, lambda_k2, subln_g, w_branch_attn,
              pool_w, pool_scale, w_branch_pool, w_out, norm2_g,
              router_group_w, router_group_b, router_expert_w, router_expert_b,
              expert_w_gate, expert_w_up, expert_w_down):
    b, s, d = x.shape
    f32 = jnp.float32
    split_pts = [D_ATTN, 2 * D_ATTN, 3 * D_ATTN, 3 * D_ATTN + D_POOL, 3 * D_ATTN + D_POOL + D_MODEL]
    for l in range(DEPTH):
        mod = jax.nn.silu(c) @ ada_w[l] + ada_b[l]
        shift1, scale1, gate1, shift2, scale2, gate2 = jnp.split(mod[:, None, :], 6, axis=-1)

        h = rms_norm(x, norm1_g[l]) * (1 + scale1) + shift1
        proj = h @ w_in[l]
        q, k, v, u, g_attn, g_pool = jnp.split(proj, split_pts, axis=-1)

        q = rms_norm(q.reshape(b, s, N_HEADS_A, 2, HEAD_DIM), q_norm_g[l])
        k = rms_norm(k.reshape(b, s, N_HEADS_A, 2, HEAD_DIM), k_norm_g[l])
        v = v.reshape(b, s, N_HEADS_A, 2 * HEAD_DIM)
        lambda_init = 0.8 - 0.6 * math.exp(-0.3 * l)
        lam = (jnp.exp(jnp.sum(lambda_q1[l].astype(f32) * lambda_k1[l].astype(f32)))
               - jnp.exp(jnp.sum(lambda_q2[l].astype(f32) * lambda_k2[l].astype(f32)))
               + lambda_init)
        o = diff_attention(q, k, v, rel_bias, lam)
        o = rms_norm(o, subln_g[l]) * (1.0 - lambda_init)
        y_attn = o.reshape(b, s, D_ATTN) @ w_branch_attn[l]

        y_pool = multiscale_pool(u, pool_w[l], pool_scale[l]) @ w_branch_pool[l]

        merged = jax.nn.sigmoid(g_attn) * y_attn + jax.nn.sigmoid(g_pool) * y_pool
        x = x + gate1 * (merged @ w_out[l])

        h2 = rms_norm(x, norm2_g[l]) * (1 + scale2) + shift2
        x = x + gate2 * hierarchical_moe(h2, router_group_w[l], router_group_b[l],
                                         router_expert_w[l], router_expert_b[l],
                                         expert_w_gate[l], expert_w_up[l], expert_w_down[l])
    return x
```

```python
import functools
import math

import jax
import jax.numpy as jnp
from jax import lax
from jax.experimental import pallas as pl
from jax.experimental.pallas import tpu as pltpu

F32 = jnp.float32
BF16 = jnp.bfloat16

CHUNK = 64
N_HEADS = 4
HEAD_DIM = 64
D_HEAD_V = 2 * HEAD_DIM
D_ATTN = N_HEADS * D_HEAD_V
POOL_WINDOWS = (2, 4, 8, 16)
POOL_GROUP_DIM = 128
D_POOL = len(POOL_WINDOWS) * POOL_GROUP_DIM
POOL_HALO = 16
N_BUCKETS = 32
MAX_DISTANCE = 128
N_EXPERT_GROUPS = 4
EXPERTS_PER_GROUP = 4
N_EXPERTS = N_EXPERT_GROUPS * EXPERTS_PER_GROUP
RMS_EPS = 1e-6
LOG2E = math.log2(math.e)
MASKED = -1e30

LANES = 128
ROUTER_ROWS = 32
EXPERT_ROW0 = 8

VMEM_LIMIT = 56 * 1024 * 1024
TOKEN_TILE = 512
ATTN_TILE = 512


def _cparams(n_axes):
    return pltpu.CompilerParams(dimension_semantics=("arbitrary",) * n_axes,
                                vmem_limit_bytes=VMEM_LIMIT)


def _const_spec(shape):
    nd = len(shape)
    return pl.BlockSpec(shape, lambda *_: (0,) * nd, pipeline_mode=pl.Buffered(1))


def _ada_kernel(ct_ref, w_ref, b_ref, o_ref):
    ct = ct_ref[...]
    s = ct * jax.nn.sigmoid(ct)
    w = w_ref[...]
    rows = [jnp.sum(w * s[:, b:b + 1], axis=0, keepdims=True) for b in range(ct.shape[1])]
    o_ref[...] = jnp.concatenate(rows, axis=0) + b_ref[...]


def _ada(c, w, b):
    bsz, d = c.shape
    n = w.shape[1]
    tn = 512
    return pl.pallas_call(
        _ada_kernel,
        out_shape=jax.ShapeDtypeStruct((bsz, n), F32),
        grid=(n // tn,),
        in_specs=[pl.BlockSpec((d, bsz), lambda j: (0, 0)),
                  pl.BlockSpec((d, tn), lambda j: (0, j)),
                  pl.BlockSpec((1, tn), lambda j: (0, j))],
        out_specs=pl.BlockSpec((bsz, tn), lambda j: (0, j)),
        compiler_params=_cparams(1),
        name="ada",
    )(c.T, w, b.reshape(1, n))


def _bias_kernel(rb_ref, bucket_ref, o_ref):
    h = pl.program_id(0)
    bucket = bucket_ref[...]
    acc = jnp.zeros(bucket.shape, F32)
    for b in range(N_BUCKETS):
        acc = jnp.where(bucket == b, rb_ref[b, h], acc)
    far = rb_ref[N_BUCKETS // 2 - 1, h]
    o_ref[...] = jnp.where(bucket < 0, MASKED, (acc - far) * LOG2E)


def _t5_bucket(rel):
    nb = N_BUCKETS // 2
    max_exact = nb // 2
    bucket = jnp.where(rel > 0, nb, 0)
    n = jnp.abs(rel)
    n_f = jnp.maximum(n, max_exact).astype(jnp.float32)
    large = max_exact + (jnp.log(n_f / max_exact) / math.log(MAX_DISTANCE / max_exact)
                         * (nb - max_exact)).astype(jnp.int32)
    large = jnp.minimum(large, nb - 1)
    return bucket + jnp.where(n < max_exact, n, large)


def _bias_tiles(rel_bias, t):
    kpos = jnp.arange(t, dtype=jnp.int32)[:, None]
    qpos = jnp.arange(t, dtype=jnp.int32)[None, :]
    rel = kpos - qpos
    visible = (kpos // CHUNK) <= (qpos // CHUNK)
    buckets = jnp.stack([_t5_bucket(rel - t), jnp.where(visible, _t5_bucket(rel), -1)])
    return pl.pallas_call(
        _bias_kernel,
        out_shape=jax.ShapeDtypeStruct((N_HEADS, 2, t, t), F32),
        grid=(N_HEADS, 2),
        in_specs=[pl.BlockSpec(memory_space=pltpu.SMEM),
                  pl.BlockSpec((None, t, t), lambda h, j: (j, 0, 0))],
        out_specs=pl.BlockSpec((None, None, t, t), lambda h, j: (h, j, 0, 0)),
        compiler_params=_cparams(2),
        name="bias_tiles",
    )(rel_bias, buckets)


def _group_rms(xc, ones_blockdiag, gain):
    ssq = jnp.dot((xc * xc).astype(BF16), ones_blockdiag, preferred_element_type=F32)
    return xc * lax.rsqrt(ssq * (1.0 / HEAD_DIM) + RMS_EPS) * gain


def _inproj_kernel(x_ref, mod_ref, g1_ref, win_ref, wvt_ref, gq_ref, gk_ref, ones_ref, poolw_ref,
                   pscale_ref, wbp_ref, q_ref, k_ref, vt_ref, sga_ref, pc_ref, ext_ref,
                   *, ts, tiles_per_batch):
    tb = pl.program_id(0) % tiles_per_batch
    x = x_ref[...]
    y = x * lax.rsqrt(jnp.mean(x * x, axis=-1, keepdims=True) + RMS_EPS)
    h = y * (g1_ref[...] * (1.0 + mod_ref[1:2, :])) + mod_ref[0:1, :]
    hb = h.astype(BF16)

    def proj(c0, c1):
        return jnp.dot(hb, win_ref[:, c0:c1], preferred_element_type=F32)

    ones_bd = ones_ref[...]
    q_ref[...] = _group_rms(proj(0, D_ATTN), ones_bd, gq_ref[...]).astype(BF16)
    k_ref[...] = _group_rms(proj(D_ATTN, 2 * D_ATTN), ones_bd, gk_ref[...]).astype(BF16)
    vt_ref[...] = lax.dot_general(wvt_ref[...], hb, (((1,), (1,)), ((), ())),
                                  preferred_element_type=F32).astype(BF16)
    c_u = 3 * D_ATTN
    c_ga = c_u + D_POOL
    c_gp = c_ga + x.shape[1]
    sga_ref[...] = jax.nn.sigmoid(proj(c_ga, c_gp)).astype(BF16)

    u = proj(c_u, c_ga)

    @pl.when(tb == 0)
    def _():
        ext_ref[0:POOL_HALO, :] = jnp.zeros((POOL_HALO, D_POOL), F32)

    ext_ref[POOL_HALO:POOL_HALO + ts, :] = u
    row = lax.broadcasted_iota(jnp.int32, (ts, 1), 0) + tb * ts
    ys = []
    for g, w in enumerate(POOL_WINDOWS):
        c0 = g * POOL_GROUP_DIM
        ug = u[:, c0:c0 + POOL_GROUP_DIM]
        acc = ug
        for d in range(1, w):
            acc = acc + ext_ref[POOL_HALO - d:POOL_HALO - d + ts, c0:c0 + POOL_GROUP_DIM]
        cnt = jnp.minimum(row + 1, w).astype(F32)
        m = acc / cnt - ug
        ys.append(jnp.dot(m.astype(BF16), poolw_ref[g], preferred_element_type=F32))
    yb = jnp.concatenate(ys, axis=1) * pscale_ref[...]
    ypool = jnp.dot(yb.astype(BF16), wbp_ref[...], preferred_element_type=F32)
    pc_ref[...] = (jax.nn.sigmoid(proj(c_gp, c_gp + x.shape[1])) * ypool).astype(BF16)
    ext_ref[0:POOL_HALO, :] = u[ts - POOL_HALO:ts, :]


def _inproj(x2, mod3, g1, w_in, gq, gk, pool_w, pool_scale, w_bp, *, bsz, seq, ts):
    n, d = x2.shape
    d_in = w_in.shape[1]
    tiles_per_batch = seq // ts
    win_b = w_in.astype(BF16)
    wvt = w_in[:, 2 * D_ATTN:3 * D_ATTN].T.astype(BF16)
    idx = jnp.arange(D_ATTN) // HEAD_DIM
    ones_bd = (idx[:, None] == idx[None, :]).astype(BF16)
    kern = functools.partial(_inproj_kernel, ts=ts, tiles_per_batch=tiles_per_batch)
    tok = lambda i: (i, 0)
    out_shape = (jax.ShapeDtypeStruct((n, D_ATTN), BF16),
                 jax.ShapeDtypeStruct((n, D_ATTN), BF16),
                 jax.ShapeDtypeStruct((bsz, D_ATTN, seq), BF16),
                 jax.ShapeDtypeStruct((n, d), BF16),
                 jax.ShapeDtypeStruct((n, d), BF16))
    return pl.pallas_call(
        kern,
        out_shape=out_shape,
        grid=(n // ts,),
        in_specs=[pl.BlockSpec((ts, d), tok),
                  pl.BlockSpec((None, 6, d), lambda i: (i // tiles_per_batch, 0, 0)),
                  _const_spec((1, d)),
                  _const_spec((d, d_in)),
                  _const_spec((D_ATTN, d)),
                  _const_spec((1, D_ATTN)),
                  _const_spec((1, D_ATTN)),
                  _const_spec((D_ATTN, D_ATTN)),
                  _const_spec(pool_w.shape),
                  _const_spec((1, D_POOL)),
                  _const_spec((D_POOL, d))],
        out_specs=(pl.BlockSpec((ts, D_ATTN), tok),
                   pl.BlockSpec((ts, D_ATTN), tok),
                   pl.BlockSpec((None, D_ATTN, ts),
                                lambda i: (i // tiles_per_batch, 0, i % tiles_per_batch)),
                   pl.BlockSpec((ts, d), tok),
                   pl.BlockSpec((ts, d), tok)),
        scratch_shapes=[pltpu.VMEM((POOL_HALO + ts, D_POOL), F32)],
        compiler_params=_cparams(1),
        name="inproj",
    )(x2, mod3, g1.reshape(1, d), win_b, wvt, gq.reshape(1, D_ATTN), gk.reshape(1, D_ATTN), ones_bd,
      pool_w.astype(BF16), pool_scale.reshape(1, D_POOL), w_bp.astype(BF16))


def _attn_kernel(q_ref, k_ref, vt_ref, bias_ref, lq1_ref, lk1_ref, lq2_ref, lk2_ref, subg_ref,
                 o_ref, m_ref, l_ref, acc_ref, *, t, lambda_init):
    i = pl.program_id(2)
    q = q_ref[...]
    lane = lax.broadcasted_iota(jnp.int32, q.shape, 1)
    zero = jnp.zeros_like(q)
    qm = (jnp.where(lane < HEAD_DIM, q, zero), jnp.where(lane >= HEAD_DIM, q, zero))

    m_ref[...] = jnp.full(m_ref.shape, MASKED, F32)
    l_ref[...] = jnp.zeros(l_ref.shape, F32)
    acc_ref[...] = jnp.zeros(acc_ref.shape, F32)

    def tile(j, bias):
        start = pl.multiple_of(j * t, t)
        kt = k_ref[pl.ds(start, t), :]
        vt = vt_ref[:, pl.ds(start, t)]
        for mp in range(2):
            s = lax.dot_general(kt, qm[mp], (((1,), (1,)), ((), ())), preferred_element_type=F32)
            if bias is not None:
                s = s + bias
            m_old = m_ref[mp]
            m_new = jnp.maximum(m_old, jnp.max(s, axis=0, keepdims=True))
            alpha = jnp.exp2(m_old - m_new)
            p = jnp.exp2(s - m_new)
            l_ref[mp] = alpha * l_ref[mp] + jnp.sum(p, axis=0, keepdims=True)
            acc_ref[mp] = alpha * acc_ref[mp] + jnp.dot(vt, p.astype(BF16), preferred_element_type=F32)
            m_ref[mp] = m_new

    def far_body(j, carry):
        tile(j, None)
        return carry

    lax.fori_loop(0, jnp.maximum(i - 1, 0), far_body, 0)

    @pl.when(i >= 1)
    def _():
        tile(i - 1, bias_ref[0])

    tile(i, bias_ref[1])

    lam = (jnp.exp(jnp.sum(lq1_ref[...] * lk1_ref[...], axis=1, keepdims=True))
           - jnp.exp(jnp.sum(lq2_ref[...] * lk2_ref[...], axis=1, keepdims=True)) + lambda_init)
    ot = acc_ref[0] / l_ref[0] - lam * (acc_ref[1] / l_ref[1])
    ot = ot * lax.rsqrt(jnp.mean(ot * ot, axis=0, keepdims=True) + RMS_EPS)
    ot = ot * subg_ref[...] * (1.0 - lambda_init)
    o_ref[...] = ot.T.astype(BF16)


def _attention(q, k, vt, bias, lq1, lk1, lq2, lk2, subln_g, *, t, lambda_init):
    bsz, seq, _ = q.shape
    kern = functools.partial(_attn_kernel, t=t, lambda_init=lambda_init)
    vec = lambda a: a.reshape(1, HEAD_DIM)
    return pl.pallas_call(
        kern,
        out_shape=jax.ShapeDtypeStruct((bsz, seq, D_ATTN), BF16),
        grid=(bsz, N_HEADS, seq // t),
        in_specs=[pl.BlockSpec((None, t, D_HEAD_V), lambda b, h, i: (b, i, h)),
                  pl.BlockSpec((None, seq, D_HEAD_V), lambda b, h, i: (b, 0, h)),
                  pl.BlockSpec((None, D_HEAD_V, seq), lambda b, h, i: (b, h, 0)),
                  pl.BlockSpec((None, 2, t, t), lambda b, h, i: (h, 0, 0, 0)),
                  _const_spec((1, HEAD_DIM)), _const_spec((1, HEAD_DIM)),
                  _const_spec((1, HEAD_DIM)), _const_spec((1, HEAD_DIM)),
                  _const_spec((D_HEAD_V, 1))],
        out_specs=pl.BlockSpec((None, t, D_HEAD_V), lambda b, h, i: (b, i, h)),
        scratch_shapes=[pltpu.VMEM((2, 1, t), F32), pltpu.VMEM((2, 1, t), F32),
                        pltpu.VMEM((2, D_HEAD_V, t), F32)],
        compiler_params=_cparams(3),
        name="attn",
    )(q, k, vt, bias, vec(lq1), vec(lk1), vec(lq2), vec(lk2), subln_g.reshape(D_HEAD_V, 1))


def _first_max(rows):
    best = rows[0]
    for r in rows[1:]:
        best = jnp.maximum(best, r)
    idx = jnp.full(best.shape, len(rows) - 1, jnp.int32)
    for j in range(len(rows) - 2, -1, -1):
        idx = jnp.where(rows[j] == best, j, idx)
    return best, idx


def _post_kernel(o_ref, sga_ref, pc_ref, x_ref, mod_ref, g2_ref, wba_ref, wout_ref, wr_ref, br_ref,
                 x1_ref, h2_ref, comb_ref):
    ya = jnp.dot(o_ref[...], wba_ref[...], preferred_element_type=F32)
    merged = sga_ref[...].astype(F32) * ya + pc_ref[...].astype(F32)
    z = jnp.dot(merged.astype(BF16), wout_ref[...], preferred_element_type=F32)
    x1 = x_ref[...] + mod_ref[2:3, :] * z
    x1_ref[...] = x1
    y = x1 * lax.rsqrt(jnp.mean(x1 * x1, axis=-1, keepdims=True) + RMS_EPS)
    h2 = y * (g2_ref[...] * (1.0 + mod_ref[4:5, :])) + mod_ref[3:4, :]
    hi = h2.astype(BF16)
    lo = (h2 - hi.astype(F32)).astype(BF16)
    h2_ref[...] = hi

    nt = (((1,), (1,)), ((), ()))
    a = lax.dot_general(wr_ref[...], hi, nt, preferred_element_type=F32)
    b = lax.dot_general(wr_ref[0:ROUTER_ROWS, :], lo, nt, preferred_element_type=F32)
    logits = a[0:ROUTER_ROWS] + a[ROUTER_ROWS:2 * ROUTER_ROWS] + b + br_ref[...]

    gl = [logits[g:g + 1, :] for g in range(N_EXPERT_GROUPS)]
    gmax, gidx = _first_max(gl)
    gsum = gl[0] * 0.0
    for r in gl:
        gsum = gsum + jnp.exp(r - gmax)
    g_val = 1.0 / gsum
    es = []
    for r in range(EXPERTS_PER_GROUP):
        sel = jnp.zeros_like(gmax)
        for g in range(N_EXPERT_GROUPS):
            row = EXPERT_ROW0 + g * EXPERTS_PER_GROUP + r
            sel = jnp.where(gidx == g, logits[row:row + 1, :], sel)
        es.append(sel)
    e1, i1 = _first_max(es)
    rest = [jnp.where(i1 == r, -jnp.inf, es[r]) for r in range(EXPERTS_PER_GROUP)]
    e2, i2 = _first_max(rest)
    r21 = jnp.exp(e2 - e1)
    w1 = g_val / (1.0 + r21)
    w2 = g_val * r21 / (1.0 + r21)
    id1 = gidx * EXPERTS_PER_GROUP + i1
    id2 = gidx * EXPERTS_PER_GROUP + i2
    erow = lax.broadcasted_iota(jnp.int32, (LANES, gmax.shape[1]), 0)
    comb_t = jnp.where(erow == id1, w1, 0.0) + jnp.where(erow == id2, w2, 0.0)
    comb_ref[...] = comb_t.T


def _post(o2, sga, pc, x2, mod3, g2, w_ba, w_out, wg_r, bg_r, we_r, be_r, *, seq, ts):
    n, d = x2.shape
    tiles_per_batch = seq // ts
    wr = jnp.zeros((ROUTER_ROWS, d), F32)
    wr = wr.at[0:N_EXPERT_GROUPS].set(wg_r.T).at[EXPERT_ROW0:EXPERT_ROW0 + N_EXPERTS].set(we_r.T)
    wr_hi = wr.astype(BF16)
    wr_lo = (wr - wr_hi.astype(F32)).astype(BF16)
    br = jnp.zeros((ROUTER_ROWS, 1), F32)
    br = br.at[0:N_EXPERT_GROUPS, 0].set(bg_r).at[EXPERT_ROW0:EXPERT_ROW0 + N_EXPERTS, 0].set(be_r)
    tok = lambda i: (i, 0)
    return pl.pallas_call(
        _post_kernel,
        out_shape=(jax.ShapeDtypeStruct((n, d), F32),
                   jax.ShapeDtypeStruct((n, d), BF16),
                   jax.ShapeDtypeStruct((n, LANES), F32)),
        grid=(n // ts,),
        in_specs=[pl.BlockSpec((ts, D_ATTN), tok),
                  pl.BlockSpec((ts, d), tok),
                  pl.BlockSpec((ts, d), tok),
                  pl.BlockSpec((ts, d), tok),
                  pl.BlockSpec((None, 6, d), lambda i: (i // tiles_per_batch, 0, 0)),
                  _const_spec((1, d)),
                  _const_spec((D_ATTN, d)),
                  _const_spec((d, d)),
                  _const_spec((2 * ROUTER_ROWS, d)),
                  _const_spec((ROUTER_ROWS, 1))],
        out_specs=(pl.BlockSpec((ts, d), tok), pl.BlockSpec((ts, d), tok),
                   pl.BlockSpec((ts, LANES), tok)),
        compiler_params=_cparams(1),
        name="post",
    )(o2, sga, pc, x2, mod3, g2.reshape(1, d), w_ba.astype(BF16), w_out.astype(BF16),
      jnp.concatenate([wr_hi, wr_lo], axis=0), br)


def _moe_kernel(h2_ref, comb_ref, wg_ref, wu_ref, wd_ref, x1_ref, mod_ref, out_ref, acc_ref):
    e = pl.program_id(1)

    @pl.when(e == 0)
    def _():
        acc_ref[...] = jnp.zeros(acc_ref.shape, F32)

    h = h2_ref[...]
    a = jnp.dot(h, wg_ref[...], preferred_element_type=F32)
    b = jnp.dot(h, wu_ref[...], preferred_element_type=F32)
    hid = (a * jax.nn.sigmoid(a)) * b
    y = jnp.dot(hid.astype(BF16), wd_ref[...], preferred_element_type=F32)
    comb = comb_ref[...]
    lane = lax.broadcasted_iota(jnp.int32, comb.shape, 1)
    w = jnp.sum(jnp.where(lane == e, comb, 0.0), axis=1, keepdims=True)
    acc_ref[...] += w * y

    @pl.when(e == N_EXPERTS - 1)
    def _():
        out_ref[...] = x1_ref[...] + mod_ref[5:6, :] * acc_ref[...]


def _moe(h2, comb, w_gate, w_up, w_down, x1, mod3, *, seq, ts):
    n, d = x1.shape
    de = w_gate.shape[2]
    tiles_per_batch = seq // ts
    tok = lambda i, e: (i, 0)
    return pl.pallas_call(
        _moe_kernel,
        out_shape=jax.ShapeDtypeStruct((n, d), F32),
        grid=(n // ts, N_EXPERTS),
        in_specs=[pl.BlockSpec((ts, d), tok),
                  pl.BlockSpec((ts, LANES), tok),
                  pl.BlockSpec((None, d, de), lambda i, e: (e, 0, 0)),
                  pl.BlockSpec((None, d, de), lambda i, e: (e, 0, 0)),
                  pl.BlockSpec((None, de, d), lambda i, e: (e, 0, 0)),
                  pl.BlockSpec((ts, d), tok),
                  pl.BlockSpec((None, 6, d), lambda i, e: (i // tiles_per_batch, 0, 0))],
        out_specs=pl.BlockSpec((ts, d), tok),
        scratch_shapes=[pltpu.VMEM((ts, d), F32)],
        compiler_params=_cparams(2),
        name="moe",
    )(h2, comb, w_gate.astype(BF16), w_up.astype(BF16), w_down.astype(BF16), x1, mod3)


def _tile(seq, pref):
    t = min(pref, seq)
    assert seq % t == 0 and t % LANES == 0, (seq, t)
    return t


def kernel(x, c, rel_bias, ada_w, ada_b, norm1_g, w_in, q_norm_g, k_norm_g, lambda_q1, lambda_k1,
           lambda_q2, lambda_k2, subln_g, w_branch_attn, pool_w, pool_scale, w_branch_pool, w_out,
           norm2_g, router_group_w, router_group_b, router_expert_w, router_expert_b,
           expert_w_gate, expert_w_up, expert_w_down):
    bsz, seq, d = x.shape
    n = bsz * seq
    ts = _tile(seq, TOKEN_TILE)
    t_attn = _tile(seq, ATTN_TILE)
    bias = _bias_tiles(rel_bias, t_attn)
    x2 = x.reshape(n, d)
    for l in range(ada_w.shape[0]):
        lambda_init = 0.8 - 0.6 * math.exp(-0.3 * l)
        mod3 = _ada(c, ada_w[l], ada_b[l]).reshape(bsz, 6, d)
        gq = jnp.tile(q_norm_g[l], D_ATTN // HEAD_DIM) * (HEAD_DIM ** -0.5 * LOG2E)
        gk = jnp.tile(k_norm_g[l], D_ATTN // HEAD_DIM)
        q, k, vt, sga, pc = _inproj(x2, mod3, norm1_g[l], w_in[l], gq, gk, pool_w[l], pool_scale[l],
                                    w_branch_pool[l], bsz=bsz, seq=seq, ts=ts)
        o = _attention(q.reshape(bsz, seq, D_ATTN), k.reshape(bsz, seq, D_ATTN), vt, bias,
                       lambda_q1[l], lambda_k1[l], lambda_q2[l], lambda_k2[l], subln_g[l],
                       t=t_attn, lambda_init=lambda_init)
        x1, h2, comb = _post(o.reshape(n, D_ATTN), sga, pc, x2, mod3, norm2_g[l], w_branch_attn[l],
                             w_out[l], router_group_w[l], router_group_b[l], router_expert_w[l],
                             router_expert_b[l], seq=seq, ts=ts)
        x2 = _moe(h2, comb, expert_w_gate[l], expert_w_up[l], expert_w_down[l], x1, mod3,
                  seq=seq, ts=ts)
    return x2.reshape(bsz, seq, d)
```

```python
import functools
import math

import jax
import jax.numpy as jnp
from jax import lax
from jax.experimental import pallas as pl
from jax.experimental.pallas import tpu as pltpu

F32 = jnp.float32
BF16 = jnp.bfloat16

CHUNK = 64
N_HEADS = 4
HEAD_DIM = 64
D_HEAD_V = 2 * HEAD_DIM
V_ROWS = D_HEAD_V + 16
D_ATTN = N_HEADS * D_HEAD_V
POOL_WINDOWS = (2, 4, 8, 16)
POOL_GROUP_DIM = 128
D_POOL = len(POOL_WINDOWS) * POOL_GROUP_DIM
POOL_HALO = 16
N_BUCKETS = 32
MAX_DISTANCE = 128
N_EXPERT_GROUPS = 4
EXPERTS_PER_GROUP = 4
N_EXPERTS = N_EXPERT_GROUPS * EXPERTS_PER_GROUP
RMS_EPS = 1e-6
LOG2E = math.log2(math.e)
MASKED = -1e30

LANES = 128
ROUTER_ROWS = 32
EXPERT_ROW0 = 8

VMEM_LIMIT = 56 * 1024 * 1024
TOKEN_TILE = 512
ATTN_TILE = 512


def _cparams(n_axes):
    return pltpu.CompilerParams(dimension_semantics=("arbitrary",) * n_axes,
                                vmem_limit_bytes=VMEM_LIMIT)


def _const_spec(shape):
    nd = len(shape)
    return pl.BlockSpec(shape, lambda *_: (0,) * nd, pipeline_mode=pl.Buffered(1))


def _ada_kernel(ct_ref, w_ref, b_ref, o_ref):
    ct = ct_ref[...]
    s = ct * jax.nn.sigmoid(ct)
    w = w_ref[...]
    rows = [jnp.sum(w * s[:, b:b + 1], axis=0, keepdims=True) for b in range(ct.shape[1])]
    o_ref[...] = jnp.concatenate(rows, axis=0) + b_ref[...]


def _ada(c, w, b):
    bsz, d = c.shape
    n = w.shape[1]
    tn = 512
    return pl.pallas_call(
        _ada_kernel,
        out_shape=jax.ShapeDtypeStruct((bsz, n), F32),
        grid=(n // tn,),
        in_specs=[pl.BlockSpec((d, bsz), lambda j: (0, 0)),
                  pl.BlockSpec((d, tn), lambda j: (0, j)),
                  pl.BlockSpec((1, tn), lambda j: (0, j))],
        out_specs=pl.BlockSpec((bsz, tn), lambda j: (0, j)),
        compiler_params=_cparams(1),
        name="ada",
    )(c.T, w, b.reshape(1, n))


def _bias_kernel(rb_ref, bucket_ref, o_ref):
    h = pl.program_id(0)
    bucket = bucket_ref[...]
    acc = jnp.zeros(bucket.shape, F32)
    for b in range(N_BUCKETS):
        acc = jnp.where(bucket == b, rb_ref[b, h], acc)
    far = rb_ref[N_BUCKETS // 2 - 1, h]
    o_ref[...] = jnp.where(bucket < 0, MASKED, (acc - far) * LOG2E)


def _t5_bucket(rel):
    nb = N_BUCKETS // 2
    max_exact = nb // 2
    bucket = jnp.where(rel > 0, nb, 0)
    n = jnp.abs(rel)
    n_f = jnp.maximum(n, max_exact).astype(jnp.float32)
    large = max_exact + (jnp.log(n_f / max_exact) / math.log(MAX_DISTANCE / max_exact)
                         * (nb - max_exact)).astype(jnp.int32)
    large = jnp.minimum(large, nb - 1)
    return bucket + jnp.where(n < max_exact, n, large)


def _bias_tiles(rel_bias, t):
    kpos = jnp.arange(t, dtype=jnp.int32)[:, None]
    qpos = jnp.arange(t, dtype=jnp.int32)[None, :]
    rel = kpos - qpos
    visible = (kpos // CHUNK) <= (qpos // CHUNK)
    buckets = jnp.stack([_t5_bucket(rel - t), jnp.where(visible, _t5_bucket(rel), -1)])
    return pl.pallas_call(
        _bias_kernel,
        out_shape=jax.ShapeDtypeStruct((N_HEADS, 2, t, t), F32),
        grid=(N_HEADS, 2),
        in_specs=[pl.BlockSpec(memory_space=pltpu.SMEM),
                  pl.BlockSpec((None, t, t), lambda h, j: (j, 0, 0))],
        out_specs=pl.BlockSpec((None, None, t, t), lambda h, j: (h, j, 0, 0)),
        compiler_params=_cparams(2),
        name="bias_tiles",
    )(rel_bias, buckets)


def _group_rms(xc, ones_blockdiag, gain):
    ssq = jnp.dot((xc * xc).astype(BF16), ones_blockdiag, preferred_element_type=F32)
    return xc * lax.rsqrt(ssq * (1.0 / HEAD_DIM) + RMS_EPS) * gain


def _inproj_kernel(x_ref, mod_ref, g1_ref, win_ref, wvt_ref, gq_ref, gk_ref, ones_ref, poolw_ref,
                   pscale_ref, wbp_ref, q_ref, k_ref, vt_ref, sga_ref, pc_ref, ext_ref,
                   *, ts, tiles_per_batch):
    tb = pl.program_id(0) % tiles_per_batch
    x = x_ref[...]
    y = x * lax.rsqrt(jnp.mean(x * x, axis=-1, keepdims=True) + RMS_EPS)
    h = y * (g1_ref[...] * (1.0 + mod_ref[1:2, :])) + mod_ref[0:1, :]
    hb = h.astype(BF16)

    def proj(c0, c1):
        return jnp.dot(hb, win_ref[:, c0:c1], preferred_element_type=F32)

    ones_bd = ones_ref[...]
    q_ref[...] = _group_rms(proj(0, D_ATTN), ones_bd, gq_ref[...]).astype(BF16)
    k_ref[...] = _group_rms(proj(D_ATTN, 2 * D_ATTN), ones_bd, gk_ref[...]).astype(BF16)
    vt = lax.dot_general(wvt_ref[...], hb, (((1,), (1,)), ((), ())),
                         preferred_element_type=F32).astype(BF16)
    for hd in range(N_HEADS):
        vt_ref[hd, 0:D_HEAD_V, :] = vt[hd * D_HEAD_V:(hd + 1) * D_HEAD_V, :]
        vt_ref[hd, D_HEAD_V:V_ROWS, :] = jnp.ones((V_ROWS - D_HEAD_V, ts), BF16)
    c_u = 3 * D_ATTN
    c_ga = c_u + D_POOL
    c_gp = c_ga + x.shape[1]
    sga_ref[...] = jax.nn.sigmoid(proj(c_ga, c_gp)).astype(BF16)

    u = proj(c_u, c_ga)

    @pl.when(tb == 0)
    def _():
        ext_ref[0:POOL_HALO, :] = jnp.zeros((POOL_HALO, D_POOL), F32)

    ext_ref[POOL_HALO:POOL_HALO + ts, :] = u
    row = lax.broadcasted_iota(jnp.int32, (ts, 1), 0) + tb * ts
    ys = []
    for g, w in enumerate(POOL_WINDOWS):
        c0 = g * POOL_GROUP_DIM
        ug = u[:, c0:c0 + POOL_GROUP_DIM]
        acc = ug
        for d in range(1, w):
            acc = acc + ext_ref[POOL_HALO - d:POOL_HALO - d + ts, c0:c0 + POOL_GROUP_DIM]
        cnt = jnp.minimum(row + 1, w).astype(F32)
        m = acc / cnt - ug
        ys.append(jnp.dot(m.astype(BF16), poolw_ref[g], preferred_element_type=F32))
    yb = jnp.concatenate(ys, axis=1) * pscale_ref[...]
    ypool = jnp.dot(yb.astype(BF16), wbp_ref[...], preferred_element_type=F32)
    pc_ref[...] = (jax.nn.sigmoid(proj(c_gp, c_gp + x.shape[1])) * ypool).astype(BF16)
    ext_ref[0:POOL_HALO, :] = u[ts - POOL_HALO:ts, :]


def _inproj(x2, mod3, g1, w_in, gq, gk, pool_w, pool_scale, w_bp, *, bsz, seq, ts):
    n, d = x2.shape
    d_in = w_in.shape[1]
    tiles_per_batch = seq // ts
    win_b = w_in.astype(BF16)
    wvt = w_in[:, 2 * D_ATTN:3 * D_ATTN].T.astype(BF16)
    idx = jnp.arange(D_ATTN) // HEAD_DIM
    ones_bd = (idx[:, None] == idx[None, :]).astype(BF16)
    kern = functools.partial(_inproj_kernel, ts=ts, tiles_per_batch=tiles_per_batch)
    tok = lambda i: (i, 0)
    out_shape = (jax.ShapeDtypeStruct((n, D_ATTN), BF16),
                 jax.ShapeDtypeStruct((n, D_ATTN), BF16),
                 jax.ShapeDtypeStruct((bsz, N_HEADS, V_ROWS, seq), BF16),
                 jax.ShapeDtypeStruct((n, d), BF16),
                 jax.ShapeDtypeStruct((n, d), BF16))
    return pl.pallas_call(
        kern,
        out_shape=out_shape,
        grid=(n // ts,),
        in_specs=[pl.BlockSpec((ts, d), tok),
                  pl.BlockSpec((None, 6, d), lambda i: (i // tiles_per_batch, 0, 0)),
                  _const_spec((1, d)),
                  _const_spec((d, d_in)),
                  _const_spec((D_ATTN, d)),
                  _const_spec((1, D_ATTN)),
                  _const_spec((1, D_ATTN)),
                  _const_spec((D_ATTN, D_ATTN)),
                  _const_spec(pool_w.shape),
                  _const_spec((1, D_POOL)),
                  _const_spec((D_POOL, d))],
        out_specs=(pl.BlockSpec((ts, D_ATTN), tok),
                   pl.BlockSpec((ts, D_ATTN), tok),
                   pl.BlockSpec((None, N_HEADS, V_ROWS, ts),
                                lambda i: (i // tiles_per_batch, 0, 0, i % tiles_per_batch)),
                   pl.BlockSpec((ts, d), tok),
                   pl.BlockSpec((ts, d), tok)),
        scratch_shapes=[pltpu.VMEM((POOL_HALO + ts, D_POOL), F32)],
        compiler_params=_cparams(1),
        name="inproj",
    )(x2, mod3, g1.reshape(1, d), win_b, wvt, gq.reshape(1, D_ATTN), gk.reshape(1, D_ATTN), ones_bd,
      pool_w.astype(BF16), pool_scale.reshape(1, D_POOL), w_bp.astype(BF16))


def _attn_kernel(q_ref, k_ref, vt_ref, bias_ref, lq1_ref, lk1_ref, lq2_ref, lk2_ref, subg_ref,
                 o_ref, s_ref, mt_ref, m_ref, acc_ref, *, t, lambda_init):
    i = pl.program_id(2)
    q = q_ref[...]
    lane = lax.broadcasted_iota(jnp.int32, q.shape, 1)
    zero = jnp.zeros_like(q)
    qm = (jnp.where(lane < HEAD_DIM, q, zero), jnp.where(lane >= HEAD_DIM, q, zero))

    m_ref[...] = jnp.full(m_ref.shape, MASKED, F32)
    acc_ref[...] = jnp.zeros(acc_ref.shape, F32)

    def scores(j, bias, slot):
        kt = k_ref[pl.ds(pl.multiple_of(j * t, t), t), :]
        for mp in range(2):
            s = lax.dot_general(kt, qm[mp], (((1,), (1,)), ((), ())), preferred_element_type=F32)
            if bias is not None:
                s = s + bias
            s_ref[slot, mp] = s
            mt_ref[slot, mp] = jnp.max(s, axis=0, keepdims=True)

    def accumulate(j, slot):
        vt = vt_ref[:, pl.ds(pl.multiple_of(j * t, t), t)]
        for mp in range(2):
            m_old = m_ref[mp]
            m_new = jnp.maximum(m_old, mt_ref[slot, mp])
            p = jnp.exp2(s_ref[slot, mp] - m_new).astype(BF16)
            acc_ref[mp] = (jnp.exp2(m_old - m_new) * acc_ref[mp]
                           + jnp.dot(vt, p, preferred_element_type=F32))
            m_ref[mp] = m_new

    scores(i, bias_ref[1], 0)

    @pl.when(i >= 1)
    def _():
        scores(i - 1, bias_ref[0], 1)
        accumulate(i, 0)

    n_far = jnp.maximum(i - 1, 0)

    def pair_body(kk, carry):
        j = i - 1 - 2 * kk
        scores(j - 1, None, 0)
        accumulate(j, 1)
        scores(j - 2, None, 1)
        accumulate(j - 1, 0)
        return carry

    lax.fori_loop(0, n_far // 2, pair_body, 0)

    @pl.when(n_far % 2 == 1)
    def _():
        scores(0, None, 0)
        accumulate(1, 1)
        accumulate(0, 0)

    @pl.when(jnp.logical_and(i >= 1, n_far % 2 == 0))
    def _():
        accumulate(0, 1)

    @pl.when(i == 0)
    def _():
        accumulate(0, 0)

    lam = (jnp.exp(jnp.sum(lq1_ref[...] * lk1_ref[...], axis=1, keepdims=True))
           - jnp.exp(jnp.sum(lq2_ref[...] * lk2_ref[...], axis=1, keepdims=True)) + lambda_init)
    o1 = acc_ref[0, 0:D_HEAD_V, :] / acc_ref[0, D_HEAD_V:D_HEAD_V + 1, :]
    o2 = acc_ref[1, 0:D_HEAD_V, :] / acc_ref[1, D_HEAD_V:D_HEAD_V + 1, :]
    ot = o1 - lam * o2
    ot = ot * lax.rsqrt(jnp.mean(ot * ot, axis=0, keepdims=True) + RMS_EPS)
    ot = ot * subg_ref[...] * (1.0 - lambda_init)
    o_ref[...] = ot.T.astype(BF16)


def _attention(q, k, vt, bias, lq1, lk1, lq2, lk2, subln_g, *, t, lambda_init):
    bsz, seq, _ = q.shape
    kern = functools.partial(_attn_kernel, t=t, lambda_init=lambda_init)
    vec = lambda a: a.reshape(1, HEAD_DIM)
    return pl.pallas_call(
        kern,
        out_shape=jax.ShapeDtypeStruct((bsz, seq, D_ATTN), BF16),
        grid=(bsz, N_HEADS, seq // t),
        in_specs=[pl.BlockSpec((None, t, D_HEAD_V), lambda b, h, i: (b, i, h)),
                  pl.BlockSpec((None, seq, D_HEAD_V), lambda b, h, i: (b, 0, h)),
                  pl.BlockSpec((None, None, V_ROWS, seq), lambda b, h, i: (b, h, 0, 0)),
                  pl.BlockSpec((None, 2, t, t), lambda b, h, i: (h, 0, 0, 0)),
                  _const_spec((1, HEAD_DIM)), _const_spec((1, HEAD_DIM)),
                  _const_spec((1, HEAD_DIM)), _const_spec((1, HEAD_DIM)),
                  _const_spec((D_HEAD_V, 1))],
        out_specs=pl.BlockSpec((None, t, D_HEAD_V), lambda b, h, i: (b, i, h)),
        scratch_shapes=[pltpu.VMEM((2, 2, t, t), F32),
                        pltpu.VMEM((2, 2, 1, t), F32),
                        pltpu.VMEM((2, 1, t), F32),
                        pltpu.VMEM((2, V_ROWS, t), F32)],
        compiler_params=_cparams(3),
        name="attn",
    )(q, k, vt, bias, vec(lq1), vec(lk1), vec(lq2), vec(lk2), subln_g.reshape(D_HEAD_V, 1))


def _first_max(rows):
    best = rows[0]
    for r in rows[1:]:
        best = jnp.maximum(best, r)
    idx = jnp.full(best.shape, len(rows) - 1, jnp.int32)
    for j in range(len(rows) - 2, -1, -1):
        idx = jnp.where(rows[j] == best, j, idx)
    return best, idx


def _post_kernel(o_ref, sga_ref, pc_ref, x_ref, mod_ref, g2_ref, wba_ref, wout_ref, wr_ref, br_ref,
                 x1_ref, h2_ref, comb_ref):
    ya = jnp.dot(o_ref[...], wba_ref[...], preferred_element_type=F32)
    merged = sga_ref[...].astype(F32) * ya + pc_ref[...].astype(F32)
    z = jnp.dot(merged.astype(BF16), wout_ref[...], preferred_element_type=F32)
    x1 = x_ref[...] + mod_ref[2:3, :] * z
    x1_ref[...] = x1
    y = x1 * lax.rsqrt(jnp.mean(x1 * x1, axis=-1, keepdims=True) + RMS_EPS)
    h2 = y * (g2_ref[...] * (1.0 + mod_ref[4:5, :])) + mod_ref[3:4, :]
    hi = h2.astype(BF16)
    lo = (h2 - hi.astype(F32)).astype(BF16)
    h2_ref[...] = hi

    nt = (((1,), (1,)), ((), ()))
    a = lax.dot_general(wr_ref[...], hi, nt, preferred_element_type=F32)
    b = lax.dot_general(wr_ref[0:ROUTER_ROWS, :], lo, nt, preferred_element_type=F32)
    logits = a[0:ROUTER_ROWS] + a[ROUTER_ROWS:2 * ROUTER_ROWS] + b + br_ref[...]

    gl = [logits[g:g + 1, :] for g in range(N_EXPERT_GROUPS)]
    gmax, gidx = _first_max(gl)
    gsum = gl[0] * 0.0
    for r in gl:
        gsum = gsum + jnp.exp(r - gmax)
    g_val = 1.0 / gsum
    es = []
    for r in range(EXPERTS_PER_GROUP):
        sel = jnp.zeros_like(gmax)
        for g in range(N_EXPERT_GROUPS):
            row = EXPERT_ROW0 + g * EXPERTS_PER_GROUP + r
            sel = jnp.where(gidx == g, logits[row:row + 1, :], sel)
        es.append(sel)
    e1, i1 = _first_max(es)
    rest = [jnp.where(i1 == r, -jnp.inf, es[r]) for r in range(EXPERTS_PER_GROUP)]
    e2, i2 = _first_max(rest)
    r21 = jnp.exp(e2 - e1)
    w1 = g_val / (1.0 + r21)
    w2 = g_val * r21 / (1.0 + r21)
    id1 = gidx * EXPERTS_PER_GROUP + i1
    id2 = gidx * EXPERTS_PER_GROUP + i2
    erow = lax.broadcasted_iota(jnp.int32, (LANES, gmax.shape[1]), 0)
    comb_t = jnp.where(erow == id1, w1, 0.0) + jnp.where(erow == id2, w2, 0.0)
    comb_ref[...] = comb_t.T


def _post(o2, sga, pc, x2, mod3, g2, w_ba, w_out, wg_r, bg_r, we_r, be_r, *, seq, ts):
    n, d = x2.shape
    tiles_per_batch = seq // ts
    wr = jnp.zeros((ROUTER_ROWS, d), F32)
    wr = wr.at[0:N_EXPERT_GROUPS].set(wg_r.T).at[EXPERT_ROW0:EXPERT_ROW0 + N_EXPERTS].set(we_r.T)
    wr_hi = wr.astype(BF16)
    wr_lo = (wr - wr_hi.astype(F32)).astype(BF16)
    br = jnp.zeros((ROUTER_ROWS, 1), F32)
    br = br.at[0:N_EXPERT_GROUPS, 0].set(bg_r).at[EXPERT_ROW0:EXPERT_ROW0 + N_EXPERTS, 0].set(be_r)
    tok = lambda i: (i, 0)
    return pl.pallas_call(
        _post_kernel,
        out_shape=(jax.ShapeDtypeStruct((n, d), F32),
                   jax.ShapeDtypeStruct((n, d), BF16),
                   jax.ShapeDtypeStruct((n, LANES), F32)),
        grid=(n // ts,),
        in_specs=[pl.BlockSpec((ts, D_ATTN), tok),
                  pl.BlockSpec((ts, d), tok),
                  pl.BlockSpec((ts, d), tok),
                  pl.BlockSpec((ts, d), tok),
                  pl.BlockSpec((None, 6, d), lambda i: (i // tiles_per_batch, 0, 0)),
                  _const_spec((1, d)),
                  _const_spec((D_ATTN, d)),
                  _const_spec((d, d)),
                  _const_spec((2 * ROUTER_ROWS, d)),
                  _const_spec((ROUTER_ROWS, 1))],
        out_specs=(pl.BlockSpec((ts, d), tok), pl.BlockSpec((ts, d), tok),
                   pl.BlockSpec((ts, LANES), tok)),
        compiler_params=_cparams(1),
        name="post",
    )(o2, sga, pc, x2, mod3, g2.reshape(1, d), w_ba.astype(BF16), w_out.astype(BF16),
      jnp.concatenate([wr_hi, wr_lo], axis=0), br)


def _moe_kernel(h2_ref, comb_ref, wg_ref, wu_ref, wd_ref, x1_ref, mod_ref, out_ref, acc_ref):
    e = pl.program_id(1)

    @pl.when(e == 0)
    def _():
        acc_ref[...] = jnp.zeros(acc_ref.shape, F32)

    h = h2_ref[...]
    a = jnp.dot(h, wg_ref[...], preferred_element_type=F32)
    b = jnp.dot(h, wu_ref[...], preferred_element_type=F32)
    hid = (a * jax.nn.sigmoid(a)) * b
    y = jnp.dot(hid.astype(BF16), wd_ref[...], preferred_element_type=F32)
    comb = comb_ref[...]
    lane = lax.broadcasted_iota(jnp.int32, comb.shape, 1)
    w = jnp.sum(jnp.where(lane == e, comb, 0.0), axis=1, keepdims=True)
    acc_ref[...] += w * y

    @pl.when(e == N_EXPERTS - 1)
    def _():
        out_ref[...] = x1_ref[...] + mod_ref[5:6, :] * acc_ref[...]


def _moe(h2, comb, w_gate, w_up, w_down, x1, mod3, *, seq, ts):
    n, d = x1.shape
    de = w_gate.shape[2]
    tiles_per_batch = seq // ts
    tok = lambda i, e: (i, 0)
    return pl.pallas_call(
        _moe_kernel,
        out_shape=jax.ShapeDtypeStruct((n, d), F32),
        grid=(n // ts, N_EXPERTS),
        in_specs=[pl.BlockSpec((ts, d), tok),
                  pl.BlockSpec((ts, LANES), tok),
                  pl.BlockSpec((None, d, de), lambda i, e: (e, 0, 0)),
                  pl.BlockSpec((None, d, de), lambda i, e: (e, 0, 0)),
                  pl.BlockSpec((None, de, d), lambda i, e: (e, 0, 0)),
                  pl.BlockSpec((ts, d), tok),
                  pl.BlockSpec((None, 6, d), lambda i, e: (i // tiles_per_batch, 0, 0))],
        out_specs=pl.BlockSpec((ts, d), tok),
        scratch_shapes=[pltpu.VMEM((ts, d), F32)],
        compiler_params=_cparams(2),
        name="moe",
    )(h2, comb, w_gate.astype(BF16), w_up.astype(BF16), w_down.astype(BF16), x1, mod3)


def _tile(seq, pref):
    t = min(pref, seq)
    assert seq % t == 0 and t % LANES == 0, (seq, t)
    return t


def kernel(x, c, rel_bias, ada_w, ada_b, norm1_g, w_in, q_norm_g, k_norm_g, lambda_q1, lambda_k1,
           lambda_q2, lambda_k2, subln_g, w_branch_attn, pool_w, pool_scale, w_branch_pool, w_out,
           norm2_g, router_group_w, router_group_b, router_expert_w, router_expert_b,
           expert_w_gate, expert_w_up, expert_w_down):
    bsz, seq, d = x.shape
    n = bsz * seq
    ts = _tile(seq, TOKEN_TILE)
    t_attn = _tile(seq, ATTN_TILE)
    bias = _bias_tiles(rel_bias, t_attn)
    x2 = x.reshape(n, d)
    for l in range(ada_w.shape[0]):
        lambda_init = 0.8 - 0.6 * math.exp(-0.3 * l)
        mod3 = _ada(c, ada_w[l], ada_b[l]).reshape(bsz, 6, d)
        gq = jnp.tile(q_norm_g[l], D_ATTN // HEAD_DIM) * (HEAD_DIM ** -0.5 * LOG2E)
        gk = jnp.tile(k_norm_g[l], D_ATTN // HEAD_DIM)
        q, k, vt, sga, pc = _inproj(x2, mod3, norm1_g[l], w_in[l], gq, gk, pool_w[l], pool_scale[l],
                                    w_branch_pool[l], bsz=bsz, seq=seq, ts=ts)
        o = _attention(q.reshape(bsz, seq, D_ATTN), k.reshape(bsz, seq, D_ATTN), vt, bias,
                       lambda_q1[l], lambda_k1[l], lambda_q2[l], lambda_k2[l], subln_g[l],
                       t=t_attn, lambda_init=lambda_init)
        x1, h2, comb = _post(o.reshape(n, D_ATTN), sga, pc, x2, mod3, norm2_g[l], w_branch_attn[l],
                             w_out[l], router_group_w[l], router_group_b[l], router_expert_w[l],
                             router_expert_b[l], seq=seq, ts=ts)
        x2 = _moe(h2, comb, expert_w_gate[l], expert_w_up[l], expert_w_down[l], x1, mod3,
                  seq=seq, ts=ts)
    return x2.reshape(bsz, seq, d)
```

```python
import functools
import math

import jax
import jax.numpy as jnp
from jax import lax
from jax.experimental import pallas as pl
from jax.experimental.pallas import tpu as pltpu

F32 = jnp.float32
BF16 = jnp.bfloat16

CHUNK = 64
N_HEADS = 4
HEAD_DIM = 64
D_HEAD_V = 2 * HEAD_DIM
V_ROWS = D_HEAD_V + 16
D_ATTN = N_HEADS * D_HEAD_V
POOL_WINDOWS = (2, 4, 8, 16)
POOL_GROUP_DIM = 128
D_POOL = len(POOL_WINDOWS) * POOL_GROUP_DIM
POOL_HALO = 16
N_BUCKETS = 32
MAX_DISTANCE = 128
N_EXPERT_GROUPS = 4
EXPERTS_PER_GROUP = 4
N_EXPERTS = N_EXPERT_GROUPS * EXPERTS_PER_GROUP
PAIRS_PER_GROUP = 6
PAIR_LO = (0, 0, 0, 1, 1, 2)
PAIR_HI = (1, 2, 3, 2, 3, 3)
N_ROUTE_BUCKETS = N_EXPERT_GROUPS * PAIRS_PER_GROUP
RMS_EPS = 1e-6
LOG2E = math.log2(math.e)
MASKED = -1e30

LANES = 128
ROUTER_ROWS = 32
EXPERT_ROW0 = 8

VMEM_LIMIT = 56 * 1024 * 1024
TOKEN_TILE = 512
ATTN_TILE = 512
MOE_TILE = 256


def _cparams(n_axes):
    return pltpu.CompilerParams(dimension_semantics=("arbitrary",) * n_axes,
                                vmem_limit_bytes=VMEM_LIMIT)


def _const_spec(shape):
    nd = len(shape)
    return pl.BlockSpec(shape, lambda *_: (0,) * nd, pipeline_mode=pl.Buffered(1))


def _ada_kernel(ct_ref, w_ref, b_ref, o_ref):
    ct = ct_ref[...]
    s = ct * jax.nn.sigmoid(ct)
    w = w_ref[...]
    rows = [jnp.sum(w * s[:, b:b + 1], axis=0, keepdims=True) for b in range(ct.shape[1])]
    o_ref[...] = jnp.concatenate(rows, axis=0) + b_ref[...]


def _ada(c, w, b):
    bsz, d = c.shape
    n = w.shape[1]
    tn = 512
    return pl.pallas_call(
        _ada_kernel,
        out_shape=jax.ShapeDtypeStruct((bsz, n), F32),
        grid=(n // tn,),
        in_specs=[pl.BlockSpec((d, bsz), lambda j: (0, 0)),
                  pl.BlockSpec((d, tn), lambda j: (0, j)),
                  pl.BlockSpec((1, tn), lambda j: (0, j))],
        out_specs=pl.BlockSpec((bsz, tn), lambda j: (0, j)),
        compiler_params=_cparams(1),
        name="ada",
    )(c.T, w, b.reshape(1, n))


def _bias_kernel(rb_ref, bucket_ref, o_ref):
    h = pl.program_id(0)
    bucket = bucket_ref[...]
    acc = jnp.zeros(bucket.shape, F32)
    for b in range(N_BUCKETS):
        acc = jnp.where(bucket == b, rb_ref[b, h], acc)
    far = rb_ref[N_BUCKETS // 2 - 1, h]
    o_ref[...] = jnp.where(bucket < 0, MASKED, (acc - far) * LOG2E)


def _t5_bucket(rel):
    nb = N_BUCKETS // 2
    max_exact = nb // 2
    bucket = jnp.where(rel > 0, nb, 0)
    n = jnp.abs(rel)
    n_f = jnp.maximum(n, max_exact).astype(jnp.float32)
    large = max_exact + (jnp.log(n_f / max_exact) / math.log(MAX_DISTANCE / max_exact)
                         * (nb - max_exact)).astype(jnp.int32)
    large = jnp.minimum(large, nb - 1)
    return bucket + jnp.where(n < max_exact, n, large)


def _bias_tiles(rel_bias, t):
    kpos = jnp.arange(t, dtype=jnp.int32)[:, None]
    qpos = jnp.arange(t, dtype=jnp.int32)[None, :]
    rel = kpos - qpos
    visible = (kpos // CHUNK) <= (qpos // CHUNK)
    buckets = jnp.stack([_t5_bucket(rel - t), jnp.where(visible, _t5_bucket(rel), -1)])
    return pl.pallas_call(
        _bias_kernel,
        out_shape=jax.ShapeDtypeStruct((N_HEADS, 2, t, t), F32),
        grid=(N_HEADS, 2),
        in_specs=[pl.BlockSpec(memory_space=pltpu.SMEM),
                  pl.BlockSpec((None, t, t), lambda h, j: (j, 0, 0))],
        out_specs=pl.BlockSpec((None, None, t, t), lambda h, j: (h, j, 0, 0)),
        compiler_params=_cparams(2),
        name="bias_tiles",
    )(rel_bias, buckets)


def _group_rms(xc, ones_blockdiag, gain):
    ssq = jnp.dot((xc * xc).astype(BF16), ones_blockdiag, preferred_element_type=F32)
    return xc * lax.rsqrt(ssq * (1.0 / HEAD_DIM) + RMS_EPS) * gain


def _inproj_kernel(x_ref, mod_ref, g1_ref, win_ref, wvt_ref, gq_ref, gk_ref, ones_ref, poolw_ref,
                   pscale_ref, wbp_ref, q_ref, k_ref, vt_ref, sga_ref, pc_ref, ext_ref,
                   *, ts, tiles_per_batch):
    tb = pl.program_id(0) % tiles_per_batch
    x = x_ref[...]
    y = x * lax.rsqrt(jnp.mean(x * x, axis=-1, keepdims=True) + RMS_EPS)
    h = y * (g1_ref[...] * (1.0 + mod_ref[1:2, :])) + mod_ref[0:1, :]
    hb = h.astype(BF16)

    def proj(c0, c1):
        return jnp.dot(hb, win_ref[:, c0:c1], preferred_element_type=F32)

    ones_bd = ones_ref[...]
    q_ref[...] = _group_rms(proj(0, D_ATTN), ones_bd, gq_ref[...]).astype(BF16)
    k_ref[...] = _group_rms(proj(D_ATTN, 2 * D_ATTN), ones_bd, gk_ref[...]).astype(BF16)
    vt = lax.dot_general(wvt_ref[...], hb, (((1,), (1,)), ((), ())),
                         preferred_element_type=F32).astype(BF16)
    for hd in range(N_HEADS):
        vt_ref[hd, 0:D_HEAD_V, :] = vt[hd * D_HEAD_V:(hd + 1) * D_HEAD_V, :]
        vt_ref[hd, D_HEAD_V:V_ROWS, :] = jnp.ones((V_ROWS - D_HEAD_V, ts), BF16)
    c_u = 3 * D_ATTN
    c_ga = c_u + D_POOL
    c_gp = c_ga + x.shape[1]
    sga_ref[...] = jax.nn.sigmoid(proj(c_ga, c_gp)).astype(BF16)

    u = proj(c_u, c_ga)

    @pl.when(tb == 0)
    def _():
        ext_ref[0:POOL_HALO, :] = jnp.zeros((POOL_HALO, D_POOL), F32)

    ext_ref[POOL_HALO:POOL_HALO + ts, :] = u
    row = lax.broadcasted_iota(jnp.int32, (ts, 1), 0) + tb * ts
    ys = []
    for g, w in enumerate(POOL_WINDOWS):
        c0 = g * POOL_GROUP_DIM
        ug = u[:, c0:c0 + POOL_GROUP_DIM]
        acc = ug
        for d in range(1, w):
            acc = acc + ext_ref[POOL_HALO - d:POOL_HALO - d + ts, c0:c0 + POOL_GROUP_DIM]
        cnt = jnp.minimum(row + 1, w).astype(F32)
        m = acc / cnt - ug
        ys.append(jnp.dot(m.astype(BF16), poolw_ref[g], preferred_element_type=F32))
    yb = jnp.concatenate(ys, axis=1) * pscale_ref[...]
    ypool = jnp.dot(yb.astype(BF16), wbp_ref[...], preferred_element_type=F32)
    pc_ref[...] = (jax.nn.sigmoid(proj(c_gp, c_gp + x.shape[1])) * ypool).astype(BF16)
    ext_ref[0:POOL_HALO, :] = u[ts - POOL_HALO:ts, :]


def _inproj(x2, mod3, g1, w_in, gq, gk, pool_w, pool_scale, w_bp, *, bsz, seq, ts):
    n, d = x2.shape
    d_in = w_in.shape[1]
    tiles_per_batch = seq // ts
    win_b = w_in.astype(BF16)
    wvt = w_in[:, 2 * D_ATTN:3 * D_ATTN].T.astype(BF16)
    idx = jnp.arange(D_ATTN) // HEAD_DIM
    ones_bd = (idx[:, None] == idx[None, :]).astype(BF16)
    kern = functools.partial(_inproj_kernel, ts=ts, tiles_per_batch=tiles_per_batch)
    tok = lambda i: (i, 0)
    out_shape = (jax.ShapeDtypeStruct((n, D_ATTN), BF16),
                 jax.ShapeDtypeStruct((n, D_ATTN), BF16),
                 jax.ShapeDtypeStruct((bsz, N_HEADS, V_ROWS, seq), BF16),
                 jax.ShapeDtypeStruct((n, d), BF16),
                 jax.ShapeDtypeStruct((n, d), BF16))
    return pl.pallas_call(
        kern,
        out_shape=out_shape,
        grid=(n // ts,),
        in_specs=[pl.BlockSpec((ts, d), tok),
                  pl.BlockSpec((None, 6, d), lambda i: (i // tiles_per_batch, 0, 0)),
                  _const_spec((1, d)),
                  _const_spec((d, d_in)),
                  _const_spec((D_ATTN, d)),
                  _const_spec((1, D_ATTN)),
                  _const_spec((1, D_ATTN)),
                  _const_spec((D_ATTN, D_ATTN)),
                  _const_spec(pool_w.shape),
                  _const_spec((1, D_POOL)),
                  _const_spec((D_POOL, d))],
        out_specs=(pl.BlockSpec((ts, D_ATTN), tok),
                   pl.BlockSpec((ts, D_ATTN), tok),
                   pl.BlockSpec((None, N_HEADS, V_ROWS, ts),
                                lambda i: (i // tiles_per_batch, 0, 0, i % tiles_per_batch)),
                   pl.BlockSpec((ts, d), tok),
                   pl.BlockSpec((ts, d), tok)),
        scratch_shapes=[pltpu.VMEM((POOL_HALO + ts, D_POOL), F32)],
        compiler_params=_cparams(1),
        name="inproj",
    )(x2, mod3, g1.reshape(1, d), win_b, wvt, gq.reshape(1, D_ATTN), gk.reshape(1, D_ATTN), ones_bd,
      pool_w.astype(BF16), pool_scale.reshape(1, D_POOL), w_bp.astype(BF16))


def _attn_kernel(q_ref, k_ref, vt_ref, bias_ref, lq1_ref, lk1_ref, lq2_ref, lk2_ref, subg_ref,
                 o_ref, s_ref, mt_ref, m_ref, acc_ref, *, t, lambda_init):
    i = pl.program_id(2)
    q = q_ref[...]
    lane = lax.broadcasted_iota(jnp.int32, q.shape, 1)
    zero = jnp.zeros_like(q)
    qm = (jnp.where(lane < HEAD_DIM, q, zero), jnp.where(lane >= HEAD_DIM, q, zero))

    m_ref[...] = jnp.full(m_ref.shape, MASKED, F32)
    acc_ref[...] = jnp.zeros(acc_ref.shape, F32)

    def scores(j, bias, slot):
        kt = k_ref[pl.ds(pl.multiple_of(j * t, t), t), :]
        for mp in range(2):
            s = lax.dot_general(kt, qm[mp], (((1,), (1,)), ((), ())), preferred_element_type=F32)
            if bias is not None:
                s = s + bias
            s_ref[slot, mp] = s
            mt_ref[slot, mp] = jnp.max(s, axis=0, keepdims=True)

    def accumulate(j, slot):
        vt = vt_ref[:, pl.ds(pl.multiple_of(j * t, t), t)]
        for mp in range(2):
            m_old = m_ref[mp]
            m_new = jnp.maximum(m_old, mt_ref[slot, mp])
            p = jnp.exp2(s_ref[slot, mp] - m_new).astype(BF16)
            acc_ref[mp] = (jnp.exp2(m_old - m_new) * acc_ref[mp]
                           + jnp.dot(vt, p, preferred_element_type=F32))
            m_ref[mp] = m_new

    scores(i, bias_ref[1], 0)

    @pl.when(i >= 1)
    def _():
        scores(i - 1, bias_ref[0], 1)
        accumulate(i, 0)

    n_far = jnp.maximum(i - 1, 0)

    def pair_body(kk, carry):
        j = i - 1 - 2 * kk
        scores(j - 1, None, 0)
        accumulate(j, 1)
        scores(j - 2, None, 1)
        accumulate(j - 1, 0)
        return carry

    lax.fori_loop(0, n_far // 2, pair_body, 0)

    @pl.when(n_far % 2 == 1)
    def _():
        scores(0, None, 0)
        accumulate(1, 1)
        accumulate(0, 0)

    @pl.when(jnp.logical_and(i >= 1, n_far % 2 == 0))
    def _():
        accumulate(0, 1)

    @pl.when(i == 0)
    def _():
        accumulate(0, 0)

    lam = (jnp.exp(jnp.sum(lq1_ref[...] * lk1_ref[...], axis=1, keepdims=True))
           - jnp.exp(jnp.sum(lq2_ref[...] * lk2_ref[...], axis=1, keepdims=True)) + lambda_init)
    o1 = acc_ref[0, 0:D_HEAD_V, :] / acc_ref[0, D_HEAD_V:D_HEAD_V + 1, :]
    o2 = acc_ref[1, 0:D_HEAD_V, :] / acc_ref[1, D_HEAD_V:D_HEAD_V + 1, :]
    ot = o1 - lam * o2
    ot = ot * lax.rsqrt(jnp.mean(ot * ot, axis=0, keepdims=True) + RMS_EPS)
    ot = ot * subg_ref[...] * (1.0 - lambda_init)
    o_ref[...] = ot.T.astype(BF16)


def _attention(q, k, vt, bias, lq1, lk1, lq2, lk2, subln_g, *, t, lambda_init):
    bsz, seq, _ = q.shape
    kern = functools.partial(_attn_kernel, t=t, lambda_init=lambda_init)
    vec = lambda a: a.reshape(1, HEAD_DIM)
    return pl.pallas_call(
        kern,
        out_shape=jax.ShapeDtypeStruct((bsz, seq, D_ATTN), BF16),
        grid=(bsz, N_HEADS, seq // t),
        in_specs=[pl.BlockSpec((None, t, D_HEAD_V), lambda b, h, i: (b, i, h)),
                  pl.BlockSpec((None, seq, D_HEAD_V), lambda b, h, i: (b, 0, h)),
                  pl.BlockSpec((None, None, V_ROWS, seq), lambda b, h, i: (b, h, 0, 0)),
                  pl.BlockSpec((None, 2, t, t), lambda b, h, i: (h, 0, 0, 0)),
                  _const_spec((1, HEAD_DIM)), _const_spec((1, HEAD_DIM)),
                  _const_spec((1, HEAD_DIM)), _const_spec((1, HEAD_DIM)),
                  _const_spec((D_HEAD_V, 1))],
        out_specs=pl.BlockSpec((None, t, D_HEAD_V), lambda b, h, i: (b, i, h)),
        scratch_shapes=[pltpu.VMEM((2, 2, t, t), F32),
                        pltpu.VMEM((2, 2, 1, t), F32),
                        pltpu.VMEM((2, 1, t), F32),
                        pltpu.VMEM((2, V_ROWS, t), F32)],
        compiler_params=_cparams(3),
        name="attn",
    )(q, k, vt, bias, vec(lq1), vec(lk1), vec(lq2), vec(lk2), subln_g.reshape(D_HEAD_V, 1))


def _first_max(rows):
    best = rows[0]
    for r in rows[1:]:
        best = jnp.maximum(best, r)
    idx = jnp.full(best.shape, len(rows) - 1, jnp.int32)
    for j in range(len(rows) - 2, -1, -1):
        idx = jnp.where(rows[j] == best, j, idx)
    return best, idx


def _pack_bf16_pairs(v):
    c = v.shape[1] // 2
    hi = lax.bitcast_convert_type(v[:, :c].astype(BF16).astype(F32), jnp.uint32)
    lo = lax.bitcast_convert_type(v[:, c:].astype(BF16).astype(F32), jnp.uint32)
    return hi | (lo >> 16)


def _unpack_bf16_pairs(w):
    hi = lax.bitcast_convert_type(w & jnp.uint32(0xFFFF0000), F32)
    lo = lax.bitcast_convert_type(w << 16, F32)
    return jnp.concatenate([hi, lo], axis=1)


def _post_kernel(o_ref, sga_ref, pc_ref, x_ref, mod_ref, g2_ref, wba_ref, wout_ref, wr_ref, br_ref,
                 tri_ref, x1_ref, h2p_ref, route_ref, cnt_ref, carry_ref):
    @pl.when(pl.program_id(0) == 0)
    def _():
        carry_ref[...] = jnp.zeros(carry_ref.shape, F32)

    ya = jnp.dot(o_ref[...], wba_ref[...], preferred_element_type=F32)
    merged = sga_ref[...].astype(F32) * ya + pc_ref[...].astype(F32)
    z = jnp.dot(merged.astype(BF16), wout_ref[...], preferred_element_type=F32)
    x1 = x_ref[...] + mod_ref[2:3, :] * z
    x1_ref[...] = x1
    y = x1 * lax.rsqrt(jnp.mean(x1 * x1, axis=-1, keepdims=True) + RMS_EPS)
    h2 = y * (g2_ref[...] * (1.0 + mod_ref[4:5, :])) + mod_ref[3:4, :]
    hi = h2.astype(BF16)
    lo = (h2 - hi.astype(F32)).astype(BF16)

    nt = (((1,), (1,)), ((), ()))
    a = lax.dot_general(wr_ref[...], hi, nt, preferred_element_type=F32)
    b = lax.dot_general(wr_ref[0:ROUTER_ROWS, :], lo, nt, preferred_element_type=F32)
    logits = a[0:ROUTER_ROWS] + a[ROUTER_ROWS:2 * ROUTER_ROWS] + b + br_ref[...]

    gl = [logits[g:g + 1, :] for g in range(N_EXPERT_GROUPS)]
    gmax, gidx = _first_max(gl)
    gsum = gl[0] * 0.0
    for r in gl:
        gsum = gsum + jnp.exp(r - gmax)
    g_val = 1.0 / gsum
    es = []
    for r in range(EXPERTS_PER_GROUP):
        sel = jnp.zeros_like(gmax)
        for g in range(N_EXPERT_GROUPS):
            row = EXPERT_ROW0 + g * EXPERTS_PER_GROUP + r
            sel = jnp.where(gidx == g, logits[row:row + 1, :], sel)
        es.append(sel)
    e1, i1 = _first_max(es)
    rest = [jnp.where(i1 == r, -jnp.inf, es[r]) for r in range(EXPERTS_PER_GROUP)]
    e2, i2 = _first_max(rest)
    r21 = jnp.exp(e2 - e1)
    w1 = g_val / (1.0 + r21)
    w2 = g_val * r21 / (1.0 + r21)

    first = i1 < i2
    e_lo = jnp.where(first, i1, i2)
    e_hi = jnp.where(first, i2, i1)
    pair = jnp.where(e_lo == 0, 0, jnp.where(e_lo == 1, 3, 5)) + e_hi - e_lo - 1
    bucket = gidx * PAIRS_PER_GROUP + pair
    w_lo = jnp.where(first, w1, w2)
    w_hi = jnp.where(first, w2, w1)

    ts = bucket.shape[1]
    brow = lax.broadcasted_iota(jnp.int32, (ROUTER_ROWS, ts), 0)
    onehot = brow == bucket
    prefix = jnp.dot(jnp.where(onehot, 1.0, 0.0).astype(BF16), tri_ref[...],
                     preferred_element_type=F32)
    carry = carry_ref[...]
    rank = jnp.sum(jnp.where(onehot, prefix + carry, 0.0), axis=0, keepdims=True) - 1.0
    carry = carry + prefix[:, ts - 1:ts]
    carry_ref[...] = carry
    cnt_ref[...] = jnp.broadcast_to(carry, cnt_ref.shape).astype(jnp.int32)
    rrow = lax.broadcasted_iota(jnp.int32, (8, ts), 0)
    route_ref[...] = jnp.where(rrow == 0, bucket, jnp.where(rrow == 1, rank.astype(jnp.int32), 0))

    arow = lax.broadcasted_iota(jnp.int32, (LANES, ts), 0)
    aux_t = jnp.where(arow == 0, w_lo, jnp.where(arow == 1, w_hi, 0.0))
    half = h2.shape[1] // 2
    h2p_ref[:, 0:half] = _pack_bf16_pairs(h2)
    h2p_ref[:, half:half + LANES] = lax.bitcast_convert_type(aux_t.T, jnp.uint32)


def _post(o2, sga, pc, x2, mod3, g2, w_ba, w_out, wg_r, bg_r, we_r, be_r, *, seq, ts):
    n, d = x2.shape
    tiles_per_batch = seq // ts
    wr = jnp.zeros((ROUTER_ROWS, d), F32)
    wr = wr.at[0:N_EXPERT_GROUPS].set(wg_r.T).at[EXPERT_ROW0:EXPERT_ROW0 + N_EXPERTS].set(we_r.T)
    wr_hi = wr.astype(BF16)
    wr_lo = (wr - wr_hi.astype(F32)).astype(BF16)
    br = jnp.zeros((ROUTER_ROWS, 1), F32)
    br = br.at[0:N_EXPERT_GROUPS, 0].set(bg_r).at[EXPERT_ROW0:EXPERT_ROW0 + N_EXPERTS, 0].set(be_r)
    tok = lambda i: (i, 0)
    tidx = jnp.arange(ts)
    tri = (tidx[:, None] <= tidx[None, :]).astype(BF16)
    return pl.pallas_call(
        _post_kernel,
        out_shape=(jax.ShapeDtypeStruct((n, d), F32),
                   jax.ShapeDtypeStruct((n, d // 2 + LANES), jnp.uint32),
                   jax.ShapeDtypeStruct((8, n), jnp.int32),
                   jax.ShapeDtypeStruct((ROUTER_ROWS, LANES), jnp.int32)),
        grid=(n // ts,),
        in_specs=[pl.BlockSpec((ts, D_ATTN), tok),
                  pl.BlockSpec((ts, d), tok),
                  pl.BlockSpec((ts, d), tok),
                  pl.BlockSpec((ts, d), tok),
                  pl.BlockSpec((None, 6, d), lambda i: (i // tiles_per_batch, 0, 0)),
                  _const_spec((1, d)),
                  _const_spec((D_ATTN, d)),
                  _const_spec((d, d)),
                  _const_spec((2 * ROUTER_ROWS, d)),
                  _const_spec((ROUTER_ROWS, 1)),
                  _const_spec((ts, ts))],
        out_specs=(pl.BlockSpec((ts, d), tok), pl.BlockSpec((ts, d // 2 + LANES), tok),
                   pl.BlockSpec((8, ts), lambda i: (0, i)),
                   pl.BlockSpec((ROUTER_ROWS, LANES), lambda i: (0, 0))),
        scratch_shapes=[pltpu.VMEM((ROUTER_ROWS, 1), F32)],
        compiler_params=_cparams(1),
        name="post",
    )(o2, sga, pc, x2, mod3, g2.reshape(1, d), w_ba.astype(BF16), w_out.astype(BF16),
      jnp.concatenate([wr_hi, wr_lo], axis=0), br, tri)


def _row_copy(src, src_row, dst, dst_row, sem):
    return pltpu.make_async_copy(src.at[pl.ds(src_row, 1), :], dst.at[pl.ds(dst_row, 1), :], sem)


def _dispatch_kernel(pos_ref, h_ref, init_ref, hs_ref, sem, *, ts):
    del init_ref
    base = pl.program_id(0) * ts

    def start(r, carry):
        _row_copy(h_ref, r, hs_ref, pos_ref[base + r], sem).start()
        return carry

    def wait(r, carry):
        _row_copy(h_ref, 0, hs_ref, 0, sem).wait()
        return carry

    lax.fori_loop(0, ts, start, 0, unroll=8)
    lax.fori_loop(0, ts, wait, 0, unroll=8)


def _dispatch(pos, h2p, n_rows, *, ts):
    n, w = h2p.shape
    return pl.pallas_call(
        functools.partial(_dispatch_kernel, ts=ts),
        out_shape=jax.ShapeDtypeStruct((n_rows, w), h2p.dtype),
        grid_spec=pltpu.PrefetchScalarGridSpec(
            num_scalar_prefetch=1,
            grid=(n // ts,),
            in_specs=[pl.BlockSpec((ts, w), lambda i, pos: (i, 0)),
                      pl.BlockSpec(memory_space=pl.ANY)],
            out_specs=pl.BlockSpec(memory_space=pl.ANY),
            scratch_shapes=[pltpu.SemaphoreType.DMA]),
        input_output_aliases={2: 0},
        compiler_params=_cparams(1),
        name="dispatch",
    )(pos, h2p, jnp.zeros((n_rows, w), h2p.dtype))


def _experts_kernel(ea_ref, eb_ref, valid_ref, hs_ref, wga_ref, wua_ref, wda_ref, wgb_ref, wub_ref,
                    wdb_ref, ys_ref):
    del ea_ref, eb_ref
    t = pl.program_id(0)
    half = hs_ref.shape[1] - LANES

    @pl.when(valid_ref[t] == 1)
    def _():
        h = _unpack_bf16_pairs(hs_ref[:, 0:half]).astype(BF16)
        aux = lax.bitcast_convert_type(hs_ref[:, half:half + LANES], F32)

        def expert(wg_ref, wu_ref, wd_ref):
            a = jnp.dot(h, wg_ref[...], preferred_element_type=F32)
            b = jnp.dot(h, wu_ref[...], preferred_element_type=F32)
            hid = (a * jax.nn.sigmoid(a)) * b
            return jnp.dot(hid.astype(BF16), wd_ref[...], preferred_element_type=F32)

        y = (aux[:, 0:1] * expert(wga_ref, wua_ref, wda_ref)
             + aux[:, 1:2] * expert(wgb_ref, wub_ref, wdb_ref))
        ys_ref[...] = _pack_bf16_pairs(y)

    @pl.when(valid_ref[t] == 0)
    def _():
        ys_ref[...] = jnp.zeros(ys_ref.shape, ys_ref.dtype)


def _experts(tile_ea, tile_eb, tile_valid, hs, w_gate, w_up, w_down, *, tm):
    n_rows, w = hs.shape
    _, d, de = w_gate.shape
    wg, wu, wd = w_gate.astype(BF16), w_up.astype(BF16), w_down.astype(BF16)
    ea = lambda t, ea_r, eb_r, v_r: (ea_r[t], 0, 0)
    eb = lambda t, ea_r, eb_r, v_r: (eb_r[t], 0, 0)
    row = lambda t, ea_r, eb_r, v_r: (t, 0)
    return pl.pallas_call(
        _experts_kernel,
        out_shape=jax.ShapeDtypeStruct((n_rows, d // 2), jnp.uint32),
        grid_spec=pltpu.PrefetchScalarGridSpec(
            num_scalar_prefetch=3,
            grid=(n_rows // tm,),
            in_specs=[pl.BlockSpec((tm, w), row),
                      pl.BlockSpec((None, d, de), ea), pl.BlockSpec((None, d, de), ea),
                      pl.BlockSpec((None, de, d), ea),
                      pl.BlockSpec((None, d, de), eb), pl.BlockSpec((None, d, de), eb),
                      pl.BlockSpec((None, de, d), eb)],
            out_specs=pl.BlockSpec((tm, d // 2), row)),
        compiler_params=_cparams(1),
        name="experts",
    )(tile_ea, tile_eb, tile_valid, hs, wg, wu, wd, wg, wu, wd)


def _combine_kernel(pos_ref, ys_ref, x1_ref, mod_ref, out_ref, ybuf_ref, sem, *, ts):
    base = pl.program_id(0) * ts

    def start(r, carry):
        _row_copy(ys_ref, pos_ref[base + r], ybuf_ref, r, sem).start()
        return carry

    def wait(r, carry):
        _row_copy(ys_ref, 0, ybuf_ref, 0, sem).wait()
        return carry

    lax.fori_loop(0, ts, start, 0, unroll=8)
    lax.fori_loop(0, ts, wait, 0, unroll=8)
    out_ref[...] = x1_ref[...] + mod_ref[5:6, :] * _unpack_bf16_pairs(ybuf_ref[...])


def _combine(pos, ys, x1, mod3, *, seq, ts):
    n, d = x1.shape
    tiles_per_batch = seq // ts
    return pl.pallas_call(
        functools.partial(_combine_kernel, ts=ts),
        out_shape=jax.ShapeDtypeStruct((n, d), F32),
        grid_spec=pltpu.PrefetchScalarGridSpec(
            num_scalar_prefetch=1,
            grid=(n // ts,),
            in_specs=[pl.BlockSpec(memory_space=pl.ANY),
                      pl.BlockSpec((ts, d), lambda i, pos: (i, 0)),
                      pl.BlockSpec((None, 6, d), lambda i, pos: (i // tiles_per_batch, 0, 0))],
            out_specs=pl.BlockSpec((ts, d), lambda i, pos: (i, 0)),
            scratch_shapes=[pltpu.VMEM((ts, d // 2), jnp.uint32), pltpu.SemaphoreType.DMA]),
        compiler_params=_cparams(1),
        name="combine",
    )(pos, ys, x1, mod3)


def _route_tables(route, counts, n_tiles, tm):
    bucket, rank = route[0], route[1]
    cnt = counts[:N_ROUTE_BUCKETS, 0]
    padded = (cnt + tm - 1) // tm * tm
    end = jnp.cumsum(padded)
    pos = (end - padded)[bucket] + rank
    tiles_used = end[-1] // tm
    tile = jnp.arange(n_tiles, dtype=jnp.int32)
    valid = tile < tiles_used
    tile_bucket = jnp.searchsorted(end, jnp.minimum(tile, tiles_used - 1) * tm, side="right")
    tile_bucket = jnp.minimum(tile_bucket, N_ROUTE_BUCKETS - 1).astype(jnp.int32)
    group = tile_bucket // PAIRS_PER_GROUP
    pair = tile_bucket % PAIRS_PER_GROUP
    tile_ea = group * EXPERTS_PER_GROUP + jnp.asarray(PAIR_LO, jnp.int32)[pair]
    tile_eb = group * EXPERTS_PER_GROUP + jnp.asarray(PAIR_HI, jnp.int32)[pair]
    return pos.astype(jnp.int32), tile_ea, tile_eb, valid.astype(jnp.int32)


def _tile(seq, pref):
    t = min(pref, seq)
    assert seq % t == 0 and t % LANES == 0, (seq, t)
    return t


def kernel(x, c, rel_bias, ada_w, ada_b, norm1_g, w_in, q_norm_g, k_norm_g, lambda_q1, lambda_k1,
           lambda_q2, lambda_k2, subln_g, w_branch_attn, pool_w, pool_scale, w_branch_pool, w_out,
           norm2_g, router_group_w, router_group_b, router_expert_w, router_expert_b,
           expert_w_gate, expert_w_up, expert_w_down):
    bsz, seq, d = x.shape
    n = bsz * seq
    ts = _tile(seq, TOKEN_TILE)
    t_attn = _tile(seq, ATTN_TILE)
    tm = MOE_TILE
    n_tiles = -(-(n + N_ROUTE_BUCKETS * (tm - 1)) // tm)
    bias = _bias_tiles(rel_bias, t_attn)
    x2 = x.reshape(n, d)
    for l in range(ada_w.shape[0]):
        lambda_init = 0.8 - 0.6 * math.exp(-0.3 * l)
        mod3 = _ada(c, ada_w[l], ada_b[l]).reshape(bsz, 6, d)
        gq = jnp.tile(q_norm_g[l], D_ATTN // HEAD_DIM) * (HEAD_DIM ** -0.5 * LOG2E)
        gk = jnp.tile(k_norm_g[l], D_ATTN // HEAD_DIM)
        q, k, vt, sga, pc = _inproj(x2, mod3, norm1_g[l], w_in[l], gq, gk, pool_w[l], pool_scale[l],
                                    w_branch_pool[l], bsz=bsz, seq=seq, ts=ts)
        o = _attention(q.reshape(bsz, seq, D_ATTN), k.reshape(bsz, seq, D_ATTN), vt, bias,
                       lambda_q1[l], lambda_k1[l], lambda_q2[l], lambda_k2[l], subln_g[l],
                       t=t_attn, lambda_init=lambda_init)
        x1, h2p, route, counts = _post(o.reshape(n, D_ATTN), sga, pc, x2, mod3, norm2_g[l],
                                       w_branch_attn[l], w_out[l], router_group_w[l],
                                       router_group_b[l], router_expert_w[l], router_expert_b[l],
                                       seq=seq, ts=ts)
        pos, tile_ea, tile_eb, tile_valid = _route_tables(route, counts, n_tiles, tm)
        hs = _dispatch(pos, h2p, n_tiles * tm, ts=ts)
        ys = _experts(tile_ea, tile_eb, tile_valid, hs, expert_w_gate[l], expert_w_up[l],
                      expert_w_down[l], tm=tm)
        x2 = _combine(pos, ys, x1, mod3, seq=seq, ts=ts)
    return x2.reshape(bsz, seq, d)
```

```python
import functools
import math

import jax
import jax.numpy as jnp
from jax import lax
from jax.experimental import pallas as pl
from jax.experimental.pallas import tpu as pltpu

F32 = jnp.float32
BF16 = jnp.bfloat16

CHUNK = 64
N_HEADS = 4
HEAD_DIM = 64
D_HEAD_V = 2 * HEAD_DIM
V_ROWS = D_HEAD_V + 16
D_ATTN = N_HEADS * D_HEAD_V
POOL_WINDOWS = (2, 4, 8, 16)
POOL_GROUP_DIM = 128
D_POOL = len(POOL_WINDOWS) * POOL_GROUP_DIM
POOL_HALO = 16
N_BUCKETS = 32
MAX_DISTANCE = 128
N_EXPERT_GROUPS = 4
EXPERTS_PER_GROUP = 4
N_EXPERTS = N_EXPERT_GROUPS * EXPERTS_PER_GROUP
PAIRS_PER_GROUP = 6
PAIR_LO = (0, 0, 1, 1, 0, 2)
PAIR_HI = (1, 2, 2, 3, 3, 3)
N_ROUTE_BUCKETS = N_EXPERT_GROUPS * PAIRS_PER_GROUP
RMS_EPS = 1e-6
LOG2E = math.log2(math.e)
MASKED = -1e30

LANES = 128
SUBLANES = 8
ROUTER_ROWS = 32
EXPERT_ROW0 = 8

VMEM_LIMIT = 56 * 1024 * 1024
TOKEN_TILE = 512
ATTN_TILE = 512
MOE_TILE = 256


def _cparams(n_axes):
    return pltpu.CompilerParams(dimension_semantics=("arbitrary",) * n_axes,
                                vmem_limit_bytes=VMEM_LIMIT)


def _const_spec(shape):
    nd = len(shape)
    return pl.BlockSpec(shape, lambda *_: (0,) * nd, pipeline_mode=pl.Buffered(1))


def _ada_kernel(ct_ref, w_ref, b_ref, o_ref):
    ct = ct_ref[...]
    s = ct * jax.nn.sigmoid(ct)
    w = w_ref[...]
    rows = [jnp.sum(w * s[:, b:b + 1], axis=0, keepdims=True) for b in range(ct.shape[1])]
    o_ref[...] = jnp.concatenate(rows, axis=0) + b_ref[...]


def _ada(c, w, b):
    bsz, d = c.shape
    n = w.shape[1]
    tn = 512
    return pl.pallas_call(
        _ada_kernel,
        out_shape=jax.ShapeDtypeStruct((bsz, n), F32),
        grid=(n // tn,),
        in_specs=[pl.BlockSpec((d, bsz), lambda j: (0, 0)),
                  pl.BlockSpec((d, tn), lambda j: (0, j)),
                  pl.BlockSpec((1, tn), lambda j: (0, j))],
        out_specs=pl.BlockSpec((bsz, tn), lambda j: (0, j)),
        compiler_params=_cparams(1),
        name="ada",
    )(c.T, w, b.reshape(1, n))


def _log_bucket_starts():
    nb = N_BUCKETS // 2
    max_exact = nb // 2
    m = nb - max_exact
    ratio = MAX_DISTANCE // max_exact
    starts = []
    for k in range(1, m):
        n = max_exact
        while n ** m < max_exact ** m * ratio ** k:
            n += 1
        starts.append(n)
    return tuple(starts)


LOG_BUCKET_STARTS = _log_bucket_starts()


def _bias_kernel(rb_ref, o_ref, *, t):
    h = pl.program_id(0)
    kind = pl.program_id(1)
    nb = N_BUCKETS // 2
    max_exact = nb // 2
    kpos = lax.broadcasted_iota(jnp.int32, (t, t), 0)
    qpos = lax.broadcasted_iota(jnp.int32, (t, t), 1)
    rel = kpos - qpos - jnp.where(kind == 0, t, 0)
    n = jnp.abs(rel)

    def table(first):
        val = jnp.full((t, t), rb_ref[first + nb - 1, h], F32)
        for k in range(len(LOG_BUCKET_STARTS) - 1, -1, -1):
            val = jnp.where(n < LOG_BUCKET_STARTS[k], rb_ref[first + max_exact + k, h], val)
        for j in range(max_exact - 1, -1, -1):
            val = jnp.where(n == j, rb_ref[first + j, h], val)
        return val

    bias = jnp.where(rel > 0, table(nb), table(0))
    shift = CHUNK.bit_length() - 1
    hidden = jnp.logical_and(kind == 1, (kpos >> shift) > (qpos >> shift))
    o_ref[...] = jnp.where(hidden, MASKED, (bias - rb_ref[nb - 1, h]) * LOG2E)


def _bias_tiles(rel_bias, t):
    return pl.pallas_call(
        functools.partial(_bias_kernel, t=t),
        out_shape=jax.ShapeDtypeStruct((N_HEADS, 2, t, t), F32),
        grid=(N_HEADS, 2),
        in_specs=[pl.BlockSpec(memory_space=pltpu.SMEM)],
        out_specs=pl.BlockSpec((None, None, t, t), lambda h, j: (h, j, 0, 0)),
        compiler_params=_cparams(2),
        name="bias_tiles",
    )(rel_bias)


def _group_rms(xc, ones_blockdiag, gain):
    ssq = jnp.dot((xc * xc).astype(BF16), ones_blockdiag, preferred_element_type=F32)
    return xc * lax.rsqrt(ssq * (1.0 / HEAD_DIM) + RMS_EPS) * gain


def _inproj_kernel(x_ref, mod_ref, g1_ref, win_ref, gq_ref, gk_ref, ones_ref, poolw_ref,
                   pscale_ref, wbp_ref, q_ref, k_ref, vt_ref, sga_ref, pc_ref, ext_ref, v_ref,
                   *, ts, tiles_per_batch):
    tb = pl.program_id(0) % tiles_per_batch
    x = x_ref[...]
    y = x * lax.rsqrt(jnp.mean(x * x, axis=-1, keepdims=True) + RMS_EPS)
    h = y * (g1_ref[...] * (1.0 + mod_ref[1:2, :])) + mod_ref[0:1, :]
    hb = h.astype(BF16)

    def proj(c0, c1):
        return jnp.dot(hb, win_ref[:, c0:c1], preferred_element_type=F32)

    ones_bd = ones_ref[...]
    q_ref[...] = _group_rms(proj(0, D_ATTN), ones_bd, gq_ref[...]).astype(BF16)
    k_ref[...] = _group_rms(proj(D_ATTN, 2 * D_ATTN), ones_bd, gk_ref[...]).astype(BF16)
    v_ref[...] = proj(2 * D_ATTN, 3 * D_ATTN)
    vt = v_ref[...].T.astype(BF16)
    for hd in range(N_HEADS):
        vt_ref[hd, 0:D_HEAD_V, :] = vt[hd * D_HEAD_V:(hd + 1) * D_HEAD_V, :]
        vt_ref[hd, D_HEAD_V:V_ROWS, :] = jnp.ones((V_ROWS - D_HEAD_V, ts), BF16)
    c_u = 3 * D_ATTN
    c_ga = c_u + D_POOL
    c_gp = c_ga + x.shape[1]
    sga_ref[...] = jax.nn.sigmoid(proj(c_ga, c_gp)).astype(BF16)

    u = proj(c_u, c_ga)

    @pl.when(tb == 0)
    def _():
        ext_ref[0:POOL_HALO, :] = jnp.zeros((POOL_HALO, D_POOL), F32)

    ext_ref[POOL_HALO:POOL_HALO + ts, :] = u
    row = lax.broadcasted_iota(jnp.int32, (ts, 1), 0) + tb * ts
    ys = []
    for g, w in enumerate(POOL_WINDOWS):
        c0 = g * POOL_GROUP_DIM
        ug = u[:, c0:c0 + POOL_GROUP_DIM]
        acc = ug
        for d in range(1, w):
            acc = acc + ext_ref[POOL_HALO - d:POOL_HALO - d + ts, c0:c0 + POOL_GROUP_DIM]
        cnt = jnp.minimum(row + 1, w).astype(F32)
        m = acc / cnt - ug
        ys.append(jnp.dot(m.astype(BF16), poolw_ref[g], preferred_element_type=F32))
    yb = jnp.concatenate(ys, axis=1) * pscale_ref[...]
    ypool = jnp.dot(yb.astype(BF16), wbp_ref[...], preferred_element_type=F32)
    pc_ref[...] = (jax.nn.sigmoid(proj(c_gp, c_gp + x.shape[1])) * ypool).astype(BF16)
    ext_ref[0:POOL_HALO, :] = u[ts - POOL_HALO:ts, :]


def _inproj(x2, mod3, g1, w_in, gq, gk, pool_w, pool_scale, w_bp, *, bsz, seq, ts):
    n, d = x2.shape
    d_in = w_in.shape[1]
    tiles_per_batch = seq // ts
    win_b = w_in.astype(BF16)
    idx = jnp.arange(D_ATTN) // HEAD_DIM
    ones_bd = (idx[:, None] == idx[None, :]).astype(BF16)
    kern = functools.partial(_inproj_kernel, ts=ts, tiles_per_batch=tiles_per_batch)
    tok = lambda i: (i, 0)
    out_shape = (jax.ShapeDtypeStruct((n, D_ATTN), BF16),
                 jax.ShapeDtypeStruct((n, D_ATTN), BF16),
                 jax.ShapeDtypeStruct((bsz, N_HEADS, V_ROWS, seq), BF16),
                 jax.ShapeDtypeStruct((n, d), BF16),
                 jax.ShapeDtypeStruct((n, d), BF16))
    return pl.pallas_call(
        kern,
        out_shape=out_shape,
        grid=(n // ts,),
        in_specs=[pl.BlockSpec((ts, d), tok),
                  pl.BlockSpec((None, 6, d), lambda i: (i // tiles_per_batch, 0, 0)),
                  _const_spec((1, d)),
                  _const_spec((d, d_in)),
                  _const_spec((1, D_ATTN)),
                  _const_spec((1, D_ATTN)),
                  _const_spec((D_ATTN, D_ATTN)),
                  _const_spec(pool_w.shape),
                  _const_spec((1, D_POOL)),
                  _const_spec((D_POOL, d))],
        out_specs=(pl.BlockSpec((ts, D_ATTN), tok),
                   pl.BlockSpec((ts, D_ATTN), tok),
                   pl.BlockSpec((None, N_HEADS, V_ROWS, ts),
                                lambda i: (i // tiles_per_batch, 0, 0, i % tiles_per_batch)),
                   pl.BlockSpec((ts, d), tok),
                   pl.BlockSpec((ts, d), tok)),
        scratch_shapes=[pltpu.VMEM((POOL_HALO + ts, D_POOL), F32), pltpu.VMEM((ts, D_ATTN), F32)],
        compiler_params=_cparams(1),
        name="inproj",
    )(x2, mod3, g1.reshape(1, d), win_b, gq.reshape(1, D_ATTN), gk.reshape(1, D_ATTN), ones_bd,
      pool_w.astype(BF16), pool_scale.reshape(1, D_POOL), w_bp.astype(BF16))


def _attn_kernel(q_ref, k_ref, vt_ref, bias_ref, lq1_ref, lk1_ref, lq2_ref, lk2_ref, subg_ref,
                 o_ref, s_ref, mt_ref, m_ref, acc_ref, *, t, lambda_init):
    i = pl.program_id(2)
    q = q_ref[...]
    lane = lax.broadcasted_iota(jnp.int32, q.shape, 1)
    zero = jnp.zeros_like(q)
    qm = (jnp.where(lane < HEAD_DIM, q, zero), jnp.where(lane >= HEAD_DIM, q, zero))

    m_ref[...] = jnp.full(m_ref.shape, MASKED, F32)
    acc_ref[...] = jnp.zeros(acc_ref.shape, F32)

    def scores(j, bias, slot):
        kt = k_ref[pl.ds(pl.multiple_of(j * t, t), t), :]
        for mp in range(2):
            s = lax.dot_general(kt, qm[mp], (((1,), (1,)), ((), ())), preferred_element_type=F32)
            if bias is not None:
                s = s + bias
            s_ref[slot, mp] = s
            mt_ref[slot, mp] = jnp.max(s, axis=0, keepdims=True)

    def accumulate(j, slot):
        vt = vt_ref[:, pl.ds(pl.multiple_of(j * t, t), t)]
        for mp in range(2):
            m_old = m_ref[mp]
            m_new = jnp.maximum(m_old, mt_ref[slot, mp])
            p = jnp.exp2(s_ref[slot, mp] - m_new).astype(BF16)
            acc_ref[mp] = (jnp.exp2(m_old - m_new) * acc_ref[mp]
                           + jnp.dot(vt, p, preferred_element_type=F32))
            m_ref[mp] = m_new

    scores(i, bias_ref[1], 0)

    @pl.when(i >= 1)
    def _():
        scores(i - 1, bias_ref[0], 1)
        accumulate(i, 0)

    n_far = jnp.maximum(i - 1, 0)

    def pair_body(kk, carry):
        j = i - 1 - 2 * kk
        scores(j - 1, None, 0)
        accumulate(j, 1)
        scores(j - 2, None, 1)
        accumulate(j - 1, 0)
        return carry

    lax.fori_loop(0, n_far // 2, pair_body, 0)

    @pl.when(n_far % 2 == 1)
    def _():
        scores(0, None, 0)
        accumulate(1, 1)
        accumulate(0, 0)

    @pl.when(jnp.logical_and(i >= 1, n_far % 2 == 0))
    def _():
        accumulate(0, 1)

    @pl.when(i == 0)
    def _():
        accumulate(0, 0)

    lam = (jnp.exp(jnp.sum(lq1_ref[...] * lk1_ref[...], axis=1, keepdims=True))
           - jnp.exp(jnp.sum(lq2_ref[...] * lk2_ref[...], axis=1, keepdims=True)) + lambda_init)
    o1 = acc_ref[0, 0:D_HEAD_V, :] / acc_ref[0, D_HEAD_V:D_HEAD_V + 1, :]
    o2 = acc_ref[1, 0:D_HEAD_V, :] / acc_ref[1, D_HEAD_V:D_HEAD_V + 1, :]
    ot = o1 - lam * o2
    ot = ot * lax.rsqrt(jnp.mean(ot * ot, axis=0, keepdims=True) + RMS_EPS)
    ot = ot * subg_ref[...] * (1.0 - lambda_init)
    o_ref[...] = ot.T.astype(BF16)


def _attention(q, k, vt, bias, lq1, lk1, lq2, lk2, subln_g, *, t, lambda_init):
    bsz, seq, _ = q.shape
    kern = functools.partial(_attn_kernel, t=t, lambda_init=lambda_init)
    vec = lambda a: a.reshape(1, HEAD_DIM)
    return pl.pallas_call(
        kern,
        out_shape=jax.ShapeDtypeStruct((bsz, seq, D_ATTN), BF16),
        grid=(bsz, N_HEADS, seq // t),
        in_specs=[pl.BlockSpec((None, t, D_HEAD_V), lambda b, h, i: (b, i, h)),
                  pl.BlockSpec((None, seq, D_HEAD_V), lambda b, h, i: (b, 0, h)),
                  pl.BlockSpec((None, None, V_ROWS, seq), lambda b, h, i: (b, h, 0, 0)),
                  pl.BlockSpec((None, 2, t, t), lambda b, h, i: (h, 0, 0, 0)),
                  _const_spec((1, HEAD_DIM)), _const_spec((1, HEAD_DIM)),
                  _const_spec((1, HEAD_DIM)), _const_spec((1, HEAD_DIM)),
                  _const_spec((D_HEAD_V, 1))],
        out_specs=pl.BlockSpec((None, t, D_HEAD_V), lambda b, h, i: (b, i, h)),
        scratch_shapes=[pltpu.VMEM((2, 2, t, t), F32),
                        pltpu.VMEM((2, 2, 1, t), F32),
                        pltpu.VMEM((2, 1, t), F32),
                        pltpu.VMEM((2, V_ROWS, t), F32)],
        compiler_params=_cparams(3),
        name="attn",
    )(q, k, vt, bias, vec(lq1), vec(lk1), vec(lq2), vec(lk2), subln_g.reshape(D_HEAD_V, 1))


def _first_max(rows):
    best = rows[0]
    for r in rows[1:]:
        best = jnp.maximum(best, r)
    idx = jnp.full(best.shape, len(rows) - 1, jnp.int32)
    for j in range(len(rows) - 2, -1, -1):
        idx = jnp.where(rows[j] == best, j, idx)
    return best, idx


def _post_kernel(o_ref, sga_ref, pc_ref, x_ref, mod_ref, g2_ref, wba_ref, wout_ref, wr_ref, br_ref,
                 tri_ref, x1_ref, h2p_ref, route_ref, cnt_ref, carry_ref):
    @pl.when(pl.program_id(0) == 0)
    def _():
        carry_ref[...] = jnp.zeros(carry_ref.shape, F32)

    ya = jnp.dot(o_ref[...], wba_ref[...], preferred_element_type=F32)
    merged = sga_ref[...].astype(F32) * ya + pc_ref[...].astype(F32)
    z = jnp.dot(merged.astype(BF16), wout_ref[...], preferred_element_type=F32)
    x1 = x_ref[...] + mod_ref[2:3, :] * z
    x1_ref[...] = x1
    y = x1 * lax.rsqrt(jnp.mean(x1 * x1, axis=-1, keepdims=True) + RMS_EPS)
    h2 = y * (g2_ref[...] * (1.0 + mod_ref[4:5, :])) + mod_ref[3:4, :]
    hi = h2.astype(BF16)
    lo = (h2 - hi.astype(F32)).astype(BF16)

    nt = (((1,), (1,)), ((), ()))
    a = lax.dot_general(wr_ref[...], hi, nt, preferred_element_type=F32)
    b = lax.dot_general(wr_ref[0:ROUTER_ROWS, :], lo, nt, preferred_element_type=F32)
    logits = a[0:ROUTER_ROWS] + a[ROUTER_ROWS:2 * ROUTER_ROWS] + b + br_ref[...]

    gl = [logits[g:g + 1, :] for g in range(N_EXPERT_GROUPS)]
    gmax, gidx = _first_max(gl)
    gsum = gl[0] * 0.0
    for r in gl:
        gsum = gsum + jnp.exp(r - gmax)
    g_val = 1.0 / gsum
    es = []
    for r in range(EXPERTS_PER_GROUP):
        sel = jnp.zeros_like(gmax)
        for g in range(N_EXPERT_GROUPS):
            row = EXPERT_ROW0 + g * EXPERTS_PER_GROUP + r
            sel = jnp.where(gidx == g, logits[row:row + 1, :], sel)
        es.append(sel)
    e1, i1 = _first_max(es)
    rest = [jnp.where(i1 == r, -jnp.inf, es[r]) for r in range(EXPERTS_PER_GROUP)]
    e2, i2 = _first_max(rest)
    r21 = jnp.exp(e2 - e1)
    w1 = g_val / (1.0 + r21)
    w2 = g_val * r21 / (1.0 + r21)

    first = i1 < i2
    e_lo = jnp.where(first, i1, i2)
    e_hi = jnp.where(first, i2, i1)
    pair = jnp.zeros_like(e_lo)
    for p in range(1, PAIRS_PER_GROUP):
        pair = jnp.where(jnp.logical_and(e_lo == PAIR_LO[p], e_hi == PAIR_HI[p]), p, pair)
    bucket = gidx * PAIRS_PER_GROUP + pair
    w_lo = jnp.where(first, w1, w2)
    w_hi = jnp.where(first, w2, w1)

    ts = bucket.shape[1]
    brow = lax.broadcasted_iota(jnp.int32, (ROUTER_ROWS, ts), 0)
    onehot = brow == bucket
    prefix = jnp.dot(jnp.where(onehot, 1.0, 0.0).astype(BF16), tri_ref[...],
                     preferred_element_type=F32)
    carry = carry_ref[...]
    rank = jnp.sum(jnp.where(onehot, prefix + carry, 0.0), axis=0, keepdims=True) - 1.0
    carry = carry + prefix[:, ts - 1:ts]
    carry_ref[...] = carry
    cnt_ref[...] = jnp.broadcast_to(carry, cnt_ref.shape).astype(jnp.int32)
    rrow = lax.broadcasted_iota(jnp.int32, (8, ts), 0)
    route_ref[...] = jnp.where(rrow == 0, bucket, jnp.where(rrow == 1, rank.astype(jnp.int32), 0))

    arow = lax.broadcasted_iota(jnp.int32, (LANES, ts), 0)
    aux_t = jnp.where(arow == 0, w_lo, jnp.where(arow == 1, w_hi, 0.0))
    d = h2.shape[1]
    h2p_ref[:, 0:d] = h2
    h2p_ref[:, d:d + LANES] = aux_t.T


def _post(o2, sga, pc, x2, mod3, g2, w_ba, w_out, wg_r, bg_r, we_r, be_r, *, seq, ts):
    n, d = x2.shape
    tiles_per_batch = seq // ts
    wr = jnp.zeros((ROUTER_ROWS, d), F32)
    wr = wr.at[0:N_EXPERT_GROUPS].set(wg_r.T).at[EXPERT_ROW0:EXPERT_ROW0 + N_EXPERTS].set(we_r.T)
    wr_hi = wr.astype(BF16)
    wr_lo = (wr - wr_hi.astype(F32)).astype(BF16)
    br = jnp.zeros((ROUTER_ROWS, 1), F32)
    br = br.at[0:N_EXPERT_GROUPS, 0].set(bg_r).at[EXPERT_ROW0:EXPERT_ROW0 + N_EXPERTS, 0].set(be_r)
    tok = lambda i: (i, 0)
    tidx = jnp.arange(ts)
    tri = (tidx[:, None] <= tidx[None, :]).astype(BF16)
    return pl.pallas_call(
        _post_kernel,
        out_shape=(jax.ShapeDtypeStruct((n, d), F32),
                   jax.ShapeDtypeStruct((n, d + LANES), F32),
                   jax.ShapeDtypeStruct((8, n), jnp.int32),
                   jax.ShapeDtypeStruct((ROUTER_ROWS, LANES), jnp.int32)),
        grid=(n // ts,),
        in_specs=[pl.BlockSpec((ts, D_ATTN), tok),
                  pl.BlockSpec((ts, d), tok),
                  pl.BlockSpec((ts, d), tok),
                  pl.BlockSpec((ts, d), tok),
                  pl.BlockSpec((None, 6, d), lambda i: (i // tiles_per_batch, 0, 0)),
                  _const_spec((1, d)),
                  _const_spec((D_ATTN, d)),
                  _const_spec((d, d)),
                  _const_spec((2 * ROUTER_ROWS, d)),
                  _const_spec((ROUTER_ROWS, 1)),
                  _const_spec((ts, ts))],
        out_specs=(pl.BlockSpec((ts, d), tok), pl.BlockSpec((ts, d + LANES), tok),
                   pl.BlockSpec((8, ts), lambda i: (0, i)),
                   pl.BlockSpec((ROUTER_ROWS, LANES), lambda i: (0, 0))),
        scratch_shapes=[pltpu.VMEM((ROUTER_ROWS, 1), F32)],
        compiler_params=_cparams(1),
        name="post",
    )(o2, sga, pc, x2, mod3, g2.reshape(1, d), w_ba.astype(BF16), w_out.astype(BF16),
      jnp.concatenate([wr_hi, wr_lo], axis=0), br, tri)


def _group_sublane(row):
    return lax.shift_right_logical(row, SUBLANES.bit_length() - 1), row & (SUBLANES - 1)


def _row_copy(src, src_row, dst, dst_row, sem):
    sg, ss = src_row
    dg, ds = dst_row
    return pltpu.make_async_copy(src.at[sg, pl.ds(ss, 1), :], dst.at[dg, pl.ds(ds, 1), :], sem)


def _dispatch_kernel(pos_ref, h_ref, init_ref, hs_ref, sem, *, ts):
    del init_ref
    base = pl.program_id(0) * ts

    def start(g, carry):
        for u in range(SUBLANES):
            p = pos_ref[base + g * SUBLANES + u]
            _row_copy(h_ref, (g, u), hs_ref, _group_sublane(p), sem).start(priority=u % 2)
        return carry

    def wait(g, carry):
        for u in range(SUBLANES):
            _row_copy(h_ref, (0, 0), hs_ref, (0, 0), sem).wait()
        return carry

    lax.fori_loop(0, ts // SUBLANES, start, 0)
    lax.fori_loop(0, ts // SUBLANES, wait, 0)


def _dispatch(pos, h2p, n_rows, *, ts):
    n, w = h2p.shape
    return pl.pallas_call(
        functools.partial(_dispatch_kernel, ts=ts),
        out_shape=jax.ShapeDtypeStruct((n_rows // SUBLANES, SUBLANES, w), h2p.dtype),
        grid_spec=pltpu.PrefetchScalarGridSpec(
            num_scalar_prefetch=1,
            grid=(n // ts,),
            in_specs=[pl.BlockSpec((ts // SUBLANES, SUBLANES, w), lambda i, pos: (i, 0, 0)),
                      pl.BlockSpec(memory_space=pl.ANY)],
            out_specs=pl.BlockSpec(memory_space=pl.ANY),
            scratch_shapes=[pltpu.SemaphoreType.DMA]),
        input_output_aliases={2: 0},
        compiler_params=_cparams(1),
        name="dispatch",
    )(pos, h2p.reshape(n // SUBLANES, SUBLANES, w),
      jnp.zeros((n_rows // SUBLANES, SUBLANES, w), h2p.dtype)).reshape(n_rows, w)


def _experts_kernel(ea_ref, eb_ref, valid_ref, hs_ref, wga_ref, wua_ref, wda_ref, wgb_ref, wub_ref,
                    wdb_ref, ys_ref):
    del ea_ref, eb_ref
    t = pl.program_id(0)
    d = hs_ref.shape[1] - LANES

    @pl.when(valid_ref[t] == 1)
    def _():
        h = hs_ref[:, 0:d].astype(BF16)
        aux = hs_ref[:, d:d + LANES]

        def expert(wg_ref, wu_ref, wd_ref):
            a = jnp.dot(h, wg_ref[...], preferred_element_type=F32)
            b = jnp.dot(h, wu_ref[...], preferred_element_type=F32)
            hid = (a * jax.nn.sigmoid(a)) * b
            return jnp.dot(hid.astype(BF16), wd_ref[...], preferred_element_type=F32)

        ys_ref[...] = (aux[:, 0:1] * expert(wga_ref, wua_ref, wda_ref)
                       + aux[:, 1:2] * expert(wgb_ref, wub_ref, wdb_ref))

    @pl.when(valid_ref[t] == 0)
    def _():
        ys_ref[...] = jnp.zeros(ys_ref.shape, ys_ref.dtype)


def _experts(tile_ea, tile_eb, tile_valid, hs, w_gate, w_up, w_down, *, tm):
    n_rows, w = hs.shape
    _, d, de = w_gate.shape
    wg, wu, wd = w_gate.astype(BF16), w_up.astype(BF16), w_down.astype(BF16)
    ea = lambda t, ea_r, eb_r, v_r: (ea_r[t], 0, 0)
    eb = lambda t, ea_r, eb_r, v_r: (eb_r[t], 0, 0)
    row = lambda t, ea_r, eb_r, v_r: (t, 0)
    return pl.pallas_call(
        _experts_kernel,
        out_shape=jax.ShapeDtypeStruct((n_rows, d), F32),
        grid_spec=pltpu.PrefetchScalarGridSpec(
            num_scalar_prefetch=3,
            grid=(n_rows // tm,),
            in_specs=[pl.BlockSpec((tm, w), row),
                      pl.BlockSpec((None, d, de), ea), pl.BlockSpec((None, d, de), ea),
                      pl.BlockSpec((None, de, d), ea),
                      pl.BlockSpec((None, d, de), eb), pl.BlockSpec((None, d, de), eb),
                      pl.BlockSpec((None, de, d), eb)],
            out_specs=pl.BlockSpec((tm, d), row)),
        compiler_params=_cparams(1),
        name="experts",
    )(tile_ea, tile_eb, tile_valid, hs, wg, wu, wd, wg, wu, wd)


def _combine_kernel(pos_ref, ys_ref, x1_ref, mod_ref, out_ref, ybuf_ref, sem, *, ts):
    i = pl.program_id(0)
    groups = ts // SUBLANES

    def gather(tile, slot):
        base = tile * ts

        def start(g, carry):
            for u in range(SUBLANES):
                p = pos_ref[base + g * SUBLANES + u]
                _row_copy(ys_ref, _group_sublane(p), ybuf_ref.at[slot], (g, u),
                          sem.at[slot]).start(priority=u % 2)
            return carry

        lax.fori_loop(0, groups, start, 0)

    def wait_all(slot):
        def wait(g, carry):
            for u in range(SUBLANES):
                _row_copy(ys_ref, (0, 0), ybuf_ref.at[slot], (0, 0), sem.at[slot]).wait()
            return carry

        lax.fori_loop(0, groups, wait, 0)

    def step(slot):
        @pl.when(i + 1 < pl.num_programs(0))
        def _():
            gather(i + 1, 1 - slot)

        wait_all(slot)
        y = ybuf_ref[slot].reshape(ts, ybuf_ref.shape[-1])
        out_ref[...] = x1_ref[...] + mod_ref[5:6, :] * y

    @pl.when(i == 0)
    def _():
        gather(0, 0)

    @pl.when(i % 2 == 0)
    def _():
        step(0)

    @pl.when(i % 2 == 1)
    def _():
        step(1)


def _combine(pos, ys, x1, mod3, *, seq, ts):
    n, d = x1.shape
    tiles_per_batch = seq // ts
    return pl.pallas_call(
        functools.partial(_combine_kernel, ts=ts),
        out_shape=jax.ShapeDtypeStruct((n, d), F32),
        grid_spec=pltpu.PrefetchScalarGridSpec(
            num_scalar_prefetch=1,
            grid=(n // ts,),
            in_specs=[pl.BlockSpec(memory_space=pl.ANY),
                      pl.BlockSpec((ts, d), lambda i, pos: (i, 0)),
                      pl.BlockSpec((None, 6, d), lambda i, pos: (i // tiles_per_batch, 0, 0))],
            out_specs=pl.BlockSpec((ts, d), lambda i, pos: (i, 0)),
            scratch_shapes=[pltpu.VMEM((2, ts // SUBLANES, SUBLANES, d), F32),
                            pltpu.SemaphoreType.DMA((2,))]),
        compiler_params=_cparams(1),
        name="combine",
    )(pos, ys.reshape(ys.shape[0] // SUBLANES, SUBLANES, d), x1, mod3)


def _route_tables(route, counts, n_tiles, tm):
    bucket, rank = route[0], route[1]
    cnt = counts[:N_ROUTE_BUCKETS, 0]
    padded = (cnt + tm - 1) // tm * tm
    end = jnp.cumsum(padded)
    pos = (end - padded)[bucket] + rank
    tiles_used = end[-1] // tm
    tile = jnp.arange(n_tiles, dtype=jnp.int32)
    valid = tile < tiles_used
    first_row = jnp.minimum(tile, tiles_used - 1) * tm
    tile_bucket = jnp.sum((end[None, :] <= first_row[:, None]).astype(jnp.int32), axis=1)
    tile_bucket = jnp.minimum(tile_bucket, N_ROUTE_BUCKETS - 1)
    group = tile_bucket // PAIRS_PER_GROUP
    pair = tile_bucket % PAIRS_PER_GROUP
    tile_ea = group * EXPERTS_PER_GROUP + jnp.asarray(PAIR_LO, jnp.int32)[pair]
    tile_eb = group * EXPERTS_PER_GROUP + jnp.asarray(PAIR_HI, jnp.int32)[pair]
    return pos.astype(jnp.int32), tile_ea, tile_eb, valid.astype(jnp.int32)


def _tile(seq, pref):
    t = min(pref, seq)
    assert seq % t == 0 and t % LANES == 0, (seq, t)
    return t


def kernel(x, c, rel_bias, ada_w, ada_b, norm1_g, w_in, q_norm_g, k_norm_g, lambda_q1, lambda_k1,
           lambda_q2, lambda_k2, subln_g, w_branch_attn, pool_w, pool_scale, w_branch_pool, w_out,
           norm2_g, router_group_w, router_group_b, router_expert_w, router_expert_b,
           expert_w_gate, expert_w_up, expert_w_down):
    bsz, seq, d = x.shape
    n = bsz * seq
    ts = _tile(seq, TOKEN_TILE)
    t_attn = _tile(seq, ATTN_TILE)
    tm = MOE_TILE
    n_tiles = -(-(n + N_ROUTE_BUCKETS * (tm - 1)) // tm)
    bias = _bias_tiles(rel_bias, t_attn)
    x2 = x.reshape(n, d)
    for l in range(ada_w.shape[0]):
        lambda_init = 0.8 - 0.6 * math.exp(-0.3 * l)
        mod3 = _ada(c, ada_w[l], ada_b[l]).reshape(bsz, 6, d)
        gq = jnp.tile(q_norm_g[l], D_ATTN // HEAD_DIM) * (HEAD_DIM ** -0.5 * LOG2E)
        gk = jnp.tile(k_norm_g[l], D_ATTN // HEAD_DIM)
        q, k, vt, sga, pc = _inproj(x2, mod3, norm1_g[l], w_in[l], gq, gk, pool_w[l], pool_scale[l],
                                    w_branch_pool[l], bsz=bsz, seq=seq, ts=ts)
        o = _attention(q.reshape(bsz, seq, D_ATTN), k.reshape(bsz, seq, D_ATTN), vt, bias,
                       lambda_q1[l], lambda_k1[l], lambda_q2[l], lambda_k2[l], subln_g[l],
                       t=t_attn, lambda_init=lambda_init)
        x1, h2p, route, counts = _post(o.reshape(n, D_ATTN), sga, pc, x2, mod3, norm2_g[l],
                                       w_branch_attn[l], w_out[l], router_group_w[l],
                                       router_group_b[l], router_expert_w[l], router_expert_b[l],
                                       seq=seq, ts=ts)
        pos, tile_ea, tile_eb, tile_valid = _route_tables(route, counts, n_tiles, tm)
        hs = _dispatch(pos, h2p, n_tiles * tm, ts=ts)
        ys = _experts(tile_ea, tile_eb, tile_valid, hs, expert_w_gate[l], expert_w_up[l],
                      expert_w_down[l], tm=tm)
        x2 = _combine(pos, ys, x1, mod3, seq=seq, ts=ts)
    return x2.reshape(bsz, seq, d)
```

```python
import functools
import math

import jax
import jax.numpy as jnp
from jax import lax
from jax.experimental import pallas as pl
from jax.experimental.pallas import tpu as pltpu

F32 = jnp.float32
BF16 = jnp.bfloat16

CHUNK = 64
N_HEADS = 4
HEAD_DIM = 64
D_HEAD_V = 2 * HEAD_DIM
V_ROWS = D_HEAD_V + 16
D_ATTN = N_HEADS * D_HEAD_V
POOL_WINDOWS = (2, 4, 8, 16)
POOL_GROUP_DIM = 128
D_POOL = len(POOL_WINDOWS) * POOL_GROUP_DIM
POOL_HALO = 16
N_BUCKETS = 32
MAX_DISTANCE = 128
N_EXPERT_GROUPS = 4
EXPERTS_PER_GROUP = 4
N_EXPERTS = N_EXPERT_GROUPS * EXPERTS_PER_GROUP
PAIRS_PER_GROUP = 6
PAIR_LO = (0, 0, 1, 1, 0, 2)
PAIR_HI = (1, 2, 2, 3, 3, 3)
N_ROUTE_BUCKETS = N_EXPERT_GROUPS * PAIRS_PER_GROUP
RMS_EPS = 1e-6
LOG2E = math.log2(math.e)
MASKED = -1e30

LANES = 128
SUBLANES = 8
ROUTER_ROWS = 32
EXPERT_ROW0 = 8

VMEM_LIMIT = 56 * 1024 * 1024
TOKEN_TILE = 512
ATTN_TILE = 512
MOE_TILE = 256


def _cparams(n_axes):
    return pltpu.CompilerParams(dimension_semantics=("arbitrary",) * n_axes,
                                vmem_limit_bytes=VMEM_LIMIT)


def _const_spec(shape):
    nd = len(shape)
    return pl.BlockSpec(shape, lambda *_: (0,) * nd, pipeline_mode=pl.Buffered(1))


def _ada_kernel(ct_ref, w_ref, b_ref, o_ref):
    @pl.when(pl.program_id(0) == 0)
    def _():
        o_ref[...] = jnp.broadcast_to(b_ref[...], o_ref.shape)

    ct = ct_ref[...]
    s = ct * jax.nn.sigmoid(ct)
    w = w_ref[...]
    rows = [jnp.sum(w * s[:, b:b + 1], axis=0, keepdims=True) for b in range(ct.shape[1])]
    o_ref[...] += jnp.concatenate(rows, axis=0)


def _ada(c, w, b):
    bsz, d = c.shape
    n = w.shape[1]
    rows = 256
    return pl.pallas_call(
        _ada_kernel,
        out_shape=jax.ShapeDtypeStruct((bsz, n), F32),
        grid=(d // rows,),
        in_specs=[pl.BlockSpec((rows, bsz), lambda j: (j, 0)),
                  pl.BlockSpec((rows, n), lambda j: (j, 0)),
                  pl.BlockSpec((1, n), lambda j: (0, 0))],
        out_specs=pl.BlockSpec((bsz, n), lambda j: (0, 0)),
        compiler_params=_cparams(1),
        name="ada",
    )(c.T, w, b.reshape(1, n))


def _log_bucket_starts():
    nb = N_BUCKETS // 2
    max_exact = nb // 2
    m = nb - max_exact
    ratio = MAX_DISTANCE // max_exact
    starts = []
    for k in range(1, m):
        n = max_exact
        while n ** m < max_exact ** m * ratio ** k:
            n += 1
        starts.append(n)
    return tuple(starts)


LOG_BUCKET_STARTS = _log_bucket_starts()


def _bias_kernel(rb_ref, o_ref, *, t):
    h = pl.program_id(0)
    kind = pl.program_id(1)
    nb = N_BUCKETS // 2
    max_exact = nb // 2
    kpos = lax.broadcasted_iota(jnp.int32, (t, t), 0)
    qpos = lax.broadcasted_iota(jnp.int32, (t, t), 1)
    rel = kpos - qpos - jnp.where(kind == 0, t, 0)
    n = jnp.abs(rel)

    def table(first):
        val = jnp.full((t, t), rb_ref[first + nb - 1, h], F32)
        for k in range(len(LOG_BUCKET_STARTS) - 1, -1, -1):
            val = jnp.where(n < LOG_BUCKET_STARTS[k], rb_ref[first + max_exact + k, h], val)
        for j in range(max_exact - 1, -1, -1):
            val = jnp.where(n == j, rb_ref[first + j, h], val)
        return val

    bias = jnp.where(rel > 0, table(nb), table(0))
    shift = CHUNK.bit_length() - 1
    hidden = jnp.logical_and(kind == 1, (kpos >> shift) > (qpos >> shift))
    o_ref[...] = jnp.where(hidden, MASKED, (bias - rb_ref[nb - 1, h]) * LOG2E)


def _bias_tiles(rel_bias, t):
    return pl.pallas_call(
        functools.partial(_bias_kernel, t=t),
        out_shape=jax.ShapeDtypeStruct((N_HEADS, 2, t, t), F32),
        grid=(N_HEADS, 2),
        in_specs=[pl.BlockSpec(memory_space=pltpu.SMEM)],
        out_specs=pl.BlockSpec((None, None, t, t), lambda h, j: (h, j, 0, 0)),
        compiler_params=_cparams(2),
        name="bias_tiles",
    )(rel_bias)


def _group_rms(xc, ones_blockdiag, gain):
    ssq = jnp.dot((xc * xc).astype(BF16), ones_blockdiag, preferred_element_type=F32)
    return xc * lax.rsqrt(ssq * (1.0 / HEAD_DIM) + RMS_EPS) * gain


def _inproj_kernel(x_ref, mod_ref, g1_ref, win_ref, gq_ref, gk_ref, ones_ref, poolw_ref,
                   pscale_ref, wbp_ref, q_ref, k_ref, vt_ref, sga_ref, pc_ref, ext_ref, v_ref,
                   *, ts, tiles_per_batch):
    tb = pl.program_id(0) % tiles_per_batch
    x = x_ref[...]
    y = x * lax.rsqrt(jnp.mean(x * x, axis=-1, keepdims=True) + RMS_EPS)
    h = y * (g1_ref[...] * (1.0 + mod_ref[1:2, :])) + mod_ref[0:1, :]
    hb = h.astype(BF16)

    def proj(c0, c1):
        return jnp.dot(hb, win_ref[:, c0:c1], preferred_element_type=F32)

    ones_bd = ones_ref[...]
    q_ref[...] = _group_rms(proj(0, D_ATTN), ones_bd, gq_ref[...]).astype(BF16)
    k_ref[...] = _group_rms(proj(D_ATTN, 2 * D_ATTN), ones_bd, gk_ref[...]).astype(BF16)
    v_ref[...] = proj(2 * D_ATTN, 3 * D_ATTN)
    vt = v_ref[...].T.astype(BF16)
    for hd in range(N_HEADS):
        vt_ref[hd, 0:D_HEAD_V, :] = vt[hd * D_HEAD_V:(hd + 1) * D_HEAD_V, :]
        vt_ref[hd, D_HEAD_V:V_ROWS, :] = jnp.ones((V_ROWS - D_HEAD_V, ts), BF16)
    c_u = 3 * D_ATTN
    c_ga = c_u + D_POOL
    c_gp = c_ga + x.shape[1]
    sga_ref[...] = jax.nn.sigmoid(proj(c_ga, c_gp)).astype(BF16)

    u = proj(c_u, c_ga)

    @pl.when(tb == 0)
    def _():
        ext_ref[0:POOL_HALO, :] = jnp.zeros((POOL_HALO, D_POOL), F32)

    ext_ref[POOL_HALO:POOL_HALO + ts, :] = u
    row = lax.broadcasted_iota(jnp.int32, (ts, 1), 0) + tb * ts
    ys = []
    for g, w in enumerate(POOL_WINDOWS):
        c0 = g * POOL_GROUP_DIM
        ug = u[:, c0:c0 + POOL_GROUP_DIM]
        acc = ug
        for d in range(1, w):
            acc = acc + ext_ref[POOL_HALO - d:POOL_HALO - d + ts, c0:c0 + POOL_GROUP_DIM]
        cnt = jnp.minimum(row + 1, w).astype(F32)
        m = acc / cnt - ug
        ys.append(jnp.dot(m.astype(BF16), poolw_ref[g], preferred_element_type=F32))
    yb = jnp.concatenate(ys, axis=1) * pscale_ref[...]
    ypool = jnp.dot(yb.astype(BF16), wbp_ref[...], preferred_element_type=F32)
    pc_ref[...] = (jax.nn.sigmoid(proj(c_gp, c_gp + x.shape[1])) * ypool).astype(BF16)
    ext_ref[0:POOL_HALO, :] = u[ts - POOL_HALO:ts, :]


def _inproj(x2, mod3, g1, w_in, gq, gk, pool_w, pool_scale, w_bp, *, bsz, seq, ts):
    n, d = x2.shape
    d_in = w_in.shape[1]
    tiles_per_batch = seq // ts
    win_b = w_in.astype(BF16)
    idx = jnp.arange(D_ATTN) // HEAD_DIM
    ones_bd = (idx[:, None] == idx[None, :]).astype(BF16)
    kern = functools.partial(_inproj_kernel, ts=ts, tiles_per_batch=tiles_per_batch)
    tok = lambda i: (i, 0)
    out_shape = (jax.ShapeDtypeStruct((n, D_ATTN), BF16),
                 jax.ShapeDtypeStruct((n, D_ATTN), BF16),
                 jax.ShapeDtypeStruct((bsz, N_HEADS, V_ROWS, seq), BF16),
                 jax.ShapeDtypeStruct((n, d), BF16),
                 jax.ShapeDtypeStruct((n, d), BF16))
    return pl.pallas_call(
        kern,
        out_shape=out_shape,
        grid=(n // ts,),
        in_specs=[pl.BlockSpec((ts, d), tok),
                  pl.BlockSpec((None, 6, d), lambda i: (i // tiles_per_batch, 0, 0)),
                  _const_spec((1, d)),
                  _const_spec((d, d_in)),
                  _const_spec((1, D_ATTN)),
                  _const_spec((1, D_ATTN)),
                  _const_spec((D_ATTN, D_ATTN)),
                  _const_spec(pool_w.shape),
                  _const_spec((1, D_POOL)),
                  _const_spec((D_POOL, d))],
        out_specs=(pl.BlockSpec((ts, D_ATTN), tok),
                   pl.BlockSpec((ts, D_ATTN), tok),
                   pl.BlockSpec((None, N_HEADS, V_ROWS, ts),
                                lambda i: (i // tiles_per_batch, 0, 0, i % tiles_per_batch)),
                   pl.BlockSpec((ts, d), tok),
                   pl.BlockSpec((ts, d), tok)),
        scratch_shapes=[pltpu.VMEM((POOL_HALO + ts, D_POOL), F32), pltpu.VMEM((ts, D_ATTN), F32)],
        compiler_params=_cparams(1),
        name="inproj",
    )(x2, mod3, g1.reshape(1, d), win_b, gq.reshape(1, D_ATTN), gk.reshape(1, D_ATTN), ones_bd,
      pool_w.astype(BF16), pool_scale.reshape(1, D_POOL), w_bp.astype(BF16))


def _attn_kernel(q_ref, qn_ref, k_ref, vt_ref, bias_ref, lq1_ref, lk1_ref, lq2_ref, lk2_ref, subg_ref,
                 o_ref, s_ref, mt_ref, m_ref, acc_ref, *, t, lambda_init):
    i = pl.program_id(2)
    last = pl.num_programs(2) - 1

    def split_maps(q):
        lane = lax.broadcasted_iota(jnp.int32, q.shape, 1)
        zero = jnp.zeros_like(q)
        return jnp.where(lane < HEAD_DIM, q, zero), jnp.where(lane >= HEAD_DIM, q, zero)

    q_now = split_maps(q_ref[...])

    m_ref[...] = jnp.full(m_ref.shape, MASKED, F32)
    acc_ref[...] = jnp.zeros(acc_ref.shape, F32)

    def scores(j, bias, slot, qm=q_now):
        kt = k_ref[pl.ds(pl.multiple_of(j * t, t), t), :]
        for mp in range(2):
            s = lax.dot_general(kt, qm[mp], (((1,), (1,)), ((), ())), preferred_element_type=F32)
            if bias is not None:
                s = s + bias
            s_ref[slot, mp] = s
            mt_ref[slot, mp] = jnp.max(s, axis=0, keepdims=True)

    def accumulate(j, slot):
        vt = vt_ref[:, pl.ds(pl.multiple_of(j * t, t), t)]
        for mp in range(2):
            m_old = m_ref[mp]
            m_new = jnp.maximum(m_old, mt_ref[slot, mp])
            p = jnp.exp2(s_ref[slot, mp] - m_new).astype(BF16)
            acc_ref[mp] = (jnp.exp2(m_old - m_new) * acc_ref[mp]
                           + jnp.dot(vt, p, preferred_element_type=F32))
            m_ref[mp] = m_new

    def next_diagonal():
        nxt = jnp.minimum(i + 1, last)
        scores(nxt, bias_ref[1], 2, split_maps(qn_ref[...]))

    @pl.when(i == 0)
    def _():
        scores(0, bias_ref[1], 0)

    @pl.when(i >= 1)
    def _():
        scores(i - 1, bias_ref[0], 1)
        accumulate(i, 2)

    n_far = jnp.maximum(i - 1, 0)

    def pair_body(kk, carry):
        j = i - 1 - 2 * kk
        scores(j - 1, None, 0)
        accumulate(j, 1)
        scores(j - 2, None, 1)
        accumulate(j - 1, 0)
        return carry

    lax.fori_loop(0, n_far // 2, pair_body, 0)

    @pl.when(n_far % 2 == 1)
    def _():
        scores(0, None, 0)
        accumulate(1, 1)
        next_diagonal()
        accumulate(0, 0)

    @pl.when(jnp.logical_and(i >= 1, n_far % 2 == 0))
    def _():
        next_diagonal()
        accumulate(0, 1)

    @pl.when(i == 0)
    def _():
        next_diagonal()
        accumulate(0, 0)

    lam = (jnp.exp(jnp.sum(lq1_ref[...] * lk1_ref[...], axis=1, keepdims=True))
           - jnp.exp(jnp.sum(lq2_ref[...] * lk2_ref[...], axis=1, keepdims=True)) + lambda_init)
    o1 = acc_ref[0, 0:D_HEAD_V, :] / acc_ref[0, D_HEAD_V:D_HEAD_V + 1, :]
    o2 = acc_ref[1, 0:D_HEAD_V, :] / acc_ref[1, D_HEAD_V:D_HEAD_V + 1, :]
    ot = o1 - lam * o2
    ot = ot * lax.rsqrt(jnp.mean(ot * ot, axis=0, keepdims=True) + RMS_EPS)
    ot = ot * subg_ref[...] * (1.0 - lambda_init)
    o_ref[...] = ot.T.astype(BF16)


def _attention(q, k, vt, bias, lq1, lk1, lq2, lk2, subln_g, *, t, lambda_init):
    bsz, seq, _ = q.shape
    nq = seq // t
    assert t + 1 >= LOG_BUCKET_STARTS[-1], "keys two tiles back must all fall in the last distance bucket"
    kern = functools.partial(_attn_kernel, t=t, lambda_init=lambda_init)
    vec = lambda a: a.reshape(1, HEAD_DIM)
    return pl.pallas_call(
        kern,
        out_shape=jax.ShapeDtypeStruct((bsz, seq, D_ATTN), BF16),
        grid=(bsz, N_HEADS, seq // t),
        in_specs=[pl.BlockSpec((None, t, D_HEAD_V), lambda b, h, i: (b, i, h)),
                  pl.BlockSpec((None, t, D_HEAD_V), lambda b, h, i: (b, jnp.minimum(i + 1, nq - 1), h)),
                  pl.BlockSpec((None, seq, D_HEAD_V), lambda b, h, i: (b, 0, h)),
                  pl.BlockSpec((None, None, V_ROWS, seq), lambda b, h, i: (b, h, 0, 0)),
                  pl.BlockSpec((None, 2, t, t), lambda b, h, i: (h, 0, 0, 0)),
                  _const_spec((1, HEAD_DIM)), _const_spec((1, HEAD_DIM)),
                  _const_spec((1, HEAD_DIM)), _const_spec((1, HEAD_DIM)),
                  _const_spec((D_HEAD_V, 1))],
        out_specs=pl.BlockSpec((None, t, D_HEAD_V), lambda b, h, i: (b, i, h)),
        scratch_shapes=[pltpu.VMEM((3, 2, t, t), F32),
                        pltpu.VMEM((3, 2, 1, t), F32),
                        pltpu.VMEM((2, 1, t), F32),
                        pltpu.VMEM((2, V_ROWS, t), F32)],
        compiler_params=_cparams(3),
        name="attn",
    )(q, q, k, vt, bias, vec(lq1), vec(lk1), vec(lq2), vec(lk2), subln_g.reshape(D_HEAD_V, 1))


def _first_max(rows):
    best = rows[0]
    for r in rows[1:]:
        best = jnp.maximum(best, r)
    idx = jnp.full(best.shape, len(rows) - 1, jnp.int32)
    for j in range(len(rows) - 2, -1, -1):
        idx = jnp.where(rows[j] == best, j, idx)
    return best, idx


def _post_kernel(o_ref, sga_ref, pc_ref, x_ref, mod_ref, g2_ref, wba_ref, wout_ref, wr_ref, br_ref,
                 tri_ref, x1_ref, h2p_ref, route_ref, cnt_ref, carry_ref):
    @pl.when(pl.program_id(0) == 0)
    def _():
        carry_ref[...] = jnp.zeros(carry_ref.shape, F32)

    ya = jnp.dot(o_ref[...], wba_ref[...], preferred_element_type=F32)
    merged = sga_ref[...].astype(F32) * ya + pc_ref[...].astype(F32)
    z = jnp.dot(merged.astype(BF16), wout_ref[...], preferred_element_type=F32)
    x1 = x_ref[...] + mod_ref[2:3, :] * z
    x1_ref[...] = x1
    y = x1 * lax.rsqrt(jnp.mean(x1 * x1, axis=-1, keepdims=True) + RMS_EPS)
    h2 = y * (g2_ref[...] * (1.0 + mod_ref[4:5, :])) + mod_ref[3:4, :]
    hi = h2.astype(BF16)
    lo = (h2 - hi.astype(F32)).astype(BF16)

    nt = (((1,), (1,)), ((), ()))
    a = lax.dot_general(wr_ref[...], hi, nt, preferred_element_type=F32)
    b = lax.dot_general(wr_ref[0:ROUTER_ROWS, :], lo, nt, preferred_element_type=F32)
    logits = a[0:ROUTER_ROWS] + a[ROUTER_ROWS:2 * ROUTER_ROWS] + b + br_ref[...]

    gl = [logits[g:g + 1, :] for g in range(N_EXPERT_GROUPS)]
    gmax, gidx = _first_max(gl)
    gsum = gl[0] * 0.0
    for r in gl:
        gsum = gsum + jnp.exp(r - gmax)
    g_val = 1.0 / gsum
    es = []
    for r in range(EXPERTS_PER_GROUP):
        sel = jnp.zeros_like(gmax)
        for g in range(N_EXPERT_GROUPS):
            row = EXPERT_ROW0 + g * EXPERTS_PER_GROUP + r
            sel = jnp.where(gidx == g, logits[row:row + 1, :], sel)
        es.append(sel)
    e1, i1 = _first_max(es)
    rest = [jnp.where(i1 == r, -jnp.inf, es[r]) for r in range(EXPERTS_PER_GROUP)]
    e2, i2 = _first_max(rest)
    r21 = jnp.exp(e2 - e1)
    w1 = g_val / (1.0 + r21)
    w2 = g_val * r21 / (1.0 + r21)

    first = i1 < i2
    e_lo = jnp.where(first, i1, i2)
    e_hi = jnp.where(first, i2, i1)
    pair = jnp.zeros_like(e_lo)
    for p in range(1, PAIRS_PER_GROUP):
        pair = jnp.where(jnp.logical_and(e_lo == PAIR_LO[p], e_hi == PAIR_HI[p]), p, pair)
    bucket = gidx * PAIRS_PER_GROUP + pair
    w_lo = jnp.where(first, w1, w2)
    w_hi = jnp.where(first, w2, w1)

    ts = bucket.shape[1]
    brow = lax.broadcasted_iota(jnp.int32, (ROUTER_ROWS, ts), 0)
    onehot = brow == bucket
    prefix = jnp.dot(jnp.where(onehot, 1.0, 0.0).astype(BF16), tri_ref[...],
                     preferred_element_type=F32)
    carry = carry_ref[...]
    rank = jnp.sum(jnp.where(onehot, prefix + carry, 0.0), axis=0, keepdims=True) - 1.0
    carry = carry + prefix[:, ts - 1:ts]
    carry_ref[...] = carry
    cnt_ref[...] = jnp.broadcast_to(carry, cnt_ref.shape).astype(jnp.int32)
    rrow = lax.broadcasted_iota(jnp.int32, (8, ts), 0)
    route_ref[...] = jnp.where(rrow == 0, bucket, jnp.where(rrow == 1, rank.astype(jnp.int32), 0))

    arow = lax.broadcasted_iota(jnp.int32, (LANES, ts), 0)
    aux_t = jnp.where(arow == 0, w_lo, jnp.where(arow == 1, w_hi, 0.0))
    d = h2.shape[1]
    h2p_ref[:, 0:d] = h2
    h2p_ref[:, d:d + LANES] = aux_t.T


def _post(o2, sga, pc, x2, mod3, g2, w_ba, w_out, wg_r, bg_r, we_r, be_r, *, seq, ts):
    n, d = x2.shape
    tiles_per_batch = seq // ts
    wr = jnp.zeros((ROUTER_ROWS, d), F32)
    wr = wr.at[0:N_EXPERT_GROUPS].set(wg_r.T).at[EXPERT_ROW0:EXPERT_ROW0 + N_EXPERTS].set(we_r.T)
    wr_hi = wr.astype(BF16)
    wr_lo = (wr - wr_hi.astype(F32)).astype(BF16)
    br = jnp.zeros((ROUTER_ROWS, 1), F32)
    br = br.at[0:N_EXPERT_GROUPS, 0].set(bg_r).at[EXPERT_ROW0:EXPERT_ROW0 + N_EXPERTS, 0].set(be_r)
    tok = lambda i: (i, 0)
    tidx = jnp.arange(ts)
    tri = (tidx[:, None] <= tidx[None, :]).astype(BF16)
    return pl.pallas_call(
        _post_kernel,
        out_shape=(jax.ShapeDtypeStruct((n, d), F32),
                   jax.ShapeDtypeStruct((n, d + LANES), F32),
                   jax.ShapeDtypeStruct((8, n), jnp.int32),
                   jax.ShapeDtypeStruct((ROUTER_ROWS, LANES), jnp.int32)),
        grid=(n // ts,),
        in_specs=[pl.BlockSpec((ts, D_ATTN), tok),
                  pl.BlockSpec((ts, d), tok),
                  pl.BlockSpec((ts, d), tok),
                  pl.BlockSpec((ts, d), tok),
                  pl.BlockSpec((None, 6, d), lambda i: (i // tiles_per_batch, 0, 0)),
                  _const_spec((1, d)),
                  _const_spec((D_ATTN, d)),
                  _const_spec((d, d)),
                  _const_spec((2 * ROUTER_ROWS, d)),
                  _const_spec((ROUTER_ROWS, 1)),
                  _const_spec((ts, ts))],
        out_specs=(pl.BlockSpec((ts, d), tok), pl.BlockSpec((ts, d + LANES), tok),
                   pl.BlockSpec((8, ts), lambda i: (0, i)),
                   pl.BlockSpec((ROUTER_ROWS, LANES), lambda i: (0, 0))),
        scratch_shapes=[pltpu.VMEM((ROUTER_ROWS, 1), F32)],
        compiler_params=_cparams(1),
        name="post",
    )(o2, sga, pc, x2, mod3, g2.reshape(1, d), w_ba.astype(BF16), w_out.astype(BF16),
      jnp.concatenate([wr_hi, wr_lo], axis=0), br, tri)


def _group_sublane(row):
    return lax.shift_right_logical(row, SUBLANES.bit_length() - 1), row & (SUBLANES - 1)


def _row_copy(src, src_row, dst, dst_row, sem):
    sg, ss = src_row
    dg, ds = dst_row
    return pltpu.make_async_copy(src.at[sg, pl.ds(ss, 1), :], dst.at[dg, pl.ds(ds, 1), :], sem)


def _dispatch_kernel(pos_ref, h_ref, init_ref, hs_ref, sem, *, ts):
    del init_ref
    base = pl.program_id(0) * ts

    def start(g, carry):
        for u in range(SUBLANES):
            p = pos_ref[base + g * SUBLANES + u]
            _row_copy(h_ref, (g, u), hs_ref, _group_sublane(p), sem).start(priority=u % 2)
        return carry

    def wait(g, carry):
        for u in range(SUBLANES):
            _row_copy(h_ref, (0, 0), hs_ref, (0, 0), sem).wait()
        return carry

    lax.fori_loop(0, ts // SUBLANES, start, 0)
    lax.fori_loop(0, ts // SUBLANES, wait, 0)


def _dispatch(pos, h2p, n_rows, *, ts):
    n, w = h2p.shape
    return pl.pallas_call(
        functools.partial(_dispatch_kernel, ts=ts),
        out_shape=jax.ShapeDtypeStruct((n_rows // SUBLANES, SUBLANES, w), h2p.dtype),
        grid_spec=pltpu.PrefetchScalarGridSpec(
            num_scalar_prefetch=1,
            grid=(n // ts,),
            in_specs=[pl.BlockSpec((ts // SUBLANES, SUBLANES, w), lambda i, pos: (i, 0, 0)),
                      pl.BlockSpec(memory_space=pl.ANY)],
            out_specs=pl.BlockSpec(memory_space=pl.ANY),
            scratch_shapes=[pltpu.SemaphoreType.DMA]),
        input_output_aliases={2: 0},
        compiler_params=_cparams(1),
        name="dispatch",
    )(pos, h2p.reshape(n // SUBLANES, SUBLANES, w),
      jnp.zeros((n_rows // SUBLANES, SUBLANES, w), h2p.dtype)).reshape(n_rows, w)


def _experts_kernel(ea_ref, eb_ref, valid_ref, hs_ref, wga_ref, wua_ref, wda_ref, wgb_ref, wub_ref,
                    wdb_ref, ys_ref):
    del ea_ref, eb_ref
    t = pl.program_id(0)
    d = hs_ref.shape[1] - LANES

    @pl.when(valid_ref[t] == 1)
    def _():
        h = hs_ref[:, 0:d].astype(BF16)
        aux = hs_ref[:, d:d + LANES]

        def expert(wg_ref, wu_ref, wd_ref):
            a = jnp.dot(h, wg_ref[...], preferred_element_type=F32)
            b = jnp.dot(h, wu_ref[...], preferred_element_type=F32)
            hid = (a * jax.nn.sigmoid(a)) * b
            return jnp.dot(hid.astype(BF16), wd_ref[...], preferred_element_type=F32)

        ys_ref[...] = (aux[:, 0:1] * expert(wga_ref, wua_ref, wda_ref)
                       + aux[:, 1:2] * expert(wgb_ref, wub_ref, wdb_ref))

    @pl.when(valid_ref[t] == 0)
    def _():
        ys_ref[...] = jnp.zeros(ys_ref.shape, ys_ref.dtype)


def _experts(tile_ea, tile_eb, tile_valid, hs, w_gate, w_up, w_down, *, tm):
    n_rows, w = hs.shape
    _, d, de = w_gate.shape
    wg, wu, wd = w_gate.astype(BF16), w_up.astype(BF16), w_down.astype(BF16)
    ea = lambda t, ea_r, eb_r, v_r: (ea_r[t], 0, 0)
    eb = lambda t, ea_r, eb_r, v_r: (eb_r[t], 0, 0)
    row = lambda t, ea_r, eb_r, v_r: (t, 0)
    return pl.pallas_call(
        _experts_kernel,
        out_shape=jax.ShapeDtypeStruct((n_rows, d), F32),
        grid_spec=pltpu.PrefetchScalarGridSpec(
            num_scalar_prefetch=3,
            grid=(n_rows // tm,),
            in_specs=[pl.BlockSpec((tm, w), row),
                      pl.BlockSpec((None, d, de), ea), pl.BlockSpec((None, d, de), ea),
                      pl.BlockSpec((None, de, d), ea),
                      pl.BlockSpec((None, d, de), eb), pl.BlockSpec((None, d, de), eb),
                      pl.BlockSpec((None, de, d), eb)],
            out_specs=pl.BlockSpec((tm, d), row)),
        compiler_params=_cparams(1),
        name="experts",
    )(tile_ea, tile_eb, tile_valid, hs, wg, wu, wd, wg, wu, wd)


def _combine_kernel(pos_ref, ys_ref, x1_ref, mod_ref, out_ref, ybuf_ref, sem, *, ts):
    i = pl.program_id(0)
    groups = ts // SUBLANES

    def gather(tile, slot):
        base = tile * ts

        def start(g, carry):
            for u in range(SUBLANES):
                p = pos_ref[base + g * SUBLANES + u]
                _row_copy(ys_ref, _group_sublane(p), ybuf_ref.at[slot], (g, u),
                          sem.at[slot]).start(priority=u % 2)
            return carry

        lax.fori_loop(0, groups, start, 0)

    def wait_all(slot):
        def wait(g, carry):
            for u in range(SUBLANES):
                _row_copy(ys_ref, (0, 0), ybuf_ref.at[slot], (0, 0), sem.at[slot]).wait()
            return carry

        lax.fori_loop(0, groups, wait, 0)

    def step(slot):
        @pl.when(i + 1 < pl.num_programs(0))
        def _():
            gather(i + 1, 1 - slot)

        wait_all(slot)
        y = ybuf_ref[slot].reshape(ts, ybuf_ref.shape[-1])
        out_ref[...] = x1_ref[...] + mod_ref[5:6, :] * y

    @pl.when(i == 0)
    def _():
        gather(0, 0)

    @pl.when(i % 2 == 0)
    def _():
        step(0)

    @pl.when(i % 2 == 1)
    def _():
        step(1)


def _combine(pos, ys, x1, mod3, *, seq, ts):
    n, d = x1.shape
    tiles_per_batch = seq // ts
    return pl.pallas_call(
        functools.partial(_combine_kernel, ts=ts),
        out_shape=jax.ShapeDtypeStruct((n, d), F32),
        grid_spec=pltpu.PrefetchScalarGridSpec(
            num_scalar_prefetch=1,
            grid=(n // ts,),
            in_specs=[pl.BlockSpec(memory_space=pl.ANY),
                      pl.BlockSpec((ts, d), lambda i, pos: (i, 0)),
                      pl.BlockSpec((None, 6, d), lambda i, pos: (i // tiles_per_batch, 0, 0))],
            out_specs=pl.BlockSpec((ts, d), lambda i, pos: (i, 0)),
            scratch_shapes=[pltpu.VMEM((2, ts // SUBLANES, SUBLANES, d), F32),
                            pltpu.SemaphoreType.DMA((2,))]),
        compiler_params=_cparams(1),
        name="combine",
    )(pos, ys.reshape(ys.shape[0] // SUBLANES, SUBLANES, d), x1, mod3)


def _route_tables(route, counts, n_tiles, tm):
    bucket, rank = route[0], route[1]
    cnt = counts[:N_ROUTE_BUCKETS, 0]
    padded = (cnt + tm - 1) // tm * tm
    end = jnp.cumsum(padded)
    pos = (end - padded)[bucket] + rank
    tiles_used = end[-1] // tm
    tile = jnp.arange(n_tiles, dtype=jnp.int32)
    valid = tile < tiles_used
    first_row = jnp.minimum(tile, tiles_used - 1) * tm
    tile_bucket = jnp.sum((end[None, :] <= first_row[:, None]).astype(jnp.int32), axis=1)
    tile_bucket = jnp.minimum(tile_bucket, N_ROUTE_BUCKETS - 1)
    group = tile_bucket // PAIRS_PER_GROUP
    pair = tile_bucket % PAIRS_PER_GROUP
    tile_ea = group * EXPERTS_PER_GROUP + jnp.asarray(PAIR_LO, jnp.int32)[pair]
    tile_eb = group * EXPERTS_PER_GROUP + jnp.asarray(PAIR_HI, jnp.int32)[pair]
    return pos.astype(jnp.int32), tile_ea, tile_eb, valid.astype(jnp.int32)


def _tile(seq, pref):
    t = min(pref, seq)
    assert seq % t == 0 and t % LANES == 0, (seq, t)
    return t


def kernel(x, c, rel_bias, ada_w, ada_b, norm1_g, w_in, q_norm_g, k_norm_g, lambda_q1, lambda_k1,
           lambda_q2, lambda_k2, subln_g, w_branch_attn, pool_w, pool_scale, w_branch_pool, w_out,
           norm2_g, router_group_w, router_group_b, router_expert_w, router_expert_b,
           expert_w_gate, expert_w_up, expert_w_down):
    bsz, seq, d = x.shape
    n = bsz * seq
    ts = _tile(seq, TOKEN_TILE)
    t_attn = _tile(seq, ATTN_TILE)
    tm = MOE_TILE
    n_tiles = -(-(n + N_ROUTE_BUCKETS * (tm - 1)) // tm)
    bias = _bias_tiles(rel_bias, t_attn)
    x2 = x.reshape(n, d)
    for l in range(ada_w.shape[0]):
        lambda_init = 0.8 - 0.6 * math.exp(-0.3 * l)
        mod3 = _ada(c, ada_w[l], ada_b[l]).reshape(bsz, 6, d)
        gq = jnp.tile(q_norm_g[l], D_ATTN // HEAD_DIM) * (HEAD_DIM ** -0.5 * LOG2E)
        gk = jnp.tile(k_norm_g[l], D_ATTN // HEAD_DIM)
        q, k, vt, sga, pc = _inproj(x2, mod3, norm1_g[l], w_in[l], gq, gk, pool_w[l], pool_scale[l],
                                    w_branch_pool[l], bsz=bsz, seq=seq, ts=ts)
        o = _attention(q.reshape(bsz, seq, D_ATTN), k.reshape(bsz, seq, D_ATTN), vt, bias,
                       lambda_q1[l], lambda_k1[l], lambda_q2[l], lambda_k2[l], subln_g[l],
                       t=t_attn, lambda_init=lambda_init)
        x1, h2p, route, counts = _post(o.reshape(n, D_ATTN), sga, pc, x2, mod3, norm2_g[l],
                                       w_branch_attn[l], w_out[l], router_group_w[l],
                                       router_group_b[l], router_expert_w[l], router_expert_b[l],
                                       seq=seq, ts=ts)
        pos, tile_ea, tile_eb, tile_valid = _route_tables(route, counts, n_tiles, tm)
        hs = _dispatch(pos, h2p, n_tiles * tm, ts=ts)
        ys = _experts(tile_ea, tile_eb, tile_valid, hs, expert_w_gate[l], expert_w_up[l],
                      expert_w_down[l], tm=tm)
        x2 = _combine(pos, ys, x1, mod3, seq=seq, ts=ts)
    return x2.reshape(bsz, seq, d)
```

```python
import functools
import math

import jax
import jax.numpy as jnp
from jax import lax
from jax.experimental import pallas as pl
from jax.experimental.pallas import tpu as pltpu

F32 = jnp.float32
BF16 = jnp.bfloat16

CHUNK = 64
N_HEADS = 4
HEAD_DIM = 64
D_HEAD_V = 2 * HEAD_DIM
V_ROWS = D_HEAD_V + 16
D_ATTN = N_HEADS * D_HEAD_V
POOL_WINDOWS = (2, 4, 8, 16)
POOL_GROUP_DIM = 128
D_POOL = len(POOL_WINDOWS) * POOL_GROUP_DIM
POOL_HALO = 16
N_BUCKETS = 32
MAX_DISTANCE = 128
N_EXPERT_GROUPS = 4
EXPERTS_PER_GROUP = 4
N_EXPERTS = N_EXPERT_GROUPS * EXPERTS_PER_GROUP
PAIRS_PER_GROUP = 6
PAIR_LO = (0, 0, 1, 1, 0, 2)
PAIR_HI = (1, 2, 2, 3, 3, 3)
N_ROUTE_BUCKETS = N_EXPERT_GROUPS * PAIRS_PER_GROUP
RMS_EPS = 1e-6
LOG2E = math.log2(math.e)
MASKED = -1e30

LANES = 128
SUBLANES = 8
ROUTER_ROWS = 32
EXPERT_ROW0 = 8

VMEM_LIMIT = 56 * 1024 * 1024
TOKEN_TILE = 512
ATTN_TILE = 512
MOE_TILE = 256


def _cparams(n_axes):
    return pltpu.CompilerParams(dimension_semantics=("arbitrary",) * n_axes,
                                vmem_limit_bytes=VMEM_LIMIT)


def _const_spec(shape):
    nd = len(shape)
    return pl.BlockSpec(shape, lambda *_: (0,) * nd, pipeline_mode=pl.Buffered(1))


def _ada_kernel(ct_ref, w_ref, b_ref, o_ref):
    @pl.when(pl.program_id(0) == 0)
    def _():
        o_ref[...] = jnp.broadcast_to(b_ref[...], o_ref.shape)

    ct = ct_ref[...]
    s = ct * jax.nn.sigmoid(ct)
    w = w_ref[...]
    rows = [jnp.sum(w * s[:, b:b + 1], axis=0, keepdims=True) for b in range(ct.shape[1])]
    o_ref[...] += jnp.concatenate(rows, axis=0)


def _ada(c, w, b):
    bsz, d = c.shape
    n = w.shape[1]
    rows = 256
    return pl.pallas_call(
        _ada_kernel,
        out_shape=jax.ShapeDtypeStruct((bsz, n), F32),
        grid=(d // rows,),
        in_specs=[pl.BlockSpec((rows, bsz), lambda j: (j, 0)),
                  pl.BlockSpec((rows, n), lambda j: (j, 0)),
                  pl.BlockSpec((1, n), lambda j: (0, 0))],
        out_specs=pl.BlockSpec((bsz, n), lambda j: (0, 0)),
        compiler_params=_cparams(1),
        name="ada",
    )(c.T, w, b.reshape(1, n))


def _log_bucket_starts():
    nb = N_BUCKETS // 2
    max_exact = nb // 2
    m = nb - max_exact
    ratio = MAX_DISTANCE // max_exact
    starts = []
    for k in range(1, m):
        n = max_exact
        while n ** m < max_exact ** m * ratio ** k:
            n += 1
        starts.append(n)
    return tuple(starts)


LOG_BUCKET_STARTS = _log_bucket_starts()


def _bias_kernel(rb_ref, o_ref, *, t):
    h = pl.program_id(0)
    kind = pl.program_id(1)
    nb = N_BUCKETS // 2
    max_exact = nb // 2
    kpos = lax.broadcasted_iota(jnp.int32, (t, t), 0)
    qpos = lax.broadcasted_iota(jnp.int32, (t, t), 1)
    rel = kpos - qpos - jnp.where(kind == 0, t, 0)
    n = jnp.abs(rel)

    def table(first):
        val = jnp.full((t, t), rb_ref[first + nb - 1, h], F32)
        for k in range(len(LOG_BUCKET_STARTS) - 1, -1, -1):
            val = jnp.where(n < LOG_BUCKET_STARTS[k], rb_ref[first + max_exact + k, h], val)
        for j in range(max_exact - 1, -1, -1):
            val = jnp.where(n == j, rb_ref[first + j, h], val)
        return val

    bias = jnp.where(rel > 0, table(nb), table(0))
    shift = CHUNK.bit_length() - 1
    hidden = jnp.logical_and(kind == 1, (kpos >> shift) > (qpos >> shift))
    o_ref[...] = jnp.where(hidden, MASKED, (bias - rb_ref[nb - 1, h]) * LOG2E)


def _bias_tiles(rel_bias, t):
    return pl.pallas_call(
        functools.partial(_bias_kernel, t=t),
        out_shape=jax.ShapeDtypeStruct((N_HEADS, 2, t, t), F32),
        grid=(N_HEADS, 2),
        in_specs=[pl.BlockSpec(memory_space=pltpu.SMEM)],
        out_specs=pl.BlockSpec((None, None, t, t), lambda h, j: (h, j, 0, 0)),
        compiler_params=_cparams(2),
        name="bias_tiles",
    )(rel_bias)


def _group_rms(xc, ones_blockdiag, gain):
    ssq = jnp.dot((xc * xc).astype(BF16), ones_blockdiag, preferred_element_type=F32)
    return xc * lax.rsqrt(ssq * (1.0 / HEAD_DIM) + RMS_EPS) * gain


def _inproj_kernel(x_ref, mod_ref, g1_ref, win_ref, gq_ref, gk_ref, ones_ref, poolw_ref,
                   pscale_ref, wbp_ref, q_ref, k_ref, vt_ref, sga_ref, pc_ref, ext_ref, v_ref,
                   *, ts, tiles_per_batch):
    tb = pl.program_id(0) % tiles_per_batch
    x = x_ref[...]
    y = x * lax.rsqrt(jnp.mean(x * x, axis=-1, keepdims=True) + RMS_EPS)
    h = y * (g1_ref[...] * (1.0 + mod_ref[1:2, :])) + mod_ref[0:1, :]
    hb = h.astype(BF16)

    def proj(c0, c1):
        return jnp.dot(hb, win_ref[:, c0:c1], preferred_element_type=F32)

    ones_bd = ones_ref[...]
    q_ref[...] = _group_rms(proj(0, D_ATTN), ones_bd, gq_ref[...]).astype(BF16)
    k_ref[...] = _group_rms(proj(D_ATTN, 2 * D_ATTN), ones_bd, gk_ref[...]).astype(BF16)
    v_ref[...] = proj(2 * D_ATTN, 3 * D_ATTN)
    vt = v_ref[...].T.astype(BF16)
    for hd in range(N_HEADS):
        vt_ref[hd, 0:D_HEAD_V, :] = vt[hd * D_HEAD_V:(hd + 1) * D_HEAD_V, :]
        vt_ref[hd, D_HEAD_V:V_ROWS, :] = jnp.ones((V_ROWS - D_HEAD_V, ts), BF16)
    c_u = 3 * D_ATTN
    c_ga = c_u + D_POOL
    c_gp = c_ga + x.shape[1]
    sga_ref[...] = jax.nn.sigmoid(proj(c_ga, c_gp)).astype(BF16)

    u = proj(c_u, c_ga)

    @pl.when(tb == 0)
    def _():
        ext_ref[0:POOL_HALO, :] = jnp.zeros((POOL_HALO, D_POOL), F32)

    ext_ref[POOL_HALO:POOL_HALO + ts, :] = u
    row = lax.broadcasted_iota(jnp.int32, (ts, 1), 0) + tb * ts
    ys = []
    for g, w in enumerate(POOL_WINDOWS):
        c0 = g * POOL_GROUP_DIM
        ug = u[:, c0:c0 + POOL_GROUP_DIM]
        acc = ug
        for d in range(1, w):
            acc = acc + ext_ref[POOL_HALO - d:POOL_HALO - d + ts, c0:c0 + POOL_GROUP_DIM]
        cnt = jnp.minimum(row + 1, w).astype(F32)
        m = acc / cnt - ug
        ys.append(jnp.dot(m.astype(BF16), poolw_ref[g], preferred_element_type=F32))
    yb = jnp.concatenate(ys, axis=1) * pscale_ref[...]
    ypool = jnp.dot(yb.astype(BF16), wbp_ref[...], preferred_element_type=F32)
    pc_ref[...] = (jax.nn.sigmoid(proj(c_gp, c_gp + x.shape[1])) * ypool).astype(BF16)
    ext_ref[0:POOL_HALO, :] = u[ts - POOL_HALO:ts, :]


def _inproj(x2, mod3, g1, w_in, gq, gk, pool_w, pool_scale, w_bp, *, bsz, seq, ts):
    n, d = x2.shape
    d_in = w_in.shape[1]
    tiles_per_batch = seq // ts
    win_b = w_in.astype(BF16)
    idx = jnp.arange(D_ATTN) // HEAD_DIM
    ones_bd = (idx[:, None] == idx[None, :]).astype(BF16)
    kern = functools.partial(_inproj_kernel, ts=ts, tiles_per_batch=tiles_per_batch)
    tok = lambda i: (i, 0)
    out_shape = (jax.ShapeDtypeStruct((n, D_ATTN), BF16),
                 jax.ShapeDtypeStruct((n, D_ATTN), BF16),
                 jax.ShapeDtypeStruct((bsz, N_HEADS, V_ROWS, seq), BF16),
                 jax.ShapeDtypeStruct((n, d), BF16),
                 jax.ShapeDtypeStruct((n, d), BF16))
    return pl.pallas_call(
        kern,
        out_shape=out_shape,
        grid=(n // ts,),
        in_specs=[pl.BlockSpec((ts, d), tok),
                  pl.BlockSpec((None, 6, d), lambda i: (i // tiles_per_batch, 0, 0)),
                  _const_spec((1, d)),
                  _const_spec((d, d_in)),
                  _const_spec((1, D_ATTN)),
                  _const_spec((1, D_ATTN)),
                  _const_spec((D_ATTN, D_ATTN)),
                  _const_spec(pool_w.shape),
                  _const_spec((1, D_POOL)),
                  _const_spec((D_POOL, d))],
        out_specs=(pl.BlockSpec((ts, D_ATTN), tok),
                   pl.BlockSpec((ts, D_ATTN), tok),
                   pl.BlockSpec((None, N_HEADS, V_ROWS, ts),
                                lambda i: (i // tiles_per_batch, 0, 0, i % tiles_per_batch)),
                   pl.BlockSpec((ts, d), tok),
                   pl.BlockSpec((ts, d), tok)),
        scratch_shapes=[pltpu.VMEM((POOL_HALO + ts, D_POOL), F32), pltpu.VMEM((ts, D_ATTN), F32)],
        compiler_params=_cparams(1),
        name="inproj",
    )(x2, mod3, g1.reshape(1, d), win_b, gq.reshape(1, D_ATTN), gk.reshape(1, D_ATTN), ones_bd,
      pool_w.astype(BF16), pool_scale.reshape(1, D_POOL), w_bp.astype(BF16))


def _attn_kernel(q_ref, qn_ref, k_ref, vt_ref, bias_ref, lq1_ref, lk1_ref, lq2_ref, lk2_ref, subg_ref,
                 o_ref, s_ref, mt_ref, m_ref, acc_ref, *, t, lambda_init):
    i = pl.program_id(2)
    last = pl.num_programs(2) - 1

    def split_maps(q):
        lane = lax.broadcasted_iota(jnp.int32, q.shape, 1)
        zero = jnp.zeros_like(q)
        return jnp.where(lane < HEAD_DIM, q, zero), jnp.where(lane >= HEAD_DIM, q, zero)

    q_now = split_maps(q_ref[...])

    m_ref[...] = jnp.full(m_ref.shape, MASKED, F32)
    acc_ref[...] = jnp.zeros(acc_ref.shape, F32)

    def scores(j, bias, slot, qm=q_now):
        kt = k_ref[pl.ds(pl.multiple_of(j * t, t), t), :]
        for mp in range(2):
            s = lax.dot_general(kt, qm[mp], (((1,), (1,)), ((), ())), preferred_element_type=F32)
            if bias is not None:
                s = s + bias
            s_ref[slot, mp] = s
            mt_ref[slot, mp] = jnp.max(s, axis=0, keepdims=True)

    def accumulate(j, slot):
        vt = vt_ref[:, pl.ds(pl.multiple_of(j * t, t), t)]
        for mp in range(2):
            m_old = m_ref[mp]
            m_new = jnp.maximum(m_old, mt_ref[slot, mp])
            p = jnp.exp2(s_ref[slot, mp] - m_new).astype(BF16)
            acc_ref[mp] = (jnp.exp2(m_old - m_new) * acc_ref[mp]
                           + jnp.dot(vt, p, preferred_element_type=F32))
            m_ref[mp] = m_new

    def next_diagonal():
        nxt = jnp.minimum(i + 1, last)
        scores(nxt, bias_ref[1], 2, split_maps(qn_ref[...]))

    @pl.when(i == 0)
    def _():
        scores(0, bias_ref[1], 0)

    @pl.when(i >= 1)
    def _():
        scores(i - 1, bias_ref[0], 1)
        accumulate(i, 2)

    n_far = jnp.maximum(i - 1, 0)

    def pair_body(kk, carry):
        j = i - 1 - 2 * kk
        scores(j - 1, None, 0)
        accumulate(j, 1)
        scores(j - 2, None, 1)
        accumulate(j - 1, 0)
        return carry

    lax.fori_loop(0, n_far // 2, pair_body, 0)

    @pl.when(n_far % 2 == 1)
    def _():
        scores(0, None, 0)
        accumulate(1, 1)
        next_diagonal()
        accumulate(0, 0)

    @pl.when(jnp.logical_and(i >= 1, n_far % 2 == 0))
    def _():
        next_diagonal()
        accumulate(0, 1)

    @pl.when(i == 0)
    def _():
        next_diagonal()
        accumulate(0, 0)

    lam = (jnp.exp(jnp.sum(lq1_ref[...] * lk1_ref[...], axis=1, keepdims=True))
           - jnp.exp(jnp.sum(lq2_ref[...] * lk2_ref[...], axis=1, keepdims=True)) + lambda_init)
    o1 = acc_ref[0, 0:D_HEAD_V, :] / acc_ref[0, D_HEAD_V:D_HEAD_V + 1, :]
    o2 = acc_ref[1, 0:D_HEAD_V, :] / acc_ref[1, D_HEAD_V:D_HEAD_V + 1, :]
    ot = o1 - lam * o2
    ot = ot * lax.rsqrt(jnp.mean(ot * ot, axis=0, keepdims=True) + RMS_EPS)
    ot = ot * subg_ref[...] * (1.0 - lambda_init)
    o_ref[...] = ot.T.astype(BF16)


def _attention(q, k, vt, bias, lq1, lk1, lq2, lk2, subln_g, *, t, lambda_init):
    bsz, seq, _ = q.shape
    nq = seq // t
    assert t + 1 >= LOG_BUCKET_STARTS[-1], "keys two tiles back must all fall in the last distance bucket"
    kern = functools.partial(_attn_kernel, t=t, lambda_init=lambda_init)
    vec = lambda a: a.reshape(1, HEAD_DIM)
    return pl.pallas_call(
        kern,
        out_shape=jax.ShapeDtypeStruct((bsz, seq, D_ATTN), BF16),
        grid=(bsz, N_HEADS, seq // t),
        in_specs=[pl.BlockSpec((None, t, D_HEAD_V), lambda b, h, i: (b, i, h)),
                  pl.BlockSpec((None, t, D_HEAD_V), lambda b, h, i: (b, jnp.minimum(i + 1, nq - 1), h)),
                  pl.BlockSpec((None, seq, D_HEAD_V), lambda b, h, i: (b, 0, h)),
                  pl.BlockSpec((None, None, V_ROWS, seq), lambda b, h, i: (b, h, 0, 0)),
                  pl.BlockSpec((None, 2, t, t), lambda b, h, i: (h, 0, 0, 0)),
                  _const_spec((1, HEAD_DIM)), _const_spec((1, HEAD_DIM)),
                  _const_spec((1, HEAD_DIM)), _const_spec((1, HEAD_DIM)),
                  _const_spec((D_HEAD_V, 1))],
        out_specs=pl.BlockSpec((None, t, D_HEAD_V), lambda b, h, i: (b, i, h)),
        scratch_shapes=[pltpu.VMEM((3, 2, t, t), F32),
                        pltpu.VMEM((3, 2, 1, t), F32),
                        pltpu.VMEM((2, 1, t), F32),
                        pltpu.VMEM((2, V_ROWS, t), F32)],
        compiler_params=_cparams(3),
        name="attn",
    )(q, q, k, vt, bias, vec(lq1), vec(lk1), vec(lq2), vec(lk2), subln_g.reshape(D_HEAD_V, 1))


def _first_max(rows):
    best = rows[0]
    for r in rows[1:]:
        best = jnp.maximum(best, r)
    idx = jnp.full(best.shape, len(rows) - 1, jnp.int32)
    for j in range(len(rows) - 2, -1, -1):
        idx = jnp.where(rows[j] == best, j, idx)
    return best, idx


def _post_kernel(o_ref, sga_ref, pc_ref, x_ref, mod_ref, g2_ref, wba_ref, wout_ref, wr_ref, br_ref,
                 tri_ref, x1_ref, h2p_ref, route_ref, cnt_ref, carry_ref):
    @pl.when(pl.program_id(0) == 0)
    def _():
        carry_ref[...] = jnp.zeros(carry_ref.shape, F32)

    ya = jnp.dot(o_ref[...], wba_ref[...], preferred_element_type=F32)
    merged = sga_ref[...].astype(F32) * ya + pc_ref[...].astype(F32)
    z = jnp.dot(merged.astype(BF16), wout_ref[...], preferred_element_type=F32)
    x1 = x_ref[...] + mod_ref[2:3, :] * z
    x1_ref[...] = x1
    y = x1 * lax.rsqrt(jnp.mean(x1 * x1, axis=-1, keepdims=True) + RMS_EPS)
    h2 = y * (g2_ref[...] * (1.0 + mod_ref[4:5, :])) + mod_ref[3:4, :]
    hi = h2.astype(BF16)
    lo = (h2 - hi.astype(F32)).astype(BF16)

    nt = (((1,), (1,)), ((), ()))
    a = lax.dot_general(wr_ref[...], hi, nt, preferred_element_type=F32)
    b = lax.dot_general(wr_ref[0:ROUTER_ROWS, :], lo, nt, preferred_element_type=F32)
    logits = a[0:ROUTER_ROWS] + a[ROUTER_ROWS:2 * ROUTER_ROWS] + b + br_ref[...]

    gl = [logits[g:g + 1, :] for g in range(N_EXPERT_GROUPS)]
    gmax, gidx = _first_max(gl)
    gsum = gl[0] * 0.0
    for r in gl:
        gsum = gsum + jnp.exp(r - gmax)
    g_val = 1.0 / gsum
    es = []
    for r in range(EXPERTS_PER_GROUP):
        sel = jnp.zeros_like(gmax)
        for g in range(N_EXPERT_GROUPS):
            row = EXPERT_ROW0 + g * EXPERTS_PER_GROUP + r
            sel = jnp.where(gidx == g, logits[row:row + 1, :], sel)
        es.append(sel)
    e1, i1 = _first_max(es)
    rest = [jnp.where(i1 == r, -jnp.inf, es[r]) for r in range(EXPERTS_PER_GROUP)]
    e2, i2 = _first_max(rest)
    r21 = jnp.exp(e2 - e1)
    w1 = g_val / (1.0 + r21)
    w2 = g_val * r21 / (1.0 + r21)

    first = i1 < i2
    e_lo = jnp.where(first, i1, i2)
    e_hi = jnp.where(first, i2, i1)
    pair = jnp.zeros_like(e_lo)
    for p in range(1, PAIRS_PER_GROUP):
        pair = jnp.where(jnp.logical_and(e_lo == PAIR_LO[p], e_hi == PAIR_HI[p]), p, pair)
    bucket = gidx * PAIRS_PER_GROUP + pair
    w_lo = jnp.where(first, w1, w2)
    w_hi = jnp.where(first, w2, w1)

    ts = bucket.shape[1]
    brow = lax.broadcasted_iota(jnp.int32, (ROUTER_ROWS, ts), 0)
    onehot = brow == bucket
    prefix = jnp.dot(jnp.where(onehot, 1.0, 0.0).astype(BF16), tri_ref[...],
                     preferred_element_type=F32)
    carry = carry_ref[...]
    rank = jnp.sum(jnp.where(onehot, prefix + carry, 0.0), axis=0, keepdims=True) - 1.0
    carry = carry + prefix[:, ts - 1:ts]
    carry_ref[...] = carry
    cnt_ref[...] = jnp.broadcast_to(carry, cnt_ref.shape).astype(jnp.int32)
    rrow = lax.broadcasted_iota(jnp.int32, (8, ts), 0)
    route_ref[...] = jnp.where(rrow == 0, bucket, jnp.where(rrow == 1, rank.astype(jnp.int32), 0))

    arow = lax.broadcasted_iota(jnp.int32, (LANES, ts), 0)
    aux_t = jnp.where(arow == 0, w_lo, jnp.where(arow == 1, w_hi, 0.0))
    d = h2.shape[1]
    h2p_ref[:, 0:d] = h2
    h2p_ref[:, d:d + LANES] = aux_t.T


def _post(o2, sga, pc, x2, mod3, g2, w_ba, w_out, wg_r, bg_r, we_r, be_r, *, seq, ts):
    n, d = x2.shape
    tiles_per_batch = seq // ts
    wr = jnp.zeros((ROUTER_ROWS, d), F32)
    wr = wr.at[0:N_EXPERT_GROUPS].set(wg_r.T).at[EXPERT_ROW0:EXPERT_ROW0 + N_EXPERTS].set(we_r.T)
    wr_hi = wr.astype(BF16)
    wr_lo = (wr - wr_hi.astype(F32)).astype(BF16)
    br = jnp.zeros((ROUTER_ROWS, 1), F32)
    br = br.at[0:N_EXPERT_GROUPS, 0].set(bg_r).at[EXPERT_ROW0:EXPERT_ROW0 + N_EXPERTS, 0].set(be_r)
    tok = lambda i: (i, 0)
    tidx = jnp.arange(ts)
    tri = (tidx[:, None] <= tidx[None, :]).astype(BF16)
    return pl.pallas_call(
        _post_kernel,
        out_shape=(jax.ShapeDtypeStruct((n, d), F32),
                   jax.ShapeDtypeStruct((n, d + LANES), F32),
                   jax.ShapeDtypeStruct((8, n), jnp.int32),
                   jax.ShapeDtypeStruct((ROUTER_ROWS, LANES), jnp.int32)),
        grid=(n // ts,),
        in_specs=[pl.BlockSpec((ts, D_ATTN), tok),
                  pl.BlockSpec((ts, d), tok),
                  pl.BlockSpec((ts, d), tok),
                  pl.BlockSpec((ts, d), tok),
                  pl.BlockSpec((None, 6, d), lambda i: (i // tiles_per_batch, 0, 0)),
                  _const_spec((1, d)),
                  _const_spec((D_ATTN, d)),
                  _const_spec((d, d)),
                  _const_spec((2 * ROUTER_ROWS, d)),
                  _const_spec((ROUTER_ROWS, 1)),
                  _const_spec((ts, ts))],
        out_specs=(pl.BlockSpec((ts, d), tok), pl.BlockSpec((ts, d + LANES), tok),
                   pl.BlockSpec((8, ts), lambda i: (0, i)),
                   pl.BlockSpec((ROUTER_ROWS, LANES), lambda i: (0, 0))),
        scratch_shapes=[pltpu.VMEM((ROUTER_ROWS, 1), F32)],
        compiler_params=_cparams(1),
        name="post",
    )(o2, sga, pc, x2, mod3, g2.reshape(1, d), w_ba.astype(BF16), w_out.astype(BF16),
      jnp.concatenate([wr_hi, wr_lo], axis=0), br, tri)


def _group_sublane(row):
    return lax.shift_right_logical(row, SUBLANES.bit_length() - 1), row & (SUBLANES - 1)


def _row_copy(src, src_row, dst, dst_row, sem):
    sg, ss = src_row
    dg, ds = dst_row
    return pltpu.make_async_copy(src.at[sg, pl.ds(ss, 1), :], dst.at[dg, pl.ds(ds, 1), :], sem)


def _dispatch_kernel(pos_ref, h_ref, init_ref, hs_ref, sem, *, ts):
    del init_ref
    base = pl.program_id(0) * ts

    def start(g, carry):
        for u in range(SUBLANES):
            p = pos_ref[base + g * SUBLANES + u]
            _row_copy(h_ref, (g, u), hs_ref, _group_sublane(p), sem).start(priority=u % 2)
        return carry

    def wait(g, carry):
        for u in range(SUBLANES):
            _row_copy(h_ref, (0, 0), hs_ref, (0, 0), sem).wait()
        return carry

    lax.fori_loop(0, ts // SUBLANES, start, 0)
    lax.fori_loop(0, ts // SUBLANES, wait, 0)


def _dispatch(pos, h2p, n_rows, *, ts):
    n, w = h2p.shape
    return pl.pallas_call(
        functools.partial(_dispatch_kernel, ts=ts),
        out_shape=jax.ShapeDtypeStruct((n_rows // SUBLANES, SUBLANES, w), h2p.dtype),
        grid_spec=pltpu.PrefetchScalarGridSpec(
            num_scalar_prefetch=1,
            grid=(n // ts,),
            in_specs=[pl.BlockSpec((ts // SUBLANES, SUBLANES, w), lambda i, pos: (i, 0, 0)),
                      pl.BlockSpec(memory_space=pl.ANY)],
            out_specs=pl.BlockSpec(memory_space=pl.ANY),
            scratch_shapes=[pltpu.SemaphoreType.DMA]),
        input_output_aliases={2: 0},
        compiler_params=_cparams(1),
        name="dispatch",
    )(pos, h2p.reshape(n // SUBLANES, SUBLANES, w),
      jnp.zeros((n_rows // SUBLANES, SUBLANES, w), h2p.dtype)).reshape(n_rows, w)


def _experts_kernel(ea_ref, eb_ref, valid_ref, hs_ref, wga_ref, wua_ref, wda_ref, wgb_ref, wub_ref,
                    wdb_ref, ys_ref):
    del ea_ref, eb_ref
    t = pl.program_id(0)
    d = hs_ref.shape[1] - LANES

    @pl.when(valid_ref[t] == 1)
    def _():
        h = hs_ref[:, 0:d].astype(BF16)
        aux = hs_ref[:, d:d + LANES]

        def expert(wg_ref, wu_ref, wd_ref):
            a = jnp.dot(h, wg_ref[...].astype(BF16), preferred_element_type=F32)
            b = jnp.dot(h, wu_ref[...].astype(BF16), preferred_element_type=F32)
            hid = (a * jax.nn.sigmoid(a)) * b
            return jnp.dot(hid.astype(BF16), wd_ref[...].astype(BF16), preferred_element_type=F32)

        ys_ref[...] = (aux[:, 0:1] * expert(wga_ref, wua_ref, wda_ref)
                       + aux[:, 1:2] * expert(wgb_ref, wub_ref, wdb_ref))

    @pl.when(valid_ref[t] == 0)
    def _():
        ys_ref[...] = jnp.zeros(ys_ref.shape, ys_ref.dtype)


def _experts(tile_ea, tile_eb, tile_valid, hs, w_gate, w_up, w_down, *, tm):
    n_rows, w = hs.shape
    _, d, de = w_gate.shape
    wg, wu, wd = w_gate, w_up, w_down
    ea = lambda t, ea_r, eb_r, v_r: (ea_r[t], 0, 0)
    eb = lambda t, ea_r, eb_r, v_r: (eb_r[t], 0, 0)
    row = lambda t, ea_r, eb_r, v_r: (t, 0)
    return pl.pallas_call(
        _experts_kernel,
        out_shape=jax.ShapeDtypeStruct((n_rows, d), F32),
        grid_spec=pltpu.PrefetchScalarGridSpec(
            num_scalar_prefetch=3,
            grid=(n_rows // tm,),
            in_specs=[pl.BlockSpec((tm, w), row),
                      pl.BlockSpec((None, d, de), ea), pl.BlockSpec((None, d, de), ea),
                      pl.BlockSpec((None, de, d), ea),
                      pl.BlockSpec((None, d, de), eb), pl.BlockSpec((None, d, de), eb),
                      pl.BlockSpec((None, de, d), eb)],
            out_specs=pl.BlockSpec((tm, d), row)),
        compiler_params=_cparams(1),
        name="experts",
    )(tile_ea, tile_eb, tile_valid, hs, wg, wu, wd, wg, wu, wd)


def _combine_kernel(pos_ref, ys_ref, x1_ref, mod_ref, out_ref, ybuf_ref, sem, *, ts):
    i = pl.program_id(0)
    groups = ts // SUBLANES

    def gather(tile, slot):
        base = tile * ts

        def start(g, carry):
            for u in range(SUBLANES):
                p = pos_ref[base + g * SUBLANES + u]
                _row_copy(ys_ref, _group_sublane(p), ybuf_ref.at[slot], (g, u),
                          sem.at[slot]).start(priority=u % 2)
            return carry

        lax.fori_loop(0, groups, start, 0)

    def wait_all(slot):
        def wait(g, carry):
            for u in range(SUBLANES):
                _row_copy(ys_ref, (0, 0), ybuf_ref.at[slot], (0, 0), sem.at[slot]).wait()
            return carry

        lax.fori_loop(0, groups, wait, 0)

    def step(slot):
        @pl.when(i + 1 < pl.num_programs(0))
        def _():
            gather(i + 1, 1 - slot)

        wait_all(slot)
        y = ybuf_ref[slot].reshape(ts, ybuf_ref.shape[-1])
        out_ref[...] = x1_ref[...] + mod_ref[5:6, :] * y

    @pl.when(i == 0)
    def _():
        gather(0, 0)

    @pl.when(i % 2 == 0)
    def _():
        step(0)

    @pl.when(i % 2 == 1)
    def _():
        step(1)


def _combine(pos, ys, x1, mod3, *, seq, ts):
    n, d = x1.shape
    tiles_per_batch = seq // ts
    return pl.pallas_call(
        functools.partial(_combine_kernel, ts=ts),
        out_shape=jax.ShapeDtypeStruct((n, d), F32),
        grid_spec=pltpu.PrefetchScalarGridSpec(
            num_scalar_prefetch=1,
            grid=(n // ts,),
            in_specs=[pl.BlockSpec(memory_space=pl.ANY),
                      pl.BlockSpec((ts, d), lambda i, pos: (i, 0)),
                      pl.BlockSpec((None, 6, d), lambda i, pos: (i // tiles_per_batch, 0, 0))],
            out_specs=pl.BlockSpec((ts, d), lambda i, pos: (i, 0)),
            scratch_shapes=[pltpu.VMEM((2, ts // SUBLANES, SUBLANES, d), F32),
                            pltpu.SemaphoreType.DMA((2,))]),
        compiler_params=_cparams(1),
        name="combine",
    )(pos, ys.reshape(ys.shape[0] // SUBLANES, SUBLANES, d), x1, mod3)


def _route_tables(route, counts, n_tiles, tm):
    bucket, rank = route[0], route[1]
    cnt = counts[:N_ROUTE_BUCKETS, 0]
    padded = (cnt + tm - 1) // tm * tm
    end = jnp.cumsum(padded)
    pos = (end - padded)[bucket] + rank
    tiles_used = end[-1] // tm
    tile = jnp.arange(n_tiles, dtype=jnp.int32)
    valid = tile < tiles_used
    first_row = jnp.minimum(tile, tiles_used - 1) * tm
    tile_bucket = jnp.sum((end[None, :] <= first_row[:, None]).astype(jnp.int32), axis=1)
    tile_bucket = jnp.minimum(tile_bucket, N_ROUTE_BUCKETS - 1)
    group = tile_bucket // PAIRS_PER_GROUP
    pair = tile_bucket % PAIRS_PER_GROUP
    tile_ea = group * EXPERTS_PER_GROUP + jnp.asarray(PAIR_LO, jnp.int32)[pair]
    tile_eb = group * EXPERTS_PER_GROUP + jnp.asarray(PAIR_HI, jnp.int32)[pair]
    return pos.astype(jnp.int32), tile_ea, tile_eb, valid.astype(jnp.int32)


def _tile(seq, pref):
    t = min(pref, seq)
    assert seq % t == 0 and t % LANES == 0, (seq, t)
    return t


def kernel(x, c, rel_bias, ada_w, ada_b, norm1_g, w_in, q_norm_g, k_norm_g, lambda_q1, lambda_k1,
           lambda_q2, lambda_k2, subln_g, w_branch_attn, pool_w, pool_scale, w_branch_pool, w_out,
           norm2_g, router_group_w, router_group_b, router_expert_w, router_expert_b,
           expert_w_gate, expert_w_up, expert_w_down):
    bsz, seq, d = x.shape
    n = bsz * seq
    ts = _tile(seq, TOKEN_TILE)
    t_attn = _tile(seq, ATTN_TILE)
    tm = MOE_TILE
    n_tiles = -(-(n + N_ROUTE_BUCKETS * (tm - 1)) // tm)
    bias = _bias_tiles(rel_bias, t_attn)
    x2 = x.reshape(n, d)
    for l in range(ada_w.shape[0]):
        lambda_init = 0.8 - 0.6 * math.exp(-0.3 * l)
        mod3 = _ada(c, ada_w[l], ada_b[l]).reshape(bsz, 6, d)
        gq = jnp.tile(q_norm_g[l], D_ATTN // HEAD_DIM) * (HEAD_DIM ** -0.5 * LOG2E)
        gk = jnp.tile(k_norm_g[l], D_ATTN // HEAD_DIM)
        q, k, vt, sga, pc = _inproj(x2, mod3, norm1_g[l], w_in[l], gq, gk, pool_w[l], pool_scale[l],
                                    w_branch_pool[l], bsz=bsz, seq=seq, ts=ts)
        o = _attention(q.reshape(bsz, seq, D_ATTN), k.reshape(bsz, seq, D_ATTN), vt, bias,
                       lambda_q1[l], lambda_k1[l], lambda_q2[l], lambda_k2[l], subln_g[l],
                       t=t_attn, lambda_init=lambda_init)
        x1, h2p, route, counts = _post(o.reshape(n, D_ATTN), sga, pc, x2, mod3, norm2_g[l],
                                       w_branch_attn[l], w_out[l], router_group_w[l],
                                       router_group_b[l], router_expert_w[l], router_expert_b[l],
                                       seq=seq, ts=ts)
        pos, tile_ea, tile_eb, tile_valid = _route_tables(route, counts, n_tiles, tm)
        hs = _dispatch(pos, h2p, n_tiles * tm, ts=ts)
        ys = _experts(tile_ea, tile_eb, tile_valid, hs, expert_w_gate[l], expert_w_up[l],
                      expert_w_down[l], tm=tm)
        x2 = _combine(pos, ys, x1, mod3, seq=seq, ts=ts)
    return x2.reshape(bsz, seq, d)
```

```python
import functools
import math

import jax
import jax.numpy as jnp
from jax import lax
from jax.experimental import pallas as pl
from jax.experimental.pallas import tpu as pltpu

F32 = jnp.float32
BF16 = jnp.bfloat16

CHUNK = 64
N_HEADS = 4
HEAD_DIM = 64
D_HEAD_V = 2 * HEAD_DIM
V_ROWS = D_HEAD_V + 16
D_ATTN = N_HEADS * D_HEAD_V
POOL_WINDOWS = (2, 4, 8, 16)
POOL_GROUP_DIM = 128
D_POOL = len(POOL_WINDOWS) * POOL_GROUP_DIM
POOL_HALO = 16
N_BUCKETS = 32
MAX_DISTANCE = 128
N_EXPERT_GROUPS = 4
EXPERTS_PER_GROUP = 4
N_EXPERTS = N_EXPERT_GROUPS * EXPERTS_PER_GROUP
PAIRS_PER_GROUP = 6
PAIR_LO = (0, 0, 1, 1, 0, 2)
PAIR_HI = (1, 2, 2, 3, 3, 3)
N_ROUTE_BUCKETS = N_EXPERT_GROUPS * PAIRS_PER_GROUP
RMS_EPS = 1e-6
LOG2E = math.log2(math.e)
MASKED = -1e30

LANES = 128
SUBLANES = 8
ROUTER_ROWS = 32
EXPERT_ROW0 = 8

VMEM_LIMIT = 56 * 1024 * 1024
TOKEN_TILE = 512
ATTN_TILE = 512
MOE_TILE = 256


def _cparams(n_axes):
    return pltpu.CompilerParams(dimension_semantics=("arbitrary",) * n_axes,
                                vmem_limit_bytes=VMEM_LIMIT)


def _const_spec(shape):
    nd = len(shape)
    return pl.BlockSpec(shape, lambda *_: (0,) * nd, pipeline_mode=pl.Buffered(1))


def _ada_kernel(ct_ref, w_ref, b_ref, o_ref):
    @pl.when(pl.program_id(0) == 0)
    def _():
        o_ref[...] = jnp.broadcast_to(b_ref[...], o_ref.shape)

    ct = ct_ref[...]
    s = ct * jax.nn.sigmoid(ct)
    w = w_ref[...]
    rows = [jnp.sum(w * s[:, b:b + 1], axis=0, keepdims=True) for b in range(ct.shape[1])]
    o_ref[...] += jnp.concatenate(rows, axis=0)


def _ada(c, w, b):
    bsz, d = c.shape
    n = w.shape[1]
    rows = 256
    return pl.pallas_call(
        _ada_kernel,
        out_shape=jax.ShapeDtypeStruct((bsz, n), F32),
        grid=(d // rows,),
        in_specs=[pl.BlockSpec((rows, bsz), lambda j: (j, 0)),
                  pl.BlockSpec((rows, n), lambda j: (j, 0)),
                  pl.BlockSpec((1, n), lambda j: (0, 0))],
        out_specs=pl.BlockSpec((bsz, n), lambda j: (0, 0)),
        compiler_params=_cparams(1),
        name="ada",
    )(c.T, w, b.reshape(1, n))


def _log_bucket_starts():
    nb = N_BUCKETS // 2
    max_exact = nb // 2
    m = nb - max_exact
    ratio = MAX_DISTANCE // max_exact
    starts = []
    for k in range(1, m):
        n = max_exact
        while n ** m < max_exact ** m * ratio ** k:
            n += 1
        starts.append(n)
    return tuple(starts)


LOG_BUCKET_STARTS = _log_bucket_starts()


def _bias_kernel(rb_ref, o_ref, *, t):
    h = pl.program_id(0)
    kind = pl.program_id(1)
    nb = N_BUCKETS // 2
    max_exact = nb // 2
    kpos = lax.broadcasted_iota(jnp.int32, (t, t), 0)
    qpos = lax.broadcasted_iota(jnp.int32, (t, t), 1)
    rel = kpos - qpos - jnp.where(kind == 0, t, 0)
    n = jnp.abs(rel)

    def table(first):
        val = jnp.full((t, t), rb_ref[first + nb - 1, h], F32)
        for k in range(len(LOG_BUCKET_STARTS) - 1, -1, -1):
            val = jnp.where(n < LOG_BUCKET_STARTS[k], rb_ref[first + max_exact + k, h], val)
        for j in range(max_exact - 1, -1, -1):
            val = jnp.where(n == j, rb_ref[first + j, h], val)
        return val

    bias = jnp.where(rel > 0, table(nb), table(0))
    shift = CHUNK.bit_length() - 1
    hidden = jnp.logical_and(kind == 1, (kpos >> shift) > (qpos >> shift))
    o_ref[...] = jnp.where(hidden, MASKED, (bias - rb_ref[nb - 1, h]) * LOG2E)


def _bias_tiles(rel_bias, t):
    return pl.pallas_call(
        functools.partial(_bias_kernel, t=t),
        out_shape=jax.ShapeDtypeStruct((N_HEADS, 2, t, t), F32),
        grid=(N_HEADS, 2),
        in_specs=[pl.BlockSpec(memory_space=pltpu.SMEM)],
        out_specs=pl.BlockSpec((None, None, t, t), lambda h, j: (h, j, 0, 0)),
        compiler_params=_cparams(2),
        name="bias_tiles",
    )(rel_bias)


def _group_rms(xc, ones_blockdiag, gain):
    ssq = jnp.dot((xc * xc).astype(BF16), ones_blockdiag, preferred_element_type=F32)
    return xc * lax.rsqrt(ssq * (1.0 / HEAD_DIM) + RMS_EPS) * gain


def _split_bf16(a):
    hi = a.astype(BF16)
    return hi, (a - hi.astype(F32)).astype(BF16)


def _pool_fold_kernel(pw_ref, ps_ref, wbp_ref, o_ref):
    a_hi, a_lo = _split_bf16(pw_ref[...] * ps_ref[...])
    b_hi, b_lo = _split_bf16(wbp_ref[...])
    dot = functools.partial(jnp.dot, preferred_element_type=F32)
    o_ref[...] = (dot(a_hi, b_hi) + dot(a_hi, b_lo) + dot(a_lo, b_hi)).astype(o_ref.dtype)


def _pool_fold(pool_w, pool_scale, w_bp):
    g, c, _ = pool_w.shape
    d = w_bp.shape[1]
    return pl.pallas_call(
        _pool_fold_kernel,
        out_shape=jax.ShapeDtypeStruct((g * c, d), BF16),
        grid=(g,),
        in_specs=[pl.BlockSpec((None, c, c), lambda i: (i, 0, 0)),
                  pl.BlockSpec((None, 1, c), lambda i: (i, 0, 0)),
                  pl.BlockSpec((c, d), lambda i: (i, 0))],
        out_specs=pl.BlockSpec((c, d), lambda i: (i, 0)),
        compiler_params=_cparams(1),
        name="pool_fold",
    )(pool_w, pool_scale.reshape(g, 1, c), w_bp)


def _inproj_kernel(x_ref, mod_ref, g1_ref, win_ref, gq_ref, gk_ref, ones_ref, wpool_ref,
                   q_ref, k_ref, vt_ref, sga_ref, pc_ref, ext_ref, v_ref,
                   *, ts, tiles_per_batch):
    tb = pl.program_id(0) % tiles_per_batch
    x = x_ref[...]
    y = x * lax.rsqrt(jnp.mean(x * x, axis=-1, keepdims=True) + RMS_EPS)
    h = y * (g1_ref[...] * (1.0 + mod_ref[1:2, :])) + mod_ref[0:1, :]
    hb = h.astype(BF16)

    def proj(c0, c1):
        return jnp.dot(hb, win_ref[:, c0:c1], preferred_element_type=F32)

    ones_bd = ones_ref[...]
    q_ref[...] = _group_rms(proj(0, D_ATTN), ones_bd, gq_ref[...]).astype(BF16)
    k_ref[...] = _group_rms(proj(D_ATTN, 2 * D_ATTN), ones_bd, gk_ref[...]).astype(BF16)
    v_ref[...] = proj(2 * D_ATTN, 3 * D_ATTN)
    vt = v_ref[...].T.astype(BF16)
    for hd in range(N_HEADS):
        vt_ref[hd, 0:D_HEAD_V, :] = vt[hd * D_HEAD_V:(hd + 1) * D_HEAD_V, :]
        vt_ref[hd, D_HEAD_V:V_ROWS, :] = jnp.ones((V_ROWS - D_HEAD_V, ts), BF16)
    c_u = 3 * D_ATTN
    c_ga = c_u + D_POOL
    c_gp = c_ga + x.shape[1]
    sga_ref[...] = jax.nn.sigmoid(proj(c_ga, c_gp)).astype(BF16)

    u = proj(c_u, c_ga)

    @pl.when(tb == 0)
    def _():
        ext_ref[0:POOL_HALO, :] = jnp.zeros((POOL_HALO, D_POOL), F32)

    ext_ref[POOL_HALO:POOL_HALO + ts, :] = u
    row = lax.broadcasted_iota(jnp.int32, (ts, 1), 0) + tb * ts
    ys = []
    for g, w in enumerate(POOL_WINDOWS):
        c0 = g * POOL_GROUP_DIM
        ug = u[:, c0:c0 + POOL_GROUP_DIM]
        acc = ug
        for d in range(1, w):
            acc = acc + ext_ref[POOL_HALO - d:POOL_HALO - d + ts, c0:c0 + POOL_GROUP_DIM]
        cnt = jnp.minimum(row + 1, w).astype(F32)
        ys.append((acc / cnt - ug).astype(BF16))
    ypool = jnp.dot(jnp.concatenate(ys, axis=1), wpool_ref[...], preferred_element_type=F32)
    pc_ref[...] = (jax.nn.sigmoid(proj(c_gp, c_gp + x.shape[1])) * ypool).astype(BF16)
    ext_ref[0:POOL_HALO, :] = u[ts - POOL_HALO:ts, :]


def _inproj(x2, mod3, g1, w_in, gq, gk, pool_w, pool_scale, w_bp, *, bsz, seq, ts):
    n, d = x2.shape
    d_in = w_in.shape[1]
    tiles_per_batch = seq // ts
    win_b = w_in.astype(BF16)
    idx = jnp.arange(D_ATTN) // HEAD_DIM
    ones_bd = (idx[:, None] == idx[None, :]).astype(BF16)
    kern = functools.partial(_inproj_kernel, ts=ts, tiles_per_batch=tiles_per_batch)
    tok = lambda i: (i, 0)
    out_shape = (jax.ShapeDtypeStruct((n, D_ATTN), BF16),
                 jax.ShapeDtypeStruct((n, D_ATTN), BF16),
                 jax.ShapeDtypeStruct((bsz, N_HEADS, V_ROWS, seq), BF16),
                 jax.ShapeDtypeStruct((n, d), BF16),
                 jax.ShapeDtypeStruct((n, d), BF16))
    return pl.pallas_call(
        kern,
        out_shape=out_shape,
        grid=(n // ts,),
        in_specs=[pl.BlockSpec((ts, d), tok),
                  pl.BlockSpec((None, 6, d), lambda i: (i // tiles_per_batch, 0, 0)),
                  _const_spec((1, d)),
                  _const_spec((d, d_in)),
                  _const_spec((1, D_ATTN)),
                  _const_spec((1, D_ATTN)),
                  _const_spec((D_ATTN, D_ATTN)),
                  _const_spec((D_POOL, d))],
        out_specs=(pl.BlockSpec((ts, D_ATTN), tok),
                   pl.BlockSpec((ts, D_ATTN), tok),
                   pl.BlockSpec((None, N_HEADS, V_ROWS, ts),
                                lambda i: (i // tiles_per_batch, 0, 0, i % tiles_per_batch)),
                   pl.BlockSpec((ts, d), tok),
                   pl.BlockSpec((ts, d), tok)),
        scratch_shapes=[pltpu.VMEM((POOL_HALO + ts, D_POOL), F32), pltpu.VMEM((ts, D_ATTN), F32)],
        compiler_params=_cparams(1),
        name="inproj",
    )(x2, mod3, g1.reshape(1, d), win_b, gq.reshape(1, D_ATTN), gk.reshape(1, D_ATTN), ones_bd,
      _pool_fold(pool_w, pool_scale, w_bp))


def _attn_kernel(q_ref, qn_ref, k_ref, vt_ref, bias_ref, lq1_ref, lk1_ref, lq2_ref, lk2_ref, subg_ref,
                 o_ref, s_ref, mt_ref, m_ref, acc_ref, *, t, lambda_init):
    i = pl.program_id(2)
    last = pl.num_programs(2) - 1

    def split_maps(q):
        lane = lax.broadcasted_iota(jnp.int32, q.shape, 1)
        zero = jnp.zeros_like(q)
        return jnp.where(lane < HEAD_DIM, q, zero), jnp.where(lane >= HEAD_DIM, q, zero)

    q_now = split_maps(q_ref[...])

    m_ref[...] = jnp.full(m_ref.shape, MASKED, F32)
    acc_ref[...] = jnp.zeros(acc_ref.shape, F32)

    def scores(j, bias, slot, qm=q_now):
        kt = k_ref[pl.ds(pl.multiple_of(j * t, t), t), :]
        for mp in range(2):
            s = lax.dot_general(kt, qm[mp], (((1,), (1,)), ((), ())), preferred_element_type=F32)
            if bias is not None:
                s = s + bias
            s_ref[slot, mp] = s
            mt_ref[slot, mp] = jnp.max(s, axis=0, keepdims=True)

    def accumulate(j, slot):
        vt = vt_ref[:, pl.ds(pl.multiple_of(j * t, t), t)]
        for mp in range(2):
            m_old = m_ref[mp]
            m_new = jnp.maximum(m_old, mt_ref[slot, mp])
            p = jnp.exp2(s_ref[slot, mp] - m_new).astype(BF16)
            acc_ref[mp] = (jnp.exp2(m_old - m_new) * acc_ref[mp]
                           + jnp.dot(vt, p, preferred_element_type=F32))
            m_ref[mp] = m_new

    def next_diagonal():
        nxt = jnp.minimum(i + 1, last)
        scores(nxt, bias_ref[1], 2, split_maps(qn_ref[...]))

    @pl.when(i == 0)
    def _():
        scores(0, bias_ref[1], 0)

    @pl.when(i >= 1)
    def _():
        scores(i - 1, bias_ref[0], 1)
        accumulate(i, 2)

    n_far = jnp.maximum(i - 1, 0)

    def pair_body(kk, carry):
        j = i - 1 - 2 * kk
        scores(j - 1, None, 0)
        accumulate(j, 1)
        scores(j - 2, None, 1)
        accumulate(j - 1, 0)
        return carry

    lax.fori_loop(0, n_far // 2, pair_body, 0)

    @pl.when(n_far % 2 == 1)
    def _():
        scores(0, None, 0)
        accumulate(1, 1)
        next_diagonal()
        accumulate(0, 0)

    @pl.when(jnp.logical_and(i >= 1, n_far % 2 == 0))
    def _():
        next_diagonal()
        accumulate(0, 1)

    @pl.when(i == 0)
    def _():
        next_diagonal()
        accumulate(0, 0)

    lam = (jnp.exp(jnp.sum(lq1_ref[...] * lk1_ref[...], axis=1, keepdims=True))
           - jnp.exp(jnp.sum(lq2_ref[...] * lk2_ref[...], axis=1, keepdims=True)) + lambda_init)
    o1 = acc_ref[0, 0:D_HEAD_V, :] / acc_ref[0, D_HEAD_V:D_HEAD_V + 1, :]
    o2 = acc_ref[1, 0:D_HEAD_V, :] / acc_ref[1, D_HEAD_V:D_HEAD_V + 1, :]
    ot = o1 - lam * o2
    ot = ot * lax.rsqrt(jnp.mean(ot * ot, axis=0, keepdims=True) + RMS_EPS)
    ot = ot * subg_ref[...] * (1.0 - lambda_init)
    o_ref[...] = ot.T.astype(BF16)


def _attention(q, k, vt, bias, lq1, lk1, lq2, lk2, subln_g, *, t, lambda_init):
    bsz, seq, _ = q.shape
    nq = seq // t
    assert t + 1 >= LOG_BUCKET_STARTS[-1], "keys two tiles back must all fall in the last distance bucket"
    kern = functools.partial(_attn_kernel, t=t, lambda_init=lambda_init)
    vec = lambda a: a.reshape(1, HEAD_DIM)
    return pl.pallas_call(
        kern,
        out_shape=jax.ShapeDtypeStruct((bsz, seq, D_ATTN), BF16),
        grid=(bsz, N_HEADS, seq // t),
        in_specs=[pl.BlockSpec((None, t, D_HEAD_V), lambda b, h, i: (b, i, h)),
                  pl.BlockSpec((None, t, D_HEAD_V), lambda b, h, i: (b, jnp.minimum(i + 1, nq - 1), h)),
                  pl.BlockSpec((None, seq, D_HEAD_V), lambda b, h, i: (b, 0, h)),
                  pl.BlockSpec((None, None, V_ROWS, seq), lambda b, h, i: (b, h, 0, 0)),
                  pl.BlockSpec((None, 2, t, t), lambda b, h, i: (h, 0, 0, 0)),
                  _const_spec((1, HEAD_DIM)), _const_spec((1, HEAD_DIM)),
                  _const_spec((1, HEAD_DIM)), _const_spec((1, HEAD_DIM)),
                  _const_spec((D_HEAD_V, 1))],
        out_specs=pl.BlockSpec((None, t, D_HEAD_V), lambda b, h, i: (b, i, h)),
        scratch_shapes=[pltpu.VMEM((3, 2, t, t), F32),
                        pltpu.VMEM((3, 2, 1, t), F32),
                        pltpu.VMEM((2, 1, t), F32),
                        pltpu.VMEM((2, V_ROWS, t), F32)],
        compiler_params=_cparams(3),
        name="attn",
    )(q, q, k, vt, bias, vec(lq1), vec(lk1), vec(lq2), vec(lk2), subln_g.reshape(D_HEAD_V, 1))


def _first_max(rows):
    best = rows[0]
    for r in rows[1:]:
        best = jnp.maximum(best, r)
    idx = jnp.full(best.shape, len(rows) - 1, jnp.int32)
    for j in range(len(rows) - 2, -1, -1):
        idx = jnp.where(rows[j] == best, j, idx)
    return best, idx


def _post_kernel(o_ref, sga_ref, pc_ref, x_ref, mod_ref, g2_ref, wba_ref, wout_ref, wr_ref, br_ref,
                 tri_ref, x1_ref, h2p_ref, route_ref, cnt_ref, carry_ref):
    @pl.when(pl.program_id(0) == 0)
    def _():
        carry_ref[...] = jnp.zeros(carry_ref.shape, F32)

    ya = jnp.dot(o_ref[...], wba_ref[...], preferred_element_type=F32)
    merged = sga_ref[...].astype(F32) * ya + pc_ref[...].astype(F32)
    z = jnp.dot(merged.astype(BF16), wout_ref[...], preferred_element_type=F32)
    x1 = x_ref[...] + mod_ref[2:3, :] * z
    x1_ref[...] = x1
    y = x1 * lax.rsqrt(jnp.mean(x1 * x1, axis=-1, keepdims=True) + RMS_EPS)
    h2 = y * (g2_ref[...] * (1.0 + mod_ref[4:5, :])) + mod_ref[3:4, :]
    hi = h2.astype(BF16)
    lo = (h2 - hi.astype(F32)).astype(BF16)

    nt = (((1,), (1,)), ((), ()))
    a = lax.dot_general(wr_ref[...], hi, nt, preferred_element_type=F32)
    b = lax.dot_general(wr_ref[0:ROUTER_ROWS, :], lo, nt, preferred_element_type=F32)
    logits = a[0:ROUTER_ROWS] + a[ROUTER_ROWS:2 * ROUTER_ROWS] + b + br_ref[...]

    gl = [logits[g:g + 1, :] for g in range(N_EXPERT_GROUPS)]
    gmax, gidx = _first_max(gl)
    gsum = gl[0] * 0.0
    for r in gl:
        gsum = gsum + jnp.exp(r - gmax)
    g_val = 1.0 / gsum
    es = []
    for r in range(EXPERTS_PER_GROUP):
        sel = jnp.zeros_like(gmax)
        for g in range(N_EXPERT_GROUPS):
            row = EXPERT_ROW0 + g * EXPERTS_PER_GROUP + r
            sel = jnp.where(gidx == g, logits[row:row + 1, :], sel)
        es.append(sel)
    e1, i1 = _first_max(es)
    rest = [jnp.where(i1 == r, -jnp.inf, es[r]) for r in range(EXPERTS_PER_GROUP)]
    e2, i2 = _first_max(rest)
    r21 = jnp.exp(e2 - e1)
    w1 = g_val / (1.0 + r21)
    w2 = g_val * r21 / (1.0 + r21)

    first = i1 < i2
    e_lo = jnp.where(first, i1, i2)
    e_hi = jnp.where(first, i2, i1)
    pair = jnp.zeros_like(e_lo)
    for p in range(1, PAIRS_PER_GROUP):
        pair = jnp.where(jnp.logical_and(e_lo == PAIR_LO[p], e_hi == PAIR_HI[p]), p, pair)
    bucket = gidx * PAIRS_PER_GROUP + pair
    w_lo = jnp.where(first, w1, w2)
    w_hi = jnp.where(first, w2, w1)

    ts = bucket.shape[1]
    brow = lax.broadcasted_iota(jnp.int32, (ROUTER_ROWS, ts), 0)
    onehot = brow == bucket
    prefix = jnp.dot(jnp.where(onehot, 1.0, 0.0).astype(BF16), tri_ref[...],
                     preferred_element_type=F32)
    carry = carry_ref[...]
    rank = jnp.sum(jnp.where(onehot, prefix + carry, 0.0), axis=0, keepdims=True) - 1.0
    carry = carry + prefix[:, ts - 1:ts]
    carry_ref[...] = carry
    cnt_ref[...] = jnp.broadcast_to(carry, cnt_ref.shape).astype(jnp.int32)
    rrow = lax.broadcasted_iota(jnp.int32, (8, ts), 0)
    route_ref[...] = jnp.where(rrow == 0, bucket, jnp.where(rrow == 1, rank.astype(jnp.int32), 0))

    arow = lax.broadcasted_iota(jnp.int32, (LANES, ts), 0)
    aux_t = jnp.where(arow == 0, w_lo, jnp.where(arow == 1, w_hi, 0.0))
    d = h2.shape[1]
    h2p_ref[:, 0:d] = h2
    h2p_ref[:, d:d + LANES] = aux_t.T


def _post(o2, sga, pc, x2, mod3, g2, w_ba, w_out, wg_r, bg_r, we_r, be_r, *, seq, ts):
    n, d = x2.shape
    tiles_per_batch = seq // ts
    wr = jnp.zeros((ROUTER_ROWS, d), F32)
    wr = wr.at[0:N_EXPERT_GROUPS].set(wg_r.T).at[EXPERT_ROW0:EXPERT_ROW0 + N_EXPERTS].set(we_r.T)
    wr_hi = wr.astype(BF16)
    wr_lo = (wr - wr_hi.astype(F32)).astype(BF16)
    br = jnp.zeros((ROUTER_ROWS, 1), F32)
    br = br.at[0:N_EXPERT_GROUPS, 0].set(bg_r).at[EXPERT_ROW0:EXPERT_ROW0 + N_EXPERTS, 0].set(be_r)
    tok = lambda i: (i, 0)
    tidx = jnp.arange(ts)
    tri = (tidx[:, None] <= tidx[None, :]).astype(BF16)
    return pl.pallas_call(
        _post_kernel,
        out_shape=(jax.ShapeDtypeStruct((n, d), F32),
                   jax.ShapeDtypeStruct((n, d + LANES), F32),
                   jax.ShapeDtypeStruct((8, n), jnp.int32),
                   jax.ShapeDtypeStruct((ROUTER_ROWS, LANES), jnp.int32)),
        grid=(n // ts,),
        in_specs=[pl.BlockSpec((ts, D_ATTN), tok),
                  pl.BlockSpec((ts, d), tok),
                  pl.BlockSpec((ts, d), tok),
                  pl.BlockSpec((ts, d), tok),
                  pl.BlockSpec((None, 6, d), lambda i: (i // tiles_per_batch, 0, 0)),
                  _const_spec((1, d)),
                  _const_spec((D_ATTN, d)),
                  _const_spec((d, d)),
                  _const_spec((2 * ROUTER_ROWS, d)),
                  _const_spec((ROUTER_ROWS, 1)),
                  _const_spec((ts, ts))],
        out_specs=(pl.BlockSpec((ts, d), tok), pl.BlockSpec((ts, d + LANES), tok),
                   pl.BlockSpec((8, ts), lambda i: (0, i)),
                   pl.BlockSpec((ROUTER_ROWS, LANES), lambda i: (0, 0))),
        scratch_shapes=[pltpu.VMEM((ROUTER_ROWS, 1), F32)],
        compiler_params=_cparams(1),
        name="post",
    )(o2, sga, pc, x2, mod3, g2.reshape(1, d), w_ba.astype(BF16), w_out.astype(BF16),
      jnp.concatenate([wr_hi, wr_lo], axis=0), br, tri)


def _group_sublane(row):
    return lax.shift_right_logical(row, SUBLANES.bit_length() - 1), row & (SUBLANES - 1)


def _row_copy(src, src_row, dst, dst_row, sem):
    sg, ss = src_row
    dg, ds = dst_row
    return pltpu.make_async_copy(src.at[sg, pl.ds(ss, 1), :], dst.at[dg, pl.ds(ds, 1), :], sem)


def _dispatch_kernel(pos_ref, h_ref, init_ref, hs_ref, sem, *, ts):
    del init_ref
    base = pl.program_id(0) * ts

    def start(g, carry):
        for u in range(SUBLANES):
            p = pos_ref[base + g * SUBLANES + u]
            _row_copy(h_ref, (g, u), hs_ref, _group_sublane(p), sem).start(priority=u % 2)
        return carry

    def wait(g, carry):
        for u in range(SUBLANES):
            _row_copy(h_ref, (0, 0), hs_ref, (0, 0), sem).wait()
        return carry

    lax.fori_loop(0, ts // SUBLANES, start, 0)
    lax.fori_loop(0, ts // SUBLANES, wait, 0)


def _dispatch(pos, h2p, n_rows, *, ts):
    n, w = h2p.shape
    return pl.pallas_call(
        functools.partial(_dispatch_kernel, ts=ts),
        out_shape=jax.ShapeDtypeStruct((n_rows // SUBLANES, SUBLANES, w), h2p.dtype),
        grid_spec=pltpu.PrefetchScalarGridSpec(
            num_scalar_prefetch=1,
            grid=(n // ts,),
            in_specs=[pl.BlockSpec((ts // SUBLANES, SUBLANES, w), lambda i, pos: (i, 0, 0)),
                      pl.BlockSpec(memory_space=pl.ANY)],
            out_specs=pl.BlockSpec(memory_space=pl.ANY),
            scratch_shapes=[pltpu.SemaphoreType.DMA]),
        input_output_aliases={2: 0},
        compiler_params=_cparams(1),
        name="dispatch",
    )(pos, h2p.reshape(n // SUBLANES, SUBLANES, w),
      jnp.zeros((n_rows // SUBLANES, SUBLANES, w), h2p.dtype)).reshape(n_rows, w)


def _experts_kernel(ea_ref, eb_ref, valid_ref, hs_ref, wga_ref, wua_ref, wda_ref, wgb_ref, wub_ref,
                    wdb_ref, ys_ref):
    del ea_ref, eb_ref
    t = pl.program_id(0)
    d = hs_ref.shape[1] - LANES

    @pl.when(valid_ref[t] == 1)
    def _():
        h = hs_ref[:, 0:d].astype(BF16)
        aux = hs_ref[:, d:d + LANES]

        def expert(wg_ref, wu_ref, wd_ref):
            a = jnp.dot(h, wg_ref[...].astype(BF16), preferred_element_type=F32)
            b = jnp.dot(h, wu_ref[...].astype(BF16), preferred_element_type=F32)
            hid = (a * jax.nn.sigmoid(a)) * b
            return jnp.dot(hid.astype(BF16), wd_ref[...].astype(BF16), preferred_element_type=F32)

        ys_ref[...] = (aux[:, 0:1] * expert(wga_ref, wua_ref, wda_ref)
                       + aux[:, 1:2] * expert(wgb_ref, wub_ref, wdb_ref))

    @pl.when(valid_ref[t] == 0)
    def _():
        ys_ref[...] = jnp.zeros(ys_ref.shape, ys_ref.dtype)


def _experts(tile_ea, tile_eb, tile_valid, hs, w_gate, w_up, w_down, *, tm):
    n_rows, w = hs.shape
    _, d, de = w_gate.shape
    wg, wu, wd = w_gate, w_up, w_down
    ea = lambda t, ea_r, eb_r, v_r: (ea_r[t], 0, 0)
    eb = lambda t, ea_r, eb_r, v_r: (eb_r[t], 0, 0)
    row = lambda t, ea_r, eb_r, v_r: (t, 0)
    return pl.pallas_call(
        _experts_kernel,
        out_shape=jax.ShapeDtypeStruct((n_rows, d), F32),
        grid_spec=pltpu.PrefetchScalarGridSpec(
            num_scalar_prefetch=3,
            grid=(n_rows // tm,),
            in_specs=[pl.BlockSpec((tm, w), row),
                      pl.BlockSpec((None, d, de), ea), pl.BlockSpec((None, d, de), ea),
                      pl.BlockSpec((None, de, d), ea),
                      pl.BlockSpec((None, d, de), eb), pl.BlockSpec((None, d, de), eb),
                      pl.BlockSpec((None, de, d), eb)],
            out_specs=pl.BlockSpec((tm, d), row)),
        compiler_params=_cparams(1),
        name="experts",
    )(tile_ea, tile_eb, tile_valid, hs, wg, wu, wd, wg, wu, wd)


def _combine_kernel(pos_ref, ys_ref, x1_ref, mod_ref, out_ref, ybuf_ref, sem, *, ts):
    i = pl.program_id(0)
    groups = ts // SUBLANES

    def gather(tile, slot):
        base = tile * ts

        def start(g, carry):
            for u in range(SUBLANES):
                p = pos_ref[base + g * SUBLANES + u]
                _row_copy(ys_ref, _group_sublane(p), ybuf_ref.at[slot], (g, u),
                          sem.at[slot]).start(priority=u % 2)
            return carry

        lax.fori_loop(0, groups, start, 0)

    def wait_all(slot):
        def wait(g, carry):
            for u in range(SUBLANES):
                _row_copy(ys_ref, (0, 0), ybuf_ref.at[slot], (0, 0), sem.at[slot]).wait()
            return carry

        lax.fori_loop(0, groups, wait, 0)

    def step(slot):
        @pl.when(i + 1 < pl.num_programs(0))
        def _():
            gather(i + 1, 1 - slot)

        wait_all(slot)
        y = ybuf_ref[slot].reshape(ts, ybuf_ref.shape[-1])
        out_ref[...] = x1_ref[...] + mod_ref[5:6, :] * y

    @pl.when(i == 0)
    def _():
        gather(0, 0)

    @pl.when(i % 2 == 0)
    def _():
        step(0)

    @pl.when(i % 2 == 1)
    def _():
        step(1)


def _combine(pos, ys, x1, mod3, *, seq, ts):
    n, d = x1.shape
    tiles_per_batch = seq // ts
    return pl.pallas_call(
        functools.partial(_combine_kernel, ts=ts),
        out_shape=jax.ShapeDtypeStruct((n, d), F32),
        grid_spec=pltpu.PrefetchScalarGridSpec(
            num_scalar_prefetch=1,
            grid=(n // ts,),
            in_specs=[pl.BlockSpec(memory_space=pl.ANY),
                      pl.BlockSpec((ts, d), lambda i, pos: (i, 0)),
                      pl.BlockSpec((None, 6, d), lambda i, pos: (i // tiles_per_batch, 0, 0))],
            out_specs=pl.BlockSpec((ts, d), lambda i, pos: (i, 0)),
            scratch_shapes=[pltpu.VMEM((2, ts // SUBLANES, SUBLANES, d), F32),
                            pltpu.SemaphoreType.DMA((2,))]),
        compiler_params=_cparams(1),
        name="combine",
    )(pos, ys.reshape(ys.shape[0] // SUBLANES, SUBLANES, d), x1, mod3)


def _route_tables(route, counts, n_tiles, tm):
    n = route.shape[1]
    bucket, rank = route[0].reshape(n // LANES, LANES), route[1].reshape(n // LANES, LANES)
    cnt = counts[:N_ROUTE_BUCKETS, 0]
    padded = (cnt + tm - 1) // tm * tm
    end = jnp.cumsum(padded)
    start = end - padded
    pos = rank
    for b in range(N_ROUTE_BUCKETS):
        pos = pos + jnp.where(bucket == b, start[b], 0)
    pos = pos.reshape(n)
    tiles_used = end[-1] // tm
    tile = jnp.arange(n_tiles, dtype=jnp.int32)
    valid = tile < tiles_used
    first_row = jnp.minimum(tile, tiles_used - 1) * tm
    tile_bucket = jnp.sum((end[None, :] <= first_row[:, None]).astype(jnp.int32), axis=1)
    tile_bucket = jnp.minimum(tile_bucket, N_ROUTE_BUCKETS - 1)
    group = tile_bucket // PAIRS_PER_GROUP
    pair = tile_bucket % PAIRS_PER_GROUP
    tile_ea = group * EXPERTS_PER_GROUP + jnp.asarray(PAIR_LO, jnp.int32)[pair]
    tile_eb = group * EXPERTS_PER_GROUP + jnp.asarray(PAIR_HI, jnp.int32)[pair]
    return pos.astype(jnp.int32), tile_ea, tile_eb, valid.astype(jnp.int32)


def _tile(seq, pref):
    t = min(pref, seq)
    assert seq % t == 0 and t % LANES == 0, (seq, t)
    return t


def kernel(x, c, rel_bias, ada_w, ada_b, norm1_g, w_in, q_norm_g, k_norm_g, lambda_q1, lambda_k1,
           lambda_q2, lambda_k2, subln_g, w_branch_attn, pool_w, pool_scale, w_branch_pool, w_out,
           norm2_g, router_group_w, router_group_b, router_expert_w, router_expert_b,
           expert_w_gate, expert_w_up, expert_w_down):
    bsz, seq, d = x.shape
    n = bsz * seq
    ts = _tile(seq, TOKEN_TILE)
    t_attn = _tile(seq, ATTN_TILE)
    tm = MOE_TILE
    n_tiles = -(-(n + N_ROUTE_BUCKETS * (tm - 1)) // tm)
    bias = _bias_tiles(rel_bias, t_attn)
    x2 = x.reshape(n, d)
    for l in range(ada_w.shape[0]):
        lambda_init = 0.8 - 0.6 * math.exp(-0.3 * l)
        mod3 = _ada(c, ada_w[l], ada_b[l]).reshape(bsz, 6, d)
        gq = jnp.tile(q_norm_g[l], D_ATTN // HEAD_DIM) * (HEAD_DIM ** -0.5 * LOG2E)
        gk = jnp.tile(k_norm_g[l], D_ATTN // HEAD_DIM)
        q, k, vt, sga, pc = _inproj(x2, mod3, norm1_g[l], w_in[l], gq, gk, pool_w[l], pool_scale[l],
                                    w_branch_pool[l], bsz=bsz, seq=seq, ts=ts)
        o = _attention(q.reshape(bsz, seq, D_ATTN), k.reshape(bsz, seq, D_ATTN), vt, bias,
                       lambda_q1[l], lambda_k1[l], lambda_q2[l], lambda_k2[l], subln_g[l],
                       t=t_attn, lambda_init=lambda_init)
        x1, h2p, route, counts = _post(o.reshape(n, D_ATTN), sga, pc, x2, mod3, norm2_g[l],
                                       w_branch_attn[l], w_out[l], router_group_w[l],
                                       router_group_b[l], router_expert_w[l], router_expert_b[l],
                                       seq=seq, ts=ts)
        pos, tile_ea, tile_eb, tile_valid = _route_tables(route, counts, n_tiles, tm)
        hs = _dispatch(pos, h2p, n_tiles * tm, ts=ts)
        ys = _experts(tile_ea, tile_eb, tile_valid, hs, expert_w_gate[l], expert_w_up[l],
                      expert_w_down[l], tm=tm)
        x2 = _combine(pos, ys, x1, mod3, seq=seq, ts=ts)
    return x2.reshape(bsz, seq, d)
```

```python
import functools
import math

import jax
import jax.numpy as jnp
from jax import lax
from jax.experimental import pallas as pl
from jax.experimental.pallas import tpu as pltpu

F32 = jnp.float32
BF16 = jnp.bfloat16

CHUNK = 64
N_HEADS = 4
HEAD_DIM = 64
D_HEAD_V = 2 * HEAD_DIM
V_ROWS = D_HEAD_V + 16
D_ATTN = N_HEADS * D_HEAD_V
POOL_WINDOWS = (2, 4, 8, 16)
POOL_GROUP_DIM = 128
D_POOL = len(POOL_WINDOWS) * POOL_GROUP_DIM
POOL_HALO = 16
N_BUCKETS = 32
MAX_DISTANCE = 128
N_EXPERT_GROUPS = 4
EXPERTS_PER_GROUP = 4
N_EXPERTS = N_EXPERT_GROUPS * EXPERTS_PER_GROUP
PAIRS_PER_GROUP = 6
PAIR_LO = (0, 0, 1, 1, 0, 2)
PAIR_HI = (1, 2, 2, 3, 3, 3)
N_ROUTE_BUCKETS = N_EXPERT_GROUPS * PAIRS_PER_GROUP
RMS_EPS = 1e-6
LOG2E = math.log2(math.e)
MASKED = -1e30

LANES = 128
SUBLANES = 8
ROUTER_ROWS = 32
EXPERT_ROW0 = 8

VMEM_LIMIT = 56 * 1024 * 1024
TOKEN_TILE = 512
ATTN_TILE = 512
MOE_TILE = 256


def _cparams(n_axes):
    return pltpu.CompilerParams(dimension_semantics=("arbitrary",) * n_axes,
                                vmem_limit_bytes=VMEM_LIMIT)


def _const_spec(shape):
    nd = len(shape)
    return pl.BlockSpec(shape, lambda *_: (0,) * nd, pipeline_mode=pl.Buffered(1))


def _ada_kernel(ct_ref, w_ref, b_ref, o_ref):
    @pl.when(pl.program_id(0) == 0)
    def _():
        o_ref[...] = jnp.broadcast_to(b_ref[...], o_ref.shape)

    ct = ct_ref[...]
    s = ct * jax.nn.sigmoid(ct)
    w = w_ref[...]
    rows = [jnp.sum(w * s[:, b:b + 1], axis=0, keepdims=True) for b in range(ct.shape[1])]
    o_ref[...] += jnp.concatenate(rows, axis=0)


def _ada(c, w, b):
    bsz, d = c.shape
    n = w.shape[1]
    rows = 256
    return pl.pallas_call(
        _ada_kernel,
        out_shape=jax.ShapeDtypeStruct((bsz, n), F32),
        grid=(d // rows,),
        in_specs=[pl.BlockSpec((rows, bsz), lambda j: (j, 0)),
                  pl.BlockSpec((rows, n), lambda j: (j, 0)),
                  pl.BlockSpec((1, n), lambda j: (0, 0))],
        out_specs=pl.BlockSpec((bsz, n), lambda j: (0, 0)),
        compiler_params=_cparams(1),
        name="ada",
    )(c.T, w, b.reshape(1, n))


def _log_bucket_starts():
    nb = N_BUCKETS // 2
    max_exact = nb // 2
    m = nb - max_exact
    ratio = MAX_DISTANCE // max_exact
    starts = []
    for k in range(1, m):
        n = max_exact
        while n ** m < max_exact ** m * ratio ** k:
            n += 1
        starts.append(n)
    return tuple(starts)


LOG_BUCKET_STARTS = _log_bucket_starts()


def _bias_kernel(rb_ref, o_ref, *, t):
    h = pl.program_id(0)
    kind = pl.program_id(1)
    nb = N_BUCKETS // 2
    max_exact = nb // 2
    kpos = lax.broadcasted_iota(jnp.int32, (t, t), 0)
    qpos = lax.broadcasted_iota(jnp.int32, (t, t), 1)
    rel = kpos - qpos - jnp.where(kind == 0, t, 0)
    n = jnp.abs(rel)

    def table(first):
        val = jnp.full((t, t), rb_ref[first + nb - 1, h], F32)
        for k in range(len(LOG_BUCKET_STARTS) - 1, -1, -1):
            val = jnp.where(n < LOG_BUCKET_STARTS[k], rb_ref[first + max_exact + k, h], val)
        for j in range(max_exact - 1, -1, -1):
            val = jnp.where(n == j, rb_ref[first + j, h], val)
        return val

    bias = jnp.where(rel > 0, table(nb), table(0))
    shift = CHUNK.bit_length() - 1
    hidden = jnp.logical_and(kind == 1, (kpos >> shift) > (qpos >> shift))
    o_ref[...] = jnp.where(hidden, MASKED, (bias - rb_ref[nb - 1, h]) * LOG2E)


def _bias_tiles(rel_bias, t):
    return pl.pallas_call(
        functools.partial(_bias_kernel, t=t),
        out_shape=jax.ShapeDtypeStruct((N_HEADS, 2, t, t), F32),
        grid=(N_HEADS, 2),
        in_specs=[pl.BlockSpec(memory_space=pltpu.SMEM)],
        out_specs=pl.BlockSpec((None, None, t, t), lambda h, j: (h, j, 0, 0)),
        compiler_params=_cparams(2),
        name="bias_tiles",
    )(rel_bias)


def _group_rms(xc, ones_blockdiag, gain):
    ssq = jnp.dot((xc * xc).astype(BF16), ones_blockdiag, preferred_element_type=F32)
    return xc * lax.rsqrt(ssq * (1.0 / HEAD_DIM) + RMS_EPS) * gain


def _split_bf16(a):
    hi = a.astype(BF16)
    return hi, (a - hi.astype(F32)).astype(BF16)


def _pool_fold_kernel(pw_ref, ps_ref, wbp_ref, o_ref):
    a_hi, a_lo = _split_bf16(pw_ref[...] * ps_ref[...])
    b_hi, b_lo = _split_bf16(wbp_ref[...])
    dot = functools.partial(jnp.dot, preferred_element_type=F32)
    o_ref[...] = (dot(a_hi, b_hi) + dot(a_hi, b_lo) + dot(a_lo, b_hi)).astype(o_ref.dtype)


def _pool_fold(pool_w, pool_scale, w_bp):
    g, c, _ = pool_w.shape
    d = w_bp.shape[1]
    return pl.pallas_call(
        _pool_fold_kernel,
        out_shape=jax.ShapeDtypeStruct((g * c, d), BF16),
        grid=(g,),
        in_specs=[pl.BlockSpec((None, c, c), lambda i: (i, 0, 0)),
                  pl.BlockSpec((None, 1, c), lambda i: (i, 0, 0)),
                  pl.BlockSpec((c, d), lambda i: (i, 0))],
        out_specs=pl.BlockSpec((c, d), lambda i: (i, 0)),
        compiler_params=_cparams(1),
        name="pool_fold",
    )(pool_w, pool_scale.reshape(g, 1, c), w_bp)


def _inproj_kernel(x_ref, mod_ref, g1_ref, win_ref, gq_ref, gk_ref, ones_ref, wpool_ref,
                   q_ref, k_ref, vt_ref, sga_ref, pc_ref, ext_ref, v_ref,
                   *, ts, tiles_per_batch):
    tb = pl.program_id(0) % tiles_per_batch
    x = x_ref[...]
    y = x * lax.rsqrt(jnp.mean(x * x, axis=-1, keepdims=True) + RMS_EPS)
    h = y * (g1_ref[...] * (1.0 + mod_ref[1:2, :])) + mod_ref[0:1, :]
    hb = h.astype(BF16)

    def proj(c0, c1):
        return jnp.dot(hb, win_ref[:, c0:c1], preferred_element_type=F32)

    ones_bd = ones_ref[...]
    q_ref[...] = _group_rms(proj(0, D_ATTN), ones_bd, gq_ref[...]).astype(BF16)
    k_ref[...] = _group_rms(proj(D_ATTN, 2 * D_ATTN), ones_bd, gk_ref[...]).astype(BF16)
    v_ref[...] = proj(2 * D_ATTN, 3 * D_ATTN)
    vt = v_ref[...].T.astype(BF16)
    for hd in range(N_HEADS):
        vt_ref[hd, 0:D_HEAD_V, :] = vt[hd * D_HEAD_V:(hd + 1) * D_HEAD_V, :]
        vt_ref[hd, D_HEAD_V:V_ROWS, :] = jnp.ones((V_ROWS - D_HEAD_V, ts), BF16)
    c_u = 3 * D_ATTN
    c_ga = c_u + D_POOL
    c_gp = c_ga + x.shape[1]
    sga_ref[...] = jax.nn.sigmoid(proj(c_ga, c_gp)).astype(BF16)

    u = proj(c_u, c_ga)

    @pl.when(tb == 0)
    def _():
        ext_ref[0:POOL_HALO, :] = jnp.zeros((POOL_HALO, D_POOL), F32)

    ext_ref[POOL_HALO:POOL_HALO + ts, :] = u
    row = lax.broadcasted_iota(jnp.int32, (ts, 1), 0) + tb * ts
    ys = []
    for g, w in enumerate(POOL_WINDOWS):
        c0 = g * POOL_GROUP_DIM
        ug = u[:, c0:c0 + POOL_GROUP_DIM]
        acc = ug
        for d in range(1, w):
            acc = acc + ext_ref[POOL_HALO - d:POOL_HALO - d + ts, c0:c0 + POOL_GROUP_DIM]
        cnt = jnp.minimum(row + 1, w).astype(F32)
        ys.append((acc / cnt - ug).astype(BF16))
    ypool = jnp.dot(jnp.concatenate(ys, axis=1), wpool_ref[...], preferred_element_type=F32)
    pc_ref[...] = (jax.nn.sigmoid(proj(c_gp, c_gp + x.shape[1])) * ypool).astype(BF16)
    ext_ref[0:POOL_HALO, :] = u[ts - POOL_HALO:ts, :]


def _inproj(x2, mod3, g1, w_in, gq, gk, pool_w, pool_scale, w_bp, *, bsz, seq, ts):
    n, d = x2.shape
    d_in = w_in.shape[1]
    tiles_per_batch = seq // ts
    win_b = w_in.astype(BF16)
    idx = jnp.arange(D_ATTN) // HEAD_DIM
    ones_bd = (idx[:, None] == idx[None, :]).astype(BF16)
    kern = functools.partial(_inproj_kernel, ts=ts, tiles_per_batch=tiles_per_batch)
    tok = lambda i: (i, 0)
    out_shape = (jax.ShapeDtypeStruct((n, D_ATTN), BF16),
                 jax.ShapeDtypeStruct((n, D_ATTN), BF16),
                 jax.ShapeDtypeStruct((bsz, N_HEADS, V_ROWS, seq), BF16),
                 jax.ShapeDtypeStruct((n, d), BF16),
                 jax.ShapeDtypeStruct((n, d), BF16))
    return pl.pallas_call(
        kern,
        out_shape=out_shape,
        grid=(n // ts,),
        in_specs=[pl.BlockSpec((ts, d), tok),
                  pl.BlockSpec((None, 6, d), lambda i: (i // tiles_per_batch, 0, 0)),
                  _const_spec((1, d)),
                  _const_spec((d, d_in)),
                  _const_spec((1, D_ATTN)),
                  _const_spec((1, D_ATTN)),
                  _const_spec((D_ATTN, D_ATTN)),
                  _const_spec((D_POOL, d))],
        out_specs=(pl.BlockSpec((ts, D_ATTN), tok),
                   pl.BlockSpec((ts, D_ATTN), tok),
                   pl.BlockSpec((None, N_HEADS, V_ROWS, ts),
                                lambda i: (i // tiles_per_batch, 0, 0, i % tiles_per_batch)),
                   pl.BlockSpec((ts, d), tok),
                   pl.BlockSpec((ts, d), tok)),
        scratch_shapes=[pltpu.VMEM((POOL_HALO + ts, D_POOL), F32), pltpu.VMEM((ts, D_ATTN), F32)],
        compiler_params=_cparams(1),
        name="inproj",
    )(x2, mod3, g1.reshape(1, d), win_b, gq.reshape(1, D_ATTN), gk.reshape(1, D_ATTN), ones_bd,
      _pool_fold(pool_w, pool_scale, w_bp))


def _attn_kernel(q_ref, qn_ref, k_ref, vt_ref, bias_ref, lq1_ref, lk1_ref, lq2_ref, lk2_ref, subg_ref,
                 o_ref, s_ref, mt_ref, m_ref, acc_ref, *, t, lambda_init):
    i = pl.program_id(2)
    last = pl.num_programs(2) - 1

    def split_maps(q):
        lane = lax.broadcasted_iota(jnp.int32, q.shape, 1)
        zero = jnp.zeros_like(q)
        return jnp.where(lane < HEAD_DIM, q, zero), jnp.where(lane >= HEAD_DIM, q, zero)

    q_now = split_maps(q_ref[...])

    m_ref[...] = jnp.full(m_ref.shape, MASKED, F32)
    acc_ref[...] = jnp.zeros(acc_ref.shape, F32)

    def scores(j, bias, slot, qm=q_now):
        kt = k_ref[pl.ds(pl.multiple_of(j * t, t), t), :]
        for mp in range(2):
            s = lax.dot_general(kt, qm[mp], (((1,), (1,)), ((), ())), preferred_element_type=F32)
            if bias is not None:
                s = s + bias
            s_ref[slot, mp] = s
            mt_ref[slot, mp] = jnp.max(s, axis=0, keepdims=True)

    def accumulate(j, slot):
        vt = vt_ref[:, pl.ds(pl.multiple_of(j * t, t), t)]
        for mp in range(2):
            m_old = m_ref[mp]
            m_new = jnp.maximum(m_old, mt_ref[slot, mp])
            p = jnp.exp2(s_ref[slot, mp] - m_new).astype(BF16)
            acc_ref[mp] = (jnp.exp2(m_old - m_new) * acc_ref[mp]
                           + jnp.dot(vt, p, preferred_element_type=F32))
            m_ref[mp] = m_new

    def next_diagonal():
        nxt = jnp.minimum(i + 1, last)
        scores(nxt, bias_ref[1], 2, split_maps(qn_ref[...]))

    @pl.when(i == 0)
    def _():
        scores(0, bias_ref[1], 0)

    @pl.when(i >= 1)
    def _():
        scores(i - 1, bias_ref[0], 1)
        accumulate(i, 2)

    n_far = jnp.maximum(i - 1, 0)

    def pair_body(kk, carry):
        j = i - 1 - 2 * kk
        scores(j - 1, None, 0)
        accumulate(j, 1)
        scores(j - 2, None, 1)
        accumulate(j - 1, 0)
        return carry

    lax.fori_loop(0, n_far // 2, pair_body, 0)

    @pl.when(n_far % 2 == 1)
    def _():
        scores(0, None, 0)
        accumulate(1, 1)
        next_diagonal()
        accumulate(0, 0)

    @pl.when(jnp.logical_and(i >= 1, n_far % 2 == 0))
    def _():
        next_diagonal()
        accumulate(0, 1)

    @pl.when(i == 0)
    def _():
        next_diagonal()
        accumulate(0, 0)

    lam = (jnp.exp(jnp.sum(lq1_ref[...] * lk1_ref[...], axis=1, keepdims=True))
           - jnp.exp(jnp.sum(lq2_ref[...] * lk2_ref[...], axis=1, keepdims=True)) + lambda_init)
    o1 = acc_ref[0, 0:D_HEAD_V, :] / acc_ref[0, D_HEAD_V:D_HEAD_V + 1, :]
    o2 = acc_ref[1, 0:D_HEAD_V, :] / acc_ref[1, D_HEAD_V:D_HEAD_V + 1, :]
    ot = o1 - lam * o2
    ot = ot * lax.rsqrt(jnp.mean(ot * ot, axis=0, keepdims=True) + RMS_EPS)
    ot = ot * subg_ref[...] * (1.0 - lambda_init)
    o_ref[...] = ot.T.astype(BF16)


def _attention(q, k, vt, bias, lq1, lk1, lq2, lk2, subln_g, *, t, lambda_init):
    bsz, seq, _ = q.shape
    nq = seq // t
    assert t + 1 >= LOG_BUCKET_STARTS[-1], "keys two tiles back must all fall in the last distance bucket"
    kern = functools.partial(_attn_kernel, t=t, lambda_init=lambda_init)
    vec = lambda a: a.reshape(1, HEAD_DIM)
    return pl.pallas_call(
        kern,
        out_shape=jax.ShapeDtypeStruct((bsz, seq, D_ATTN), BF16),
        grid=(bsz, N_HEADS, seq // t),
        in_specs=[pl.BlockSpec((None, t, D_HEAD_V), lambda b, h, i: (b, i, h)),
                  pl.BlockSpec((None, t, D_HEAD_V), lambda b, h, i: (b, jnp.minimum(i + 1, nq - 1), h)),
                  pl.BlockSpec((None, seq, D_HEAD_V), lambda b, h, i: (b, 0, h)),
                  pl.BlockSpec((None, None, V_ROWS, seq), lambda b, h, i: (b, h, 0, 0)),
                  pl.BlockSpec((None, 2, t, t), lambda b, h, i: (h, 0, 0, 0)),
                  _const_spec((1, HEAD_DIM)), _const_spec((1, HEAD_DIM)),
                  _const_spec((1, HEAD_DIM)), _const_spec((1, HEAD_DIM)),
                  _const_spec((D_HEAD_V, 1))],
        out_specs=pl.BlockSpec((None, t, D_HEAD_V), lambda b, h, i: (b, i, h)),
        scratch_shapes=[pltpu.VMEM((3, 2, t, t), F32),
                        pltpu.VMEM((3, 2, 1, t), F32),
                        pltpu.VMEM((2, 1, t), F32),
                        pltpu.VMEM((2, V_ROWS, t), F32)],
        compiler_params=_cparams(3),
        name="attn",
    )(q, q, k, vt, bias, vec(lq1), vec(lk1), vec(lq2), vec(lk2), subln_g.reshape(D_HEAD_V, 1))


def _first_max(rows):
    best = rows[0]
    for r in rows[1:]:
        best = jnp.maximum(best, r)
    idx = jnp.full(best.shape, len(rows) - 1, jnp.int32)
    for j in range(len(rows) - 2, -1, -1):
        idx = jnp.where(rows[j] == best, j, idx)
    return best, idx


def _post_kernel(o_ref, sga_ref, pc_ref, x_ref, mod_ref, g2_ref, wba_ref, wout_ref, wr_ref, br_ref,
                 tri_ref, x1_ref, h2p_ref, route_ref, cnt_ref, zero_ref, carry_ref):
    @pl.when(pl.program_id(0) == 0)
    def _():
        carry_ref[...] = jnp.zeros(carry_ref.shape, F32)

    ya = jnp.dot(o_ref[...], wba_ref[...], preferred_element_type=F32)
    merged = sga_ref[...].astype(F32) * ya + pc_ref[...].astype(F32)
    z = jnp.dot(merged.astype(BF16), wout_ref[...], preferred_element_type=F32)
    x1 = x_ref[...] + mod_ref[2:3, :] * z
    x1_ref[...] = x1
    y = x1 * lax.rsqrt(jnp.mean(x1 * x1, axis=-1, keepdims=True) + RMS_EPS)
    h2 = y * (g2_ref[...] * (1.0 + mod_ref[4:5, :])) + mod_ref[3:4, :]
    hi = h2.astype(BF16)
    lo = (h2 - hi.astype(F32)).astype(BF16)

    nt = (((1,), (1,)), ((), ()))
    a = lax.dot_general(wr_ref[...], hi, nt, preferred_element_type=F32)
    b = lax.dot_general(wr_ref[0:ROUTER_ROWS, :], lo, nt, preferred_element_type=F32)
    logits = a[0:ROUTER_ROWS] + a[ROUTER_ROWS:2 * ROUTER_ROWS] + b + br_ref[...]

    gl = [logits[g:g + 1, :] for g in range(N_EXPERT_GROUPS)]
    gmax, gidx = _first_max(gl)
    gsum = gl[0] * 0.0
    for r in gl:
        gsum = gsum + jnp.exp(r - gmax)
    g_val = 1.0 / gsum
    es = []
    for r in range(EXPERTS_PER_GROUP):
        sel = jnp.zeros_like(gmax)
        for g in range(N_EXPERT_GROUPS):
            row = EXPERT_ROW0 + g * EXPERTS_PER_GROUP + r
            sel = jnp.where(gidx == g, logits[row:row + 1, :], sel)
        es.append(sel)
    e1, i1 = _first_max(es)
    rest = [jnp.where(i1 == r, -jnp.inf, es[r]) for r in range(EXPERTS_PER_GROUP)]
    e2, i2 = _first_max(rest)
    r21 = jnp.exp(e2 - e1)
    w1 = g_val / (1.0 + r21)
    w2 = g_val * r21 / (1.0 + r21)

    first = i1 < i2
    e_lo = jnp.where(first, i1, i2)
    e_hi = jnp.where(first, i2, i1)
    pair = jnp.zeros_like(e_lo)
    for p in range(1, PAIRS_PER_GROUP):
        pair = jnp.where(jnp.logical_and(e_lo == PAIR_LO[p], e_hi == PAIR_HI[p]), p, pair)
    bucket = gidx * PAIRS_PER_GROUP + pair
    w_lo = jnp.where(first, w1, w2)
    w_hi = jnp.where(first, w2, w1)

    ts = bucket.shape[1]
    brow = lax.broadcasted_iota(jnp.int32, (ROUTER_ROWS, ts), 0)
    onehot = brow == bucket
    prefix = jnp.dot(jnp.where(onehot, 1.0, 0.0).astype(BF16), tri_ref[...],
                     preferred_element_type=F32)
    carry = carry_ref[...]
    rank = jnp.sum(jnp.where(onehot, prefix + carry, 0.0), axis=0, keepdims=True) - 1.0
    carry = carry + prefix[:, ts - 1:ts]
    carry_ref[...] = carry
    cnt_ref[...] = jnp.broadcast_to(carry, cnt_ref.shape).astype(jnp.int32)
    chunks = ts // LANES
    rank_i = rank.astype(jnp.int32)
    for j in range(chunks):
        route_ref[j:j + 1, :] = bucket[:, j * LANES:(j + 1) * LANES]
        route_ref[chunks + j:chunks + j + 1, :] = rank_i[:, j * LANES:(j + 1) * LANES]
    zero_ref[...] = jnp.zeros(zero_ref.shape, zero_ref.dtype)

    arow = lax.broadcasted_iota(jnp.int32, (LANES, ts), 0)
    aux_t = jnp.where(arow == 0, w_lo, jnp.where(arow == 1, w_hi, 0.0))
    d = h2.shape[1]
    h2p_ref[:, 0:d] = h2
    h2p_ref[:, d:d + LANES] = aux_t.T


def _post(o2, sga, pc, x2, mod3, g2, w_ba, w_out, wg_r, bg_r, we_r, be_r, *, seq, ts, n_rows):
    n, d = x2.shape
    tiles_per_batch = seq // ts
    steps = n // ts
    zero_rows = n_rows // steps
    assert zero_rows * steps == n_rows and zero_rows % SUBLANES == 0, (n_rows, steps)
    wr = jnp.zeros((ROUTER_ROWS, d), F32)
    wr = wr.at[0:N_EXPERT_GROUPS].set(wg_r.T).at[EXPERT_ROW0:EXPERT_ROW0 + N_EXPERTS].set(we_r.T)
    wr_hi = wr.astype(BF16)
    wr_lo = (wr - wr_hi.astype(F32)).astype(BF16)
    br = jnp.zeros((ROUTER_ROWS, 1), F32)
    br = br.at[0:N_EXPERT_GROUPS, 0].set(bg_r).at[EXPERT_ROW0:EXPERT_ROW0 + N_EXPERTS, 0].set(be_r)
    tok = lambda i: (i, 0)
    tidx = jnp.arange(ts)
    tri = (tidx[:, None] <= tidx[None, :]).astype(BF16)
    return pl.pallas_call(
        _post_kernel,
        out_shape=(jax.ShapeDtypeStruct((n, d), F32),
                   jax.ShapeDtypeStruct((n, d + LANES), F32),
                   jax.ShapeDtypeStruct((steps, 2 * (ts // LANES), LANES), jnp.int32),
                   jax.ShapeDtypeStruct((ROUTER_ROWS, LANES), jnp.int32),
                   jax.ShapeDtypeStruct((n_rows, d + LANES), F32)),
        grid=(steps,),
        in_specs=[pl.BlockSpec((ts, D_ATTN), tok),
                  pl.BlockSpec((ts, d), tok),
                  pl.BlockSpec((ts, d), tok),
                  pl.BlockSpec((ts, d), tok),
                  pl.BlockSpec((None, 6, d), lambda i: (i // tiles_per_batch, 0, 0)),
                  _const_spec((1, d)),
                  _const_spec((D_ATTN, d)),
                  _const_spec((d, d)),
                  _const_spec((2 * ROUTER_ROWS, d)),
                  _const_spec((ROUTER_ROWS, 1)),
                  _const_spec((ts, ts))],
        out_specs=(pl.BlockSpec((ts, d), tok), pl.BlockSpec((ts, d + LANES), tok),
                   pl.BlockSpec((None, 2 * (ts // LANES), LANES), lambda i: (i, 0, 0)),
                   pl.BlockSpec((ROUTER_ROWS, LANES), lambda i: (0, 0)),
                   pl.BlockSpec((zero_rows, d + LANES), tok)),
        scratch_shapes=[pltpu.VMEM((ROUTER_ROWS, 1), F32)],
        compiler_params=_cparams(1),
        name="post",
    )(o2, sga, pc, x2, mod3, g2.reshape(1, d), w_ba.astype(BF16), w_out.astype(BF16),
      jnp.concatenate([wr_hi, wr_lo], axis=0), br, tri)


def _group_sublane(row):
    return lax.shift_right_logical(row, SUBLANES.bit_length() - 1), row & (SUBLANES - 1)


def _row_copy(src, src_row, dst, dst_row, sem):
    sg, ss = src_row
    dg, ds = dst_row
    return pltpu.make_async_copy(src.at[sg, pl.ds(ss, 1), :], dst.at[dg, pl.ds(ds, 1), :], sem)


def _dispatch_kernel(pos_ref, h_ref, init_ref, hs_ref, sem, *, ts):
    del init_ref
    base = pl.program_id(0) * ts

    def start(g, carry):
        for u in range(SUBLANES):
            p = pos_ref[base + g * SUBLANES + u]
            _row_copy(h_ref, (g, u), hs_ref, _group_sublane(p), sem).start(priority=u % 2)
        return carry

    def wait(g, carry):
        for u in range(SUBLANES):
            _row_copy(h_ref, (0, 0), hs_ref, (0, 0), sem).wait()
        return carry

    lax.fori_loop(0, ts // SUBLANES, start, 0)
    lax.fori_loop(0, ts // SUBLANES, wait, 0)


def _dispatch(pos, h2p, cleared, *, ts):
    n, w = h2p.shape
    n_rows = cleared.shape[0]
    return pl.pallas_call(
        functools.partial(_dispatch_kernel, ts=ts),
        out_shape=jax.ShapeDtypeStruct((n_rows // SUBLANES, SUBLANES, w), h2p.dtype),
        grid_spec=pltpu.PrefetchScalarGridSpec(
            num_scalar_prefetch=1,
            grid=(n // ts,),
            in_specs=[pl.BlockSpec((ts // SUBLANES, SUBLANES, w), lambda i, pos: (i, 0, 0)),
                      pl.BlockSpec(memory_space=pl.ANY)],
            out_specs=pl.BlockSpec(memory_space=pl.ANY),
            scratch_shapes=[pltpu.SemaphoreType.DMA]),
        input_output_aliases={2: 0},
        compiler_params=_cparams(1),
        name="dispatch",
    )(pos, h2p.reshape(n // SUBLANES, SUBLANES, w),
      cleared.reshape(n_rows // SUBLANES, SUBLANES, w)).reshape(n_rows, w)


def _experts_kernel(ea_ref, eb_ref, valid_ref, hs_ref, wga_ref, wua_ref, wda_ref, wgb_ref, wub_ref,
                    wdb_ref, ys_ref):
    del ea_ref, eb_ref
    t = pl.program_id(0)
    d = hs_ref.shape[1] - LANES

    @pl.when(valid_ref[t] == 1)
    def _():
        h = hs_ref[:, 0:d].astype(BF16)
        aux = hs_ref[:, d:d + LANES]

        def expert(wg_ref, wu_ref, wd_ref):
            a = jnp.dot(h, wg_ref[...].astype(BF16), preferred_element_type=F32)
            b = jnp.dot(h, wu_ref[...].astype(BF16), preferred_element_type=F32)
            hid = (a * jax.nn.sigmoid(a)) * b
            return jnp.dot(hid.astype(BF16), wd_ref[...].astype(BF16), preferred_element_type=F32)

        ys_ref[...] = (aux[:, 0:1] * expert(wga_ref, wua_ref, wda_ref)
                       + aux[:, 1:2] * expert(wgb_ref, wub_ref, wdb_ref))

    @pl.when(valid_ref[t] == 0)
    def _():
        ys_ref[...] = jnp.zeros(ys_ref.shape, ys_ref.dtype)


def _experts(tile_ea, tile_eb, tile_valid, hs, w_gate, w_up, w_down, *, tm):
    n_rows, w = hs.shape
    _, d, de = w_gate.shape
    wg, wu, wd = w_gate, w_up, w_down
    ea = lambda t, ea_r, eb_r, v_r: (ea_r[t], 0, 0)
    eb = lambda t, ea_r, eb_r, v_r: (eb_r[t], 0, 0)
    row = lambda t, ea_r, eb_r, v_r: (t, 0)
    return pl.pallas_call(
        _experts_kernel,
        out_shape=jax.ShapeDtypeStruct((n_rows, d), F32),
        grid_spec=pltpu.PrefetchScalarGridSpec(
            num_scalar_prefetch=3,
            grid=(n_rows // tm,),
            in_specs=[pl.BlockSpec((tm, w), row),
                      pl.BlockSpec((None, d, de), ea), pl.BlockSpec((None, d, de), ea),
                      pl.BlockSpec((None, de, d), ea),
                      pl.BlockSpec((None, d, de), eb), pl.BlockSpec((None, d, de), eb),
                      pl.BlockSpec((None, de, d), eb)],
            out_specs=pl.BlockSpec((tm, d), row)),
        compiler_params=_cparams(1),
        name="experts",
    )(tile_ea, tile_eb, tile_valid, hs, wg, wu, wd, wg, wu, wd)


def _combine_kernel(pos_ref, ys_ref, x1_ref, mod_ref, out_ref, ybuf_ref, sem, *, ts):
    i = pl.program_id(0)
    groups = ts // SUBLANES

    def gather(tile, slot):
        base = tile * ts

        def start(g, carry):
            for u in range(SUBLANES):
                p = pos_ref[base + g * SUBLANES + u]
                _row_copy(ys_ref, _group_sublane(p), ybuf_ref.at[slot], (g, u),
                          sem.at[slot]).start(priority=u % 2)
            return carry

        lax.fori_loop(0, groups, start, 0)

    def wait_all(slot):
        def wait(g, carry):
            for u in range(SUBLANES):
                _row_copy(ys_ref, (0, 0), ybuf_ref.at[slot], (0, 0), sem.at[slot]).wait()
            return carry

        lax.fori_loop(0, groups, wait, 0)

    def step(slot):
        @pl.when(i + 1 < pl.num_programs(0))
        def _():
            gather(i + 1, 1 - slot)

        wait_all(slot)
        y = ybuf_ref[slot].reshape(ts, ybuf_ref.shape[-1])
        out_ref[...] = x1_ref[...] + mod_ref[5:6, :] * y

    @pl.when(i == 0)
    def _():
        gather(0, 0)

    @pl.when(i % 2 == 0)
    def _():
        step(0)

    @pl.when(i % 2 == 1)
    def _():
        step(1)


def _combine(pos, ys, x1, mod3, *, seq, ts):
    n, d = x1.shape
    tiles_per_batch = seq // ts
    return pl.pallas_call(
        functools.partial(_combine_kernel, ts=ts),
        out_shape=jax.ShapeDtypeStruct((n, d), F32),
        grid_spec=pltpu.PrefetchScalarGridSpec(
            num_scalar_prefetch=1,
            grid=(n // ts,),
            in_specs=[pl.BlockSpec(memory_space=pl.ANY),
                      pl.BlockSpec((ts, d), lambda i, pos: (i, 0)),
                      pl.BlockSpec((None, 6, d), lambda i, pos: (i // tiles_per_batch, 0, 0))],
            out_specs=pl.BlockSpec((ts, d), lambda i, pos: (i, 0)),
            scratch_shapes=[pltpu.VMEM((2, ts // SUBLANES, SUBLANES, d), F32),
                            pltpu.SemaphoreType.DMA((2,))]),
        compiler_params=_cparams(1),
        name="combine",
    )(pos, ys.reshape(ys.shape[0] // SUBLANES, SUBLANES, d), x1, mod3)


def _route_tables(route, counts, n_tiles, tm):
    chunks = route.shape[1] // 2
    bucket, rank = route[:, :chunks, :], route[:, chunks:, :]
    cnt = counts[:N_ROUTE_BUCKETS, 0]
    padded = (cnt + tm - 1) // tm * tm
    end = jnp.cumsum(padded)
    start = end - padded
    pos = rank
    for b in range(N_ROUTE_BUCKETS):
        pos = pos + jnp.where(bucket == b, start[b], 0)
    pos = pos.reshape(-1)
    tiles_used = end[-1] // tm
    tile = jnp.arange(n_tiles, dtype=jnp.int32)
    valid = tile < tiles_used
    first_row = jnp.minimum(tile, tiles_used - 1) * tm
    tile_bucket = jnp.sum((end[None, :] <= first_row[:, None]).astype(jnp.int32), axis=1)
    tile_bucket = jnp.minimum(tile_bucket, N_ROUTE_BUCKETS - 1)
    group = tile_bucket // PAIRS_PER_GROUP
    pair = tile_bucket % PAIRS_PER_GROUP
    tile_ea = group * EXPERTS_PER_GROUP + jnp.asarray(PAIR_LO, jnp.int32)[pair]
    tile_eb = group * EXPERTS_PER_GROUP + jnp.asarray(PAIR_HI, jnp.int32)[pair]
    return pos.astype(jnp.int32), tile_ea, tile_eb, valid.astype(jnp.int32)


def _tile(seq, pref):
    t = min(pref, seq)
    assert seq % t == 0 and t % LANES == 0, (seq, t)
    return t


def kernel(x, c, rel_bias, ada_w, ada_b, norm1_g, w_in, q_norm_g, k_norm_g, lambda_q1, lambda_k1,
           lambda_q2, lambda_k2, subln_g, w_branch_attn, pool_w, pool_scale, w_branch_pool, w_out,
           norm2_g, router_group_w, router_group_b, router_expert_w, router_expert_b,
           expert_w_gate, expert_w_up, expert_w_down):
    bsz, seq, d = x.shape
    n = bsz * seq
    ts = _tile(seq, TOKEN_TILE)
    t_attn = _tile(seq, ATTN_TILE)
    tm = MOE_TILE
    n_tiles = -(-(n + N_ROUTE_BUCKETS * (tm - 1)) // tm)
    while (n_tiles * tm) % (n // ts * SUBLANES):
        n_tiles += 1
    bias = _bias_tiles(rel_bias, t_attn)
    x2 = x.reshape(n, d)
    for l in range(ada_w.shape[0]):
        lambda_init = 0.8 - 0.6 * math.exp(-0.3 * l)
        mod3 = _ada(c, ada_w[l], ada_b[l]).reshape(bsz, 6, d)
        gq = jnp.tile(q_norm_g[l], D_ATTN // HEAD_DIM) * (HEAD_DIM ** -0.5 * LOG2E)
        gk = jnp.tile(k_norm_g[l], D_ATTN // HEAD_DIM)
        q, k, vt, sga, pc = _inproj(x2, mod3, norm1_g[l], w_in[l], gq, gk, pool_w[l], pool_scale[l],
                                    w_branch_pool[l], bsz=bsz, seq=seq, ts=ts)
        o = _attention(q.reshape(bsz, seq, D_ATTN), k.reshape(bsz, seq, D_ATTN), vt, bias,
                       lambda_q1[l], lambda_k1[l], lambda_q2[l], lambda_k2[l], subln_g[l],
                       t=t_attn, lambda_init=lambda_init)
        x1, h2p, route, counts, cleared = _post(
            o.reshape(n, D_ATTN), sga, pc, x2, mod3, norm2_g[l], w_branch_attn[l], w_out[l],
            router_group_w[l], router_group_b[l], router_expert_w[l], router_expert_b[l],
            seq=seq, ts=ts, n_rows=n_tiles * tm)
        pos, tile_ea, tile_eb, tile_valid = _route_tables(route, counts, n_tiles, tm)
        hs = _dispatch(pos, h2p, cleared, ts=ts)
        ys = _experts(tile_ea, tile_eb, tile_valid, hs, expert_w_gate[l], expert_w_up[l],
                      expert_w_down[l], tm=tm)
        x2 = _combine(pos, ys, x1, mod3, seq=seq, ts=ts)
    return x2.reshape(bsz, seq, d)
```

```python
import functools
import math

import jax
import jax.numpy as jnp
from jax import lax
from jax.experimental import pallas as pl
from jax.experimental.pallas import tpu as pltpu

F32 = jnp.float32
BF16 = jnp.bfloat16

CHUNK = 64
N_HEADS = 4
HEAD_DIM = 64
D_HEAD_V = 2 * HEAD_DIM
V_ROWS = D_HEAD_V + 16
D_ATTN = N_HEADS * D_HEAD_V
POOL_WINDOWS = (2, 4, 8, 16)
POOL_GROUP_DIM = 128
D_POOL = len(POOL_WINDOWS) * POOL_GROUP_DIM
POOL_HALO = 16
N_BUCKETS = 32
MAX_DISTANCE = 128
N_EXPERT_GROUPS = 4
EXPERTS_PER_GROUP = 4
N_EXPERTS = N_EXPERT_GROUPS * EXPERTS_PER_GROUP
PAIRS_PER_GROUP = 6
PAIR_LO = (0, 0, 1, 1, 0, 2)
PAIR_HI = (1, 2, 2, 3, 3, 3)
N_ROUTE_BUCKETS = N_EXPERT_GROUPS * PAIRS_PER_GROUP
RMS_EPS = 1e-6
LOG2E = math.log2(math.e)
MASKED = -1e30

LANES = 128
SUBLANES = 8
ROUTER_ROWS = 32
EXPERT_ROW0 = 8

VMEM_LIMIT = 56 * 1024 * 1024
TOKEN_TILE = 512
ATTN_TILE = 512
MOE_TILE = 256
ROW_COPY_TILE = 1024


def _cparams(n_axes):
    return pltpu.CompilerParams(dimension_semantics=("arbitrary",) * n_axes,
                                vmem_limit_bytes=VMEM_LIMIT)


def _const_spec(shape):
    nd = len(shape)
    return pl.BlockSpec(shape, lambda *_: (0,) * nd, pipeline_mode=pl.Buffered(1))


def _ada_kernel(ct_ref, w_ref, b_ref, o_ref):
    @pl.when(pl.program_id(0) == 0)
    def _():
        o_ref[...] = jnp.broadcast_to(b_ref[...], o_ref.shape)

    ct = ct_ref[...]
    s = ct * jax.nn.sigmoid(ct)
    w = w_ref[...]
    rows = [jnp.sum(w * s[:, b:b + 1], axis=0, keepdims=True) for b in range(ct.shape[1])]
    o_ref[...] += jnp.concatenate(rows, axis=0)


def _ada(c, w, b):
    bsz, d = c.shape
    n = w.shape[1]
    rows = 256
    return pl.pallas_call(
        _ada_kernel,
        out_shape=jax.ShapeDtypeStruct((bsz, n), F32),
        grid=(d // rows,),
        in_specs=[pl.BlockSpec((rows, bsz), lambda j: (j, 0)),
                  pl.BlockSpec((rows, n), lambda j: (j, 0)),
                  pl.BlockSpec((1, n), lambda j: (0, 0))],
        out_specs=pl.BlockSpec((bsz, n), lambda j: (0, 0)),
        compiler_params=_cparams(1),
        name="ada",
    )(c.T, w, b.reshape(1, n))


def _log_bucket_starts():
    nb = N_BUCKETS // 2
    max_exact = nb // 2
    m = nb - max_exact
    ratio = MAX_DISTANCE // max_exact
    starts = []
    for k in range(1, m):
        n = max_exact
        while n ** m < max_exact ** m * ratio ** k:
            n += 1
        starts.append(n)
    return tuple(starts)


LOG_BUCKET_STARTS = _log_bucket_starts()


def _bias_kernel(rb_ref, o_ref, *, t):
    h = pl.program_id(0)
    kind = pl.program_id(1)
    nb = N_BUCKETS // 2
    max_exact = nb // 2
    kpos = lax.broadcasted_iota(jnp.int32, (t, t), 0)
    qpos = lax.broadcasted_iota(jnp.int32, (t, t), 1)
    rel = kpos - qpos - jnp.where(kind == 0, t, 0)
    n = jnp.abs(rel)

    def table(first):
        val = jnp.full((t, t), rb_ref[first + nb - 1, h], F32)
        for k in range(len(LOG_BUCKET_STARTS) - 1, -1, -1):
            val = jnp.where(n < LOG_BUCKET_STARTS[k], rb_ref[first + max_exact + k, h], val)
        for j in range(max_exact - 1, -1, -1):
            val = jnp.where(n == j, rb_ref[first + j, h], val)
        return val

    bias = jnp.where(rel > 0, table(nb), table(0))
    shift = CHUNK.bit_length() - 1
    hidden = jnp.logical_and(kind == 1, (kpos >> shift) > (qpos >> shift))
    o_ref[...] = jnp.where(hidden, MASKED, (bias - rb_ref[nb - 1, h]) * LOG2E)


def _bias_tiles(rel_bias, t):
    return pl.pallas_call(
        functools.partial(_bias_kernel, t=t),
        out_shape=jax.ShapeDtypeStruct((N_HEADS, 2, t, t), F32),
        grid=(N_HEADS, 2),
        in_specs=[pl.BlockSpec(memory_space=pltpu.SMEM)],
        out_specs=pl.BlockSpec((None, None, t, t), lambda h, j: (h, j, 0, 0)),
        compiler_params=_cparams(2),
        name="bias_tiles",
    )(rel_bias)


def _group_rms(xc, ones_blockdiag, gain):
    ssq = jnp.dot((xc * xc).astype(BF16), ones_blockdiag, preferred_element_type=F32)
    return xc * lax.rsqrt(ssq * (1.0 / HEAD_DIM) + RMS_EPS) * gain


def _split_bf16(a):
    hi = a.astype(BF16)
    return hi, (a - hi.astype(F32)).astype(BF16)


def _pool_fold_kernel(pw_ref, ps_ref, wbp_ref, o_ref):
    a_hi, a_lo = _split_bf16(pw_ref[...] * ps_ref[...])
    b_hi, b_lo = _split_bf16(wbp_ref[...])
    dot = functools.partial(jnp.dot, preferred_element_type=F32)
    o_ref[...] = (dot(a_hi, b_hi) + dot(a_hi, b_lo) + dot(a_lo, b_hi)).astype(o_ref.dtype)


def _pool_fold(pool_w, pool_scale, w_bp):
    g, c, _ = pool_w.shape
    d = w_bp.shape[1]
    return pl.pallas_call(
        _pool_fold_kernel,
        out_shape=jax.ShapeDtypeStruct((g * c, d), BF16),
        grid=(g,),
        in_specs=[pl.BlockSpec((None, c, c), lambda i: (i, 0, 0)),
                  pl.BlockSpec((None, 1, c), lambda i: (i, 0, 0)),
                  pl.BlockSpec((c, d), lambda i: (i, 0))],
        out_specs=pl.BlockSpec((c, d), lambda i: (i, 0)),
        compiler_params=_cparams(1),
        name="pool_fold",
    )(pool_w, pool_scale.reshape(g, 1, c), w_bp)


def _inproj_kernel(x_ref, mod_ref, g1_ref, win_ref, gq_ref, gk_ref, ones_ref, wpool_ref,
                   q_ref, k_ref, vt_ref, sga_ref, pc_ref, ext_ref, v_ref,
                   *, ts, tiles_per_batch):
    tb = pl.program_id(0) % tiles_per_batch
    x = x_ref[...]
    y = x * lax.rsqrt(jnp.mean(x * x, axis=-1, keepdims=True) + RMS_EPS)
    h = y * (g1_ref[...] * (1.0 + mod_ref[1:2, :])) + mod_ref[0:1, :]
    hb = h.astype(BF16)

    def proj(c0, c1):
        return jnp.dot(hb, win_ref[:, c0:c1], preferred_element_type=F32)

    ones_bd = ones_ref[...]
    q_ref[...] = _group_rms(proj(0, D_ATTN), ones_bd, gq_ref[...]).astype(BF16)
    k_ref[...] = _group_rms(proj(D_ATTN, 2 * D_ATTN), ones_bd, gk_ref[...]).astype(BF16)
    v_ref[...] = proj(2 * D_ATTN, 3 * D_ATTN)
    vt = v_ref[...].T.astype(BF16)
    for hd in range(N_HEADS):
        vt_ref[hd, 0:D_HEAD_V, :] = vt[hd * D_HEAD_V:(hd + 1) * D_HEAD_V, :]
        vt_ref[hd, D_HEAD_V:V_ROWS, :] = jnp.ones((V_ROWS - D_HEAD_V, ts), BF16)
    c_u = 3 * D_ATTN
    c_ga = c_u + D_POOL
    c_gp = c_ga + x.shape[1]
    sga_ref[...] = jax.nn.sigmoid(proj(c_ga, c_gp)).astype(BF16)

    u = proj(c_u, c_ga)

    @pl.when(tb == 0)
    def _():
        ext_ref[0:POOL_HALO, :] = jnp.zeros((POOL_HALO, D_POOL), F32)

    ext_ref[POOL_HALO:POOL_HALO + ts, :] = u
    row = lax.broadcasted_iota(jnp.int32, (ts, 1), 0) + tb * ts
    ys = []
    for g, w in enumerate(POOL_WINDOWS):
        c0 = g * POOL_GROUP_DIM
        ug = u[:, c0:c0 + POOL_GROUP_DIM]
        acc = ug
        for d in range(1, w):
            acc = acc + ext_ref[POOL_HALO - d:POOL_HALO - d + ts, c0:c0 + POOL_GROUP_DIM]
        cnt = jnp.minimum(row + 1, w).astype(F32)
        ys.append((acc / cnt - ug).astype(BF16))
    ypool = jnp.dot(jnp.concatenate(ys, axis=1), wpool_ref[...], preferred_element_type=F32)
    pc_ref[...] = (jax.nn.sigmoid(proj(c_gp, c_gp + x.shape[1])) * ypool).astype(BF16)
    ext_ref[0:POOL_HALO, :] = u[ts - POOL_HALO:ts, :]


def _inproj(x2, mod3, g1, w_in, gq, gk, pool_w, pool_scale, w_bp, *, bsz, seq, ts):
    n, d = x2.shape
    d_in = w_in.shape[1]
    tiles_per_batch = seq // ts
    win_b = w_in.astype(BF16)
    idx = jnp.arange(D_ATTN) // HEAD_DIM
    ones_bd = (idx[:, None] == idx[None, :]).astype(BF16)
    kern = functools.partial(_inproj_kernel, ts=ts, tiles_per_batch=tiles_per_batch)
    tok = lambda i: (i, 0)
    out_shape = (jax.ShapeDtypeStruct((n, D_ATTN), BF16),
                 jax.ShapeDtypeStruct((n, D_ATTN), BF16),
                 jax.ShapeDtypeStruct((bsz, N_HEADS, V_ROWS, seq), BF16),
                 jax.ShapeDtypeStruct((n, d), BF16),
                 jax.ShapeDtypeStruct((n, d), BF16))
    return pl.pallas_call(
        kern,
        out_shape=out_shape,
        grid=(n // ts,),
        in_specs=[pl.BlockSpec((ts, d), tok),
                  pl.BlockSpec((None, 6, d), lambda i: (i // tiles_per_batch, 0, 0)),
                  _const_spec((1, d)),
                  _const_spec((d, d_in)),
                  _const_spec((1, D_ATTN)),
                  _const_spec((1, D_ATTN)),
                  _const_spec((D_ATTN, D_ATTN)),
                  _const_spec((D_POOL, d))],
        out_specs=(pl.BlockSpec((ts, D_ATTN), tok),
                   pl.BlockSpec((ts, D_ATTN), tok),
                   pl.BlockSpec((None, N_HEADS, V_ROWS, ts),
                                lambda i: (i // tiles_per_batch, 0, 0, i % tiles_per_batch)),
                   pl.BlockSpec((ts, d), tok),
                   pl.BlockSpec((ts, d), tok)),
        scratch_shapes=[pltpu.VMEM((POOL_HALO + ts, D_POOL), F32), pltpu.VMEM((ts, D_ATTN), F32)],
        compiler_params=_cparams(1),
        name="inproj",
    )(x2, mod3, g1.reshape(1, d), win_b, gq.reshape(1, D_ATTN), gk.reshape(1, D_ATTN), ones_bd,
      _pool_fold(pool_w, pool_scale, w_bp))


def _attn_kernel(q_ref, qn_ref, k_ref, vt_ref, bias_ref, lq1_ref, lk1_ref, lq2_ref, lk2_ref, subg_ref,
                 o_ref, s_ref, mt_ref, m_ref, acc_ref, *, t, lambda_init):
    i = pl.program_id(2)
    last = pl.num_programs(2) - 1

    def split_maps(q):
        lane = lax.broadcasted_iota(jnp.int32, q.shape, 1)
        zero = jnp.zeros_like(q)
        return jnp.where(lane < HEAD_DIM, q, zero), jnp.where(lane >= HEAD_DIM, q, zero)

    q_now = split_maps(q_ref[...])

    m_ref[...] = jnp.full(m_ref.shape, MASKED, F32)
    acc_ref[...] = jnp.zeros(acc_ref.shape, F32)

    def scores(j, bias, slot, qm=q_now):
        kt = k_ref[pl.ds(pl.multiple_of(j * t, t), t), :]
        for mp in range(2):
            s = lax.dot_general(kt, qm[mp], (((1,), (1,)), ((), ())), preferred_element_type=F32)
            if bias is not None:
                s = s + bias
            s_ref[slot, mp] = s
            mt_ref[slot, mp] = jnp.max(s, axis=0, keepdims=True)

    def accumulate(j, slot):
        vt = vt_ref[:, pl.ds(pl.multiple_of(j * t, t), t)]
        for mp in range(2):
            m_old = m_ref[mp]
            m_new = jnp.maximum(m_old, mt_ref[slot, mp])
            p = jnp.exp2(s_ref[slot, mp] - m_new).astype(BF16)
            acc_ref[mp] = (jnp.exp2(m_old - m_new) * acc_ref[mp]
                           + jnp.dot(vt, p, preferred_element_type=F32))
            m_ref[mp] = m_new

    def next_diagonal():
        nxt = jnp.minimum(i + 1, last)
        scores(nxt, bias_ref[1], 2, split_maps(qn_ref[...]))

    @pl.when(i == 0)
    def _():
        scores(0, bias_ref[1], 0)

    @pl.when(i >= 1)
    def _():
        scores(i - 1, bias_ref[0], 1)
        accumulate(i, 2)

    n_far = jnp.maximum(i - 1, 0)

    def pair_body(kk, carry):
        j = i - 1 - 2 * kk
        scores(j - 1, None, 0)
        accumulate(j, 1)
        scores(j - 2, None, 1)
        accumulate(j - 1, 0)
        return carry

    lax.fori_loop(0, n_far // 2, pair_body, 0)

    @pl.when(n_far % 2 == 1)
    def _():
        scores(0, None, 0)
        accumulate(1, 1)
        next_diagonal()
        accumulate(0, 0)

    @pl.when(jnp.logical_and(i >= 1, n_far % 2 == 0))
    def _():
        next_diagonal()
        accumulate(0, 1)

    @pl.when(i == 0)
    def _():
        next_diagonal()
        accumulate(0, 0)

    lam = (jnp.exp(jnp.sum(lq1_ref[...] * lk1_ref[...], axis=1, keepdims=True))
           - jnp.exp(jnp.sum(lq2_ref[...] * lk2_ref[...], axis=1, keepdims=True)) + lambda_init)
    o1 = acc_ref[0, 0:D_HEAD_V, :] / acc_ref[0, D_HEAD_V:D_HEAD_V + 1, :]
    o2 = acc_ref[1, 0:D_HEAD_V, :] / acc_ref[1, D_HEAD_V:D_HEAD_V + 1, :]
    ot = o1 - lam * o2
    ot = ot * lax.rsqrt(jnp.mean(ot * ot, axis=0, keepdims=True) + RMS_EPS)
    ot = ot * subg_ref[...] * (1.0 - lambda_init)
    o_ref[...] = ot.T.astype(BF16)


def _attention(q, k, vt, bias, lq1, lk1, lq2, lk2, subln_g, *, t, lambda_init):
    bsz, seq, _ = q.shape
    nq = seq // t
    assert t + 1 >= LOG_BUCKET_STARTS[-1], "keys two tiles back must all fall in the last distance bucket"
    kern = functools.partial(_attn_kernel, t=t, lambda_init=lambda_init)
    vec = lambda a: a.reshape(1, HEAD_DIM)
    return pl.pallas_call(
        kern,
        out_shape=jax.ShapeDtypeStruct((bsz, seq, D_ATTN), BF16),
        grid=(bsz, N_HEADS, seq // t),
        in_specs=[pl.BlockSpec((None, t, D_HEAD_V), lambda b, h, i: (b, i, h)),
                  pl.BlockSpec((None, t, D_HEAD_V), lambda b, h, i: (b, jnp.minimum(i + 1, nq - 1), h)),
                  pl.BlockSpec((None, seq, D_HEAD_V), lambda b, h, i: (b, 0, h)),
                  pl.BlockSpec((None, None, V_ROWS, seq), lambda b, h, i: (b, h, 0, 0)),
                  pl.BlockSpec((None, 2, t, t), lambda b, h, i: (h, 0, 0, 0)),
                  _const_spec((1, HEAD_DIM)), _const_spec((1, HEAD_DIM)),
                  _const_spec((1, HEAD_DIM)), _const_spec((1, HEAD_DIM)),
                  _const_spec((D_HEAD_V, 1))],
        out_specs=pl.BlockSpec((None, t, D_HEAD_V), lambda b, h, i: (b, i, h)),
        scratch_shapes=[pltpu.VMEM((3, 2, t, t), F32),
                        pltpu.VMEM((3, 2, 1, t), F32),
                        pltpu.VMEM((2, 1, t), F32),
                        pltpu.VMEM((2, V_ROWS, t), F32)],
        compiler_params=_cparams(3),
        name="attn",
    )(q, q, k, vt, bias, vec(lq1), vec(lk1), vec(lq2), vec(lk2), subln_g.reshape(D_HEAD_V, 1))


def _first_max(rows):
    best = rows[0]
    for r in rows[1:]:
        best = jnp.maximum(best, r)
    idx = jnp.full(best.shape, len(rows) - 1, jnp.int32)
    for j in range(len(rows) - 2, -1, -1):
        idx = jnp.where(rows[j] == best, j, idx)
    return best, idx


def _post_kernel(o_ref, sga_ref, pc_ref, x_ref, mod_ref, g2_ref, wba_ref, wout_ref, wr_ref, br_ref,
                 tri_ref, x1_ref, h2p_ref, route_ref, cnt_ref, zero_ref, carry_ref):
    @pl.when(pl.program_id(0) == 0)
    def _():
        carry_ref[...] = jnp.zeros(carry_ref.shape, F32)

    ya = jnp.dot(o_ref[...], wba_ref[...], preferred_element_type=F32)
    merged = sga_ref[...].astype(F32) * ya + pc_ref[...].astype(F32)
    z = jnp.dot(merged.astype(BF16), wout_ref[...], preferred_element_type=F32)
    x1 = x_ref[...] + mod_ref[2:3, :] * z
    x1_ref[...] = x1
    y = x1 * lax.rsqrt(jnp.mean(x1 * x1, axis=-1, keepdims=True) + RMS_EPS)
    h2 = y * (g2_ref[...] * (1.0 + mod_ref[4:5, :])) + mod_ref[3:4, :]
    hi = h2.astype(BF16)
    lo = (h2 - hi.astype(F32)).astype(BF16)

    nt = (((1,), (1,)), ((), ()))
    a = lax.dot_general(wr_ref[...], hi, nt, preferred_element_type=F32)
    b = lax.dot_general(wr_ref[0:ROUTER_ROWS, :], lo, nt, preferred_element_type=F32)
    logits = a[0:ROUTER_ROWS] + a[ROUTER_ROWS:2 * ROUTER_ROWS] + b + br_ref[...]

    gl = [logits[g:g + 1, :] for g in range(N_EXPERT_GROUPS)]
    gmax, gidx = _first_max(gl)
    gsum = gl[0] * 0.0
    for r in gl:
        gsum = gsum + jnp.exp(r - gmax)
    g_val = 1.0 / gsum
    es = []
    for r in range(EXPERTS_PER_GROUP):
        sel = jnp.zeros_like(gmax)
        for g in range(N_EXPERT_GROUPS):
            row = EXPERT_ROW0 + g * EXPERTS_PER_GROUP + r
            sel = jnp.where(gidx == g, logits[row:row + 1, :], sel)
        es.append(sel)
    e1, i1 = _first_max(es)
    rest = [jnp.where(i1 == r, -jnp.inf, es[r]) for r in range(EXPERTS_PER_GROUP)]
    e2, i2 = _first_max(rest)
    r21 = jnp.exp(e2 - e1)
    w1 = g_val / (1.0 + r21)
    w2 = g_val * r21 / (1.0 + r21)

    first = i1 < i2
    e_lo = jnp.where(first, i1, i2)
    e_hi = jnp.where(first, i2, i1)
    pair = jnp.zeros_like(e_lo)
    for p in range(1, PAIRS_PER_GROUP):
        pair = jnp.where(jnp.logical_and(e_lo == PAIR_LO[p], e_hi == PAIR_HI[p]), p, pair)
    bucket = gidx * PAIRS_PER_GROUP + pair
    w_lo = jnp.where(first, w1, w2)
    w_hi = jnp.where(first, w2, w1)

    ts = bucket.shape[1]
    brow = lax.broadcasted_iota(jnp.int32, (ROUTER_ROWS, ts), 0)
    onehot = brow == bucket
    prefix = jnp.dot(jnp.where(onehot, 1.0, 0.0).astype(BF16), tri_ref[...],
                     preferred_element_type=F32)
    carry = carry_ref[...]
    rank = jnp.sum(jnp.where(onehot, prefix + carry, 0.0), axis=0, keepdims=True) - 1.0
    carry = carry + prefix[:, ts - 1:ts]
    carry_ref[...] = carry
    cnt_ref[...] = jnp.broadcast_to(carry, cnt_ref.shape).astype(jnp.int32)
    chunks = ts // LANES
    rank_i = rank.astype(jnp.int32)
    for j in range(chunks):
        route_ref[j:j + 1, :] = bucket[:, j * LANES:(j + 1) * LANES]
        route_ref[chunks + j:chunks + j + 1, :] = rank_i[:, j * LANES:(j + 1) * LANES]
    zero_ref[...] = jnp.zeros(zero_ref.shape, zero_ref.dtype)

    arow = lax.broadcasted_iota(jnp.int32, (LANES, ts), 0)
    aux_t = jnp.where(arow == 0, w_lo, jnp.where(arow == 1, w_hi, 0.0))
    d = h2.shape[1]
    h2p_ref[:, 0:d] = h2
    h2p_ref[:, d:d + LANES] = aux_t.T


def _post(o2, sga, pc, x2, mod3, g2, w_ba, w_out, wg_r, bg_r, we_r, be_r, *, seq, ts, n_rows):
    n, d = x2.shape
    tiles_per_batch = seq // ts
    steps = n // ts
    zero_rows = n_rows // steps
    assert zero_rows * steps == n_rows and zero_rows % SUBLANES == 0, (n_rows, steps)
    wr = jnp.zeros((ROUTER_ROWS, d), F32)
    wr = wr.at[0:N_EXPERT_GROUPS].set(wg_r.T).at[EXPERT_ROW0:EXPERT_ROW0 + N_EXPERTS].set(we_r.T)
    wr_hi = wr.astype(BF16)
    wr_lo = (wr - wr_hi.astype(F32)).astype(BF16)
    br = jnp.zeros((ROUTER_ROWS, 1), F32)
    br = br.at[0:N_EXPERT_GROUPS, 0].set(bg_r).at[EXPERT_ROW0:EXPERT_ROW0 + N_EXPERTS, 0].set(be_r)
    tok = lambda i: (i, 0)
    tidx = jnp.arange(ts)
    tri = (tidx[:, None] <= tidx[None, :]).astype(BF16)
    return pl.pallas_call(
        _post_kernel,
        out_shape=(jax.ShapeDtypeStruct((n, d), F32),
                   jax.ShapeDtypeStruct((n, d + LANES), F32),
                   jax.ShapeDtypeStruct((steps, 2 * (ts // LANES), LANES), jnp.int32),
                   jax.ShapeDtypeStruct((ROUTER_ROWS, LANES), jnp.int32),
                   jax.ShapeDtypeStruct((n_rows, d + LANES), F32)),
        grid=(steps,),
        in_specs=[pl.BlockSpec((ts, D_ATTN), tok),
                  pl.BlockSpec((ts, d), tok),
                  pl.BlockSpec((ts, d), tok),
                  pl.BlockSpec((ts, d), tok),
                  pl.BlockSpec((None, 6, d), lambda i: (i // tiles_per_batch, 0, 0)),
                  _const_spec((1, d)),
                  _const_spec((D_ATTN, d)),
                  _const_spec((d, d)),
                  _const_spec((2 * ROUTER_ROWS, d)),
                  _const_spec((ROUTER_ROWS, 1)),
                  _const_spec((ts, ts))],
        out_specs=(pl.BlockSpec((ts, d), tok), pl.BlockSpec((ts, d + LANES), tok),
                   pl.BlockSpec((None, 2 * (ts // LANES), LANES), lambda i: (i, 0, 0)),
                   pl.BlockSpec((ROUTER_ROWS, LANES), lambda i: (0, 0)),
                   pl.BlockSpec((zero_rows, d + LANES), tok)),
        scratch_shapes=[pltpu.VMEM((ROUTER_ROWS, 1), F32)],
        compiler_params=_cparams(1),
        name="post",
    )(o2, sga, pc, x2, mod3, g2.reshape(1, d), w_ba.astype(BF16), w_out.astype(BF16),
      jnp.concatenate([wr_hi, wr_lo], axis=0), br, tri)


def _group_sublane(row):
    return lax.shift_right_logical(row, SUBLANES.bit_length() - 1), row & (SUBLANES - 1)


def _row_copy(src, src_row, dst, dst_row, sem):
    sg, ss = src_row
    dg, ds = dst_row
    return pltpu.make_async_copy(src.at[sg, pl.ds(ss, 1), :], dst.at[dg, pl.ds(ds, 1), :], sem)


def _dispatch_kernel(pos_ref, h_ref, init_ref, hs_ref, sem, *, ts):
    del init_ref
    base = pl.program_id(0) * ts

    def start(g, carry):
        for u in range(SUBLANES):
            p = pos_ref[base + g * SUBLANES + u]
            _row_copy(h_ref, (g, u), hs_ref, _group_sublane(p), sem).start(priority=u % 2)
        return carry

    def wait(g, carry):
        for u in range(SUBLANES):
            _row_copy(h_ref, (0, 0), hs_ref, (0, 0), sem).wait()
        return carry

    lax.fori_loop(0, ts // SUBLANES, start, 0)
    lax.fori_loop(0, ts // SUBLANES, wait, 0)


def _dispatch(pos, h2p, cleared, *, ts):
    n, w = h2p.shape
    n_rows = cleared.shape[0]
    return pl.pallas_call(
        functools.partial(_dispatch_kernel, ts=ts),
        out_shape=jax.ShapeDtypeStruct((n_rows // SUBLANES, SUBLANES, w), h2p.dtype),
        grid_spec=pltpu.PrefetchScalarGridSpec(
            num_scalar_prefetch=1,
            grid=(n // ts,),
            in_specs=[pl.BlockSpec((ts // SUBLANES, SUBLANES, w), lambda i, pos: (i, 0, 0)),
                      pl.BlockSpec(memory_space=pl.ANY)],
            out_specs=pl.BlockSpec(memory_space=pl.ANY),
            scratch_shapes=[pltpu.SemaphoreType.DMA]),
        input_output_aliases={2: 0},
        compiler_params=_cparams(1),
        name="dispatch",
    )(pos, h2p.reshape(n // SUBLANES, SUBLANES, w),
      cleared.reshape(n_rows // SUBLANES, SUBLANES, w)).reshape(n_rows, w)


def _experts_kernel(ea_ref, eb_ref, valid_ref, hs_ref, wga_ref, wua_ref, wda_ref, wgb_ref, wub_ref,
                    wdb_ref, ys_ref):
    del ea_ref, eb_ref
    t = pl.program_id(0)
    d = hs_ref.shape[1] - LANES

    @pl.when(valid_ref[t] == 1)
    def _():
        h = hs_ref[:, 0:d].astype(BF16)
        aux = hs_ref[:, d:d + LANES]

        def expert(wg_ref, wu_ref, wd_ref):
            a = jnp.dot(h, wg_ref[...].astype(BF16), preferred_element_type=F32)
            b = jnp.dot(h, wu_ref[...].astype(BF16), preferred_element_type=F32)
            hid = (a * jax.nn.sigmoid(a)) * b
            return jnp.dot(hid.astype(BF16), wd_ref[...].astype(BF16), preferred_element_type=F32)

        ys_ref[...] = (aux[:, 0:1] * expert(wga_ref, wua_ref, wda_ref)
                       + aux[:, 1:2] * expert(wgb_ref, wub_ref, wdb_ref))

    @pl.when(valid_ref[t] == 0)
    def _():
        ys_ref[...] = jnp.zeros(ys_ref.shape, ys_ref.dtype)


def _experts(tile_ea, tile_eb, tile_valid, hs, w_gate, w_up, w_down, *, tm):
    n_rows, w = hs.shape
    _, d, de = w_gate.shape
    wg, wu, wd = w_gate, w_up, w_down
    ea = lambda t, ea_r, eb_r, v_r: (ea_r[t], 0, 0)
    eb = lambda t, ea_r, eb_r, v_r: (eb_r[t], 0, 0)
    row = lambda t, ea_r, eb_r, v_r: (t, 0)
    return pl.pallas_call(
        _experts_kernel,
        out_shape=jax.ShapeDtypeStruct((n_rows, d), F32),
        grid_spec=pltpu.PrefetchScalarGridSpec(
            num_scalar_prefetch=3,
            grid=(n_rows // tm,),
            in_specs=[pl.BlockSpec((tm, w), row),
                      pl.BlockSpec((None, d, de), ea), pl.BlockSpec((None, d, de), ea),
                      pl.BlockSpec((None, de, d), ea),
                      pl.BlockSpec((None, d, de), eb), pl.BlockSpec((None, d, de), eb),
                      pl.BlockSpec((None, de, d), eb)],
            out_specs=pl.BlockSpec((tm, d), row)),
        compiler_params=_cparams(1),
        name="experts",
    )(tile_ea, tile_eb, tile_valid, hs, wg, wu, wd, wg, wu, wd)


def _combine_kernel(pos_ref, ys_ref, x1_ref, mod_ref, out_ref, ybuf_ref, sem, *, ts):
    i = pl.program_id(0)
    groups = ts // SUBLANES

    def gather(tile, slot):
        base = tile * ts

        def start(g, carry):
            for u in range(SUBLANES):
                p = pos_ref[base + g * SUBLANES + u]
                _row_copy(ys_ref, _group_sublane(p), ybuf_ref.at[slot], (g, u),
                          sem.at[slot]).start(priority=u % 2)
            return carry

        lax.fori_loop(0, groups, start, 0)

    def wait_all(slot):
        def wait(g, carry):
            for u in range(SUBLANES):
                _row_copy(ys_ref, (0, 0), ybuf_ref.at[slot], (0, 0), sem.at[slot]).wait()
            return carry

        lax.fori_loop(0, groups, wait, 0)

    def step(slot):
        @pl.when(i + 1 < pl.num_programs(0))
        def _():
            gather(i + 1, 1 - slot)

        wait_all(slot)
        y = ybuf_ref[slot].reshape(ts, ybuf_ref.shape[-1])
        out_ref[...] = x1_ref[...] + mod_ref[5:6, :] * y

    @pl.when(i == 0)
    def _():
        gather(0, 0)

    @pl.when(i % 2 == 0)
    def _():
        step(0)

    @pl.when(i % 2 == 1)
    def _():
        step(1)


def _combine(pos, ys, x1, mod3, *, seq, ts):
    n, d = x1.shape
    tiles_per_batch = seq // ts
    return pl.pallas_call(
        functools.partial(_combine_kernel, ts=ts),
        out_shape=jax.ShapeDtypeStruct((n, d), F32),
        grid_spec=pltpu.PrefetchScalarGridSpec(
            num_scalar_prefetch=1,
            grid=(n // ts,),
            in_specs=[pl.BlockSpec(memory_space=pl.ANY),
                      pl.BlockSpec((ts, d), lambda i, pos: (i, 0)),
                      pl.BlockSpec((None, 6, d), lambda i, pos: (i // tiles_per_batch, 0, 0))],
            out_specs=pl.BlockSpec((ts, d), lambda i, pos: (i, 0)),
            scratch_shapes=[pltpu.VMEM((2, ts // SUBLANES, SUBLANES, d), F32),
                            pltpu.SemaphoreType.DMA((2,))]),
        compiler_params=_cparams(1),
        name="combine",
    )(pos, ys.reshape(ys.shape[0] // SUBLANES, SUBLANES, d), x1, mod3)


def _pos_kernel(start_ref, route_ref, pos_ref):
    chunks = pos_ref.shape[1]
    bucket = route_ref[:, 0:chunks, :]
    pos = route_ref[:, chunks:2 * chunks, :]
    for b in range(N_ROUTE_BUCKETS):
        pos = pos + jnp.where(bucket == b, start_ref[b], 0)
    pos_ref[...] = pos


def _route_tables(route, counts, n_tiles, tm):
    steps, rows, _ = route.shape
    cnt = counts[:N_ROUTE_BUCKETS, 0]
    padded = (cnt + tm - 1) // tm * tm
    end = jnp.cumsum(padded)
    pos = pl.pallas_call(
        _pos_kernel,
        out_shape=jax.ShapeDtypeStruct((steps, rows // 2, LANES), jnp.int32),
        in_specs=[pl.BlockSpec(memory_space=pltpu.SMEM), pl.BlockSpec(memory_space=pltpu.VMEM)],
        out_specs=pl.BlockSpec(memory_space=pltpu.VMEM),
        name="pos",
    )(end - padded, route).reshape(-1)
    tiles_used = end[-1] // tm
    tile = jnp.arange(n_tiles, dtype=jnp.int32)
    valid = tile < tiles_used
    first_row = jnp.minimum(tile, tiles_used - 1) * tm
    tile_bucket = jnp.sum((end[None, :] <= first_row[:, None]).astype(jnp.int32), axis=1)
    tile_bucket = jnp.minimum(tile_bucket, N_ROUTE_BUCKETS - 1)
    group = tile_bucket // PAIRS_PER_GROUP
    pair = tile_bucket % PAIRS_PER_GROUP
    tile_ea = group * EXPERTS_PER_GROUP + jnp.asarray(PAIR_LO, jnp.int32)[pair]
    tile_eb = group * EXPERTS_PER_GROUP + jnp.asarray(PAIR_HI, jnp.int32)[pair]
    return pos.astype(jnp.int32), tile_ea, tile_eb, valid.astype(jnp.int32)


def _tile(seq, pref):
    t = min(pref, seq)
    assert seq % t == 0 and t % LANES == 0, (seq, t)
    return t


def kernel(x, c, rel_bias, ada_w, ada_b, norm1_g, w_in, q_norm_g, k_norm_g, lambda_q1, lambda_k1,
           lambda_q2, lambda_k2, subln_g, w_branch_attn, pool_w, pool_scale, w_branch_pool, w_out,
           norm2_g, router_group_w, router_group_b, router_expert_w, router_expert_b,
           expert_w_gate, expert_w_up, expert_w_down):
    bsz, seq, d = x.shape
    n = bsz * seq
    ts = _tile(seq, TOKEN_TILE)
    t_attn = _tile(seq, ATTN_TILE)
    tr = _tile(seq, ROW_COPY_TILE)
    tm = MOE_TILE
    n_tiles = -(-(n + N_ROUTE_BUCKETS * (tm - 1)) // tm)
    while (n_tiles * tm) % (n // ts * SUBLANES):
        n_tiles += 1
    bias = _bias_tiles(rel_bias, t_attn)
    x2 = x.reshape(n, d)
    for l in range(ada_w.shape[0]):
        lambda_init = 0.8 - 0.6 * math.exp(-0.3 * l)
        mod3 = _ada(c, ada_w[l], ada_b[l]).reshape(bsz, 6, d)
        gq = jnp.tile(q_norm_g[l], D_ATTN // HEAD_DIM) * (HEAD_DIM ** -0.5 * LOG2E)
        gk = jnp.tile(k_norm_g[l], D_ATTN // HEAD_DIM)
        q, k, vt, sga, pc = _inproj(x2, mod3, norm1_g[l], w_in[l], gq, gk, pool_w[l], pool_scale[l],
                                    w_branch_pool[l], bsz=bsz, seq=seq, ts=ts)
        o = _attention(q.reshape(bsz, seq, D_ATTN), k.reshape(bsz, seq, D_ATTN), vt, bias,
                       lambda_q1[l], lambda_k1[l], lambda_q2[l], lambda_k2[l], subln_g[l],
                       t=t_attn, lambda_init=lambda_init)
        x1, h2p, route, counts, cleared = _post(
            o.reshape(n, D_ATTN), sga, pc, x2, mod3, norm2_g[l], w_branch_attn[l], w_out[l],
            router_group_w[l], router_group_b[l], router_expert_w[l], router_expert_b[l],
            seq=seq, ts=ts, n_rows=n_tiles * tm)
        pos, tile_ea, tile_eb, tile_valid = _route_tables(route, counts, n_tiles, tm)
        hs = _dispatch(pos, h2p, cleared, ts=tr)
        ys = _experts(tile_ea, tile_eb, tile_valid, hs, expert_w_gate[l], expert_w_up[l],
                      expert_w_down[l], tm=tm)
        x2 = _combine(pos, ys, x1, mod3, seq=seq, ts=tr)
    return x2.reshape(bsz, seq, d)
```

```python
import functools
import math

import jax
import jax.numpy as jnp
from jax import lax
from jax.experimental import pallas as pl
from jax.experimental.pallas import tpu as pltpu

F32 = jnp.float32
BF16 = jnp.bfloat16

CHUNK = 64
N_HEADS = 4
HEAD_DIM = 64
D_HEAD_V = 2 * HEAD_DIM
V_ROWS = D_HEAD_V + 16
D_ATTN = N_HEADS * D_HEAD_V
POOL_WINDOWS = (2, 4, 8, 16)
POOL_GROUP_DIM = 128
D_POOL = len(POOL_WINDOWS) * POOL_GROUP_DIM
POOL_HALO = 16
N_BUCKETS = 32
MAX_DISTANCE = 128
N_EXPERT_GROUPS = 4
EXPERTS_PER_GROUP = 4
N_EXPERTS = N_EXPERT_GROUPS * EXPERTS_PER_GROUP
PAIRS_PER_GROUP = 6
PAIR_LO = (0, 0, 1, 1, 0, 2)
PAIR_HI = (1, 2, 2, 3, 3, 3)
N_ROUTE_BUCKETS = N_EXPERT_GROUPS * PAIRS_PER_GROUP
RMS_EPS = 1e-6
LOG2E = math.log2(math.e)
MASKED = -1e30

LANES = 128
SUBLANES = 8
ROUTER_ROWS = 32
EXPERT_ROW0 = 8

VMEM_LIMIT = 56 * 1024 * 1024
TOKEN_TILE = 512
ATTN_TILE = 512
MOE_TILE = 256
ROW_COPY_TILE = 1024


def _cparams(n_axes):
    return pltpu.CompilerParams(dimension_semantics=("arbitrary",) * n_axes,
                                vmem_limit_bytes=VMEM_LIMIT)


def _const_spec(shape):
    nd = len(shape)
    return pl.BlockSpec(shape, lambda *_: (0,) * nd, pipeline_mode=pl.Buffered(1))


def _ada_kernel(ct_ref, w_ref, b_ref, o_ref):
    @pl.when(pl.program_id(0) == 0)
    def _():
        o_ref[...] = jnp.broadcast_to(b_ref[...], o_ref.shape)

    ct = ct_ref[...]
    s = ct * jax.nn.sigmoid(ct)
    w = w_ref[...]
    rows = [jnp.sum(w * s[:, b:b + 1], axis=0, keepdims=True) for b in range(ct.shape[1])]
    o_ref[...] += jnp.concatenate(rows, axis=0)


def _ada(c, w, b):
    bsz, d = c.shape
    n = w.shape[1]
    rows = 256
    return pl.pallas_call(
        _ada_kernel,
        out_shape=jax.ShapeDtypeStruct((bsz, n), F32),
        grid=(d // rows,),
        in_specs=[pl.BlockSpec((rows, bsz), lambda j: (j, 0)),
                  pl.BlockSpec((rows, n), lambda j: (j, 0)),
                  pl.BlockSpec((1, n), lambda j: (0, 0))],
        out_specs=pl.BlockSpec((bsz, n), lambda j: (0, 0)),
        compiler_params=_cparams(1),
        name="ada",
    )(c.T, w, b.reshape(1, n))


def _log_bucket_starts():
    nb = N_BUCKETS // 2
    max_exact = nb // 2
    m = nb - max_exact
    ratio = MAX_DISTANCE // max_exact
    starts = []
    for k in range(1, m):
        n = max_exact
        while n ** m < max_exact ** m * ratio ** k:
            n += 1
        starts.append(n)
    return tuple(starts)


LOG_BUCKET_STARTS = _log_bucket_starts()


def _bias_kernel(rb_ref, o_ref, *, t):
    h = pl.program_id(0)
    kind = pl.program_id(1)
    nb = N_BUCKETS // 2
    max_exact = nb // 2
    kpos = lax.broadcasted_iota(jnp.int32, (t, t), 0)
    qpos = lax.broadcasted_iota(jnp.int32, (t, t), 1)
    rel = kpos - qpos - jnp.where(kind == 0, t, 0)
    n = jnp.abs(rel)

    def table(first):
        val = jnp.full((t, t), rb_ref[first + nb - 1, h], F32)
        for k in range(len(LOG_BUCKET_STARTS) - 1, -1, -1):
            val = jnp.where(n < LOG_BUCKET_STARTS[k], rb_ref[first + max_exact + k, h], val)
        for j in range(max_exact - 1, -1, -1):
            val = jnp.where(n == j, rb_ref[first + j, h], val)
        return val

    bias = jnp.where(rel > 0, table(nb), table(0))
    shift = CHUNK.bit_length() - 1
    hidden = jnp.logical_and(kind == 1, (kpos >> shift) > (qpos >> shift))
    o_ref[...] = jnp.where(hidden, MASKED, (bias - rb_ref[nb - 1, h]) * LOG2E)


def _bias_tiles(rel_bias, t):
    return pl.pallas_call(
        functools.partial(_bias_kernel, t=t),
        out_shape=jax.ShapeDtypeStruct((N_HEADS, 2, t, t), F32),
        grid=(N_HEADS, 2),
        in_specs=[pl.BlockSpec(memory_space=pltpu.SMEM)],
        out_specs=pl.BlockSpec((None, None, t, t), lambda h, j: (h, j, 0, 0)),
        compiler_params=_cparams(2),
        name="bias_tiles",
    )(rel_bias)


def _group_rms(xc, ones_blockdiag, gain):
    ssq = jnp.dot((xc * xc).astype(BF16), ones_blockdiag, preferred_element_type=F32)
    return xc * lax.rsqrt(ssq * (1.0 / HEAD_DIM) + RMS_EPS) * gain


def _split_bf16(a):
    hi = a.astype(BF16)
    return hi, (a - hi.astype(F32)).astype(BF16)


def _pool_fold_kernel(pw_ref, ps_ref, wbp_ref, o_ref):
    a_hi, a_lo = _split_bf16(pw_ref[...] * ps_ref[...])
    b_hi, b_lo = _split_bf16(wbp_ref[...])
    dot = functools.partial(jnp.dot, preferred_element_type=F32)
    o_ref[...] = (dot(a_hi, b_hi) + dot(a_hi, b_lo) + dot(a_lo, b_hi)).astype(o_ref.dtype)


def _pool_fold(pool_w, pool_scale, w_bp):
    g, c, _ = pool_w.shape
    d = w_bp.shape[1]
    return pl.pallas_call(
        _pool_fold_kernel,
        out_shape=jax.ShapeDtypeStruct((g * c, d), BF16),
        grid=(g,),
        in_specs=[pl.BlockSpec((None, c, c), lambda i: (i, 0, 0)),
                  pl.BlockSpec((None, 1, c), lambda i: (i, 0, 0)),
                  pl.BlockSpec((c, d), lambda i: (i, 0))],
        out_specs=pl.BlockSpec((c, d), lambda i: (i, 0)),
        compiler_params=_cparams(1),
        name="pool_fold",
    )(pool_w, pool_scale.reshape(g, 1, c), w_bp)


def _inproj_kernel(x_ref, mod_ref, g1_ref, win_ref, gq_ref, gk_ref, ones_ref, wpool_ref,
                   q_ref, k_ref, vt_ref, sga_ref, pc_ref, ext_ref, v_ref,
                   *, ts, tiles_per_batch):
    tb = pl.program_id(0) % tiles_per_batch
    x = x_ref[...]
    y = x * lax.rsqrt(jnp.mean(x * x, axis=-1, keepdims=True) + RMS_EPS)
    h = y * (g1_ref[...] * (1.0 + mod_ref[1:2, :])) + mod_ref[0:1, :]
    hb = h.astype(BF16)

    def proj(c0, c1):
        return jnp.dot(hb, win_ref[:, c0:c1], preferred_element_type=F32)

    ones_bd = ones_ref[...]
    q_ref[...] = _group_rms(proj(0, D_ATTN), ones_bd, gq_ref[...]).astype(BF16)
    k_ref[...] = _group_rms(proj(D_ATTN, 2 * D_ATTN), ones_bd, gk_ref[...]).astype(BF16)
    v_ref[...] = proj(2 * D_ATTN, 3 * D_ATTN)
    vt = v_ref[...].T.astype(BF16)
    for hd in range(N_HEADS):
        vt_ref[hd, 0:D_HEAD_V, :] = vt[hd * D_HEAD_V:(hd + 1) * D_HEAD_V, :]
        vt_ref[hd, D_HEAD_V:V_ROWS, :] = jnp.ones((V_ROWS - D_HEAD_V, ts), BF16)
    c_u = 3 * D_ATTN
    c_ga = c_u + D_POOL
    c_gp = c_ga + x.shape[1]
    sga_ref[...] = jax.nn.sigmoid(proj(c_ga, c_gp)).astype(BF16)

    u = proj(c_u, c_ga)

    @pl.when(tb == 0)
    def _():
        ext_ref[0:POOL_HALO, :] = jnp.zeros((POOL_HALO, D_POOL), F32)

    ext_ref[POOL_HALO:POOL_HALO + ts, :] = u
    row = lax.broadcasted_iota(jnp.int32, (ts, 1), 0) + tb * ts
    ys = []
    for g, w in enumerate(POOL_WINDOWS):
        c0 = g * POOL_GROUP_DIM
        ug = u[:, c0:c0 + POOL_GROUP_DIM]
        acc = ug
        for d in range(1, w):
            acc = acc + ext_ref[POOL_HALO - d:POOL_HALO - d + ts, c0:c0 + POOL_GROUP_DIM]
        cnt = jnp.minimum(row + 1, w).astype(F32)
        ys.append((acc / cnt - ug).astype(BF16))
    ypool = jnp.dot(jnp.concatenate(ys, axis=1), wpool_ref[...], preferred_element_type=F32)
    pc_ref[...] = (jax.nn.sigmoid(proj(c_gp, c_gp + x.shape[1])) * ypool).astype(BF16)
    ext_ref[0:POOL_HALO, :] = u[ts - POOL_HALO:ts, :]


def _inproj(x2, mod3, g1, w_in, gq, gk, pool_w, pool_scale, w_bp, *, bsz, seq, ts):
    n, d = x2.shape
    d_in = w_in.shape[1]
    tiles_per_batch = seq // ts
    win_b = w_in.astype(BF16)
    idx = jnp.arange(D_ATTN) // HEAD_DIM
    ones_bd = (idx[:, None] == idx[None, :]).astype(BF16)
    kern = functools.partial(_inproj_kernel, ts=ts, tiles_per_batch=tiles_per_batch)
    tok = lambda i: (i, 0)
    out_shape = (jax.ShapeDtypeStruct((n, D_ATTN), BF16),
                 jax.ShapeDtypeStruct((n, D_ATTN), BF16),
                 jax.ShapeDtypeStruct((bsz, N_HEADS, V_ROWS, seq), BF16),
                 jax.ShapeDtypeStruct((n, d), BF16),
                 jax.ShapeDtypeStruct((n, d), BF16))
    return pl.pallas_call(
        kern,
        out_shape=out_shape,
        grid=(n // ts,),
        in_specs=[pl.BlockSpec((ts, d), tok),
                  pl.BlockSpec((None, 6, d), lambda i: (i // tiles_per_batch, 0, 0)),
                  _const_spec((1, d)),
                  _const_spec((d, d_in)),
                  _const_spec((1, D_ATTN)),
                  _const_spec((1, D_ATTN)),
                  _const_spec((D_ATTN, D_ATTN)),
                  _const_spec((D_POOL, d))],
        out_specs=(pl.BlockSpec((ts, D_ATTN), tok),
                   pl.BlockSpec((ts, D_ATTN), tok),
                   pl.BlockSpec((None, N_HEADS, V_ROWS, ts),
                                lambda i: (i // tiles_per_batch, 0, 0, i % tiles_per_batch)),
                   pl.BlockSpec((ts, d), tok),
                   pl.BlockSpec((ts, d), tok)),
        scratch_shapes=[pltpu.VMEM((POOL_HALO + ts, D_POOL), F32), pltpu.VMEM((ts, D_ATTN), F32)],
        compiler_params=_cparams(1),
        name="inproj",
    )(x2, mod3, g1.reshape(1, d), win_b, gq.reshape(1, D_ATTN), gk.reshape(1, D_ATTN), ones_bd,
      _pool_fold(pool_w, pool_scale, w_bp))


def _attn_kernel(q_ref, qn_ref, k_ref, vt_ref, bias_ref, lq1_ref, lk1_ref, lq2_ref, lk2_ref, subg_ref,
                 o_ref, clear_ref, s_ref, mt_ref, m_ref, acc_ref, *, t, lambda_init):
    i = pl.program_id(2)
    last = pl.num_programs(2) - 1
    clear_ref[...] = jnp.zeros(clear_ref.shape, clear_ref.dtype)

    def split_maps(q):
        lane = lax.broadcasted_iota(jnp.int32, q.shape, 1)
        zero = jnp.zeros_like(q)
        return jnp.where(lane < HEAD_DIM, q, zero), jnp.where(lane >= HEAD_DIM, q, zero)

    q_now = split_maps(q_ref[...])

    m_ref[...] = jnp.full(m_ref.shape, MASKED, F32)
    acc_ref[...] = jnp.zeros(acc_ref.shape, F32)

    def scores(j, bias, slot, qm=q_now):
        kt = k_ref[pl.ds(pl.multiple_of(j * t, t), t), :]
        for mp in range(2):
            s = lax.dot_general(kt, qm[mp], (((1,), (1,)), ((), ())), preferred_element_type=F32)
            if bias is not None:
                s = s + bias
            s_ref[slot, mp] = s
            mt_ref[slot, mp] = jnp.max(s, axis=0, keepdims=True)

    def accumulate(j, slot):
        vt = vt_ref[:, pl.ds(pl.multiple_of(j * t, t), t)]
        for mp in range(2):
            m_old = m_ref[mp]
            m_new = jnp.maximum(m_old, mt_ref[slot, mp])
            p = jnp.exp2(s_ref[slot, mp] - m_new).astype(BF16)
            acc_ref[mp] = (jnp.exp2(m_old - m_new) * acc_ref[mp]
                           + jnp.dot(vt, p, preferred_element_type=F32))
            m_ref[mp] = m_new

    def next_diagonal():
        nxt = jnp.minimum(i + 1, last)
        scores(nxt, bias_ref[1], 2, split_maps(qn_ref[...]))

    @pl.when(i == 0)
    def _():
        scores(0, bias_ref[1], 0)

    @pl.when(i >= 1)
    def _():
        scores(i - 1, bias_ref[0], 1)
        accumulate(i, 2)

    n_far = jnp.maximum(i - 1, 0)

    def pair_body(kk, carry):
        j = i - 1 - 2 * kk
        scores(j - 1, None, 0)
        accumulate(j, 1)
        scores(j - 2, None, 1)
        accumulate(j - 1, 0)
        return carry

    lax.fori_loop(0, n_far // 2, pair_body, 0)

    @pl.when(n_far % 2 == 1)
    def _():
        scores(0, None, 0)
        accumulate(1, 1)
        next_diagonal()
        accumulate(0, 0)

    @pl.when(jnp.logical_and(i >= 1, n_far % 2 == 0))
    def _():
        next_diagonal()
        accumulate(0, 1)

    @pl.when(i == 0)
    def _():
        next_diagonal()
        accumulate(0, 0)

    lam = (jnp.exp(jnp.sum(lq1_ref[...] * lk1_ref[...], axis=1, keepdims=True))
           - jnp.exp(jnp.sum(lq2_ref[...] * lk2_ref[...], axis=1, keepdims=True)) + lambda_init)
    o1 = acc_ref[0, 0:D_HEAD_V, :] / acc_ref[0, D_HEAD_V:D_HEAD_V + 1, :]
    o2 = acc_ref[1, 0:D_HEAD_V, :] / acc_ref[1, D_HEAD_V:D_HEAD_V + 1, :]
    ot = o1 - lam * o2
    ot = ot * lax.rsqrt(jnp.mean(ot * ot, axis=0, keepdims=True) + RMS_EPS)
    ot = ot * subg_ref[...] * (1.0 - lambda_init)
    o_ref[...] = ot.T.astype(BF16)


def _attention(q, k, vt, bias, lq1, lk1, lq2, lk2, subln_g, *, t, lambda_init, clear_shape):
    bsz, seq, _ = q.shape
    nq = seq // t
    assert t + 1 >= LOG_BUCKET_STARTS[-1], "keys two tiles back must all fall in the last distance bucket"
    steps = bsz * N_HEADS * nq
    clear_rows = clear_shape[0] // steps
    assert clear_rows * steps == clear_shape[0] and clear_rows % SUBLANES == 0, (clear_shape, steps)
    kern = functools.partial(_attn_kernel, t=t, lambda_init=lambda_init)
    vec = lambda a: a.reshape(1, HEAD_DIM)
    return pl.pallas_call(
        kern,
        out_shape=(jax.ShapeDtypeStruct((bsz, seq, D_ATTN), BF16),
                   jax.ShapeDtypeStruct(clear_shape, F32)),
        grid=(bsz, N_HEADS, seq // t),
        in_specs=[pl.BlockSpec((None, t, D_HEAD_V), lambda b, h, i: (b, i, h)),
                  pl.BlockSpec((None, t, D_HEAD_V), lambda b, h, i: (b, jnp.minimum(i + 1, nq - 1), h)),
                  pl.BlockSpec((None, seq, D_HEAD_V), lambda b, h, i: (b, 0, h)),
                  pl.BlockSpec((None, None, V_ROWS, seq), lambda b, h, i: (b, h, 0, 0)),
                  pl.BlockSpec((None, 2, t, t), lambda b, h, i: (h, 0, 0, 0)),
                  _const_spec((1, HEAD_DIM)), _const_spec((1, HEAD_DIM)),
                  _const_spec((1, HEAD_DIM)), _const_spec((1, HEAD_DIM)),
                  _const_spec((D_HEAD_V, 1))],
        out_specs=(pl.BlockSpec((None, t, D_HEAD_V), lambda b, h, i: (b, i, h)),
                   pl.BlockSpec((clear_rows, clear_shape[1]),
                                lambda b, h, i: ((b * N_HEADS + h) * nq + i, 0))),
        scratch_shapes=[pltpu.VMEM((3, 2, t, t), F32),
                        pltpu.VMEM((3, 2, 1, t), F32),
                        pltpu.VMEM((2, 1, t), F32),
                        pltpu.VMEM((2, V_ROWS, t), F32)],
        compiler_params=_cparams(3),
        name="attn",
    )(q, q, k, vt, bias, vec(lq1), vec(lk1), vec(lq2), vec(lk2), subln_g.reshape(D_HEAD_V, 1))


def _first_max(rows):
    best = rows[0]
    for r in rows[1:]:
        best = jnp.maximum(best, r)
    idx = jnp.full(best.shape, len(rows) - 1, jnp.int32)
    for j in range(len(rows) - 2, -1, -1):
        idx = jnp.where(rows[j] == best, j, idx)
    return best, idx


def _post_kernel(o_ref, sga_ref, pc_ref, x_ref, mod_ref, g2_ref, wba_ref, wout_ref, wr_ref, br_ref,
                 tri_ref, x1_ref, h2p_ref, route_ref, cnt_ref, carry_ref):
    @pl.when(pl.program_id(0) == 0)
    def _():
        carry_ref[...] = jnp.zeros(carry_ref.shape, F32)

    ya = jnp.dot(o_ref[...], wba_ref[...], preferred_element_type=F32)
    merged = sga_ref[...].astype(F32) * ya + pc_ref[...].astype(F32)
    z = jnp.dot(merged.astype(BF16), wout_ref[...], preferred_element_type=F32)
    x1 = x_ref[...] + mod_ref[2:3, :] * z
    x1_ref[...] = x1
    y = x1 * lax.rsqrt(jnp.mean(x1 * x1, axis=-1, keepdims=True) + RMS_EPS)
    h2 = y * (g2_ref[...] * (1.0 + mod_ref[4:5, :])) + mod_ref[3:4, :]
    hi = h2.astype(BF16)
    lo = (h2 - hi.astype(F32)).astype(BF16)

    nt = (((1,), (1,)), ((), ()))
    a = lax.dot_general(wr_ref[...], hi, nt, preferred_element_type=F32)
    b = lax.dot_general(wr_ref[0:ROUTER_ROWS, :], lo, nt, preferred_element_type=F32)
    logits = a[0:ROUTER_ROWS] + a[ROUTER_ROWS:2 * ROUTER_ROWS] + b + br_ref[...]

    gl = [logits[g:g + 1, :] for g in range(N_EXPERT_GROUPS)]
    gmax, gidx = _first_max(gl)
    gsum = gl[0] * 0.0
    for r in gl:
        gsum = gsum + jnp.exp(r - gmax)
    g_val = 1.0 / gsum
    es = []
    for r in range(EXPERTS_PER_GROUP):
        sel = jnp.zeros_like(gmax)
        for g in range(N_EXPERT_GROUPS):
            row = EXPERT_ROW0 + g * EXPERTS_PER_GROUP + r
            sel = jnp.where(gidx == g, logits[row:row + 1, :], sel)
        es.append(sel)
    e1, i1 = _first_max(es)
    rest = [jnp.where(i1 == r, -jnp.inf, es[r]) for r in range(EXPERTS_PER_GROUP)]
    e2, i2 = _first_max(rest)
    r21 = jnp.exp(e2 - e1)
    w1 = g_val / (1.0 + r21)
    w2 = g_val * r21 / (1.0 + r21)

    first = i1 < i2
    e_lo = jnp.where(first, i1, i2)
    e_hi = jnp.where(first, i2, i1)
    pair = jnp.zeros_like(e_lo)
    for p in range(1, PAIRS_PER_GROUP):
        pair = jnp.where(jnp.logical_and(e_lo == PAIR_LO[p], e_hi == PAIR_HI[p]), p, pair)
    bucket = gidx * PAIRS_PER_GROUP + pair
    w_lo = jnp.where(first, w1, w2)
    w_hi = jnp.where(first, w2, w1)

    ts = bucket.shape[1]
    brow = lax.broadcasted_iota(jnp.int32, (ROUTER_ROWS, ts), 0)
    onehot = brow == bucket
    prefix = jnp.dot(jnp.where(onehot, 1.0, 0.0).astype(BF16), tri_ref[...],
                     preferred_element_type=F32)
    carry = carry_ref[...]
    rank = jnp.sum(jnp.where(onehot, prefix + carry, 0.0), axis=0, keepdims=True) - 1.0
    carry = carry + prefix[:, ts - 1:ts]
    carry_ref[...] = carry
    cnt_ref[...] = jnp.broadcast_to(carry, cnt_ref.shape).astype(jnp.int32)
    chunks = ts // LANES
    rank_i = rank.astype(jnp.int32)
    for j in range(chunks):
        route_ref[j:j + 1, :] = bucket[:, j * LANES:(j + 1) * LANES]
        route_ref[chunks + j:chunks + j + 1, :] = rank_i[:, j * LANES:(j + 1) * LANES]

    arow = lax.broadcasted_iota(jnp.int32, (LANES, ts), 0)
    aux_t = jnp.where(arow == 0, w_lo, jnp.where(arow == 1, w_hi, 0.0))
    d = h2.shape[1]
    h2p_ref[:, 0:d] = h2
    h2p_ref[:, d:d + LANES] = aux_t.T


def _post(o2, sga, pc, x2, mod3, g2, w_ba, w_out, wg_r, bg_r, we_r, be_r, *, seq, ts):
    n, d = x2.shape
    tiles_per_batch = seq // ts
    steps = n // ts
    wr = jnp.zeros((ROUTER_ROWS, d), F32)
    wr = wr.at[0:N_EXPERT_GROUPS].set(wg_r.T).at[EXPERT_ROW0:EXPERT_ROW0 + N_EXPERTS].set(we_r.T)
    wr_hi = wr.astype(BF16)
    wr_lo = (wr - wr_hi.astype(F32)).astype(BF16)
    br = jnp.zeros((ROUTER_ROWS, 1), F32)
    br = br.at[0:N_EXPERT_GROUPS, 0].set(bg_r).at[EXPERT_ROW0:EXPERT_ROW0 + N_EXPERTS, 0].set(be_r)
    tok = lambda i: (i, 0)
    tidx = jnp.arange(ts)
    tri = (tidx[:, None] <= tidx[None, :]).astype(BF16)
    return pl.pallas_call(
        _post_kernel,
        out_shape=(jax.ShapeDtypeStruct((n, d), F32),
                   jax.ShapeDtypeStruct((n, d + LANES), F32),
                   jax.ShapeDtypeStruct((steps, 2 * (ts // LANES), LANES), jnp.int32),
                   jax.ShapeDtypeStruct((ROUTER_ROWS, LANES), jnp.int32)),
        grid=(steps,),
        in_specs=[pl.BlockSpec((ts, D_ATTN), tok),
                  pl.BlockSpec((ts, d), tok),
                  pl.BlockSpec((ts, d), tok),
                  pl.BlockSpec((ts, d), tok),
                  pl.BlockSpec((None, 6, d), lambda i: (i // tiles_per_batch, 0, 0)),
                  _const_spec((1, d)),
                  _const_spec((D_ATTN, d)),
                  _const_spec((d, d)),
                  _const_spec((2 * ROUTER_ROWS, d)),
                  _const_spec((ROUTER_ROWS, 1)),
                  _const_spec((ts, ts))],
        out_specs=(pl.BlockSpec((ts, d), tok), pl.BlockSpec((ts, d + LANES), tok),
                   pl.BlockSpec((None, 2 * (ts // LANES), LANES), lambda i: (i, 0, 0)),
                   pl.BlockSpec((ROUTER_ROWS, LANES), lambda i: (0, 0))),
        scratch_shapes=[pltpu.VMEM((ROUTER_ROWS, 1), F32)],
        compiler_params=_cparams(1),
        name="post",
    )(o2, sga, pc, x2, mod3, g2.reshape(1, d), w_ba.astype(BF16), w_out.astype(BF16),
      jnp.concatenate([wr_hi, wr_lo], axis=0), br, tri)


def _group_sublane(row):
    return lax.shift_right_logical(row, SUBLANES.bit_length() - 1), row & (SUBLANES - 1)


def _row_copy(src, src_row, dst, dst_row, sem):
    sg, ss = src_row
    dg, ds = dst_row
    return pltpu.make_async_copy(src.at[sg, pl.ds(ss, 1), :], dst.at[dg, pl.ds(ds, 1), :], sem)


def _dispatch_kernel(pos_ref, h_ref, init_ref, hs_ref, sem, *, ts):
    del init_ref
    base = pl.program_id(0) * ts

    def start(g, carry):
        for u in range(SUBLANES):
            p = pos_ref[base + g * SUBLANES + u]
            _row_copy(h_ref, (g, u), hs_ref, _group_sublane(p), sem).start(priority=u % 2)
        return carry

    def wait(g, carry):
        for u in range(SUBLANES):
            _row_copy(h_ref, (0, 0), hs_ref, (0, 0), sem).wait()
        return carry

    lax.fori_loop(0, ts // SUBLANES, start, 0)
    lax.fori_loop(0, ts // SUBLANES, wait, 0)


def _dispatch(pos, h2p, cleared, *, ts):
    n, w = h2p.shape
    n_rows = cleared.shape[0]
    return pl.pallas_call(
        functools.partial(_dispatch_kernel, ts=ts),
        out_shape=jax.ShapeDtypeStruct((n_rows // SUBLANES, SUBLANES, w), h2p.dtype),
        grid_spec=pltpu.PrefetchScalarGridSpec(
            num_scalar_prefetch=1,
            grid=(n // ts,),
            in_specs=[pl.BlockSpec((ts // SUBLANES, SUBLANES, w), lambda i, pos: (i, 0, 0)),
                      pl.BlockSpec(memory_space=pl.ANY)],
            out_specs=pl.BlockSpec(memory_space=pl.ANY),
            scratch_shapes=[pltpu.SemaphoreType.DMA]),
        input_output_aliases={2: 0},
        compiler_params=_cparams(1),
        name="dispatch",
    )(pos, h2p.reshape(n // SUBLANES, SUBLANES, w),
      cleared.reshape(n_rows // SUBLANES, SUBLANES, w)).reshape(n_rows, w)


def _experts_kernel(ea_ref, eb_ref, valid_ref, hs_ref, wga_ref, wua_ref, wda_ref, wgb_ref, wub_ref,
                    wdb_ref, ys_ref):
    del ea_ref, eb_ref
    t = pl.program_id(0)
    d = hs_ref.shape[1] - LANES

    @pl.when(valid_ref[t] == 1)
    def _():
        h = hs_ref[:, 0:d].astype(BF16)
        aux = hs_ref[:, d:d + LANES]

        def expert(wg_ref, wu_ref, wd_ref):
            a = jnp.dot(h, wg_ref[...].astype(BF16), preferred_element_type=F32)
            b = jnp.dot(h, wu_ref[...].astype(BF16), preferred_element_type=F32)
            hid = (a * jax.nn.sigmoid(a)) * b
            return jnp.dot(hid.astype(BF16), wd_ref[...].astype(BF16), preferred_element_type=F32)

        ys_ref[...] = (aux[:, 0:1] * expert(wga_ref, wua_ref, wda_ref)
                       + aux[:, 1:2] * expert(wgb_ref, wub_ref, wdb_ref))

    @pl.when(valid_ref[t] == 0)
    def _():
        ys_ref[...] = jnp.zeros(ys_ref.shape, ys_ref.dtype)


def _experts(tile_ea, tile_eb, tile_valid, hs, w_gate, w_up, w_down, *, tm):
    n_rows, w = hs.shape
    _, d, de = w_gate.shape
    wg, wu, wd = w_gate, w_up, w_down
    ea = lambda t, ea_r, eb_r, v_r: (ea_r[t], 0, 0)
    eb = lambda t, ea_r, eb_r, v_r: (eb_r[t], 0, 0)
    row = lambda t, ea_r, eb_r, v_r: (t, 0)
    return pl.pallas_call(
        _experts_kernel,
        out_shape=jax.ShapeDtypeStruct((n_rows, d), F32),
        grid_spec=pltpu.PrefetchScalarGridSpec(
            num_scalar_prefetch=3,
            grid=(n_rows // tm,),
            in_specs=[pl.BlockSpec((tm, w), row),
                      pl.BlockSpec((None, d, de), ea), pl.BlockSpec((None, d, de), ea),
                      pl.BlockSpec((None, de, d), ea),
                      pl.BlockSpec((None, d, de), eb), pl.BlockSpec((None, d, de), eb),
                      pl.BlockSpec((None, de, d), eb)],
            out_specs=pl.BlockSpec((tm, d), row)),
        compiler_params=_cparams(1),
        name="experts",
    )(tile_ea, tile_eb, tile_valid, hs, wg, wu, wd, wg, wu, wd)


def _combine_kernel(pos_ref, ys_ref, x1_ref, mod_ref, out_ref, ybuf_ref, sem, *, ts):
    i = pl.program_id(0)
    groups = ts // SUBLANES

    def gather(tile, slot):
        base = tile * ts

        def start(g, carry):
            for u in range(SUBLANES):
                p = pos_ref[base + g * SUBLANES + u]
                _row_copy(ys_ref, _group_sublane(p), ybuf_ref.at[slot], (g, u),
                          sem.at[slot]).start(priority=u % 2)
            return carry

        lax.fori_loop(0, groups, start, 0)

    def wait_all(slot):
        def wait(g, carry):
            for u in range(SUBLANES):
                _row_copy(ys_ref, (0, 0), ybuf_ref.at[slot], (0, 0), sem.at[slot]).wait()
            return carry

        lax.fori_loop(0, groups, wait, 0)

    def step(slot):
        @pl.when(i + 1 < pl.num_programs(0))
        def _():
            gather(i + 1, 1 - slot)

        wait_all(slot)
        y = ybuf_ref[slot].reshape(ts, ybuf_ref.shape[-1])
        out_ref[...] = x1_ref[...] + mod_ref[5:6, :] * y

    @pl.when(i == 0)
    def _():
        gather(0, 0)

    @pl.when(i % 2 == 0)
    def _():
        step(0)

    @pl.when(i % 2 == 1)
    def _():
        step(1)


def _combine(pos, ys, x1, mod3, *, seq, ts):
    n, d = x1.shape
    tiles_per_batch = seq // ts
    return pl.pallas_call(
        functools.partial(_combine_kernel, ts=ts),
        out_shape=jax.ShapeDtypeStruct((n, d), F32),
        grid_spec=pltpu.PrefetchScalarGridSpec(
            num_scalar_prefetch=1,
            grid=(n // ts,),
            in_specs=[pl.BlockSpec(memory_space=pl.ANY),
                      pl.BlockSpec((ts, d), lambda i, pos: (i, 0)),
                      pl.BlockSpec((None, 6, d), lambda i, pos: (i // tiles_per_batch, 0, 0))],
            out_specs=pl.BlockSpec((ts, d), lambda i, pos: (i, 0)),
            scratch_shapes=[pltpu.VMEM((2, ts // SUBLANES, SUBLANES, d), F32),
                            pltpu.SemaphoreType.DMA((2,))]),
        compiler_params=_cparams(1),
        name="combine",
    )(pos, ys.reshape(ys.shape[0] // SUBLANES, SUBLANES, d), x1, mod3)


def _pos_kernel(start_ref, route_ref, pos_ref):
    chunks = pos_ref.shape[1]
    bucket = route_ref[:, 0:chunks, :]
    pos = route_ref[:, chunks:2 * chunks, :]
    for b in range(N_ROUTE_BUCKETS):
        pos = pos + jnp.where(bucket == b, start_ref[b], 0)
    pos_ref[...] = pos


def _route_tables(route, counts, n_tiles, tm):
    steps, rows, _ = route.shape
    cnt = counts[:N_ROUTE_BUCKETS, 0]
    padded = (cnt + tm - 1) // tm * tm
    end = jnp.cumsum(padded)
    pos = pl.pallas_call(
        _pos_kernel,
        out_shape=jax.ShapeDtypeStruct((steps, rows // 2, LANES), jnp.int32),
        in_specs=[pl.BlockSpec(memory_space=pltpu.SMEM), pl.BlockSpec(memory_space=pltpu.VMEM)],
        out_specs=pl.BlockSpec(memory_space=pltpu.VMEM),
        name="pos",
    )(end - padded, route).reshape(-1)
    tiles_used = end[-1] // tm
    tile = jnp.arange(n_tiles, dtype=jnp.int32)
    valid = tile < tiles_used
    first_row = jnp.minimum(tile, tiles_used - 1) * tm
    tile_bucket = jnp.sum((end[None, :] <= first_row[:, None]).astype(jnp.int32), axis=1)
    tile_bucket = jnp.minimum(tile_bucket, N_ROUTE_BUCKETS - 1)
    group = tile_bucket // PAIRS_PER_GROUP
    pair = tile_bucket % PAIRS_PER_GROUP
    tile_ea = group * EXPERTS_PER_GROUP + jnp.asarray(PAIR_LO, jnp.int32)[pair]
    tile_eb = group * EXPERTS_PER_GROUP + jnp.asarray(PAIR_HI, jnp.int32)[pair]
    return pos.astype(jnp.int32), tile_ea, tile_eb, valid.astype(jnp.int32)


def _tile(seq, pref):
    t = min(pref, seq)
    assert seq % t == 0 and t % LANES == 0, (seq, t)
    return t


def kernel(x, c, rel_bias, ada_w, ada_b, norm1_g, w_in, q_norm_g, k_norm_g, lambda_q1, lambda_k1,
           lambda_q2, lambda_k2, subln_g, w_branch_attn, pool_w, pool_scale, w_branch_pool, w_out,
           norm2_g, router_group_w, router_group_b, router_expert_w, router_expert_b,
           expert_w_gate, expert_w_up, expert_w_down):
    bsz, seq, d = x.shape
    n = bsz * seq
    ts = _tile(seq, TOKEN_TILE)
    t_attn = _tile(seq, ATTN_TILE)
    tr = _tile(seq, ROW_COPY_TILE)
    tm = MOE_TILE
    n_tiles = -(-(n + N_ROUTE_BUCKETS * (tm - 1)) // tm)
    attn_steps = bsz * N_HEADS * (seq // t_attn)
    while (n_tiles * tm) % (attn_steps * SUBLANES):
        n_tiles += 1
    bias = _bias_tiles(rel_bias, t_attn)
    x2 = x.reshape(n, d)
    for l in range(ada_w.shape[0]):
        lambda_init = 0.8 - 0.6 * math.exp(-0.3 * l)
        mod3 = _ada(c, ada_w[l], ada_b[l]).reshape(bsz, 6, d)
        gq = jnp.tile(q_norm_g[l], D_ATTN // HEAD_DIM) * (HEAD_DIM ** -0.5 * LOG2E)
        gk = jnp.tile(k_norm_g[l], D_ATTN // HEAD_DIM)
        q, k, vt, sga, pc = _inproj(x2, mod3, norm1_g[l], w_in[l], gq, gk, pool_w[l], pool_scale[l],
                                    w_branch_pool[l], bsz=bsz, seq=seq, ts=ts)
        o, cleared = _attention(q.reshape(bsz, seq, D_ATTN), k.reshape(bsz, seq, D_ATTN), vt, bias,
                                lambda_q1[l], lambda_k1[l], lambda_q2[l], lambda_k2[l], subln_g[l],
                                t=t_attn, lambda_init=lambda_init,
                                clear_shape=(n_tiles * tm, d + LANES))
        x1, h2p, route, counts = _post(
            o.reshape(n, D_ATTN), sga, pc, x2, mod3, norm2_g[l], w_branch_attn[l], w_out[l],
            router_group_w[l], router_group_b[l], router_expert_w[l], router_expert_b[l],
            seq=seq, ts=ts)
        pos, tile_ea, tile_eb, tile_valid = _route_tables(route, counts, n_tiles, tm)
        hs = _dispatch(pos, h2p, cleared, ts=tr)
        ys = _experts(tile_ea, tile_eb, tile_valid, hs, expert_w_gate[l], expert_w_up[l],
                      expert_w_down[l], tm=tm)
        x2 = _combine(pos, ys, x1, mod3, seq=seq, ts=tr)
    return x2.reshape(bsz, seq, d)
```

```python
import functools
import math

import jax
import jax.numpy as jnp
from jax import lax
from jax.experimental import pallas as pl
from jax.experimental.pallas import tpu as pltpu

F32 = jnp.float32
BF16 = jnp.bfloat16

CHUNK = 64
N_HEADS = 4
HEAD_DIM = 64
D_HEAD_V = 2 * HEAD_DIM
V_ROWS = D_HEAD_V + 16
D_ATTN = N_HEADS * D_HEAD_V
POOL_WINDOWS = (2, 4, 8, 16)
POOL_GROUP_DIM = 128
D_POOL = len(POOL_WINDOWS) * POOL_GROUP_DIM
POOL_HALO = 16
N_BUCKETS = 32
MAX_DISTANCE = 128
N_EXPERT_GROUPS = 4
EXPERTS_PER_GROUP = 4
N_EXPERTS = N_EXPERT_GROUPS * EXPERTS_PER_GROUP
PAIRS_PER_GROUP = 6
PAIR_LO = (0, 0, 1, 1, 0, 2)
PAIR_HI = (1, 2, 2, 3, 3, 3)
N_ROUTE_BUCKETS = N_EXPERT_GROUPS * PAIRS_PER_GROUP
RMS_EPS = 1e-6
LOG2E = math.log2(math.e)
MASKED = -1e30

LANES = 128
SUBLANES = 8
ROUTER_ROWS = 32
EXPERT_ROW0 = 8

VMEM_LIMIT = 56 * 1024 * 1024
TOKEN_TILE = 512
ATTN_TILE = 512
MOE_TILE = 256
ROW_COPY_TILE = 1024


def _cparams(n_axes):
    return pltpu.CompilerParams(dimension_semantics=("arbitrary",) * n_axes,
                                vmem_limit_bytes=VMEM_LIMIT)


def _const_spec(shape):
    nd = len(shape)
    return pl.BlockSpec(shape, lambda *_: (0,) * nd, pipeline_mode=pl.Buffered(1))


def _ada_kernel(ct_ref, w_ref, b_ref, o_ref):
    @pl.when(pl.program_id(0) == 0)
    def _():
        o_ref[...] = jnp.broadcast_to(b_ref[...], o_ref.shape)

    ct = ct_ref[...]
    s = ct * jax.nn.sigmoid(ct)
    w = w_ref[...]
    rows = [jnp.sum(w * s[:, b:b + 1], axis=0, keepdims=True) for b in range(ct.shape[1])]
    o_ref[...] += jnp.concatenate(rows, axis=0)


def _ada(c, w, b):
    bsz, d = c.shape
    n = w.shape[1]
    rows = 256
    return pl.pallas_call(
        _ada_kernel,
        out_shape=jax.ShapeDtypeStruct((bsz, n), F32),
        grid=(d // rows,),
        in_specs=[pl.BlockSpec((rows, bsz), lambda j: (j, 0)),
                  pl.BlockSpec((rows, n), lambda j: (j, 0)),
                  pl.BlockSpec((1, n), lambda j: (0, 0))],
        out_specs=pl.BlockSpec((bsz, n), lambda j: (0, 0)),
        compiler_params=_cparams(1),
        name="ada",
    )(c.T, w, b.reshape(1, n))


def _log_bucket_starts():
    nb = N_BUCKETS // 2
    max_exact = nb // 2
    m = nb - max_exact
    ratio = MAX_DISTANCE // max_exact
    starts = []
    for k in range(1, m):
        n = max_exact
        while n ** m < max_exact ** m * ratio ** k:
            n += 1
        starts.append(n)
    return tuple(starts)


LOG_BUCKET_STARTS = _log_bucket_starts()


def _bias_kernel(rb_ref, o_ref, *, t):
    h = pl.program_id(0)
    kind = pl.program_id(1)
    nb = N_BUCKETS // 2
    max_exact = nb // 2
    kpos = lax.broadcasted_iota(jnp.int32, (t, t), 0)
    qpos = lax.broadcasted_iota(jnp.int32, (t, t), 1)
    rel = kpos - qpos - jnp.where(kind == 0, t, 0)
    n = jnp.abs(rel)

    def table(first):
        val = jnp.full((t, t), rb_ref[first + nb - 1, h], F32)
        for k in range(len(LOG_BUCKET_STARTS) - 1, -1, -1):
            val = jnp.where(n < LOG_BUCKET_STARTS[k], rb_ref[first + max_exact + k, h], val)
        for j in range(max_exact - 1, -1, -1):
            val = jnp.where(n == j, rb_ref[first + j, h], val)
        return val

    bias = jnp.where(rel > 0, table(nb), table(0))
    shift = CHUNK.bit_length() - 1
    hidden = jnp.logical_and(kind == 1, (kpos >> shift) > (qpos >> shift))
    o_ref[...] = jnp.where(hidden, MASKED, (bias - rb_ref[nb - 1, h]) * LOG2E)


def _bias_tiles(rel_bias, t):
    return pl.pallas_call(
        functools.partial(_bias_kernel, t=t),
        out_shape=jax.ShapeDtypeStruct((N_HEADS, 2, t, t), F32),
        grid=(N_HEADS, 2),
        in_specs=[pl.BlockSpec(memory_space=pltpu.SMEM)],
        out_specs=pl.BlockSpec((None, None, t, t), lambda h, j: (h, j, 0, 0)),
        compiler_params=_cparams(2),
        name="bias_tiles",
    )(rel_bias)


def _group_rms(xc, ones_blockdiag, gain):
    ssq = jnp.dot((xc * xc).astype(BF16), ones_blockdiag, preferred_element_type=F32)
    return xc * lax.rsqrt(ssq * (1.0 / HEAD_DIM) + RMS_EPS) * gain


def _split_bf16(a):
    hi = a.astype(BF16)
    return hi, (a - hi.astype(F32)).astype(BF16)


def _pool_fold_kernel(pw_ref, ps_ref, wbp_ref, o_ref):
    a_hi, a_lo = _split_bf16(pw_ref[...] * ps_ref[...])
    b_hi, b_lo = _split_bf16(wbp_ref[...])
    dot = functools.partial(jnp.dot, preferred_element_type=F32)
    o_ref[...] = (dot(a_hi, b_hi) + dot(a_hi, b_lo) + dot(a_lo, b_hi)).astype(o_ref.dtype)


def _pool_fold(pool_w, pool_scale, w_bp):
    g, c, _ = pool_w.shape
    d = w_bp.shape[1]
    return pl.pallas_call(
        _pool_fold_kernel,
        out_shape=jax.ShapeDtypeStruct((g * c, d), BF16),
        grid=(g,),
        in_specs=[pl.BlockSpec((None, c, c), lambda i: (i, 0, 0)),
                  pl.BlockSpec((None, 1, c), lambda i: (i, 0, 0)),
                  pl.BlockSpec((c, d), lambda i: (i, 0))],
        out_specs=pl.BlockSpec((c, d), lambda i: (i, 0)),
        compiler_params=_cparams(1),
        name="pool_fold",
    )(pool_w, pool_scale.reshape(g, 1, c), w_bp)


def _inproj_kernel(x_ref, mod_ref, g1_ref, win_ref, gq_ref, gk_ref, ones_ref, wpool_ref,
                   q_ref, k_ref, vt_ref, sga_ref, pc_ref, ext_ref, v_ref,
                   *, ts, tiles_per_batch):
    tb = pl.program_id(0) % tiles_per_batch
    x = x_ref[...]
    y = x * lax.rsqrt(jnp.mean(x * x, axis=-1, keepdims=True) + RMS_EPS)
    h = y * (g1_ref[...] * (1.0 + mod_ref[1:2, :])) + mod_ref[0:1, :]
    hb = h.astype(BF16)

    def proj(c0, c1):
        return jnp.dot(hb, win_ref[:, c0:c1], preferred_element_type=F32)

    ones_bd = ones_ref[...]
    q_ref[...] = _group_rms(proj(0, D_ATTN), ones_bd, gq_ref[...]).astype(BF16)
    k_ref[...] = _group_rms(proj(D_ATTN, 2 * D_ATTN), ones_bd, gk_ref[...]).astype(BF16)
    v_ref[...] = proj(2 * D_ATTN, 3 * D_ATTN)
    vt = v_ref[...].T.astype(BF16)
    for hd in range(N_HEADS):
        vt_ref[hd, 0:D_HEAD_V, :] = vt[hd * D_HEAD_V:(hd + 1) * D_HEAD_V, :]
        vt_ref[hd, D_HEAD_V:V_ROWS, :] = jnp.ones((V_ROWS - D_HEAD_V, ts), BF16)
    c_u = 3 * D_ATTN
    c_ga = c_u + D_POOL
    c_gp = c_ga + x.shape[1]
    sga_ref[...] = jax.nn.sigmoid(proj(c_ga, c_gp)).astype(BF16)

    u = proj(c_u, c_ga)

    @pl.when(tb == 0)
    def _():
        ext_ref[0:POOL_HALO, :] = jnp.zeros((POOL_HALO, D_POOL), F32)

    ext_ref[POOL_HALO:POOL_HALO + ts, :] = u
    row = lax.broadcasted_iota(jnp.int32, (ts, 1), 0) + tb * ts
    ys = []
    for g, w in enumerate(POOL_WINDOWS):
        c0 = g * POOL_GROUP_DIM
        ug = u[:, c0:c0 + POOL_GROUP_DIM]
        acc = ug
        for d in range(1, w):
            acc = acc + ext_ref[POOL_HALO - d:POOL_HALO - d + ts, c0:c0 + POOL_GROUP_DIM]
        cnt = jnp.minimum(row + 1, w).astype(F32)
        ys.append((acc / cnt - ug).astype(BF16))
    ypool = jnp.dot(jnp.concatenate(ys, axis=1), wpool_ref[...], preferred_element_type=F32)
    pc_ref[...] = (jax.nn.sigmoid(proj(c_gp, c_gp + x.shape[1])) * ypool).astype(BF16)
    ext_ref[0:POOL_HALO, :] = u[ts - POOL_HALO:ts, :]


def _inproj(x2, mod3, g1, w_in, gq, gk, pool_w, pool_scale, w_bp, *, bsz, seq, ts):
    n, d = x2.shape
    d_in = w_in.shape[1]
    tiles_per_batch = seq // ts
    win_b = w_in.astype(BF16)
    idx = jnp.arange(D_ATTN) // HEAD_DIM
    ones_bd = (idx[:, None] == idx[None, :]).astype(BF16)
    kern = functools.partial(_inproj_kernel, ts=ts, tiles_per_batch=tiles_per_batch)
    tok = lambda i: (i, 0)
    out_shape = (jax.ShapeDtypeStruct((n, D_ATTN), BF16),
                 jax.ShapeDtypeStruct((n, D_ATTN), BF16),
                 jax.ShapeDtypeStruct((bsz, N_HEADS, V_ROWS, seq), BF16),
                 jax.ShapeDtypeStruct((n, d), BF16),
                 jax.ShapeDtypeStruct((n, d), BF16))
    return pl.pallas_call(
        kern,
        out_shape=out_shape,
        grid=(n // ts,),
        in_specs=[pl.BlockSpec((ts, d), tok),
                  pl.BlockSpec((None, 6, d), lambda i: (i // tiles_per_batch, 0, 0)),
                  _const_spec((1, d)),
                  _const_spec((d, d_in)),
                  _const_spec((1, D_ATTN)),
                  _const_spec((1, D_ATTN)),
                  _const_spec((D_ATTN, D_ATTN)),
                  _const_spec((D_POOL, d))],
        out_specs=(pl.BlockSpec((ts, D_ATTN), tok),
                   pl.BlockSpec((ts, D_ATTN), tok),
                   pl.BlockSpec((None, N_HEADS, V_ROWS, ts),
                                lambda i: (i // tiles_per_batch, 0, 0, i % tiles_per_batch)),
                   pl.BlockSpec((ts, d), tok),
                   pl.BlockSpec((ts, d), tok)),
        scratch_shapes=[pltpu.VMEM((POOL_HALO + ts, D_POOL), F32), pltpu.VMEM((ts, D_ATTN), F32)],
        compiler_params=_cparams(1),
        name="inproj",
    )(x2, mod3, g1.reshape(1, d), win_b, gq.reshape(1, D_ATTN), gk.reshape(1, D_ATTN), ones_bd,
      _pool_fold(pool_w, pool_scale, w_bp))


def _attn_kernel(q_ref, qn_ref, k_ref, vt_ref, bias_ref, lq1_ref, lk1_ref, lq2_ref, lk2_ref, subg_ref,
                 o_ref, clear_ref, s_ref, mt_ref, m_ref, acc_ref, *, t, lambda_init):
    i = pl.program_id(2)
    last = pl.num_programs(2) - 1
    clear_ref[...] = jnp.zeros(clear_ref.shape, clear_ref.dtype)

    def split_maps(q):
        lane = lax.broadcasted_iota(jnp.int32, q.shape, 1)
        zero = jnp.zeros_like(q)
        return jnp.where(lane < HEAD_DIM, q, zero), jnp.where(lane >= HEAD_DIM, q, zero)

    q_now = split_maps(q_ref[...])

    m_ref[...] = jnp.full(m_ref.shape, MASKED, F32)
    acc_ref[...] = jnp.zeros(acc_ref.shape, F32)

    def scores(j, bias, slot, qm=q_now):
        kt = k_ref[pl.ds(pl.multiple_of(j * t, t), t), :]
        for mp in range(2):
            s = lax.dot_general(kt, qm[mp], (((1,), (1,)), ((), ())), preferred_element_type=F32)
            if bias is not None:
                s = s + bias
            s_ref[slot, mp] = s
            mt_ref[slot, mp] = jnp.max(s, axis=0, keepdims=True)

    def accumulate(j, slot):
        vt = vt_ref[:, pl.ds(pl.multiple_of(j * t, t), t)]
        for mp in range(2):
            m_old = m_ref[mp]
            m_new = jnp.maximum(m_old, mt_ref[slot, mp])
            p = jnp.exp2(s_ref[slot, mp] - m_new).astype(BF16)
            acc_ref[mp] = (jnp.exp2(m_old - m_new) * acc_ref[mp]
                           + jnp.dot(vt, p, preferred_element_type=F32))
            m_ref[mp] = m_new

    def next_diagonal():
        nxt = jnp.minimum(i + 1, last)
        scores(nxt, bias_ref[1], 2, split_maps(qn_ref[...]))

    @pl.when(i == 0)
    def _():
        scores(0, bias_ref[1], 0)

    @pl.when(i >= 1)
    def _():
        scores(i - 1, bias_ref[0], 1)
        accumulate(i, 2)

    n_far = jnp.maximum(i - 1, 0)

    def pair(kk):
        j = i - 1 - 2 * kk
        scores(j - 1, None, 0)
        accumulate(j, 1)
        scores(j - 2, None, 1)
        accumulate(j - 1, 0)

    def quad_body(qq, carry):
        pair(2 * qq)
        pair(2 * qq + 1)
        return carry

    n_pairs = n_far // 2
    lax.fori_loop(0, n_pairs // 2, quad_body, 0)

    @pl.when(n_pairs % 2 == 1)
    def _():
        pair(n_pairs - 1)

    @pl.when(n_far % 2 == 1)
    def _():
        scores(0, None, 0)
        accumulate(1, 1)
        next_diagonal()
        accumulate(0, 0)

    @pl.when(jnp.logical_and(i >= 1, n_far % 2 == 0))
    def _():
        next_diagonal()
        accumulate(0, 1)

    @pl.when(i == 0)
    def _():
        next_diagonal()
        accumulate(0, 0)

    lam = (jnp.exp(jnp.sum(lq1_ref[...] * lk1_ref[...], axis=1, keepdims=True))
           - jnp.exp(jnp.sum(lq2_ref[...] * lk2_ref[...], axis=1, keepdims=True)) + lambda_init)
    o1 = acc_ref[0, 0:D_HEAD_V, :] / acc_ref[0, D_HEAD_V:D_HEAD_V + 1, :]
    o2 = acc_ref[1, 0:D_HEAD_V, :] / acc_ref[1, D_HEAD_V:D_HEAD_V + 1, :]
    ot = o1 - lam * o2
    ot = ot * lax.rsqrt(jnp.mean(ot * ot, axis=0, keepdims=True) + RMS_EPS)
    ot = ot * subg_ref[...] * (1.0 - lambda_init)
    o_ref[...] = ot.T.astype(BF16)


def _attention(q, k, vt, bias, lq1, lk1, lq2, lk2, subln_g, *, t, lambda_init, clear_shape):
    bsz, seq, _ = q.shape
    nq = seq // t
    assert t + 1 >= LOG_BUCKET_STARTS[-1], "keys two tiles back must all fall in the last distance bucket"
    steps = bsz * N_HEADS * nq
    clear_rows = clear_shape[0] // steps
    assert clear_rows * steps == clear_shape[0] and clear_rows % SUBLANES == 0, (clear_shape, steps)
    kern = functools.partial(_attn_kernel, t=t, lambda_init=lambda_init)
    vec = lambda a: a.reshape(1, HEAD_DIM)
    return pl.pallas_call(
        kern,
        out_shape=(jax.ShapeDtypeStruct((bsz, seq, D_ATTN), BF16),
                   jax.ShapeDtypeStruct(clear_shape, F32)),
        grid=(bsz, N_HEADS, seq // t),
        in_specs=[pl.BlockSpec((None, t, D_HEAD_V), lambda b, h, i: (b, i, h)),
                  pl.BlockSpec((None, t, D_HEAD_V), lambda b, h, i: (b, jnp.minimum(i + 1, nq - 1), h)),
                  pl.BlockSpec((None, seq, D_HEAD_V), lambda b, h, i: (b, 0, h)),
                  pl.BlockSpec((None, None, V_ROWS, seq), lambda b, h, i: (b, h, 0, 0)),
                  pl.BlockSpec((None, 2, t, t), lambda b, h, i: (h, 0, 0, 0)),
                  _const_spec((1, HEAD_DIM)), _const_spec((1, HEAD_DIM)),
                  _const_spec((1, HEAD_DIM)), _const_spec((1, HEAD_DIM)),
                  _const_spec((D_HEAD_V, 1))],
        out_specs=(pl.BlockSpec((None, t, D_HEAD_V), lambda b, h, i: (b, i, h)),
                   pl.BlockSpec((clear_rows, clear_shape[1]),
                                lambda b, h, i: ((b * N_HEADS + h) * nq + i, 0))),
        scratch_shapes=[pltpu.VMEM((3, 2, t, t), F32),
                        pltpu.VMEM((3, 2, 1, t), F32),
                        pltpu.VMEM((2, 1, t), F32),
                        pltpu.VMEM((2, V_ROWS, t), F32)],
        compiler_params=_cparams(3),
        name="attn",
    )(q, q, k, vt, bias, vec(lq1), vec(lk1), vec(lq2), vec(lk2), subln_g.reshape(D_HEAD_V, 1))


def _first_max(rows):
    best = rows[0]
    for r in rows[1:]:
        best = jnp.maximum(best, r)
    idx = jnp.full(best.shape, len(rows) - 1, jnp.int32)
    for j in range(len(rows) - 2, -1, -1):
        idx = jnp.where(rows[j] == best, j, idx)
    return best, idx


def _post_kernel(o_ref, sga_ref, pc_ref, x_ref, mod_ref, g2_ref, wba_ref, wout_ref, wr_ref, br_ref,
                 tri_ref, x1_ref, h2p_ref, route_ref, cnt_ref, carry_ref):
    @pl.when(pl.program_id(0) == 0)
    def _():
        carry_ref[...] = jnp.zeros(carry_ref.shape, F32)

    ya = jnp.dot(o_ref[...], wba_ref[...], preferred_element_type=F32)
    merged = sga_ref[...].astype(F32) * ya + pc_ref[...].astype(F32)
    z = jnp.dot(merged.astype(BF16), wout_ref[...], preferred_element_type=F32)
    x1 = x_ref[...] + mod_ref[2:3, :] * z
    x1_ref[...] = x1
    y = x1 * lax.rsqrt(jnp.mean(x1 * x1, axis=-1, keepdims=True) + RMS_EPS)
    h2 = y * (g2_ref[...] * (1.0 + mod_ref[4:5, :])) + mod_ref[3:4, :]
    hi = h2.astype(BF16)
    lo = (h2 - hi.astype(F32)).astype(BF16)

    nt = (((1,), (1,)), ((), ()))
    a = lax.dot_general(wr_ref[...], hi, nt, preferred_element_type=F32)
    b = lax.dot_general(wr_ref[0:ROUTER_ROWS, :], lo, nt, preferred_element_type=F32)
    logits = a[0:ROUTER_ROWS] + a[ROUTER_ROWS:2 * ROUTER_ROWS] + b + br_ref[...]

    gl = [logits[g:g + 1, :] for g in range(N_EXPERT_GROUPS)]
    gmax, gidx = _first_max(gl)
    gsum = gl[0] * 0.0
    for r in gl:
        gsum = gsum + jnp.exp(r - gmax)
    g_val = 1.0 / gsum
    es = []
    for r in range(EXPERTS_PER_GROUP):
        sel = jnp.zeros_like(gmax)
        for g in range(N_EXPERT_GROUPS):
            row = EXPERT_ROW0 + g * EXPERTS_PER_GROUP + r
            sel = jnp.where(gidx == g, logits[row:row + 1, :], sel)
        es.append(sel)
    e1, i1 = _first_max(es)
    rest = [jnp.where(i1 == r, -jnp.inf, es[r]) for r in range(EXPERTS_PER_GROUP)]
    e2, i2 = _first_max(rest)
    r21 = jnp.exp(e2 - e1)
    w1 = g_val / (1.0 + r21)
    w2 = g_val * r21 / (1.0 + r21)

    first = i1 < i2
    e_lo = jnp.where(first, i1, i2)
    e_hi = jnp.where(first, i2, i1)
    pair = jnp.zeros_like(e_lo)
    for p in range(1, PAIRS_PER_GROUP):
        pair = jnp.where(jnp.logical_and(e_lo == PAIR_LO[p], e_hi == PAIR_HI[p]), p, pair)
    bucket = gidx * PAIRS_PER_GROUP + pair
    w_lo = jnp.where(first, w1, w2)
    w_hi = jnp.where(first, w2, w1)

    ts = bucket.shape[1]
    brow = lax.broadcasted_iota(jnp.int32, (ROUTER_ROWS, ts), 0)
    onehot = brow == bucket
    prefix = jnp.dot(jnp.where(onehot, 1.0, 0.0).astype(BF16), tri_ref[...],
                     preferred_element_type=F32)
    carry = carry_ref[...]
    rank = jnp.sum(jnp.where(onehot, prefix + carry, 0.0), axis=0, keepdims=True) - 1.0
    carry = carry + prefix[:, ts - 1:ts]
    carry_ref[...] = carry
    cnt_ref[...] = jnp.broadcast_to(carry, cnt_ref.shape).astype(jnp.int32)
    chunks = ts // LANES
    rank_i = rank.astype(jnp.int32)
    for j in range(chunks):
        route_ref[j:j + 1, :] = bucket[:, j * LANES:(j + 1) * LANES]
        route_ref[chunks + j:chunks + j + 1, :] = rank_i[:, j * LANES:(j + 1) * LANES]

    arow = lax.broadcasted_iota(jnp.int32, (LANES, ts), 0)
    aux_t = jnp.where(arow == 0, w_lo, jnp.where(arow == 1, w_hi, 0.0))
    d = h2.shape[1]
    h2p_ref[:, 0:d] = h2
    h2p_ref[:, d:d + LANES] = aux_t.T


def _post(o2, sga, pc, x2, mod3, g2, w_ba, w_out, wg_r, bg_r, we_r, be_r, *, seq, ts):
    n, d = x2.shape
    tiles_per_batch = seq // ts
    steps = n // ts
    wr = jnp.zeros((ROUTER_ROWS, d), F32)
    wr = wr.at[0:N_EXPERT_GROUPS].set(wg_r.T).at[EXPERT_ROW0:EXPERT_ROW0 + N_EXPERTS].set(we_r.T)
    wr_hi = wr.astype(BF16)
    wr_lo = (wr - wr_hi.astype(F32)).astype(BF16)
    br = jnp.zeros((ROUTER_ROWS, 1), F32)
    br = br.at[0:N_EXPERT_GROUPS, 0].set(bg_r).at[EXPERT_ROW0:EXPERT_ROW0 + N_EXPERTS, 0].set(be_r)
    tok = lambda i: (i, 0)
    tidx = jnp.arange(ts)
    tri = (tidx[:, None] <= tidx[None, :]).astype(BF16)
    return pl.pallas_call(
        _post_kernel,
        out_shape=(jax.ShapeDtypeStruct((n, d), F32),
                   jax.ShapeDtypeStruct((n, d + LANES), F32),
                   jax.ShapeDtypeStruct((steps, 2 * (ts // LANES), LANES), jnp.int32),
                   jax.ShapeDtypeStruct((ROUTER_ROWS, LANES), jnp.int32)),
        grid=(steps,),
        in_specs=[pl.BlockSpec((ts, D_ATTN), tok),
                  pl.BlockSpec((ts, d), tok),
                  pl.BlockSpec((ts, d), tok),
                  pl.BlockSpec((ts, d), tok),
                  pl.BlockSpec((None, 6, d), lambda i: (i // tiles_per_batch, 0, 0)),
                  _const_spec((1, d)),
                  _const_spec((D_ATTN, d)),
                  _const_spec((d, d)),
                  _const_spec((2 * ROUTER_ROWS, d)),
                  _const_spec((ROUTER_ROWS, 1)),
                  _const_spec((ts, ts))],
        out_specs=(pl.BlockSpec((ts, d), tok), pl.BlockSpec((ts, d + LANES), tok),
                   pl.BlockSpec((None, 2 * (ts // LANES), LANES), lambda i: (i, 0, 0)),
                   pl.BlockSpec((ROUTER_ROWS, LANES), lambda i: (0, 0))),
        scratch_shapes=[pltpu.VMEM((ROUTER_ROWS, 1), F32)],
        compiler_params=_cparams(1),
        name="post",
    )(o2, sga, pc, x2, mod3, g2.reshape(1, d), w_ba.astype(BF16), w_out.astype(BF16),
      jnp.concatenate([wr_hi, wr_lo], axis=0), br, tri)


def _group_sublane(row):
    return lax.shift_right_logical(row, SUBLANES.bit_length() - 1), row & (SUBLANES - 1)


def _row_copy(src, src_row, dst, dst_row, sem):
    sg, ss = src_row
    dg, ds = dst_row
    return pltpu.make_async_copy(src.at[sg, pl.ds(ss, 1), :], dst.at[dg, pl.ds(ds, 1), :], sem)


def _dispatch_kernel(pos_ref, h_ref, init_ref, hs_ref, sem, *, ts):
    del init_ref
    base = pl.program_id(0) * ts

    def start(g, carry):
        for u in range(SUBLANES):
            p = pos_ref[base + g * SUBLANES + u]
            _row_copy(h_ref, (g, u), hs_ref, _group_sublane(p), sem).start(priority=u % 2)
        return carry

    def wait(g, carry):
        for u in range(SUBLANES):
            _row_copy(h_ref, (0, 0), hs_ref, (0, 0), sem).wait()
        return carry

    lax.fori_loop(0, ts // SUBLANES, start, 0)
    lax.fori_loop(0, ts // SUBLANES, wait, 0)


def _dispatch(pos, h2p, cleared, *, ts):
    n, w = h2p.shape
    n_rows = cleared.shape[0]
    return pl.pallas_call(
        functools.partial(_dispatch_kernel, ts=ts),
        out_shape=jax.ShapeDtypeStruct((n_rows // SUBLANES, SUBLANES, w), h2p.dtype),
        grid_spec=pltpu.PrefetchScalarGridSpec(
            num_scalar_prefetch=1,
            grid=(n // ts,),
            in_specs=[pl.BlockSpec((ts // SUBLANES, SUBLANES, w), lambda i, pos: (i, 0, 0)),
                      pl.BlockSpec(memory_space=pl.ANY)],
            out_specs=pl.BlockSpec(memory_space=pl.ANY),
            scratch_shapes=[pltpu.SemaphoreType.DMA]),
        input_output_aliases={2: 0},
        compiler_params=_cparams(1),
        name="dispatch",
    )(pos, h2p.reshape(n // SUBLANES, SUBLANES, w),
      cleared.reshape(n_rows // SUBLANES, SUBLANES, w)).reshape(n_rows, w)


def _experts_kernel(ea_ref, eb_ref, valid_ref, hs_ref, wga_ref, wua_ref, wda_ref, wgb_ref, wub_ref,
                    wdb_ref, ys_ref):
    del ea_ref, eb_ref
    t = pl.program_id(0)
    d = hs_ref.shape[1] - LANES

    @pl.when(valid_ref[t] == 1)
    def _():
        h = hs_ref[:, 0:d].astype(BF16)
        aux = hs_ref[:, d:d + LANES]

        def expert(wg_ref, wu_ref, wd_ref):
            a = jnp.dot(h, wg_ref[...].astype(BF16), preferred_element_type=F32)
            b = jnp.dot(h, wu_ref[...].astype(BF16), preferred_element_type=F32)
            hid = (a * jax.nn.sigmoid(a)) * b
            return jnp.dot(hid.astype(BF16), wd_ref[...].astype(BF16), preferred_element_type=F32)

        ys_ref[...] = (aux[:, 0:1] * expert(wga_ref, wua_ref, wda_ref)
                       + aux[:, 1:2] * expert(wgb_ref, wub_ref, wdb_ref))

    @pl.when(valid_ref[t] == 0)
    def _():
        ys_ref[...] = jnp.zeros(ys_ref.shape, ys_ref.dtype)


def _experts(tile_ea, tile_eb, tile_valid, hs, w_gate, w_up, w_down, *, tm):
    n_rows, w = hs.shape
    _, d, de = w_gate.shape
    wg, wu, wd = w_gate, w_up, w_down
    ea = lambda t, ea_r, eb_r, v_r: (ea_r[t], 0, 0)
    eb = lambda t, ea_r, eb_r, v_r: (eb_r[t], 0, 0)
    row = lambda t, ea_r, eb_r, v_r: (t, 0)
    return pl.pallas_call(
        _experts_kernel,
        out_shape=jax.ShapeDtypeStruct((n_rows, d), F32),
        grid_spec=pltpu.PrefetchScalarGridSpec(
            num_scalar_prefetch=3,
            grid=(n_rows // tm,),
            in_specs=[pl.BlockSpec((tm, w), row),
                      pl.BlockSpec((None, d, de), ea), pl.BlockSpec((None, d, de), ea),
                      pl.BlockSpec((None, de, d), ea),
                      pl.BlockSpec((None, d, de), eb), pl.BlockSpec((None, d, de), eb),
                      pl.BlockSpec((None, de, d), eb)],
            out_specs=pl.BlockSpec((tm, d), row)),
        compiler_params=_cparams(1),
        name="experts",
    )(tile_ea, tile_eb, tile_valid, hs, wg, wu, wd, wg, wu, wd)


def _combine_kernel(pos_ref, ys_ref, x1_ref, mod_ref, out_ref, ybuf_ref, sem, *, ts):
    i = pl.program_id(0)
    groups = ts // SUBLANES

    def gather(tile, slot):
        base = tile * ts

        def start(g, carry):
            for u in range(SUBLANES):
                p = pos_ref[base + g * SUBLANES + u]
                _row_copy(ys_ref, _group_sublane(p), ybuf_ref.at[slot], (g, u),
                          sem.at[slot]).start(priority=u % 2)
            return carry

        lax.fori_loop(0, groups, start, 0)

    def wait_all(slot):
        def wait(g, carry):
            for u in range(SUBLANES):
                _row_copy(ys_ref, (0, 0), ybuf_ref.at[slot], (0, 0), sem.at[slot]).wait()
            return carry

        lax.fori_loop(0, groups, wait, 0)

    def step(slot):
        @pl.when(i + 1 < pl.num_programs(0))
        def _():
            gather(i + 1, 1 - slot)

        wait_all(slot)
        y = ybuf_ref[slot].reshape(ts, ybuf_ref.shape[-1])
        out_ref[...] = x1_ref[...] + mod_ref[5:6, :] * y

    @pl.when(i == 0)
    def _():
        gather(0, 0)

    @pl.when(i % 2 == 0)
    def _():
        step(0)

    @pl.when(i % 2 == 1)
    def _():
        step(1)


def _combine(pos, ys, x1, mod3, *, seq, ts):
    n, d = x1.shape
    tiles_per_batch = seq // ts
    return pl.pallas_call(
        functools.partial(_combine_kernel, ts=ts),
        out_shape=jax.ShapeDtypeStruct((n, d), F32),
        grid_spec=pltpu.PrefetchScalarGridSpec(
            num_scalar_prefetch=1,
            grid=(n // ts,),
            in_specs=[pl.BlockSpec(memory_space=pl.ANY),
                      pl.BlockSpec((ts, d), lambda i, pos: (i, 0)),
                      pl.BlockSpec((None, 6, d), lambda i, pos: (i // tiles_per_batch, 0, 0))],
            out_specs=pl.BlockSpec((ts, d), lambda i, pos: (i, 0)),
            scratch_shapes=[pltpu.VMEM((2, ts // SUBLANES, SUBLANES, d), F32),
                            pltpu.SemaphoreType.DMA((2,))]),
        compiler_params=_cparams(1),
        name="combine",
    )(pos, ys.reshape(ys.shape[0] // SUBLANES, SUBLANES, d), x1, mod3)


def _pos_kernel(start_ref, route_ref, pos_ref):
    chunks = pos_ref.shape[1]
    bucket = route_ref[:, 0:chunks, :]
    pos = route_ref[:, chunks:2 * chunks, :]
    for b in range(N_ROUTE_BUCKETS):
        pos = pos + jnp.where(bucket == b, start_ref[b], 0)
    pos_ref[...] = pos


def _route_tables(route, counts, n_tiles, tm):
    steps, rows, _ = route.shape
    cnt = counts[:N_ROUTE_BUCKETS, 0]
    padded = (cnt + tm - 1) // tm * tm
    end = jnp.cumsum(padded)
    pos = pl.pallas_call(
        _pos_kernel,
        out_shape=jax.ShapeDtypeStruct((steps, rows // 2, LANES), jnp.int32),
        in_specs=[pl.BlockSpec(memory_space=pltpu.SMEM), pl.BlockSpec(memory_space=pltpu.VMEM)],
        out_specs=pl.BlockSpec(memory_space=pltpu.VMEM),
        name="pos",
    )(end - padded, route).reshape(-1)
    tiles_used = end[-1] // tm
    tile = jnp.arange(n_tiles, dtype=jnp.int32)
    valid = tile < tiles_used
    first_row = jnp.minimum(tile, tiles_used - 1) * tm
    tile_bucket = jnp.sum((end[None, :] <= first_row[:, None]).astype(jnp.int32), axis=1)
    tile_bucket = jnp.minimum(tile_bucket, N_ROUTE_BUCKETS - 1)
    group = tile_bucket // PAIRS_PER_GROUP
    pair = tile_bucket % PAIRS_PER_GROUP
    tile_ea = group * EXPERTS_PER_GROUP + jnp.asarray(PAIR_LO, jnp.int32)[pair]
    tile_eb = group * EXPERTS_PER_GROUP + jnp.asarray(PAIR_HI, jnp.int32)[pair]
    return pos.astype(jnp.int32), tile_ea, tile_eb, valid.astype(jnp.int32)


def _tile(seq, pref):
    t = min(pref, seq)
    assert seq % t == 0 and t % LANES == 0, (seq, t)
    return t


def kernel(x, c, rel_bias, ada_w, ada_b, norm1_g, w_in, q_norm_g, k_norm_g, lambda_q1, lambda_k1,
           lambda_q2, lambda_k2, subln_g, w_branch_attn, pool_w, pool_scale, w_branch_pool, w_out,
           norm2_g, router_group_w, router_group_b, router_expert_w, router_expert_b,
           expert_w_gate, expert_w_up, expert_w_down):
    bsz, seq, d = x.shape
    n = bsz * seq
    ts = _tile(seq, TOKEN_TILE)
    t_attn = _tile(seq, ATTN_TILE)
    tr = _tile(seq, ROW_COPY_TILE)
    tm = MOE_TILE
    n_tiles = -(-(n + N_ROUTE_BUCKETS * (tm - 1)) // tm)
    attn_steps = bsz * N_HEADS * (seq // t_attn)
    while (n_tiles * tm) % (attn_steps * SUBLANES):
        n_tiles += 1
    bias = _bias_tiles(rel_bias, t_attn)
    x2 = x.reshape(n, d)
    for l in range(ada_w.shape[0]):
        lambda_init = 0.8 - 0.6 * math.exp(-0.3 * l)
        mod3 = _ada(c, ada_w[l], ada_b[l]).reshape(bsz, 6, d)
        gq = jnp.tile(q_norm_g[l], D_ATTN // HEAD_DIM) * (HEAD_DIM ** -0.5 * LOG2E)
        gk = jnp.tile(k_norm_g[l], D_ATTN // HEAD_DIM)
        q, k, vt, sga, pc = _inproj(x2, mod3, norm1_g[l], w_in[l], gq, gk, pool_w[l], pool_scale[l],
                                    w_branch_pool[l], bsz=bsz, seq=seq, ts=ts)
        o, cleared = _attention(q.reshape(bsz, seq, D_ATTN), k.reshape(bsz, seq, D_ATTN), vt, bias,
                                lambda_q1[l], lambda_k1[l], lambda_q2[l], lambda_k2[l], subln_g[l],
                                t=t_attn, lambda_init=lambda_init,
                                clear_shape=(n_tiles * tm, d + LANES))
        x1, h2p, route, counts = _post(
            o.reshape(n, D_ATTN), sga, pc, x2, mod3, norm2_g[l], w_branch_attn[l], w_out[l],
            router_group_w[l], router_group_b[l], router_expert_w[l], router_expert_b[l],
            seq=seq, ts=ts)
        pos, tile_ea, tile_eb, tile_valid = _route_tables(route, counts, n_tiles, tm)
        hs = _dispatch(pos, h2p, cleared, ts=tr)
        ys = _experts(tile_ea, tile_eb, tile_valid, hs, expert_w_gate[l], expert_w_up[l],
                      expert_w_down[l], tm=tm)
        x2 = _combine(pos, ys, x1, mod3, seq=seq, ts=tr)
    return x2.reshape(bsz, seq, d)
```

```python
import functools
import math

import jax
import jax.numpy as jnp
from jax import lax
from jax.experimental import pallas as pl
from jax.experimental.pallas import tpu as pltpu

F32 = jnp.float32
BF16 = jnp.bfloat16

CHUNK = 64
N_HEADS = 4
HEAD_DIM = 64
D_HEAD_V = 2 * HEAD_DIM
V_ROWS = D_HEAD_V + 16
D_ATTN = N_HEADS * D_HEAD_V
POOL_WINDOWS = (2, 4, 8, 16)
POOL_GROUP_DIM = 128
D_POOL = len(POOL_WINDOWS) * POOL_GROUP_DIM
POOL_HALO = 16
N_BUCKETS = 32
MAX_DISTANCE = 128
N_EXPERT_GROUPS = 4
EXPERTS_PER_GROUP = 4
N_EXPERTS = N_EXPERT_GROUPS * EXPERTS_PER_GROUP
PAIRS_PER_GROUP = 6
PAIR_LO = (0, 0, 1, 1, 0, 2)
PAIR_HI = (1, 2, 2, 3, 3, 3)
N_ROUTE_BUCKETS = N_EXPERT_GROUPS * PAIRS_PER_GROUP
RMS_EPS = 1e-6
LOG2E = math.log2(math.e)
MASKED = -1e30

LANES = 128
SUBLANES = 8
ROUTER_ROWS = 32
EXPERT_ROW0 = 8

VMEM_LIMIT = 56 * 1024 * 1024
TOKEN_TILE = 512
ATTN_TILE = 512
MOE_TILE = 256
ROW_COPY_TILE = 1024


def _cparams(n_axes):
    return pltpu.CompilerParams(dimension_semantics=("arbitrary",) * n_axes,
                                vmem_limit_bytes=VMEM_LIMIT)


def _const_spec(shape):
    nd = len(shape)
    return pl.BlockSpec(shape, lambda *_: (0,) * nd, pipeline_mode=pl.Buffered(1))


def _ada_kernel(ct_ref, w_ref, b_ref, o_ref):
    @pl.when(pl.program_id(0) == 0)
    def _():
        o_ref[...] = jnp.broadcast_to(b_ref[...], o_ref.shape)

    ct = ct_ref[...]
    s = ct * jax.nn.sigmoid(ct)
    w = w_ref[...]
    rows = [jnp.sum(w * s[:, b:b + 1], axis=0, keepdims=True) for b in range(ct.shape[1])]
    o_ref[...] += jnp.concatenate(rows, axis=0)


def _ada(c, w, b):
    bsz, d = c.shape
    n = w.shape[1]
    rows = 256
    return pl.pallas_call(
        _ada_kernel,
        out_shape=jax.ShapeDtypeStruct((bsz, n), F32),
        grid=(d // rows,),
        in_specs=[pl.BlockSpec((rows, bsz), lambda j: (j, 0)),
                  pl.BlockSpec((rows, n), lambda j: (j, 0)),
                  pl.BlockSpec((1, n), lambda j: (0, 0))],
        out_specs=pl.BlockSpec((bsz, n), lambda j: (0, 0)),
        compiler_params=_cparams(1),
        name="ada",
    )(c.T, w, b.reshape(1, n))


def _log_bucket_starts():
    nb = N_BUCKETS // 2
    max_exact = nb // 2
    m = nb - max_exact
    ratio = MAX_DISTANCE // max_exact
    starts = []
    for k in range(1, m):
        n = max_exact
        while n ** m < max_exact ** m * ratio ** k:
            n += 1
        starts.append(n)
    return tuple(starts)


LOG_BUCKET_STARTS = _log_bucket_starts()


def _bias_kernel(rb_ref, o_ref, *, t):
    h = pl.program_id(0)
    kind = pl.program_id(1)
    nb = N_BUCKETS // 2
    max_exact = nb // 2
    kpos = lax.broadcasted_iota(jnp.int32, (t, t), 0)
    qpos = lax.broadcasted_iota(jnp.int32, (t, t), 1)
    rel = kpos - qpos - jnp.where(kind == 0, t, 0)
    n = jnp.abs(rel)

    def table(first):
        val = jnp.full((t, t), rb_ref[first + nb - 1, h], F32)
        for k in range(len(LOG_BUCKET_STARTS) - 1, -1, -1):
            val = jnp.where(n < LOG_BUCKET_STARTS[k], rb_ref[first + max_exact + k, h], val)
        for j in range(max_exact - 1, -1, -1):
            val = jnp.where(n == j, rb_ref[first + j, h], val)
        return val

    bias = jnp.where(rel > 0, table(nb), table(0))
    shift = CHUNK.bit_length() - 1
    hidden = jnp.logical_and(kind == 1, (kpos >> shift) > (qpos >> shift))
    o_ref[...] = jnp.where(hidden, MASKED, (bias - rb_ref[nb - 1, h]) * LOG2E)


def _bias_tiles(rel_bias, t):
    return pl.pallas_call(
        functools.partial(_bias_kernel, t=t),
        out_shape=jax.ShapeDtypeStruct((N_HEADS, 2, t, t), F32),
        grid=(N_HEADS, 2),
        in_specs=[pl.BlockSpec(memory_space=pltpu.SMEM)],
        out_specs=pl.BlockSpec((None, None, t, t), lambda h, j: (h, j, 0, 0)),
        compiler_params=_cparams(2),
        name="bias_tiles",
    )(rel_bias)


def _group_rms(xc, ones_blockdiag, gain):
    ssq = jnp.dot((xc * xc).astype(BF16), ones_blockdiag, preferred_element_type=F32)
    return xc * lax.rsqrt(ssq * (1.0 / HEAD_DIM) + RMS_EPS) * gain


def _split_bf16(a):
    hi = a.astype(BF16)
    return hi, (a - hi.astype(F32)).astype(BF16)


def _pool_fold_kernel(pw_ref, ps_ref, wbp_ref, o_ref):
    a_hi, a_lo = _split_bf16(pw_ref[...] * ps_ref[...])
    b_hi, b_lo = _split_bf16(wbp_ref[...])
    dot = functools.partial(jnp.dot, preferred_element_type=F32)
    o_ref[...] = (dot(a_hi, b_hi) + dot(a_hi, b_lo) + dot(a_lo, b_hi)).astype(o_ref.dtype)


def _pool_fold(pool_w, pool_scale, w_bp):
    g, c, _ = pool_w.shape
    d = w_bp.shape[1]
    return pl.pallas_call(
        _pool_fold_kernel,
        out_shape=jax.ShapeDtypeStruct((g * c, d), BF16),
        grid=(g,),
        in_specs=[pl.BlockSpec((None, c, c), lambda i: (i, 0, 0)),
                  pl.BlockSpec((None, 1, c), lambda i: (i, 0, 0)),
                  pl.BlockSpec((c, d), lambda i: (i, 0))],
        out_specs=pl.BlockSpec((c, d), lambda i: (i, 0)),
        compiler_params=_cparams(1),
        name="pool_fold",
    )(pool_w, pool_scale.reshape(g, 1, c), w_bp)


def _inproj_kernel(x_ref, mod_ref, g1_ref, win_ref, gq_ref, gk_ref, ones_ref, wpool_ref,
                   q_ref, k_ref, vt_ref, sga_ref, pc_ref, ext_ref, v_ref,
                   *, ts, tiles_per_batch):
    tb = pl.program_id(0) % tiles_per_batch
    x = x_ref[...]
    y = x * lax.rsqrt(jnp.mean(x * x, axis=-1, keepdims=True) + RMS_EPS)
    h = y * (g1_ref[...] * (1.0 + mod_ref[1:2, :])) + mod_ref[0:1, :]
    hb = h.astype(BF16)

    def proj(c0, c1):
        return jnp.dot(hb, win_ref[:, c0:c1], preferred_element_type=F32)

    ones_bd = ones_ref[...]
    q_ref[...] = _group_rms(proj(0, D_ATTN), ones_bd, gq_ref[...]).astype(BF16)
    k_ref[...] = _group_rms(proj(D_ATTN, 2 * D_ATTN), ones_bd, gk_ref[...]).astype(BF16)
    v_ref[...] = proj(2 * D_ATTN, 3 * D_ATTN)
    vt = v_ref[...].T.astype(BF16)
    for hd in range(N_HEADS):
        vt_ref[hd, 0:D_HEAD_V, :] = vt[hd * D_HEAD_V:(hd + 1) * D_HEAD_V, :]
        vt_ref[hd, D_HEAD_V:V_ROWS, :] = jnp.ones((V_ROWS - D_HEAD_V, ts), BF16)
    c_u = 3 * D_ATTN
    c_ga = c_u + D_POOL
    c_gp = c_ga + x.shape[1]
    sga_ref[...] = jax.nn.sigmoid(proj(c_ga, c_gp)).astype(BF16)

    u = proj(c_u, c_ga)

    @pl.when(tb == 0)
    def _():
        ext_ref[0:POOL_HALO, :] = jnp.zeros((POOL_HALO, D_POOL), F32)

    ext_ref[POOL_HALO:POOL_HALO + ts, :] = u
    row = lax.broadcasted_iota(jnp.int32, (ts, 1), 0) + tb * ts
    ys = []
    for g, w in enumerate(POOL_WINDOWS):
        c0 = g * POOL_GROUP_DIM
        ug = u[:, c0:c0 + POOL_GROUP_DIM]
        acc = ug
        for d in range(1, w):
            acc = acc + ext_ref[POOL_HALO - d:POOL_HALO - d + ts, c0:c0 + POOL_GROUP_DIM]
        cnt = jnp.minimum(row + 1, w).astype(F32)
        ys.append((acc / cnt - ug).astype(BF16))
    ypool = jnp.dot(jnp.concatenate(ys, axis=1), wpool_ref[...], preferred_element_type=F32)
    pc_ref[...] = (jax.nn.sigmoid(proj(c_gp, c_gp + x.shape[1])) * ypool).astype(BF16)
    ext_ref[0:POOL_HALO, :] = u[ts - POOL_HALO:ts, :]


def _inproj(x2, mod3, g1, w_in, gq, gk, pool_w, pool_scale, w_bp, *, bsz, seq, ts):
    n, d = x2.shape
    d_in = w_in.shape[1]
    tiles_per_batch = seq // ts
    win_b = w_in.astype(BF16)
    idx = jnp.arange(D_ATTN) // HEAD_DIM
    ones_bd = (idx[:, None] == idx[None, :]).astype(BF16)
    kern = functools.partial(_inproj_kernel, ts=ts, tiles_per_batch=tiles_per_batch)
    tok = lambda i: (i, 0)
    out_shape = (jax.ShapeDtypeStruct((n, D_ATTN), BF16),
                 jax.ShapeDtypeStruct((n, D_ATTN), BF16),
                 jax.ShapeDtypeStruct((bsz, N_HEADS, V_ROWS, seq), BF16),
                 jax.ShapeDtypeStruct((n, d), BF16),
                 jax.ShapeDtypeStruct((n, d), BF16))
    return pl.pallas_call(
        kern,
        out_shape=out_shape,
        grid=(n // ts,),
        in_specs=[pl.BlockSpec((ts, d), tok),
                  pl.BlockSpec((None, 6, d), lambda i: (i // tiles_per_batch, 0, 0)),
                  _const_spec((1, d)),
                  _const_spec((d, d_in)),
                  _const_spec((1, D_ATTN)),
                  _const_spec((1, D_ATTN)),
                  _const_spec((D_ATTN, D_ATTN)),
                  _const_spec((D_POOL, d))],
        out_specs=(pl.BlockSpec((ts, D_ATTN), tok),
                   pl.BlockSpec((ts, D_ATTN), tok),
                   pl.BlockSpec((None, N_HEADS, V_ROWS, ts),
                                lambda i: (i // tiles_per_batch, 0, 0, i % tiles_per_batch)),
                   pl.BlockSpec((ts, d), tok),
                   pl.BlockSpec((ts, d), tok)),
        scratch_shapes=[pltpu.VMEM((POOL_HALO + ts, D_POOL), F32), pltpu.VMEM((ts, D_ATTN), F32)],
        compiler_params=_cparams(1),
        name="inproj",
    )(x2, mod3, g1.reshape(1, d), win_b, gq.reshape(1, D_ATTN), gk.reshape(1, D_ATTN), ones_bd,
      _pool_fold(pool_w, pool_scale, w_bp))


def _attn_kernel(q_ref, qn_ref, k_ref, vt_ref, bias_ref, lq1_ref, lk1_ref, lq2_ref, lk2_ref, subg_ref,
                 *rest, t, lambda_init, n_cast):
    cast_in = rest[:n_cast]
    o_ref, clear_ref = rest[n_cast:n_cast + 2]
    cast_out = rest[n_cast + 2:2 * n_cast + 2]
    s_ref, mt_ref, m_ref, acc_ref = rest[2 * n_cast + 2:]
    i = pl.program_id(2)
    last = pl.num_programs(2) - 1
    clear_ref[...] = jnp.zeros(clear_ref.shape, clear_ref.dtype)
    for src, dst in zip(cast_in, cast_out):
        dst[...] = src[...].astype(dst.dtype)

    def split_maps(q):
        lane = lax.broadcasted_iota(jnp.int32, q.shape, 1)
        zero = jnp.zeros_like(q)
        return jnp.where(lane < HEAD_DIM, q, zero), jnp.where(lane >= HEAD_DIM, q, zero)

    q_now = split_maps(q_ref[...])

    m_ref[...] = jnp.full(m_ref.shape, MASKED, F32)
    acc_ref[...] = jnp.zeros(acc_ref.shape, F32)

    def scores(j, bias, slot, qm=q_now):
        kt = k_ref[pl.ds(pl.multiple_of(j * t, t), t), :]
        for mp in range(2):
            s = lax.dot_general(kt, qm[mp], (((1,), (1,)), ((), ())), preferred_element_type=F32)
            if bias is not None:
                s = s + bias
            s_ref[slot, mp] = s
            mt_ref[slot, mp] = jnp.max(s, axis=0, keepdims=True)

    def accumulate(j, slot):
        vt = vt_ref[:, pl.ds(pl.multiple_of(j * t, t), t)]
        for mp in range(2):
            m_old = m_ref[mp]
            m_new = jnp.maximum(m_old, mt_ref[slot, mp])
            p = jnp.exp2(s_ref[slot, mp] - m_new).astype(BF16)
            acc_ref[mp] = (jnp.exp2(m_old - m_new) * acc_ref[mp]
                           + jnp.dot(vt, p, preferred_element_type=F32))
            m_ref[mp] = m_new

    def next_diagonal():
        nxt = jnp.minimum(i + 1, last)
        scores(nxt, bias_ref[1], 2, split_maps(qn_ref[...]))

    @pl.when(i == 0)
    def _():
        scores(0, bias_ref[1], 0)

    @pl.when(i >= 1)
    def _():
        scores(i - 1, bias_ref[0], 1)
        accumulate(i, 2)

    n_far = jnp.maximum(i - 1, 0)

    def pair(kk):
        j = i - 1 - 2 * kk
        scores(j - 1, None, 0)
        accumulate(j, 1)
        scores(j - 2, None, 1)
        accumulate(j - 1, 0)

    def quad_body(qq, carry):
        pair(2 * qq)
        pair(2 * qq + 1)
        return carry

    n_pairs = n_far // 2
    lax.fori_loop(0, n_pairs // 2, quad_body, 0)

    @pl.when(n_pairs % 2 == 1)
    def _():
        pair(n_pairs - 1)

    @pl.when(n_far % 2 == 1)
    def _():
        scores(0, None, 0)
        accumulate(1, 1)
        next_diagonal()
        accumulate(0, 0)

    @pl.when(jnp.logical_and(i >= 1, n_far % 2 == 0))
    def _():
        next_diagonal()
        accumulate(0, 1)

    @pl.when(i == 0)
    def _():
        next_diagonal()
        accumulate(0, 0)

    lam = (jnp.exp(jnp.sum(lq1_ref[...] * lk1_ref[...], axis=1, keepdims=True))
           - jnp.exp(jnp.sum(lq2_ref[...] * lk2_ref[...], axis=1, keepdims=True)) + lambda_init)
    o1 = acc_ref[0, 0:D_HEAD_V, :] / acc_ref[0, D_HEAD_V:D_HEAD_V + 1, :]
    o2 = acc_ref[1, 0:D_HEAD_V, :] / acc_ref[1, D_HEAD_V:D_HEAD_V + 1, :]
    ot = o1 - lam * o2
    ot = ot * lax.rsqrt(jnp.mean(ot * ot, axis=0, keepdims=True) + RMS_EPS)
    ot = ot * subg_ref[...] * (1.0 - lambda_init)
    o_ref[...] = ot.T.astype(BF16)


def _attention(q, k, vt, bias, lq1, lk1, lq2, lk2, subln_g, *, t, lambda_init, clear_shape, to_bf16):
    bsz, seq, _ = q.shape
    nq = seq // t
    assert t + 1 >= LOG_BUCKET_STARTS[-1], "keys two tiles back must all fall in the last distance bucket"
    steps = bsz * N_HEADS * nq
    step_id = lambda b, h, i: ((b * N_HEADS + h) * nq + i, 0)
    clear_rows = clear_shape[0] // steps
    assert clear_rows * steps == clear_shape[0] and clear_rows % SUBLANES == 0, (clear_shape, steps)
    cast_specs = []
    for a in to_bf16:
        rows = a.shape[0] // steps
        assert rows * steps == a.shape[0] and rows % (2 * SUBLANES) == 0, (a.shape, steps)
        cast_specs.append(pl.BlockSpec((rows, a.shape[1]), step_id))
    kern = functools.partial(_attn_kernel, t=t, lambda_init=lambda_init, n_cast=len(to_bf16))
    vec = lambda a: a.reshape(1, HEAD_DIM)
    return pl.pallas_call(
        kern,
        out_shape=(jax.ShapeDtypeStruct((bsz, seq, D_ATTN), BF16),
                   jax.ShapeDtypeStruct(clear_shape, F32),
                   *[jax.ShapeDtypeStruct(a.shape, BF16) for a in to_bf16]),
        grid=(bsz, N_HEADS, seq // t),
        in_specs=[pl.BlockSpec((None, t, D_HEAD_V), lambda b, h, i: (b, i, h)),
                  pl.BlockSpec((None, t, D_HEAD_V), lambda b, h, i: (b, jnp.minimum(i + 1, nq - 1), h)),
                  pl.BlockSpec((None, seq, D_HEAD_V), lambda b, h, i: (b, 0, h)),
                  pl.BlockSpec((None, None, V_ROWS, seq), lambda b, h, i: (b, h, 0, 0)),
                  pl.BlockSpec((None, 2, t, t), lambda b, h, i: (h, 0, 0, 0)),
                  _const_spec((1, HEAD_DIM)), _const_spec((1, HEAD_DIM)),
                  _const_spec((1, HEAD_DIM)), _const_spec((1, HEAD_DIM)),
                  _const_spec((D_HEAD_V, 1)), *cast_specs],
        out_specs=(pl.BlockSpec((None, t, D_HEAD_V), lambda b, h, i: (b, i, h)),
                   pl.BlockSpec((clear_rows, clear_shape[1]), step_id), *cast_specs),
        scratch_shapes=[pltpu.VMEM((3, 2, t, t), F32),
                        pltpu.VMEM((3, 2, 1, t), F32),
                        pltpu.VMEM((2, 1, t), F32),
                        pltpu.VMEM((2, V_ROWS, t), F32)],
        compiler_params=_cparams(3),
        name="attn",
    )(q, q, k, vt, bias, vec(lq1), vec(lk1), vec(lq2), vec(lk2), subln_g.reshape(D_HEAD_V, 1),
      *to_bf16)


def _first_max(rows):
    best = rows[0]
    for r in rows[1:]:
        best = jnp.maximum(best, r)
    idx = jnp.full(best.shape, len(rows) - 1, jnp.int32)
    for j in range(len(rows) - 2, -1, -1):
        idx = jnp.where(rows[j] == best, j, idx)
    return best, idx


def _post_kernel(o_ref, sga_ref, pc_ref, x_ref, mod_ref, g2_ref, wba_ref, wout_ref, wr_ref, br_ref,
                 tri_ref, x1_ref, h2p_ref, route_ref, cnt_ref, carry_ref):
    @pl.when(pl.program_id(0) == 0)
    def _():
        carry_ref[...] = jnp.zeros(carry_ref.shape, F32)

    ya = jnp.dot(o_ref[...], wba_ref[...], preferred_element_type=F32)
    merged = sga_ref[...].astype(F32) * ya + pc_ref[...].astype(F32)
    z = jnp.dot(merged.astype(BF16), wout_ref[...], preferred_element_type=F32)
    x1 = x_ref[...] + mod_ref[2:3, :] * z
    x1_ref[...] = x1
    y = x1 * lax.rsqrt(jnp.mean(x1 * x1, axis=-1, keepdims=True) + RMS_EPS)
    h2 = y * (g2_ref[...] * (1.0 + mod_ref[4:5, :])) + mod_ref[3:4, :]
    hi = h2.astype(BF16)
    lo = (h2 - hi.astype(F32)).astype(BF16)

    nt = (((1,), (1,)), ((), ()))
    a = lax.dot_general(wr_ref[...], hi, nt, preferred_element_type=F32)
    b = lax.dot_general(wr_ref[0:ROUTER_ROWS, :], lo, nt, preferred_element_type=F32)
    logits = a[0:ROUTER_ROWS] + a[ROUTER_ROWS:2 * ROUTER_ROWS] + b + br_ref[...]

    gl = [logits[g:g + 1, :] for g in range(N_EXPERT_GROUPS)]
    gmax, gidx = _first_max(gl)
    gsum = gl[0] * 0.0
    for r in gl:
        gsum = gsum + jnp.exp(r - gmax)
    g_val = 1.0 / gsum
    es = []
    for r in range(EXPERTS_PER_GROUP):
        sel = jnp.zeros_like(gmax)
        for g in range(N_EXPERT_GROUPS):
            row = EXPERT_ROW0 + g * EXPERTS_PER_GROUP + r
            sel = jnp.where(gidx == g, logits[row:row + 1, :], sel)
        es.append(sel)
    e1, i1 = _first_max(es)
    rest = [jnp.where(i1 == r, -jnp.inf, es[r]) for r in range(EXPERTS_PER_GROUP)]
    e2, i2 = _first_max(rest)
    r21 = jnp.exp(e2 - e1)
    w1 = g_val / (1.0 + r21)
    w2 = g_val * r21 / (1.0 + r21)

    first = i1 < i2
    e_lo = jnp.where(first, i1, i2)
    e_hi = jnp.where(first, i2, i1)
    pair = jnp.zeros_like(e_lo)
    for p in range(1, PAIRS_PER_GROUP):
        pair = jnp.where(jnp.logical_and(e_lo == PAIR_LO[p], e_hi == PAIR_HI[p]), p, pair)
    bucket = gidx * PAIRS_PER_GROUP + pair
    w_lo = jnp.where(first, w1, w2)
    w_hi = jnp.where(first, w2, w1)

    ts = bucket.shape[1]
    brow = lax.broadcasted_iota(jnp.int32, (ROUTER_ROWS, ts), 0)
    onehot = brow == bucket
    prefix = jnp.dot(jnp.where(onehot, 1.0, 0.0).astype(BF16), tri_ref[...],
                     preferred_element_type=F32)
    carry = carry_ref[...]
    rank = jnp.sum(jnp.where(onehot, prefix + carry, 0.0), axis=0, keepdims=True) - 1.0
    carry = carry + prefix[:, ts - 1:ts]
    carry_ref[...] = carry
    cnt_ref[...] = jnp.broadcast_to(carry, cnt_ref.shape).astype(jnp.int32)
    chunks = ts // LANES
    rank_i = rank.astype(jnp.int32)
    for j in range(chunks):
        route_ref[j:j + 1, :] = bucket[:, j * LANES:(j + 1) * LANES]
        route_ref[chunks + j:chunks + j + 1, :] = rank_i[:, j * LANES:(j + 1) * LANES]

    arow = lax.broadcasted_iota(jnp.int32, (LANES, ts), 0)
    aux_t = jnp.where(arow == 0, w_lo, jnp.where(arow == 1, w_hi, 0.0))
    d = h2.shape[1]
    h2p_ref[:, 0:d] = h2
    h2p_ref[:, d:d + LANES] = aux_t.T


def _post(o2, sga, pc, x2, mod3, g2, w_ba, w_out, wg_r, bg_r, we_r, be_r, *, seq, ts):
    n, d = x2.shape
    tiles_per_batch = seq // ts
    steps = n // ts
    wr = jnp.zeros((ROUTER_ROWS, d), F32)
    wr = wr.at[0:N_EXPERT_GROUPS].set(wg_r.T).at[EXPERT_ROW0:EXPERT_ROW0 + N_EXPERTS].set(we_r.T)
    wr_hi = wr.astype(BF16)
    wr_lo = (wr - wr_hi.astype(F32)).astype(BF16)
    br = jnp.zeros((ROUTER_ROWS, 1), F32)
    br = br.at[0:N_EXPERT_GROUPS, 0].set(bg_r).at[EXPERT_ROW0:EXPERT_ROW0 + N_EXPERTS, 0].set(be_r)
    tok = lambda i: (i, 0)
    tidx = jnp.arange(ts)
    tri = (tidx[:, None] <= tidx[None, :]).astype(BF16)
    return pl.pallas_call(
        _post_kernel,
        out_shape=(jax.ShapeDtypeStruct((n, d), F32),
                   jax.ShapeDtypeStruct((n, d + LANES), F32),
                   jax.ShapeDtypeStruct((steps, 2 * (ts // LANES), LANES), jnp.int32),
                   jax.ShapeDtypeStruct((ROUTER_ROWS, LANES), jnp.int32)),
        grid=(steps,),
        in_specs=[pl.BlockSpec((ts, D_ATTN), tok),
                  pl.BlockSpec((ts, d), tok),
                  pl.BlockSpec((ts, d), tok),
                  pl.BlockSpec((ts, d), tok),
                  pl.BlockSpec((None, 6, d), lambda i: (i // tiles_per_batch, 0, 0)),
                  _const_spec((1, d)),
                  _const_spec((D_ATTN, d)),
                  _const_spec((d, d)),
                  _const_spec((2 * ROUTER_ROWS, d)),
                  _const_spec((ROUTER_ROWS, 1)),
                  _const_spec((ts, ts))],
        out_specs=(pl.BlockSpec((ts, d), tok), pl.BlockSpec((ts, d + LANES), tok),
                   pl.BlockSpec((None, 2 * (ts // LANES), LANES), lambda i: (i, 0, 0)),
                   pl.BlockSpec((ROUTER_ROWS, LANES), lambda i: (0, 0))),
        scratch_shapes=[pltpu.VMEM((ROUTER_ROWS, 1), F32)],
        compiler_params=_cparams(1),
        name="post",
    )(o2, sga, pc, x2, mod3, g2.reshape(1, d), w_ba.astype(BF16), w_out.astype(BF16),
      jnp.concatenate([wr_hi, wr_lo], axis=0), br, tri)


def _group_sublane(row):
    return lax.shift_right_logical(row, SUBLANES.bit_length() - 1), row & (SUBLANES - 1)


def _row_copy(src, src_row, dst, dst_row, sem):
    sg, ss = src_row
    dg, ds = dst_row
    return pltpu.make_async_copy(src.at[sg, pl.ds(ss, 1), :], dst.at[dg, pl.ds(ds, 1), :], sem)


def _dispatch_kernel(pos_ref, h_ref, init_ref, hs_ref, sem, *, ts):
    del init_ref
    base = pl.program_id(0) * ts

    def start(g, carry):
        for u in range(SUBLANES):
            p = pos_ref[base + g * SUBLANES + u]
            _row_copy(h_ref, (g, u), hs_ref, _group_sublane(p), sem).start(priority=u % 2)
        return carry

    def wait(g, carry):
        for u in range(SUBLANES):
            _row_copy(h_ref, (0, 0), hs_ref, (0, 0), sem).wait()
        return carry

    lax.fori_loop(0, ts // SUBLANES, start, 0)
    lax.fori_loop(0, ts // SUBLANES, wait, 0)


def _dispatch(pos, h2p, cleared, *, ts):
    n, w = h2p.shape
    n_rows = cleared.shape[0]
    return pl.pallas_call(
        functools.partial(_dispatch_kernel, ts=ts),
        out_shape=jax.ShapeDtypeStruct((n_rows // SUBLANES, SUBLANES, w), h2p.dtype),
        grid_spec=pltpu.PrefetchScalarGridSpec(
            num_scalar_prefetch=1,
            grid=(n // ts,),
            in_specs=[pl.BlockSpec((ts // SUBLANES, SUBLANES, w), lambda i, pos: (i, 0, 0)),
                      pl.BlockSpec(memory_space=pl.ANY)],
            out_specs=pl.BlockSpec(memory_space=pl.ANY),
            scratch_shapes=[pltpu.SemaphoreType.DMA]),
        input_output_aliases={2: 0},
        compiler_params=_cparams(1),
        name="dispatch",
    )(pos, h2p.reshape(n // SUBLANES, SUBLANES, w),
      cleared.reshape(n_rows // SUBLANES, SUBLANES, w)).reshape(n_rows, w)


def _experts_kernel(ea_ref, eb_ref, valid_ref, hs_ref, wga_ref, wua_ref, wda_ref, wgb_ref, wub_ref,
                    wdb_ref, ys_ref):
    del ea_ref, eb_ref
    t = pl.program_id(0)
    d = hs_ref.shape[1] - LANES

    @pl.when(valid_ref[t] == 1)
    def _():
        h = hs_ref[:, 0:d].astype(BF16)
        aux = hs_ref[:, d:d + LANES]

        def expert(wg_ref, wu_ref, wd_ref):
            a = jnp.dot(h, wg_ref[...], preferred_element_type=F32)
            b = jnp.dot(h, wu_ref[...], preferred_element_type=F32)
            hid = (a * jax.nn.sigmoid(a)) * b
            return jnp.dot(hid.astype(BF16), wd_ref[...], preferred_element_type=F32)

        ys_ref[...] = (aux[:, 0:1] * expert(wga_ref, wua_ref, wda_ref)
                       + aux[:, 1:2] * expert(wgb_ref, wub_ref, wdb_ref))

    @pl.when(valid_ref[t] == 0)
    def _():
        ys_ref[...] = jnp.zeros(ys_ref.shape, ys_ref.dtype)


def _experts(tile_ea, tile_eb, tile_valid, hs, w_gate, w_up, w_down, *, tm):
    n_rows, w = hs.shape
    _, d, de = w_gate.shape
    wg, wu, wd = w_gate, w_up, w_down
    ea = lambda t, ea_r, eb_r, v_r: (ea_r[t], 0, 0)
    eb = lambda t, ea_r, eb_r, v_r: (eb_r[t], 0, 0)
    row = lambda t, ea_r, eb_r, v_r: (t, 0)
    return pl.pallas_call(
        _experts_kernel,
        out_shape=jax.ShapeDtypeStruct((n_rows, d), F32),
        grid_spec=pltpu.PrefetchScalarGridSpec(
            num_scalar_prefetch=3,
            grid=(n_rows // tm,),
            in_specs=[pl.BlockSpec((tm, w), row),
                      pl.BlockSpec((None, d, de), ea), pl.BlockSpec((None, d, de), ea),
                      pl.BlockSpec((None, de, d), ea),
                      pl.BlockSpec((None, d, de), eb), pl.BlockSpec((None, d, de), eb),
                      pl.BlockSpec((None, de, d), eb)],
            out_specs=pl.BlockSpec((tm, d), row)),
        compiler_params=_cparams(1),
        name="experts",
    )(tile_ea, tile_eb, tile_valid, hs, wg, wu, wd, wg, wu, wd)


def _combine_kernel(pos_ref, ys_ref, x1_ref, mod_ref, out_ref, ybuf_ref, sem, *, ts):
    i = pl.program_id(0)
    groups = ts // SUBLANES

    def gather(tile, slot):
        base = tile * ts

        def start(g, carry):
            for u in range(SUBLANES):
                p = pos_ref[base + g * SUBLANES + u]
                _row_copy(ys_ref, _group_sublane(p), ybuf_ref.at[slot], (g, u),
                          sem.at[slot]).start(priority=u % 2)
            return carry

        lax.fori_loop(0, groups, start, 0)

    def wait_all(slot):
        def wait(g, carry):
            for u in range(SUBLANES):
                _row_copy(ys_ref, (0, 0), ybuf_ref.at[slot], (0, 0), sem.at[slot]).wait()
            return carry

        lax.fori_loop(0, groups, wait, 0)

    def step(slot):
        @pl.when(i + 1 < pl.num_programs(0))
        def _():
            gather(i + 1, 1 - slot)

        wait_all(slot)
        y = ybuf_ref[slot].reshape(ts, ybuf_ref.shape[-1])
        out_ref[...] = x1_ref[...] + mod_ref[5:6, :] * y

    @pl.when(i == 0)
    def _():
        gather(0, 0)

    @pl.when(i % 2 == 0)
    def _():
        step(0)

    @pl.when(i % 2 == 1)
    def _():
        step(1)


def _combine(pos, ys, x1, mod3, *, seq, ts):
    n, d = x1.shape
    tiles_per_batch = seq // ts
    return pl.pallas_call(
        functools.partial(_combine_kernel, ts=ts),
        out_shape=jax.ShapeDtypeStruct((n, d), F32),
        grid_spec=pltpu.PrefetchScalarGridSpec(
            num_scalar_prefetch=1,
            grid=(n // ts,),
            in_specs=[pl.BlockSpec(memory_space=pl.ANY),
                      pl.BlockSpec((ts, d), lambda i, pos: (i, 0)),
                      pl.BlockSpec((None, 6, d), lambda i, pos: (i // tiles_per_batch, 0, 0))],
            out_specs=pl.BlockSpec((ts, d), lambda i, pos: (i, 0)),
            scratch_shapes=[pltpu.VMEM((2, ts // SUBLANES, SUBLANES, d), F32),
                            pltpu.SemaphoreType.DMA((2,))]),
        compiler_params=_cparams(1),
        name="combine",
    )(pos, ys.reshape(ys.shape[0] // SUBLANES, SUBLANES, d), x1, mod3)


def _pos_kernel(start_ref, route_ref, pos_ref):
    chunks = pos_ref.shape[1]
    bucket = route_ref[:, 0:chunks, :]
    pos = route_ref[:, chunks:2 * chunks, :]
    for b in range(N_ROUTE_BUCKETS):
        pos = pos + jnp.where(bucket == b, start_ref[b], 0)
    pos_ref[...] = pos


def _route_tables(route, counts, n_tiles, tm):
    steps, rows, _ = route.shape
    cnt = counts[:N_ROUTE_BUCKETS, 0]
    padded = (cnt + tm - 1) // tm * tm
    end = jnp.cumsum(padded)
    pos = pl.pallas_call(
        _pos_kernel,
        out_shape=jax.ShapeDtypeStruct((steps, rows // 2, LANES), jnp.int32),
        in_specs=[pl.BlockSpec(memory_space=pltpu.SMEM), pl.BlockSpec(memory_space=pltpu.VMEM)],
        out_specs=pl.BlockSpec(memory_space=pltpu.VMEM),
        name="pos",
    )(end - padded, route).reshape(-1)
    tiles_used = end[-1] // tm
    tile = jnp.arange(n_tiles, dtype=jnp.int32)
    valid = tile < tiles_used
    first_row = jnp.minimum(tile, tiles_used - 1) * tm
    tile_bucket = jnp.sum((end[None, :] <= first_row[:, None]).astype(jnp.int32), axis=1)
    tile_bucket = jnp.minimum(tile_bucket, N_ROUTE_BUCKETS - 1)
    group = tile_bucket // PAIRS_PER_GROUP
    pair = tile_bucket % PAIRS_PER_GROUP
    tile_ea = group * EXPERTS_PER_GROUP + jnp.asarray(PAIR_LO, jnp.int32)[pair]
    tile_eb = group * EXPERTS_PER_GROUP + jnp.asarray(PAIR_HI, jnp.int32)[pair]
    return pos.astype(jnp.int32), tile_ea, tile_eb, valid.astype(jnp.int32)


def _tile(seq, pref):
    t = min(pref, seq)
    assert seq % t == 0 and t % LANES == 0, (seq, t)
    return t


def kernel(x, c, rel_bias, ada_w, ada_b, norm1_g, w_in, q_norm_g, k_norm_g, lambda_q1, lambda_k1,
           lambda_q2, lambda_k2, subln_g, w_branch_attn, pool_w, pool_scale, w_branch_pool, w_out,
           norm2_g, router_group_w, router_group_b, router_expert_w, router_expert_b,
           expert_w_gate, expert_w_up, expert_w_down):
    bsz, seq, d = x.shape
    n = bsz * seq
    ts = _tile(seq, TOKEN_TILE)
    t_attn = _tile(seq, ATTN_TILE)
    tr = _tile(seq, ROW_COPY_TILE)
    tm = MOE_TILE
    n_tiles = -(-(n + N_ROUTE_BUCKETS * (tm - 1)) // tm)
    attn_steps = bsz * N_HEADS * (seq // t_attn)
    while (n_tiles * tm) % (attn_steps * SUBLANES):
        n_tiles += 1
    bias = _bias_tiles(rel_bias, t_attn)
    x2 = x.reshape(n, d)
    for l in range(ada_w.shape[0]):
        lambda_init = 0.8 - 0.6 * math.exp(-0.3 * l)
        mod3 = _ada(c, ada_w[l], ada_b[l]).reshape(bsz, 6, d)
        gq = jnp.tile(q_norm_g[l], D_ATTN // HEAD_DIM) * (HEAD_DIM ** -0.5 * LOG2E)
        gk = jnp.tile(k_norm_g[l], D_ATTN // HEAD_DIM)
        q, k, vt, sga, pc = _inproj(x2, mod3, norm1_g[l], w_in[l], gq, gk, pool_w[l], pool_scale[l],
                                    w_branch_pool[l], bsz=bsz, seq=seq, ts=ts)
        experts_f32 = (expert_w_gate[l], expert_w_up[l], expert_w_down[l])
        o, cleared, *experts_bf16 = _attention(
            q.reshape(bsz, seq, D_ATTN), k.reshape(bsz, seq, D_ATTN), vt, bias,
            lambda_q1[l], lambda_k1[l], lambda_q2[l], lambda_k2[l], subln_g[l],
            t=t_attn, lambda_init=lambda_init, clear_shape=(n_tiles * tm, d + LANES),
            to_bf16=[w.reshape(-1, w.shape[-1]) for w in experts_f32])
        wg_b, wu_b, wd_b = [wb.reshape(w.shape) for wb, w in zip(experts_bf16, experts_f32)]
        x1, h2p, route, counts = _post(
            o.reshape(n, D_ATTN), sga, pc, x2, mod3, norm2_g[l], w_branch_attn[l], w_out[l],
            router_group_w[l], router_group_b[l], router_expert_w[l], router_expert_b[l],
            seq=seq, ts=ts)
        pos, tile_ea, tile_eb, tile_valid = _route_tables(route, counts, n_tiles, tm)
        hs = _dispatch(pos, h2p, cleared, ts=tr)
        ys = _experts(tile_ea, tile_eb, tile_valid, hs, wg_b, wu_b, wd_b, tm=tm)
        x2 = _combine(pos, ys, x1, mod3, seq=seq, ts=tr)
    return x2.reshape(bsz, seq, d)
```

```python
import functools
import math

import jax
import jax.numpy as jnp
from jax import lax
from jax.experimental import pallas as pl
from jax.experimental.pallas import tpu as pltpu

F32 = jnp.float32
BF16 = jnp.bfloat16

CHUNK = 64
N_HEADS = 4
HEAD_DIM = 64
D_HEAD_V = 2 * HEAD_DIM
V_ROWS = D_HEAD_V + 16
D_ATTN = N_HEADS * D_HEAD_V
POOL_WINDOWS = (2, 4, 8, 16)
POOL_GROUP_DIM = 128
D_POOL = len(POOL_WINDOWS) * POOL_GROUP_DIM
POOL_HALO = 16
N_BUCKETS = 32
MAX_DISTANCE = 128
N_EXPERT_GROUPS = 4
EXPERTS_PER_GROUP = 4
N_EXPERTS = N_EXPERT_GROUPS * EXPERTS_PER_GROUP
PAIRS_PER_GROUP = 6
PAIR_LO = (0, 0, 1, 1, 0, 2)
PAIR_HI = (1, 2, 2, 3, 3, 3)
N_ROUTE_BUCKETS = N_EXPERT_GROUPS * PAIRS_PER_GROUP
RMS_EPS = 1e-6
LOG2E = math.log2(math.e)
MASKED = -1e30

LANES = 128
SUBLANES = 8
ROUTER_ROWS = 32
EXPERT_ROW0 = 8

VMEM_LIMIT = 56 * 1024 * 1024
TOKEN_TILE = 512
INPROJ_TILE = 1024
ATTN_TILE = 512
MOE_TILE = 256
ROW_COPY_TILE = 1024


def _cparams(n_axes):
    return pltpu.CompilerParams(dimension_semantics=("arbitrary",) * n_axes,
                                vmem_limit_bytes=VMEM_LIMIT)


def _const_spec(shape):
    nd = len(shape)
    return pl.BlockSpec(shape, lambda *_: (0,) * nd, pipeline_mode=pl.Buffered(1))


def _ada_kernel(ct_ref, w_ref, b_ref, o_ref):
    @pl.when(pl.program_id(0) == 0)
    def _():
        o_ref[...] = jnp.broadcast_to(b_ref[...], o_ref.shape)

    ct = ct_ref[...]
    s = ct * jax.nn.sigmoid(ct)
    w = w_ref[...]
    rows = [jnp.sum(w * s[:, b:b + 1], axis=0, keepdims=True) for b in range(ct.shape[1])]
    o_ref[...] += jnp.concatenate(rows, axis=0)


def _ada(c, w, b):
    bsz, d = c.shape
    n = w.shape[1]
    rows = 256
    return pl.pallas_call(
        _ada_kernel,
        out_shape=jax.ShapeDtypeStruct((bsz, n), F32),
        grid=(d // rows,),
        in_specs=[pl.BlockSpec((rows, bsz), lambda j: (j, 0)),
                  pl.BlockSpec((rows, n), lambda j: (j, 0)),
                  pl.BlockSpec((1, n), lambda j: (0, 0))],
        out_specs=pl.BlockSpec((bsz, n), lambda j: (0, 0)),
        compiler_params=_cparams(1),
        name="ada",
    )(c.T, w, b.reshape(1, n))


def _log_bucket_starts():
    nb = N_BUCKETS // 2
    max_exact = nb // 2
    m = nb - max_exact
    ratio = MAX_DISTANCE // max_exact
    starts = []
    for k in range(1, m):
        n = max_exact
        while n ** m < max_exact ** m * ratio ** k:
            n += 1
        starts.append(n)
    return tuple(starts)


LOG_BUCKET_STARTS = _log_bucket_starts()


def _bias_kernel(rb_ref, o_ref, *, t):
    h = pl.program_id(0)
    kind = pl.program_id(1)
    nb = N_BUCKETS // 2
    max_exact = nb // 2
    kpos = lax.broadcasted_iota(jnp.int32, (t, t), 0)
    qpos = lax.broadcasted_iota(jnp.int32, (t, t), 1)
    rel = kpos - qpos - jnp.where(kind == 0, t, 0)
    n = jnp.abs(rel)

    def table(first):
        val = jnp.full((t, t), rb_ref[first + nb - 1, h], F32)
        for k in range(len(LOG_BUCKET_STARTS) - 1, -1, -1):
            val = jnp.where(n < LOG_BUCKET_STARTS[k], rb_ref[first + max_exact + k, h], val)
        for j in range(max_exact - 1, -1, -1):
            val = jnp.where(n == j, rb_ref[first + j, h], val)
        return val

    bias = jnp.where(rel > 0, table(nb), table(0))
    shift = CHUNK.bit_length() - 1
    hidden = jnp.logical_and(kind == 1, (kpos >> shift) > (qpos >> shift))
    o_ref[...] = jnp.where(hidden, MASKED, (bias - rb_ref[nb - 1, h]) * LOG2E)


def _bias_tiles(rel_bias, t):
    return pl.pallas_call(
        functools.partial(_bias_kernel, t=t),
        out_shape=jax.ShapeDtypeStruct((N_HEADS, 2, t, t), F32),
        grid=(N_HEADS, 2),
        in_specs=[pl.BlockSpec(memory_space=pltpu.SMEM)],
        out_specs=pl.BlockSpec((None, None, t, t), lambda h, j: (h, j, 0, 0)),
        compiler_params=_cparams(2),
        name="bias_tiles",
    )(rel_bias)


def _group_rms(xc, ones_blockdiag, gain):
    ssq = jnp.dot((xc * xc).astype(BF16), ones_blockdiag, preferred_element_type=F32)
    return xc * lax.rsqrt(ssq * (1.0 / HEAD_DIM) + RMS_EPS) * gain


def _split_bf16(a):
    hi = a.astype(BF16)
    return hi, (a - hi.astype(F32)).astype(BF16)


def _pool_fold_kernel(pw_ref, ps_ref, wbp_ref, o_ref):
    a_hi, a_lo = _split_bf16(pw_ref[...] * ps_ref[...])
    b_hi, b_lo = _split_bf16(wbp_ref[...])
    dot = functools.partial(jnp.dot, preferred_element_type=F32)
    o_ref[...] = (dot(a_hi, b_hi) + dot(a_hi, b_lo) + dot(a_lo, b_hi)).astype(o_ref.dtype)


def _pool_fold(pool_w, pool_scale, w_bp):
    g, c, _ = pool_w.shape
    d = w_bp.shape[1]
    return pl.pallas_call(
        _pool_fold_kernel,
        out_shape=jax.ShapeDtypeStruct((g * c, d), BF16),
        grid=(g,),
        in_specs=[pl.BlockSpec((None, c, c), lambda i: (i, 0, 0)),
                  pl.BlockSpec((None, 1, c), lambda i: (i, 0, 0)),
                  pl.BlockSpec((c, d), lambda i: (i, 0))],
        out_specs=pl.BlockSpec((c, d), lambda i: (i, 0)),
        compiler_params=_cparams(1),
        name="pool_fold",
    )(pool_w, pool_scale.reshape(g, 1, c), w_bp)


def _inproj_kernel(x_ref, mod_ref, g1_ref, win_ref, gq_ref, gk_ref, ones_ref, wpool_ref,
                   q_ref, k_ref, vt_ref, sga_ref, pc_ref, ext_ref, v_ref,
                   *, ts, tiles_per_batch):
    tb = pl.program_id(0) % tiles_per_batch
    x = x_ref[...]
    y = x * lax.rsqrt(jnp.mean(x * x, axis=-1, keepdims=True) + RMS_EPS)
    h = y * (g1_ref[...] * (1.0 + mod_ref[1:2, :])) + mod_ref[0:1, :]
    hb = h.astype(BF16)

    def proj(c0, c1):
        return jnp.dot(hb, win_ref[:, c0:c1], preferred_element_type=F32)

    ones_bd = ones_ref[...]
    q_ref[...] = _group_rms(proj(0, D_ATTN), ones_bd, gq_ref[...]).astype(BF16)
    k_ref[...] = _group_rms(proj(D_ATTN, 2 * D_ATTN), ones_bd, gk_ref[...]).astype(BF16)
    v_ref[...] = proj(2 * D_ATTN, 3 * D_ATTN)
    vt = v_ref[...].T.astype(BF16)
    for hd in range(N_HEADS):
        vt_ref[hd, 0:D_HEAD_V, :] = vt[hd * D_HEAD_V:(hd + 1) * D_HEAD_V, :]
        vt_ref[hd, D_HEAD_V:V_ROWS, :] = jnp.ones((V_ROWS - D_HEAD_V, ts), BF16)
    c_u = 3 * D_ATTN
    c_ga = c_u + D_POOL
    c_gp = c_ga + x.shape[1]
    sga_ref[...] = jax.nn.sigmoid(proj(c_ga, c_gp)).astype(BF16)

    u = proj(c_u, c_ga)

    @pl.when(tb == 0)
    def _():
        ext_ref[0:POOL_HALO, :] = jnp.zeros((POOL_HALO, D_POOL), F32)

    ext_ref[POOL_HALO:POOL_HALO + ts, :] = u
    row = lax.broadcasted_iota(jnp.int32, (ts, 1), 0) + tb * ts
    ys = []
    for g, w in enumerate(POOL_WINDOWS):
        c0 = g * POOL_GROUP_DIM
        ug = u[:, c0:c0 + POOL_GROUP_DIM]
        acc = ug
        for d in range(1, w):
            acc = acc + ext_ref[POOL_HALO - d:POOL_HALO - d + ts, c0:c0 + POOL_GROUP_DIM]
        cnt = jnp.minimum(row + 1, w).astype(F32)
        ys.append((acc / cnt - ug).astype(BF16))
    ypool = jnp.dot(jnp.concatenate(ys, axis=1), wpool_ref[...], preferred_element_type=F32)
    pc_ref[...] = (jax.nn.sigmoid(proj(c_gp, c_gp + x.shape[1])) * ypool).astype(BF16)
    ext_ref[0:POOL_HALO, :] = u[ts - POOL_HALO:ts, :]


def _inproj(x2, mod3, g1, w_in, gq, gk, pool_w, pool_scale, w_bp, *, bsz, seq, ts):
    n, d = x2.shape
    d_in = w_in.shape[1]
    tiles_per_batch = seq // ts
    win_b = w_in.astype(BF16)
    idx = jnp.arange(D_ATTN) // HEAD_DIM
    ones_bd = (idx[:, None] == idx[None, :]).astype(BF16)
    kern = functools.partial(_inproj_kernel, ts=ts, tiles_per_batch=tiles_per_batch)
    tok = lambda i: (i, 0)
    out_shape = (jax.ShapeDtypeStruct((n, D_ATTN), BF16),
                 jax.ShapeDtypeStruct((n, D_ATTN), BF16),
                 jax.ShapeDtypeStruct((bsz, N_HEADS, V_ROWS, seq), BF16),
                 jax.ShapeDtypeStruct((n, d), BF16),
                 jax.ShapeDtypeStruct((n, d), BF16))
    return pl.pallas_call(
        kern,
        out_shape=out_shape,
        grid=(n // ts,),
        in_specs=[pl.BlockSpec((ts, d), tok),
                  pl.BlockSpec((None, 6, d), lambda i: (i // tiles_per_batch, 0, 0)),
                  _const_spec((1, d)),
                  _const_spec((d, d_in)),
                  _const_spec((1, D_ATTN)),
                  _const_spec((1, D_ATTN)),
                  _const_spec((D_ATTN, D_ATTN)),
                  _const_spec((D_POOL, d))],
        out_specs=(pl.BlockSpec((ts, D_ATTN), tok),
                   pl.BlockSpec((ts, D_ATTN), tok),
                   pl.BlockSpec((None, N_HEADS, V_ROWS, ts),
                                lambda i: (i // tiles_per_batch, 0, 0, i % tiles_per_batch)),
                   pl.BlockSpec((ts, d), tok),
                   pl.BlockSpec((ts, d), tok)),
        scratch_shapes=[pltpu.VMEM((POOL_HALO + ts, D_POOL), F32), pltpu.VMEM((ts, D_ATTN), F32)],
        compiler_params=_cparams(1),
        name="inproj",
    )(x2, mod3, g1.reshape(1, d), win_b, gq.reshape(1, D_ATTN), gk.reshape(1, D_ATTN), ones_bd,
      _pool_fold(pool_w, pool_scale, w_bp))


def _attn_kernel(q_ref, qn_ref, k_ref, vt_ref, bias_ref, lq1_ref, lk1_ref, lq2_ref, lk2_ref, subg_ref,
                 *rest, t, lambda_init, n_cast):
    cast_in = rest[:n_cast]
    o_ref, clear_ref = rest[n_cast:n_cast + 2]
    cast_out = rest[n_cast + 2:2 * n_cast + 2]
    s_ref, mt_ref, m_ref, acc_ref = rest[2 * n_cast + 2:]
    i = pl.program_id(2)
    last = pl.num_programs(2) - 1
    clear_ref[...] = jnp.zeros(clear_ref.shape, clear_ref.dtype)
    for src, dst in zip(cast_in, cast_out):
        dst[...] = src[...].astype(dst.dtype)

    def split_maps(q):
        lane = lax.broadcasted_iota(jnp.int32, q.shape, 1)
        zero = jnp.zeros_like(q)
        return jnp.where(lane < HEAD_DIM, q, zero), jnp.where(lane >= HEAD_DIM, q, zero)

    q_now = split_maps(q_ref[...])

    m_ref[...] = jnp.full(m_ref.shape, MASKED, F32)
    acc_ref[...] = jnp.zeros(acc_ref.shape, F32)

    def scores(j, bias, slot, qm=q_now):
        kt = k_ref[pl.ds(pl.multiple_of(j * t, t), t), :]
        for mp in range(2):
            s = lax.dot_general(kt, qm[mp], (((1,), (1,)), ((), ())), preferred_element_type=F32)
            if bias is not None:
                s = s + bias
            s_ref[slot, mp] = s
            mt_ref[slot, mp] = jnp.max(s, axis=0, keepdims=True)

    def accumulate(j, slot):
        vt = vt_ref[:, pl.ds(pl.multiple_of(j * t, t), t)]
        for mp in range(2):
            m_old = m_ref[mp]
            m_new = jnp.maximum(m_old, mt_ref[slot, mp])
            p = jnp.exp2(s_ref[slot, mp] - m_new).astype(BF16)
            acc_ref[mp] = (jnp.exp2(m_old - m_new) * acc_ref[mp]
                           + jnp.dot(vt, p, preferred_element_type=F32))
            m_ref[mp] = m_new

    def next_diagonal():
        nxt = jnp.minimum(i + 1, last)
        scores(nxt, bias_ref[1], 2, split_maps(qn_ref[...]))

    @pl.when(i == 0)
    def _():
        scores(0, bias_ref[1], 0)

    @pl.when(i >= 1)
    def _():
        scores(i - 1, bias_ref[0], 1)
        accumulate(i, 2)

    n_far = jnp.maximum(i - 1, 0)

    def pair(kk):
        j = i - 1 - 2 * kk
        scores(j - 1, None, 0)
        accumulate(j, 1)
        scores(j - 2, None, 1)
        accumulate(j - 1, 0)

    def oct_body(oo, carry):
        for u in range(4):
            pair(4 * oo + u)
        return carry

    n_pairs = n_far // 2
    lax.fori_loop(0, n_pairs // 4, oct_body, 0)

    @pl.when(n_pairs % 4 >= 2)
    def _():
        pair(n_pairs // 4 * 4)
        pair(n_pairs // 4 * 4 + 1)

    @pl.when(n_pairs % 2 == 1)
    def _():
        pair(n_pairs - 1)

    @pl.when(n_far % 2 == 1)
    def _():
        scores(0, None, 0)
        accumulate(1, 1)
        next_diagonal()
        accumulate(0, 0)

    @pl.when(jnp.logical_and(i >= 1, n_far % 2 == 0))
    def _():
        next_diagonal()
        accumulate(0, 1)

    @pl.when(i == 0)
    def _():
        next_diagonal()
        accumulate(0, 0)

    lam = (jnp.exp(jnp.sum(lq1_ref[...] * lk1_ref[...], axis=1, keepdims=True))
           - jnp.exp(jnp.sum(lq2_ref[...] * lk2_ref[...], axis=1, keepdims=True)) + lambda_init)
    o1 = acc_ref[0, 0:D_HEAD_V, :] / acc_ref[0, D_HEAD_V:D_HEAD_V + 1, :]
    o2 = acc_ref[1, 0:D_HEAD_V, :] / acc_ref[1, D_HEAD_V:D_HEAD_V + 1, :]
    ot = o1 - lam * o2
    ot = ot * lax.rsqrt(jnp.mean(ot * ot, axis=0, keepdims=True) + RMS_EPS)
    ot = ot * subg_ref[...] * (1.0 - lambda_init)
    o_ref[...] = ot.T.astype(BF16)


def _attention(q, k, vt, bias, lq1, lk1, lq2, lk2, subln_g, *, t, lambda_init, clear_shape, to_bf16):
    bsz, seq, _ = q.shape
    nq = seq // t
    assert t + 1 >= LOG_BUCKET_STARTS[-1], "keys two tiles back must all fall in the last distance bucket"
    steps = bsz * N_HEADS * nq
    step_id = lambda b, h, i: ((b * N_HEADS + h) * nq + i, 0)
    clear_rows = clear_shape[0] // steps
    assert clear_rows * steps == clear_shape[0] and clear_rows % SUBLANES == 0, (clear_shape, steps)
    cast_specs = []
    for a in to_bf16:
        rows = a.shape[0] // steps
        assert rows * steps == a.shape[0] and rows % (2 * SUBLANES) == 0, (a.shape, steps)
        cast_specs.append(pl.BlockSpec((rows, a.shape[1]), step_id))
    kern = functools.partial(_attn_kernel, t=t, lambda_init=lambda_init, n_cast=len(to_bf16))
    vec = lambda a: a.reshape(1, HEAD_DIM)
    return pl.pallas_call(
        kern,
        out_shape=(jax.ShapeDtypeStruct((bsz, seq, D_ATTN), BF16),
                   jax.ShapeDtypeStruct(clear_shape, F32),
                   *[jax.ShapeDtypeStruct(a.shape, BF16) for a in to_bf16]),
        grid=(bsz, N_HEADS, seq // t),
        in_specs=[pl.BlockSpec((None, t, D_HEAD_V), lambda b, h, i: (b, i, h)),
                  pl.BlockSpec((None, t, D_HEAD_V), lambda b, h, i: (b, jnp.minimum(i + 1, nq - 1), h)),
                  pl.BlockSpec((None, seq, D_HEAD_V), lambda b, h, i: (b, 0, h)),
                  pl.BlockSpec((None, None, V_ROWS, seq), lambda b, h, i: (b, h, 0, 0)),
                  pl.BlockSpec((None, 2, t, t), lambda b, h, i: (h, 0, 0, 0)),
                  _const_spec((1, HEAD_DIM)), _const_spec((1, HEAD_DIM)),
                  _const_spec((1, HEAD_DIM)), _const_spec((1, HEAD_DIM)),
                  _const_spec((D_HEAD_V, 1)), *cast_specs],
        out_specs=(pl.BlockSpec((None, t, D_HEAD_V), lambda b, h, i: (b, i, h)),
                   pl.BlockSpec((clear_rows, clear_shape[1]), step_id), *cast_specs),
        scratch_shapes=[pltpu.VMEM((3, 2, t, t), F32),
                        pltpu.VMEM((3, 2, 1, t), F32),
                        pltpu.VMEM((2, 1, t), F32),
                        pltpu.VMEM((2, V_ROWS, t), F32)],
        compiler_params=_cparams(3),
        name="attn",
    )(q, q, k, vt, bias, vec(lq1), vec(lk1), vec(lq2), vec(lk2), subln_g.reshape(D_HEAD_V, 1),
      *to_bf16)


def _first_max(rows):
    best = rows[0]
    for r in rows[1:]:
        best = jnp.maximum(best, r)
    idx = jnp.full(best.shape, len(rows) - 1, jnp.int32)
    for j in range(len(rows) - 2, -1, -1):
        idx = jnp.where(rows[j] == best, j, idx)
    return best, idx


def _post_kernel(o_ref, sga_ref, pc_ref, x_ref, mod_ref, g2_ref, wba_ref, wout_ref, wr_ref, br_ref,
                 tri_ref, x1_ref, h2p_ref, route_ref, cnt_ref, carry_ref):
    @pl.when(pl.program_id(0) == 0)
    def _():
        carry_ref[...] = jnp.zeros(carry_ref.shape, F32)

    ya = jnp.dot(o_ref[...], wba_ref[...], preferred_element_type=F32)
    merged = sga_ref[...].astype(F32) * ya + pc_ref[...].astype(F32)
    z = jnp.dot(merged.astype(BF16), wout_ref[...], preferred_element_type=F32)
    x1 = x_ref[...] + mod_ref[2:3, :] * z
    x1_ref[...] = x1
    y = x1 * lax.rsqrt(jnp.mean(x1 * x1, axis=-1, keepdims=True) + RMS_EPS)
    h2 = y * (g2_ref[...] * (1.0 + mod_ref[4:5, :])) + mod_ref[3:4, :]
    hi = h2.astype(BF16)
    lo = (h2 - hi.astype(F32)).astype(BF16)

    nt = (((1,), (1,)), ((), ()))
    a = lax.dot_general(wr_ref[...], hi, nt, preferred_element_type=F32)
    b = lax.dot_general(wr_ref[0:ROUTER_ROWS, :], lo, nt, preferred_element_type=F32)
    logits = a[0:ROUTER_ROWS] + a[ROUTER_ROWS:2 * ROUTER_ROWS] + b + br_ref[...]

    gl = [logits[g:g + 1, :] for g in range(N_EXPERT_GROUPS)]
    gmax, gidx = _first_max(gl)
    gsum = gl[0] * 0.0
    for r in gl:
        gsum = gsum + jnp.exp(r - gmax)
    g_val = 1.0 / gsum
    es = []
    for r in range(EXPERTS_PER_GROUP):
        sel = jnp.zeros_like(gmax)
        for g in range(N_EXPERT_GROUPS):
            row = EXPERT_ROW0 + g * EXPERTS_PER_GROUP + r
            sel = jnp.where(gidx == g, logits[row:row + 1, :], sel)
        es.append(sel)
    e1, i1 = _first_max(es)
    rest = [jnp.where(i1 == r, -jnp.inf, es[r]) for r in range(EXPERTS_PER_GROUP)]
    e2, i2 = _first_max(rest)
    r21 = jnp.exp(e2 - e1)
    w1 = g_val / (1.0 + r21)
    w2 = g_val * r21 / (1.0 + r21)

    first = i1 < i2
    e_lo = jnp.where(first, i1, i2)
    e_hi = jnp.where(first, i2, i1)
    pair = jnp.zeros_like(e_lo)
    for p in range(1, PAIRS_PER_GROUP):
        pair = jnp.where(jnp.logical_and(e_lo == PAIR_LO[p], e_hi == PAIR_HI[p]), p, pair)
    bucket = gidx * PAIRS_PER_GROUP + pair
    w_lo = jnp.where(first, w1, w2)
    w_hi = jnp.where(first, w2, w1)

    ts = bucket.shape[1]
    brow = lax.broadcasted_iota(jnp.int32, (ROUTER_ROWS, ts), 0)
    onehot = brow == bucket
    prefix = jnp.dot(jnp.where(onehot, 1.0, 0.0).astype(BF16), tri_ref[...],
                     preferred_element_type=F32)
    carry = carry_ref[...]
    rank = jnp.sum(jnp.where(onehot, prefix + carry, 0.0), axis=0, keepdims=True) - 1.0
    carry = carry + prefix[:, ts - 1:ts]
    carry_ref[...] = carry
    cnt_ref[...] = jnp.broadcast_to(carry, cnt_ref.shape).astype(jnp.int32)
    chunks = ts // LANES
    rank_i = rank.astype(jnp.int32)
    for j in range(chunks):
        route_ref[j:j + 1, :] = bucket[:, j * LANES:(j + 1) * LANES]
        route_ref[chunks + j:chunks + j + 1, :] = rank_i[:, j * LANES:(j + 1) * LANES]

    arow = lax.broadcasted_iota(jnp.int32, (LANES, ts), 0)
    aux_t = jnp.where(arow == 0, w_lo, jnp.where(arow == 1, w_hi, 0.0))
    d = h2.shape[1]
    h2p_ref[:, 0:d] = h2
    h2p_ref[:, d:d + LANES] = aux_t.T


def _post(o2, sga, pc, x2, mod3, g2, w_ba, w_out, wg_r, bg_r, we_r, be_r, *, seq, ts):
    n, d = x2.shape
    tiles_per_batch = seq // ts
    steps = n // ts
    wr = jnp.zeros((ROUTER_ROWS, d), F32)
    wr = wr.at[0:N_EXPERT_GROUPS].set(wg_r.T).at[EXPERT_ROW0:EXPERT_ROW0 + N_EXPERTS].set(we_r.T)
    wr_hi = wr.astype(BF16)
    wr_lo = (wr - wr_hi.astype(F32)).astype(BF16)
    br = jnp.zeros((ROUTER_ROWS, 1), F32)
    br = br.at[0:N_EXPERT_GROUPS, 0].set(bg_r).at[EXPERT_ROW0:EXPERT_ROW0 + N_EXPERTS, 0].set(be_r)
    tok = lambda i: (i, 0)
    tidx = jnp.arange(ts)
    tri = (tidx[:, None] <= tidx[None, :]).astype(BF16)
    return pl.pallas_call(
        _post_kernel,
        out_shape=(jax.ShapeDtypeStruct((n, d), F32),
                   jax.ShapeDtypeStruct((n, d + LANES), F32),
                   jax.ShapeDtypeStruct((steps, 2 * (ts // LANES), LANES), jnp.int32),
                   jax.ShapeDtypeStruct((ROUTER_ROWS, LANES), jnp.int32)),
        grid=(steps,),
        in_specs=[pl.BlockSpec((ts, D_ATTN), tok),
                  pl.BlockSpec((ts, d), tok),
                  pl.BlockSpec((ts, d), tok),
                  pl.BlockSpec((ts, d), tok),
                  pl.BlockSpec((None, 6, d), lambda i: (i // tiles_per_batch, 0, 0)),
                  _const_spec((1, d)),
                  _const_spec((D_ATTN, d)),
                  _const_spec((d, d)),
                  _const_spec((2 * ROUTER_ROWS, d)),
                  _const_spec((ROUTER_ROWS, 1)),
                  _const_spec((ts, ts))],
        out_specs=(pl.BlockSpec((ts, d), tok), pl.BlockSpec((ts, d + LANES), tok),
                   pl.BlockSpec((None, 2 * (ts // LANES), LANES), lambda i: (i, 0, 0)),
                   pl.BlockSpec((ROUTER_ROWS, LANES), lambda i: (0, 0))),
        scratch_shapes=[pltpu.VMEM((ROUTER_ROWS, 1), F32)],
        compiler_params=_cparams(1),
        name="post",
    )(o2, sga, pc, x2, mod3, g2.reshape(1, d), w_ba.astype(BF16), w_out.astype(BF16),
      jnp.concatenate([wr_hi, wr_lo], axis=0), br, tri)


def _group_sublane(row):
    return lax.shift_right_logical(row, SUBLANES.bit_length() - 1), row & (SUBLANES - 1)


def _row_copy(src, src_row, dst, dst_row, sem):
    sg, ss = src_row
    dg, ds = dst_row
    return pltpu.make_async_copy(src.at[sg, pl.ds(ss, 1), :], dst.at[dg, pl.ds(ds, 1), :], sem)


def _dispatch_kernel(pos_ref, h_ref, init_ref, hs_ref, sem, *, ts):
    del init_ref
    base = pl.program_id(0) * ts

    def start(g, carry):
        for u in range(SUBLANES):
            p = pos_ref[base + g * SUBLANES + u]
            _row_copy(h_ref, (g, u), hs_ref, _group_sublane(p), sem).start(priority=u % 2)
        return carry

    def wait(g, carry):
        for u in range(SUBLANES):
            _row_copy(h_ref, (0, 0), hs_ref, (0, 0), sem).wait()
        return carry

    lax.fori_loop(0, ts // SUBLANES, start, 0)
    lax.fori_loop(0, ts // SUBLANES, wait, 0)


def _dispatch(pos, h2p, cleared, *, ts):
    n, w = h2p.shape
    n_rows = cleared.shape[0]
    return pl.pallas_call(
        functools.partial(_dispatch_kernel, ts=ts),
        out_shape=jax.ShapeDtypeStruct((n_rows // SUBLANES, SUBLANES, w), h2p.dtype),
        grid_spec=pltpu.PrefetchScalarGridSpec(
            num_scalar_prefetch=1,
            grid=(n // ts,),
            in_specs=[pl.BlockSpec((ts // SUBLANES, SUBLANES, w), lambda i, pos: (i, 0, 0)),
                      pl.BlockSpec(memory_space=pl.ANY)],
            out_specs=pl.BlockSpec(memory_space=pl.ANY),
            scratch_shapes=[pltpu.SemaphoreType.DMA]),
        input_output_aliases={2: 0},
        compiler_params=_cparams(1),
        name="dispatch",
    )(pos, h2p.reshape(n // SUBLANES, SUBLANES, w),
      cleared.reshape(n_rows // SUBLANES, SUBLANES, w)).reshape(n_rows, w)


def _experts_kernel(ea_ref, eb_ref, valid_ref, hs_ref, wga_ref, wua_ref, wda_ref, wgb_ref, wub_ref,
                    wdb_ref, ys_ref):
    del ea_ref, eb_ref
    t = pl.program_id(0)
    d = hs_ref.shape[1] - LANES

    @pl.when(valid_ref[t] == 1)
    def _():
        h = hs_ref[:, 0:d].astype(BF16)
        aux = hs_ref[:, d:d + LANES]

        def expert(wg_ref, wu_ref, wd_ref):
            a = jnp.dot(h, wg_ref[...], preferred_element_type=F32)
            b = jnp.dot(h, wu_ref[...], preferred_element_type=F32)
            hid = (a * jax.nn.sigmoid(a)) * b
            return jnp.dot(hid.astype(BF16), wd_ref[...], preferred_element_type=F32)

        ys_ref[...] = (aux[:, 0:1] * expert(wga_ref, wua_ref, wda_ref)
                       + aux[:, 1:2] * expert(wgb_ref, wub_ref, wdb_ref))

    @pl.when(valid_ref[t] == 0)
    def _():
        ys_ref[...] = jnp.zeros(ys_ref.shape, ys_ref.dtype)


def _experts(tile_ea, tile_eb, tile_valid, hs, w_gate, w_up, w_down, *, tm):
    n_rows, w = hs.shape
    _, d, de = w_gate.shape
    wg, wu, wd = w_gate, w_up, w_down
    ea = lambda t, ea_r, eb_r, v_r: (ea_r[t], 0, 0)
    eb = lambda t, ea_r, eb_r, v_r: (eb_r[t], 0, 0)
    row = lambda t, ea_r, eb_r, v_r: (t, 0)
    return pl.pallas_call(
        _experts_kernel,
        out_shape=jax.ShapeDtypeStruct((n_rows, d), F32),
        grid_spec=pltpu.PrefetchScalarGridSpec(
            num_scalar_prefetch=3,
            grid=(n_rows // tm,),
            in_specs=[pl.BlockSpec((tm, w), row),
                      pl.BlockSpec((None, d, de), ea), pl.BlockSpec((None, d, de), ea),
                      pl.BlockSpec((None, de, d), ea),
                      pl.BlockSpec((None, d, de), eb), pl.BlockSpec((None, d, de), eb),
                      pl.BlockSpec((None, de, d), eb)],
            out_specs=pl.BlockSpec((tm, d), row)),
        compiler_params=_cparams(1),
        name="experts",
    )(tile_ea, tile_eb, tile_valid, hs, wg, wu, wd, wg, wu, wd)


def _combine_kernel(pos_ref, ys_ref, x1_ref, mod_ref, out_ref, ybuf_ref, sem, *, ts):
    i = pl.program_id(0)
    groups = ts // SUBLANES

    def gather(tile, slot):
        base = tile * ts

        def start(g, carry):
            for u in range(SUBLANES):
                p = pos_ref[base + g * SUBLANES + u]
                _row_copy(ys_ref, _group_sublane(p), ybuf_ref.at[slot], (g, u),
                          sem.at[slot]).start(priority=u % 2)
            return carry

        lax.fori_loop(0, groups, start, 0)

    def wait_all(slot):
        def wait(g, carry):
            for u in range(SUBLANES):
                _row_copy(ys_ref, (0, 0), ybuf_ref.at[slot], (0, 0), sem.at[slot]).wait()
            return carry

        lax.fori_loop(0, groups, wait, 0)

    def step(slot):
        @pl.when(i + 1 < pl.num_programs(0))
        def _():
            gather(i + 1, 1 - slot)

        wait_all(slot)
        y = ybuf_ref[slot].reshape(ts, ybuf_ref.shape[-1])
        out_ref[...] = x1_ref[...] + mod_ref[5:6, :] * y

    @pl.when(i == 0)
    def _():
        gather(0, 0)

    @pl.when(i % 2 == 0)
    def _():
        step(0)

    @pl.when(i % 2 == 1)
    def _():
        step(1)


def _combine(pos, ys, x1, mod3, *, seq, ts):
    n, d = x1.shape
    tiles_per_batch = seq // ts
    return pl.pallas_call(
        functools.partial(_combine_kernel, ts=ts),
        out_shape=jax.ShapeDtypeStruct((n, d), F32),
        grid_spec=pltpu.PrefetchScalarGridSpec(
            num_scalar_prefetch=1,
            grid=(n // ts,),
            in_specs=[pl.BlockSpec(memory_space=pl.ANY),
                      pl.BlockSpec((ts, d), lambda i, pos: (i, 0)),
                      pl.BlockSpec((None, 6, d), lambda i, pos: (i // tiles_per_batch, 0, 0))],
            out_specs=pl.BlockSpec((ts, d), lambda i, pos: (i, 0)),
            scratch_shapes=[pltpu.VMEM((2, ts // SUBLANES, SUBLANES, d), F32),
                            pltpu.SemaphoreType.DMA((2,))]),
        compiler_params=_cparams(1),
        name="combine",
    )(pos, ys.reshape(ys.shape[0] // SUBLANES, SUBLANES, d), x1, mod3)


def _pos_kernel(start_ref, route_ref, pos_ref):
    chunks = pos_ref.shape[1]
    bucket = route_ref[:, 0:chunks, :]
    pos = route_ref[:, chunks:2 * chunks, :]
    for b in range(N_ROUTE_BUCKETS):
        pos = pos + jnp.where(bucket == b, start_ref[b], 0)
    pos_ref[...] = pos


def _route_tables(route, counts, n_tiles, tm):
    steps, rows, _ = route.shape
    cnt = counts[:N_ROUTE_BUCKETS, 0]
    padded = (cnt + tm - 1) // tm * tm
    end = jnp.cumsum(padded)
    pos = pl.pallas_call(
        _pos_kernel,
        out_shape=jax.ShapeDtypeStruct((steps, rows // 2, LANES), jnp.int32),
        in_specs=[pl.BlockSpec(memory_space=pltpu.SMEM), pl.BlockSpec(memory_space=pltpu.VMEM)],
        out_specs=pl.BlockSpec(memory_space=pltpu.VMEM),
        name="pos",
    )(end - padded, route).reshape(-1)
    tiles_used = end[-1] // tm
    tile = jnp.arange(n_tiles, dtype=jnp.int32)
    valid = tile < tiles_used
    first_row = jnp.minimum(tile, tiles_used - 1) * tm
    tile_bucket = jnp.sum((end[None, :] <= first_row[:, None]).astype(jnp.int32), axis=1)
    tile_bucket = jnp.minimum(tile_bucket, N_ROUTE_BUCKETS - 1)
    group = tile_bucket // PAIRS_PER_GROUP
    pair = tile_bucket % PAIRS_PER_GROUP
    tile_ea = group * EXPERTS_PER_GROUP + jnp.asarray(PAIR_LO, jnp.int32)[pair]
    tile_eb = group * EXPERTS_PER_GROUP + jnp.asarray(PAIR_HI, jnp.int32)[pair]
    return pos.astype(jnp.int32), tile_ea, tile_eb, valid.astype(jnp.int32)


def _tile(seq, pref):
    t = min(pref, seq)
    assert seq % t == 0 and t % LANES == 0, (seq, t)
    return t


def kernel(x, c, rel_bias, ada_w, ada_b, norm1_g, w_in, q_norm_g, k_norm_g, lambda_q1, lambda_k1,
           lambda_q2, lambda_k2, subln_g, w_branch_attn, pool_w, pool_scale, w_branch_pool, w_out,
           norm2_g, router_group_w, router_group_b, router_expert_w, router_expert_b,
           expert_w_gate, expert_w_up, expert_w_down):
    bsz, seq, d = x.shape
    n = bsz * seq
    ts = _tile(seq, TOKEN_TILE)
    t_attn = _tile(seq, ATTN_TILE)
    tr = _tile(seq, ROW_COPY_TILE)
    tm = MOE_TILE
    n_tiles = -(-(n + N_ROUTE_BUCKETS * (tm - 1)) // tm)
    attn_steps = bsz * N_HEADS * (seq // t_attn)
    while (n_tiles * tm) % (attn_steps * SUBLANES):
        n_tiles += 1
    bias = _bias_tiles(rel_bias, t_attn)
    x2 = x.reshape(n, d)
    for l in range(ada_w.shape[0]):
        lambda_init = 0.8 - 0.6 * math.exp(-0.3 * l)
        mod3 = _ada(c, ada_w[l], ada_b[l]).reshape(bsz, 6, d)
        gq = jnp.tile(q_norm_g[l], D_ATTN // HEAD_DIM) * (HEAD_DIM ** -0.5 * LOG2E)
        gk = jnp.tile(k_norm_g[l], D_ATTN // HEAD_DIM)
        q, k, vt, sga, pc = _inproj(x2, mod3, norm1_g[l], w_in[l], gq, gk, pool_w[l], pool_scale[l],
                                    w_branch_pool[l], bsz=bsz, seq=seq, ts=_tile(seq, INPROJ_TILE))
        experts_f32 = (expert_w_gate[l], expert_w_up[l], expert_w_down[l])
        o, cleared, *experts_bf16 = _attention(
            q.reshape(bsz, seq, D_ATTN), k.reshape(bsz, seq, D_ATTN), vt, bias,
            lambda_q1[l], lambda_k1[l], lambda_q2[l], lambda_k2[l], subln_g[l],
            t=t_attn, lambda_init=lambda_init, clear_shape=(n_tiles * tm, d + LANES),
            to_bf16=[w.reshape(-1, w.shape[-1]) for w in experts_f32])
        wg_b, wu_b, wd_b = [wb.reshape(w.shape) for wb, w in zip(experts_bf16, experts_f32)]
        x1, h2p, route, counts = _post(
            o.reshape(n, D_ATTN), sga, pc, x2, mod3, norm2_g[l], w_branch_attn[l], w_out[l],
            router_group_w[l], router_group_b[l], router_expert_w[l], router_expert_b[l],
            seq=seq, ts=ts)
        pos, tile_ea, tile_eb, tile_valid = _route_tables(route, counts, n_tiles, tm)
        hs = _dispatch(pos, h2p, cleared, ts=tr)
        ys = _experts(tile_ea, tile_eb, tile_valid, hs, wg_b, wu_b, wd_b, tm=tm)
        x2 = _combine(pos, ys, x1, mod3, seq=seq, ts=tr)
    return x2.reshape(bsz, seq, d)
```

```python
import functools
import math

import jax
import jax.numpy as jnp
from jax import lax
from jax.experimental import pallas as pl
from jax.experimental.pallas import tpu as pltpu

F32 = jnp.float32
BF16 = jnp.bfloat16

CHUNK = 64
N_HEADS = 4
HEAD_DIM = 64
D_HEAD_V = 2 * HEAD_DIM
V_ROWS = D_HEAD_V + 16
D_ATTN = N_HEADS * D_HEAD_V
POOL_WINDOWS = (2, 4, 8, 16)
POOL_GROUP_DIM = 128
D_POOL = len(POOL_WINDOWS) * POOL_GROUP_DIM
POOL_HALO = 16
N_BUCKETS = 32
MAX_DISTANCE = 128
N_EXPERT_GROUPS = 4
EXPERTS_PER_GROUP = 4
N_EXPERTS = N_EXPERT_GROUPS * EXPERTS_PER_GROUP
PAIRS_PER_GROUP = 6
PAIR_LO = (0, 0, 1, 1, 0, 2)
PAIR_HI = (1, 2, 2, 3, 3, 3)
N_ROUTE_BUCKETS = N_EXPERT_GROUPS * PAIRS_PER_GROUP
RMS_EPS = 1e-6
LOG2E = math.log2(math.e)
MASKED = -1e30

LANES = 128
SUBLANES = 8
ROUTER_ROWS = 32
EXPERT_ROW0 = 8

VMEM_LIMIT = 56 * 1024 * 1024
TOKEN_TILE = 512
INPROJ_TILE = 1024
ATTN_TILE = 512
MOE_TILE = 256
ROW_COPY_TILE = 2048


def _cparams(n_axes):
    return pltpu.CompilerParams(dimension_semantics=("arbitrary",) * n_axes,
                                vmem_limit_bytes=VMEM_LIMIT)


def _const_spec(shape):
    nd = len(shape)
    return pl.BlockSpec(shape, lambda *_: (0,) * nd, pipeline_mode=pl.Buffered(1))


def _ada_kernel(ct_ref, w_ref, b_ref, o_ref):
    @pl.when(pl.program_id(0) == 0)
    def _():
        o_ref[...] = jnp.broadcast_to(b_ref[...], o_ref.shape)

    ct = ct_ref[...]
    s = ct * jax.nn.sigmoid(ct)
    w = w_ref[...]
    rows = [jnp.sum(w * s[:, b:b + 1], axis=0, keepdims=True) for b in range(ct.shape[1])]
    o_ref[...] += jnp.concatenate(rows, axis=0)


def _ada(c, w, b):
    bsz, d = c.shape
    n = w.shape[1]
    rows = 256
    return pl.pallas_call(
        _ada_kernel,
        out_shape=jax.ShapeDtypeStruct((bsz, n), F32),
        grid=(d // rows,),
        in_specs=[pl.BlockSpec((rows, bsz), lambda j: (j, 0)),
                  pl.BlockSpec((rows, n), lambda j: (j, 0)),
                  pl.BlockSpec((1, n), lambda j: (0, 0))],
        out_specs=pl.BlockSpec((bsz, n), lambda j: (0, 0)),
        compiler_params=_cparams(1),
        name="ada",
    )(c.T, w, b.reshape(1, n))


def _log_bucket_starts():
    nb = N_BUCKETS // 2
    max_exact = nb // 2
    m = nb - max_exact
    ratio = MAX_DISTANCE // max_exact
    starts = []
    for k in range(1, m):
        n = max_exact
        while n ** m < max_exact ** m * ratio ** k:
            n += 1
        starts.append(n)
    return tuple(starts)


LOG_BUCKET_STARTS = _log_bucket_starts()


def _bias_kernel(rb_ref, o_ref, *, t):
    h = pl.program_id(0)
    kind = pl.program_id(1)
    nb = N_BUCKETS // 2
    max_exact = nb // 2
    kpos = lax.broadcasted_iota(jnp.int32, (t, t), 0)
    qpos = lax.broadcasted_iota(jnp.int32, (t, t), 1)
    rel = kpos - qpos - jnp.where(kind == 0, t, 0)
    n = jnp.abs(rel)

    def table(first):
        val = jnp.full((t, t), rb_ref[first + nb - 1, h], F32)
        for k in range(len(LOG_BUCKET_STARTS) - 1, -1, -1):
            val = jnp.where(n < LOG_BUCKET_STARTS[k], rb_ref[first + max_exact + k, h], val)
        for j in range(max_exact - 1, -1, -1):
            val = jnp.where(n == j, rb_ref[first + j, h], val)
        return val

    bias = jnp.where(rel > 0, table(nb), table(0))
    shift = CHUNK.bit_length() - 1
    hidden = jnp.logical_and(kind == 1, (kpos >> shift) > (qpos >> shift))
    o_ref[...] = jnp.where(hidden, MASKED, (bias - rb_ref[nb - 1, h]) * LOG2E)


def _bias_tiles(rel_bias, t):
    return pl.pallas_call(
        functools.partial(_bias_kernel, t=t),
        out_shape=jax.ShapeDtypeStruct((N_HEADS, 2, t, t), F32),
        grid=(N_HEADS, 2),
        in_specs=[pl.BlockSpec(memory_space=pltpu.SMEM)],
        out_specs=pl.BlockSpec((None, None, t, t), lambda h, j: (h, j, 0, 0)),
        compiler_params=_cparams(2),
        name="bias_tiles",
    )(rel_bias)


def _group_rms(xc, ones_blockdiag, gain):
    ssq = jnp.dot((xc * xc).astype(BF16), ones_blockdiag, preferred_element_type=F32)
    return xc * lax.rsqrt(ssq * (1.0 / HEAD_DIM) + RMS_EPS) * gain


def _split_bf16(a):
    hi = a.astype(BF16)
    return hi, (a - hi.astype(F32)).astype(BF16)


def _pool_fold_kernel(pw_ref, ps_ref, wbp_ref, o_ref):
    a_hi, a_lo = _split_bf16(pw_ref[...] * ps_ref[...])
    b_hi, b_lo = _split_bf16(wbp_ref[...])
    dot = functools.partial(jnp.dot, preferred_element_type=F32)
    o_ref[...] = (dot(a_hi, b_hi) + dot(a_hi, b_lo) + dot(a_lo, b_hi)).astype(o_ref.dtype)


def _pool_fold(pool_w, pool_scale, w_bp):
    g, c, _ = pool_w.shape
    d = w_bp.shape[1]
    return pl.pallas_call(
        _pool_fold_kernel,
        out_shape=jax.ShapeDtypeStruct((g * c, d), BF16),
        grid=(g,),
        in_specs=[pl.BlockSpec((None, c, c), lambda i: (i, 0, 0)),
                  pl.BlockSpec((None, 1, c), lambda i: (i, 0, 0)),
                  pl.BlockSpec((c, d), lambda i: (i, 0))],
        out_specs=pl.BlockSpec((c, d), lambda i: (i, 0)),
        compiler_params=_cparams(1),
        name="pool_fold",
    )(pool_w, pool_scale.reshape(g, 1, c), w_bp)


def _inproj_kernel(x_ref, mod_ref, g1_ref, win_ref, gq_ref, gk_ref, ones_ref, wpool_ref,
                   q_ref, k_ref, vt_ref, sga_ref, pc_ref, ext_ref, v_ref,
                   *, ts, tiles_per_batch):
    tb = pl.program_id(0) % tiles_per_batch
    x = x_ref[...]
    y = x * lax.rsqrt(jnp.mean(x * x, axis=-1, keepdims=True) + RMS_EPS)
    h = y * (g1_ref[...] * (1.0 + mod_ref[1:2, :])) + mod_ref[0:1, :]
    hb = h.astype(BF16)

    def proj(c0, c1):
        return jnp.dot(hb, win_ref[:, c0:c1], preferred_element_type=F32)

    ones_bd = ones_ref[...]
    q_ref[...] = _group_rms(proj(0, D_ATTN), ones_bd, gq_ref[...]).astype(BF16)
    k_ref[...] = _group_rms(proj(D_ATTN, 2 * D_ATTN), ones_bd, gk_ref[...]).astype(BF16)
    v_ref[...] = proj(2 * D_ATTN, 3 * D_ATTN)
    vt = v_ref[...].T.astype(BF16)
    for hd in range(N_HEADS):
        vt_ref[hd, 0:D_HEAD_V, :] = vt[hd * D_HEAD_V:(hd + 1) * D_HEAD_V, :]
        vt_ref[hd, D_HEAD_V:V_ROWS, :] = jnp.ones((V_ROWS - D_HEAD_V, ts), BF16)
    c_u = 3 * D_ATTN
    c_ga = c_u + D_POOL
    c_gp = c_ga + x.shape[1]
    sga_ref[...] = jax.nn.sigmoid(proj(c_ga, c_gp)).astype(BF16)

    u = proj(c_u, c_ga)

    @pl.when(tb == 0)
    def _():
        ext_ref[0:POOL_HALO, :] = jnp.zeros((POOL_HALO, D_POOL), F32)

    ext_ref[POOL_HALO:POOL_HALO + ts, :] = u
    row = lax.broadcasted_iota(jnp.int32, (ts, 1), 0) + tb * ts
    ys = []
    for g, w in enumerate(POOL_WINDOWS):
        c0 = g * POOL_GROUP_DIM
        ug = u[:, c0:c0 + POOL_GROUP_DIM]
        acc = ug
        for d in range(1, w):
            acc = acc + ext_ref[POOL_HALO - d:POOL_HALO - d + ts, c0:c0 + POOL_GROUP_DIM]
        cnt = jnp.minimum(row + 1, w).astype(F32)
        ys.append((acc / cnt - ug).astype(BF16))
    ypool = jnp.dot(jnp.concatenate(ys, axis=1), wpool_ref[...], preferred_element_type=F32)
    pc_ref[...] = (jax.nn.sigmoid(proj(c_gp, c_gp + x.shape[1])) * ypool).astype(BF16)
    ext_ref[0:POOL_HALO, :] = u[ts - POOL_HALO:ts, :]


def _inproj(x2, mod3, g1, w_in, gq, gk, pool_w, pool_scale, w_bp, *, bsz, seq, ts):
    n, d = x2.shape
    d_in = w_in.shape[1]
    tiles_per_batch = seq // ts
    win_b = w_in.astype(BF16)
    idx = jnp.arange(D_ATTN) // HEAD_DIM
    ones_bd = (idx[:, None] == idx[None, :]).astype(BF16)
    kern = functools.partial(_inproj_kernel, ts=ts, tiles_per_batch=tiles_per_batch)
    tok = lambda i: (i, 0)
    out_shape = (jax.ShapeDtypeStruct((n, D_ATTN), BF16),
                 jax.ShapeDtypeStruct((n, D_ATTN), BF16),
                 jax.ShapeDtypeStruct((bsz, N_HEADS, V_ROWS, seq), BF16),
                 jax.ShapeDtypeStruct((n, d), BF16),
                 jax.ShapeDtypeStruct((n, d), BF16))
    return pl.pallas_call(
        kern,
        out_shape=out_shape,
        grid=(n // ts,),
        in_specs=[pl.BlockSpec((ts, d), tok),
                  pl.BlockSpec((None, 6, d), lambda i: (i // tiles_per_batch, 0, 0)),
                  _const_spec((1, d)),
                  _const_spec((d, d_in)),
                  _const_spec((1, D_ATTN)),
                  _const_spec((1, D_ATTN)),
                  _const_spec((D_ATTN, D_ATTN)),
                  _const_spec((D_POOL, d))],
        out_specs=(pl.BlockSpec((ts, D_ATTN), tok),
                   pl.BlockSpec((ts, D_ATTN), tok),
                   pl.BlockSpec((None, N_HEADS, V_ROWS, ts),
                                lambda i: (i // tiles_per_batch, 0, 0, i % tiles_per_batch)),
                   pl.BlockSpec((ts, d), tok),
                   pl.BlockSpec((ts, d), tok)),
        scratch_shapes=[pltpu.VMEM((POOL_HALO + ts, D_POOL), F32), pltpu.VMEM((ts, D_ATTN), F32)],
        compiler_params=_cparams(1),
        name="inproj",
    )(x2, mod3, g1.reshape(1, d), win_b, gq.reshape(1, D_ATTN), gk.reshape(1, D_ATTN), ones_bd,
      _pool_fold(pool_w, pool_scale, w_bp))


def _attn_kernel(q_ref, qn_ref, k_ref, vt_ref, bias_ref, lq1_ref, lk1_ref, lq2_ref, lk2_ref, subg_ref,
                 *rest, t, lambda_init, n_cast):
    cast_in = rest[:n_cast]
    o_ref, clear_ref = rest[n_cast:n_cast + 2]
    cast_out = rest[n_cast + 2:2 * n_cast + 2]
    s_ref, mt_ref, m_ref, acc_ref = rest[2 * n_cast + 2:]
    i = pl.program_id(2)
    last = pl.num_programs(2) - 1
    clear_ref[...] = jnp.zeros(clear_ref.shape, clear_ref.dtype)
    for src, dst in zip(cast_in, cast_out):
        dst[...] = src[...].astype(dst.dtype)

    def split_maps(q):
        lane = lax.broadcasted_iota(jnp.int32, q.shape, 1)
        zero = jnp.zeros_like(q)
        return jnp.where(lane < HEAD_DIM, q, zero), jnp.where(lane >= HEAD_DIM, q, zero)

    q_now = split_maps(q_ref[...])

    m_ref[...] = jnp.full(m_ref.shape, MASKED, F32)
    acc_ref[...] = jnp.zeros(acc_ref.shape, F32)

    def scores(j, bias, slot, qm=q_now):
        kt = k_ref[pl.ds(pl.multiple_of(j * t, t), t), :]
        for mp in range(2):
            s = lax.dot_general(kt, qm[mp], (((1,), (1,)), ((), ())), preferred_element_type=F32)
            if bias is not None:
                s = s + bias
            s_ref[slot, mp] = s
            mt_ref[slot, mp] = jnp.max(s, axis=0, keepdims=True)

    def accumulate(j, slot):
        vt = vt_ref[:, pl.ds(pl.multiple_of(j * t, t), t)]
        for mp in range(2):
            m_old = m_ref[mp]
            m_new = jnp.maximum(m_old, mt_ref[slot, mp])
            p = jnp.exp2(s_ref[slot, mp] - m_new).astype(BF16)
            acc_ref[mp] = (jnp.exp2(m_old - m_new) * acc_ref[mp]
                           + jnp.dot(vt, p, preferred_element_type=F32))
            m_ref[mp] = m_new

    def next_diagonal():
        nxt = jnp.minimum(i + 1, last)
        scores(nxt, bias_ref[1], 2, split_maps(qn_ref[...]))

    @pl.when(i == 0)
    def _():
        scores(0, bias_ref[1], 0)

    @pl.when(i >= 1)
    def _():
        scores(i - 1, bias_ref[0], 1)
        accumulate(i, 2)

    n_far = jnp.maximum(i - 1, 0)

    def pair(kk):
        j = i - 1 - 2 * kk
        scores(j - 1, None, 0)
        accumulate(j, 1)
        scores(j - 2, None, 1)
        accumulate(j - 1, 0)

    def oct_body(oo, carry):
        for u in range(4):
            pair(4 * oo + u)
        return carry

    n_pairs = n_far // 2
    lax.fori_loop(0, n_pairs // 4, oct_body, 0)

    @pl.when(n_pairs % 4 >= 2)
    def _():
        pair(n_pairs // 4 * 4)
        pair(n_pairs // 4 * 4 + 1)

    @pl.when(n_pairs % 2 == 1)
    def _():
        pair(n_pairs - 1)

    def finish(slot):
        next_diagonal()
        accumulate(0, slot)
        lam = (jnp.exp(jnp.sum(lq1_ref[...] * lk1_ref[...], axis=1, keepdims=True))
               - jnp.exp(jnp.sum(lq2_ref[...] * lk2_ref[...], axis=1, keepdims=True)) + lambda_init)
        o1 = acc_ref[0, 0:D_HEAD_V, :] / acc_ref[0, D_HEAD_V:D_HEAD_V + 1, :]
        o2 = acc_ref[1, 0:D_HEAD_V, :] / acc_ref[1, D_HEAD_V:D_HEAD_V + 1, :]
        ot = o1 - lam * o2
        ot = ot * lax.rsqrt(jnp.mean(ot * ot, axis=0, keepdims=True) + RMS_EPS)
        ot = ot * subg_ref[...] * (1.0 - lambda_init)
        o_ref[...] = ot.T.astype(BF16)

    @pl.when(n_far % 2 == 1)
    def _():
        scores(0, None, 0)
        accumulate(1, 1)
        finish(0)

    @pl.when(jnp.logical_and(i >= 1, n_far % 2 == 0))
    def _():
        finish(1)

    @pl.when(i == 0)
    def _():
        finish(0)


def _attention(q, k, vt, bias, lq1, lk1, lq2, lk2, subln_g, *, t, lambda_init, clear_shape, to_bf16):
    bsz, seq, _ = q.shape
    nq = seq // t
    assert t + 1 >= LOG_BUCKET_STARTS[-1], "keys two tiles back must all fall in the last distance bucket"
    steps = bsz * N_HEADS * nq
    step_id = lambda b, h, i: ((b * N_HEADS + h) * nq + i, 0)
    clear_rows = clear_shape[0] // steps
    assert clear_rows * steps == clear_shape[0] and clear_rows % SUBLANES == 0, (clear_shape, steps)
    cast_specs = []
    for a in to_bf16:
        rows = a.shape[0] // steps
        assert rows * steps == a.shape[0] and rows % (2 * SUBLANES) == 0, (a.shape, steps)
        cast_specs.append(pl.BlockSpec((rows, a.shape[1]), step_id))
    kern = functools.partial(_attn_kernel, t=t, lambda_init=lambda_init, n_cast=len(to_bf16))
    vec = lambda a: a.reshape(1, HEAD_DIM)
    return pl.pallas_call(
        kern,
        out_shape=(jax.ShapeDtypeStruct((bsz, seq, D_ATTN), BF16),
                   jax.ShapeDtypeStruct(clear_shape, F32),
                   *[jax.ShapeDtypeStruct(a.shape, BF16) for a in to_bf16]),
        grid=(bsz, N_HEADS, seq // t),
        in_specs=[pl.BlockSpec((None, t, D_HEAD_V), lambda b, h, i: (b, i, h)),
                  pl.BlockSpec((None, t, D_HEAD_V), lambda b, h, i: (b, jnp.minimum(i + 1, nq - 1), h)),
                  pl.BlockSpec((None, seq, D_HEAD_V), lambda b, h, i: (b, 0, h)),
                  pl.BlockSpec((None, None, V_ROWS, seq), lambda b, h, i: (b, h, 0, 0)),
                  pl.BlockSpec((None, 2, t, t), lambda b, h, i: (h, 0, 0, 0)),
                  _const_spec((1, HEAD_DIM)), _const_spec((1, HEAD_DIM)),
                  _const_spec((1, HEAD_DIM)), _const_spec((1, HEAD_DIM)),
                  _const_spec((D_HEAD_V, 1)), *cast_specs],
        out_specs=(pl.BlockSpec((None, t, D_HEAD_V), lambda b, h, i: (b, i, h)),
                   pl.BlockSpec((clear_rows, clear_shape[1]), step_id), *cast_specs),
        scratch_shapes=[pltpu.VMEM((3, 2, t, t), F32),
                        pltpu.VMEM((3, 2, 1, t), F32),
                        pltpu.VMEM((2, 1, t), F32),
                        pltpu.VMEM((2, V_ROWS, t), F32)],
        compiler_params=_cparams(3),
        name="attn",
    )(q, q, k, vt, bias, vec(lq1), vec(lk1), vec(lq2), vec(lk2), subln_g.reshape(D_HEAD_V, 1),
      *to_bf16)


def _first_max(rows):
    best = rows[0]
    for r in rows[1:]:
        best = jnp.maximum(best, r)
    idx = jnp.full(best.shape, len(rows) - 1, jnp.int32)
    for j in range(len(rows) - 2, -1, -1):
        idx = jnp.where(rows[j] == best, j, idx)
    return best, idx


def _post_kernel(o_ref, sga_ref, pc_ref, x_ref, mod_ref, g2_ref, wba_ref, wout_ref, wr_ref, br_ref,
                 tri_ref, x1_ref, h2p_ref, route_ref, cnt_ref, carry_ref):
    @pl.when(pl.program_id(0) == 0)
    def _():
        carry_ref[...] = jnp.zeros(carry_ref.shape, F32)

    ya = jnp.dot(o_ref[...], wba_ref[...], preferred_element_type=F32)
    merged = sga_ref[...].astype(F32) * ya + pc_ref[...].astype(F32)
    z = jnp.dot(merged.astype(BF16), wout_ref[...], preferred_element_type=F32)
    x1 = x_ref[...] + mod_ref[2:3, :] * z
    x1_ref[...] = x1
    y = x1 * lax.rsqrt(jnp.mean(x1 * x1, axis=-1, keepdims=True) + RMS_EPS)
    h2 = y * (g2_ref[...] * (1.0 + mod_ref[4:5, :])) + mod_ref[3:4, :]
    hi = h2.astype(BF16)
    lo = (h2 - hi.astype(F32)).astype(BF16)

    nt = (((1,), (1,)), ((), ()))
    a = lax.dot_general(wr_ref[...], hi, nt, preferred_element_type=F32)
    b = lax.dot_general(wr_ref[0:ROUTER_ROWS, :], lo, nt, preferred_element_type=F32)
    logits = a[0:ROUTER_ROWS] + a[ROUTER_ROWS:2 * ROUTER_ROWS] + b + br_ref[...]

    gl = [logits[g:g + 1, :] for g in range(N_EXPERT_GROUPS)]
    gmax, gidx = _first_max(gl)
    gsum = gl[0] * 0.0
    for r in gl:
        gsum = gsum + jnp.exp(r - gmax)
    g_val = 1.0 / gsum
    es = []
    for r in range(EXPERTS_PER_GROUP):
        sel = jnp.zeros_like(gmax)
        for g in range(N_EXPERT_GROUPS):
            row = EXPERT_ROW0 + g * EXPERTS_PER_GROUP + r
            sel = jnp.where(gidx == g, logits[row:row + 1, :], sel)
        es.append(sel)
    e1, i1 = _first_max(es)
    rest = [jnp.where(i1 == r, -jnp.inf, es[r]) for r in range(EXPERTS_PER_GROUP)]
    e2, i2 = _first_max(rest)
    r21 = jnp.exp(e2 - e1)
    w1 = g_val / (1.0 + r21)
    w2 = g_val * r21 / (1.0 + r21)

    first = i1 < i2
    e_lo = jnp.where(first, i1, i2)
    e_hi = jnp.where(first, i2, i1)
    pair = jnp.zeros_like(e_lo)
    for p in range(1, PAIRS_PER_GROUP):
        pair = jnp.where(jnp.logical_and(e_lo == PAIR_LO[p], e_hi == PAIR_HI[p]), p, pair)
    bucket = gidx * PAIRS_PER_GROUP + pair
    w_lo = jnp.where(first, w1, w2)
    w_hi = jnp.where(first, w2, w1)

    ts = bucket.shape[1]
    brow = lax.broadcasted_iota(jnp.int32, (ROUTER_ROWS, ts), 0)
    onehot = brow == bucket
    prefix = jnp.dot(jnp.where(onehot, 1.0, 0.0).astype(BF16), tri_ref[...],
                     preferred_element_type=F32)
    carry = carry_ref[...]
    rank = jnp.sum(jnp.where(onehot, prefix + carry, 0.0), axis=0, keepdims=True) - 1.0
    carry = carry + prefix[:, ts - 1:ts]
    carry_ref[...] = carry
    cnt_ref[...] = jnp.broadcast_to(carry, cnt_ref.shape).astype(jnp.int32)
    chunks = ts // LANES
    rank_i = rank.astype(jnp.int32)
    for j in range(chunks):
        route_ref[j:j + 1, :] = bucket[:, j * LANES:(j + 1) * LANES]
        route_ref[chunks + j:chunks + j + 1, :] = rank_i[:, j * LANES:(j + 1) * LANES]

    arow = lax.broadcasted_iota(jnp.int32, (LANES, ts), 0)
    aux_t = jnp.where(arow == 0, w_lo, jnp.where(arow == 1, w_hi, 0.0))
    d = h2.shape[1]
    h2p_ref[:, 0:d] = h2
    h2p_ref[:, d:d + LANES] = aux_t.T


def _post(o2, sga, pc, x2, mod3, g2, w_ba, w_out, wg_r, bg_r, we_r, be_r, *, seq, ts):
    n, d = x2.shape
    tiles_per_batch = seq // ts
    steps = n // ts
    wr = jnp.zeros((ROUTER_ROWS, d), F32)
    wr = wr.at[0:N_EXPERT_GROUPS].set(wg_r.T).at[EXPERT_ROW0:EXPERT_ROW0 + N_EXPERTS].set(we_r.T)
    wr_hi = wr.astype(BF16)
    wr_lo = (wr - wr_hi.astype(F32)).astype(BF16)
    br = jnp.zeros((ROUTER_ROWS, 1), F32)
    br = br.at[0:N_EXPERT_GROUPS, 0].set(bg_r).at[EXPERT_ROW0:EXPERT_ROW0 + N_EXPERTS, 0].set(be_r)
    tok = lambda i: (i, 0)
    tidx = jnp.arange(ts)
    tri = (tidx[:, None] <= tidx[None, :]).astype(BF16)
    return pl.pallas_call(
        _post_kernel,
        out_shape=(jax.ShapeDtypeStruct((n, d), F32),
                   jax.ShapeDtypeStruct((n, d + LANES), F32),
                   jax.ShapeDtypeStruct((steps, 2 * (ts // LANES), LANES), jnp.int32),
                   jax.ShapeDtypeStruct((ROUTER_ROWS, LANES), jnp.int32)),
        grid=(steps,),
        in_specs=[pl.BlockSpec((ts, D_ATTN), tok),
                  pl.BlockSpec((ts, d), tok),
                  pl.BlockSpec((ts, d), tok),
                  pl.BlockSpec((ts, d), tok),
                  pl.BlockSpec((None, 6, d), lambda i: (i // tiles_per_batch, 0, 0)),
                  _const_spec((1, d)),
                  _const_spec((D_ATTN, d)),
                  _const_spec((d, d)),
                  _const_spec((2 * ROUTER_ROWS, d)),
                  _const_spec((ROUTER_ROWS, 1)),
                  _const_spec((ts, ts))],
        out_specs=(pl.BlockSpec((ts, d), tok), pl.BlockSpec((ts, d + LANES), tok),
                   pl.BlockSpec((None, 2 * (ts // LANES), LANES), lambda i: (i, 0, 0)),
                   pl.BlockSpec((ROUTER_ROWS, LANES), lambda i: (0, 0))),
        scratch_shapes=[pltpu.VMEM((ROUTER_ROWS, 1), F32)],
        compiler_params=_cparams(1),
        name="post",
    )(o2, sga, pc, x2, mod3, g2.reshape(1, d), w_ba.astype(BF16), w_out.astype(BF16),
      jnp.concatenate([wr_hi, wr_lo], axis=0), br, tri)


def _group_sublane(row):
    return lax.shift_right_logical(row, SUBLANES.bit_length() - 1), row & (SUBLANES - 1)


def _row_copy(src, src_row, dst, dst_row, sem):
    sg, ss = src_row
    dg, ds = dst_row
    return pltpu.make_async_copy(src.at[sg, pl.ds(ss, 1), :], dst.at[dg, pl.ds(ds, 1), :], sem)


def _dispatch_kernel(pos_ref, h_ref, init_ref, hs_ref, sem, *, ts):
    del init_ref
    base = pl.program_id(0) * ts

    def start(g, carry):
        for u in range(SUBLANES):
            p = pos_ref[base + g * SUBLANES + u]
            _row_copy(h_ref, (g, u), hs_ref, _group_sublane(p), sem).start(priority=u % 2)
        return carry

    def wait(g, carry):
        for u in range(SUBLANES):
            _row_copy(h_ref, (0, 0), hs_ref, (0, 0), sem).wait()
        return carry

    lax.fori_loop(0, ts // SUBLANES, start, 0)
    lax.fori_loop(0, ts // SUBLANES, wait, 0)


def _dispatch(pos, h2p, cleared, *, ts):
    n, w = h2p.shape
    n_rows = cleared.shape[0]
    return pl.pallas_call(
        functools.partial(_dispatch_kernel, ts=ts),
        out_shape=jax.ShapeDtypeStruct((n_rows // SUBLANES, SUBLANES, w), h2p.dtype),
        grid_spec=pltpu.PrefetchScalarGridSpec(
            num_scalar_prefetch=1,
            grid=(n // ts,),
            in_specs=[pl.BlockSpec((ts // SUBLANES, SUBLANES, w), lambda i, pos: (i, 0, 0)),
                      pl.BlockSpec(memory_space=pl.ANY)],
            out_specs=pl.BlockSpec(memory_space=pl.ANY),
            scratch_shapes=[pltpu.SemaphoreType.DMA]),
        input_output_aliases={2: 0},
        compiler_params=_cparams(1),
        name="dispatch",
    )(pos, h2p.reshape(n // SUBLANES, SUBLANES, w),
      cleared.reshape(n_rows // SUBLANES, SUBLANES, w)).reshape(n_rows, w)


def _experts_kernel(ea_ref, eb_ref, valid_ref, hs_ref, wga_ref, wua_ref, wda_ref, wgb_ref, wub_ref,
                    wdb_ref, ys_ref):
    del ea_ref, eb_ref
    t = pl.program_id(0)
    d = hs_ref.shape[1] - LANES

    @pl.when(valid_ref[t] == 1)
    def _():
        h = hs_ref[:, 0:d].astype(BF16)
        aux = hs_ref[:, d:d + LANES]

        def expert(wg_ref, wu_ref, wd_ref):
            a = jnp.dot(h, wg_ref[...], preferred_element_type=F32)
            b = jnp.dot(h, wu_ref[...], preferred_element_type=F32)
            hid = (a * jax.nn.sigmoid(a)) * b
            return jnp.dot(hid.astype(BF16), wd_ref[...], preferred_element_type=F32)

        ys_ref[...] = (aux[:, 0:1] * expert(wga_ref, wua_ref, wda_ref)
                       + aux[:, 1:2] * expert(wgb_ref, wub_ref, wdb_ref))

    @pl.when(valid_ref[t] == 0)
    def _():
        ys_ref[...] = jnp.zeros(ys_ref.shape, ys_ref.dtype)


def _experts(tile_ea, tile_eb, tile_valid, hs, w_gate, w_up, w_down, *, tm):
    n_rows, w = hs.shape
    _, d, de = w_gate.shape
    wg, wu, wd = w_gate, w_up, w_down
    ea = lambda t, ea_r, eb_r, v_r: (ea_r[t], 0, 0)
    eb = lambda t, ea_r, eb_r, v_r: (eb_r[t], 0, 0)
    row = lambda t, ea_r, eb_r, v_r: (t, 0)
    return pl.pallas_call(
        _experts_kernel,
        out_shape=jax.ShapeDtypeStruct((n_rows, d), F32),
        grid_spec=pltpu.PrefetchScalarGridSpec(
            num_scalar_prefetch=3,
            grid=(n_rows // tm,),
            in_specs=[pl.BlockSpec((tm, w), row),
                      pl.BlockSpec((None, d, de), ea), pl.BlockSpec((None, d, de), ea),
                      pl.BlockSpec((None, de, d), ea),
                      pl.BlockSpec((None, d, de), eb), pl.BlockSpec((None, d, de), eb),
                      pl.BlockSpec((None, de, d), eb)],
            out_specs=pl.BlockSpec((tm, d), row)),
        compiler_params=_cparams(1),
        name="experts",
    )(tile_ea, tile_eb, tile_valid, hs, wg, wu, wd, wg, wu, wd)


def _combine_kernel(pos_ref, ys_ref, x1_ref, mod_ref, out_ref, ybuf_ref, sem, *, ts):
    i = pl.program_id(0)
    groups = ts // SUBLANES

    def gather(tile, slot):
        base = tile * ts

        def start(g, carry):
            for u in range(SUBLANES):
                p = pos_ref[base + g * SUBLANES + u]
                _row_copy(ys_ref, _group_sublane(p), ybuf_ref.at[slot], (g, u),
                          sem.at[slot]).start(priority=u % 2)
            return carry

        lax.fori_loop(0, groups, start, 0)

    def wait_all(slot):
        def wait(g, carry):
            for u in range(SUBLANES):
                _row_copy(ys_ref, (0, 0), ybuf_ref.at[slot], (0, 0), sem.at[slot]).wait()
            return carry

        lax.fori_loop(0, groups, wait, 0)

    def step(slot):
        @pl.when(i + 1 < pl.num_programs(0))
        def _():
            gather(i + 1, 1 - slot)

        wait_all(slot)
        y = ybuf_ref[slot].reshape(ts, ybuf_ref.shape[-1])
        out_ref[...] = x1_ref[...] + mod_ref[5:6, :] * y

    @pl.when(i == 0)
    def _():
        gather(0, 0)

    @pl.when(i % 2 == 0)
    def _():
        step(0)

    @pl.when(i % 2 == 1)
    def _():
        step(1)


def _combine(pos, ys, x1, mod3, *, seq, ts):
    n, d = x1.shape
    tiles_per_batch = seq // ts
    return pl.pallas_call(
        functools.partial(_combine_kernel, ts=ts),
        out_shape=jax.ShapeDtypeStruct((n, d), F32),
        grid_spec=pltpu.PrefetchScalarGridSpec(
            num_scalar_prefetch=1,
            grid=(n // ts,),
            in_specs=[pl.BlockSpec(memory_space=pl.ANY),
                      pl.BlockSpec((ts, d), lambda i, pos: (i, 0)),
                      pl.BlockSpec((None, 6, d), lambda i, pos: (i // tiles_per_batch, 0, 0))],
            out_specs=pl.BlockSpec((ts, d), lambda i, pos: (i, 0)),
            scratch_shapes=[pltpu.VMEM((2, ts // SUBLANES, SUBLANES, d), F32),
                            pltpu.SemaphoreType.DMA((2,))]),
        compiler_params=_cparams(1),
        name="combine",
    )(pos, ys.reshape(ys.shape[0] // SUBLANES, SUBLANES, d), x1, mod3)


def _pos_kernel(start_ref, route_ref, pos_ref):
    chunks = pos_ref.shape[1]
    bucket = route_ref[:, 0:chunks, :]
    pos = route_ref[:, chunks:2 * chunks, :]
    for b in range(N_ROUTE_BUCKETS):
        pos = pos + jnp.where(bucket == b, start_ref[b], 0)
    pos_ref[...] = pos


def _route_tables(route, counts, n_tiles, tm):
    steps, rows, _ = route.shape
    cnt = counts[:N_ROUTE_BUCKETS, 0]
    padded = (cnt + tm - 1) // tm * tm
    end = jnp.cumsum(padded)
    pos = pl.pallas_call(
        _pos_kernel,
        out_shape=jax.ShapeDtypeStruct((steps, rows // 2, LANES), jnp.int32),
        in_specs=[pl.BlockSpec(memory_space=pltpu.SMEM), pl.BlockSpec(memory_space=pltpu.VMEM)],
        out_specs=pl.BlockSpec(memory_space=pltpu.VMEM),
        name="pos",
    )(end - padded, route).reshape(-1)
    tiles_used = end[-1] // tm
    tile = jnp.arange(n_tiles, dtype=jnp.int32)
    valid = tile < tiles_used
    first_row = jnp.minimum(tile, tiles_used - 1) * tm
    tile_bucket = jnp.sum((end[None, :] <= first_row[:, None]).astype(jnp.int32), axis=1)
    tile_bucket = jnp.minimum(tile_bucket, N_ROUTE_BUCKETS - 1)
    group = tile_bucket // PAIRS_PER_GROUP
    pair = tile_bucket % PAIRS_PER_GROUP
    tile_ea = group * EXPERTS_PER_GROUP + jnp.asarray(PAIR_LO, jnp.int32)[pair]
    tile_eb = group * EXPERTS_PER_GROUP + jnp.asarray(PAIR_HI, jnp.int32)[pair]
    return pos.astype(jnp.int32), tile_ea, tile_eb, valid.astype(jnp.int32)


def _tile(seq, pref):
    t = min(pref, seq)
    assert seq % t == 0 and t % LANES == 0, (seq, t)
    return t


def kernel(x, c, rel_bias, ada_w, ada_b, norm1_g, w_in, q_norm_g, k_norm_g, lambda_q1, lambda_k1,
           lambda_q2, lambda_k2, subln_g, w_branch_attn, pool_w, pool_scale, w_branch_pool, w_out,
           norm2_g, router_group_w, router_group_b, router_expert_w, router_expert_b,
           expert_w_gate, expert_w_up, expert_w_down):
    bsz, seq, d = x.shape
    n = bsz * seq
    ts = _tile(seq, TOKEN_TILE)
    t_attn = _tile(seq, ATTN_TILE)
    tr = _tile(seq, ROW_COPY_TILE)
    tm = MOE_TILE
    n_tiles = -(-(n + N_ROUTE_BUCKETS * (tm - 1)) // tm)
    attn_steps = bsz * N_HEADS * (seq // t_attn)
    while (n_tiles * tm) % (attn_steps * SUBLANES):
        n_tiles += 1
    bias = _bias_tiles(rel_bias, t_attn)
    x2 = x.reshape(n, d)
    for l in range(ada_w.shape[0]):
        lambda_init = 0.8 - 0.6 * math.exp(-0.3 * l)
        mod3 = _ada(c, ada_w[l], ada_b[l]).reshape(bsz, 6, d)
        gq = jnp.tile(q_norm_g[l], D_ATTN // HEAD_DIM) * (HEAD_DIM ** -0.5 * LOG2E)
        gk = jnp.tile(k_norm_g[l], D_ATTN // HEAD_DIM)
        q, k, vt, sga, pc = _inproj(x2, mod3, norm1_g[l], w_in[l], gq, gk, pool_w[l], pool_scale[l],
                                    w_branch_pool[l], bsz=bsz, seq=seq, ts=_tile(seq, INPROJ_TILE))
        experts_f32 = (expert_w_gate[l], expert_w_up[l], expert_w_down[l])
        o, cleared, *experts_bf16 = _attention(
            q.reshape(bsz, seq, D_ATTN), k.reshape(bsz, seq, D_ATTN), vt, bias,
            lambda_q1[l], lambda_k1[l], lambda_q2[l], lambda_k2[l], subln_g[l],
            t=t_attn, lambda_init=lambda_init, clear_shape=(n_tiles * tm, d + LANES),
            to_bf16=[w.reshape(-1, w.shape[-1]) for w in experts_f32])
        wg_b, wu_b, wd_b = [wb.reshape(w.shape) for wb, w in zip(experts_bf16, experts_f32)]
        x1, h2p, route, counts = _post(
            o.reshape(n, D_ATTN), sga, pc, x2, mod3, norm2_g[l], w_branch_attn[l], w_out[l],
            router_group_w[l], router_group_b[l], router_expert_w[l], router_expert_b[l],
            seq=seq, ts=ts)
        pos, tile_ea, tile_eb, tile_valid = _route_tables(route, counts, n_tiles, tm)
        hs = _dispatch(pos, h2p, cleared, ts=tr)
        ys = _experts(tile_ea, tile_eb, tile_valid, hs, wg_b, wu_b, wd_b, tm=tm)
        x2 = _combine(pos, ys, x1, mod3, seq=seq, ts=tr)
    return x2.reshape(bsz, seq, d)
```

```python
import functools
import math

import jax
import jax.numpy as jnp
from jax import lax
from jax.experimental import pallas as pl
from jax.experimental.pallas import tpu as pltpu

F32 = jnp.float32
BF16 = jnp.bfloat16

CHUNK = 64
N_HEADS = 4
HEAD_DIM = 64
D_HEAD_V = 2 * HEAD_DIM
V_ROWS = D_HEAD_V + 16
D_ATTN = N_HEADS * D_HEAD_V
POOL_WINDOWS = (2, 4, 8, 16)
POOL_GROUP_DIM = 128
D_POOL = len(POOL_WINDOWS) * POOL_GROUP_DIM
POOL_HALO = 16
N_BUCKETS = 32
MAX_DISTANCE = 128
N_EXPERT_GROUPS = 4
EXPERTS_PER_GROUP = 4
N_EXPERTS = N_EXPERT_GROUPS * EXPERTS_PER_GROUP
PAIRS_PER_GROUP = 6
PAIR_LO = (0, 0, 1, 1, 0, 2)
PAIR_HI = (1, 2, 2, 3, 3, 3)
N_ROUTE_BUCKETS = N_EXPERT_GROUPS * PAIRS_PER_GROUP
RMS_EPS = 1e-6
LOG2E = math.log2(math.e)
MASKED = -1e30

LANES = 128
SUBLANES = 8
ROUTER_ROWS = 32
EXPERT_ROW0 = 8

VMEM_LIMIT = 56 * 1024 * 1024
TOKEN_TILE = 512
INPROJ_TILE = 1024
ATTN_TILE = 512
MOE_TILE = 256
ROW_COPY_TILE = 1024


def _cparams(n_axes):
    return pltpu.CompilerParams(dimension_semantics=("arbitrary",) * n_axes,
                                vmem_limit_bytes=VMEM_LIMIT)


def _const_spec(shape):
    nd = len(shape)
    return pl.BlockSpec(shape, lambda *_: (0,) * nd, pipeline_mode=pl.Buffered(1))


def _ada_kernel(ct_ref, w_ref, b_ref, o_ref):
    @pl.when(pl.program_id(0) == 0)
    def _():
        o_ref[...] = jnp.broadcast_to(b_ref[...], o_ref.shape)

    ct = ct_ref[...]
    s = ct * jax.nn.sigmoid(ct)
    w = w_ref[...]
    rows = [jnp.sum(w * s[:, b:b + 1], axis=0, keepdims=True) for b in range(ct.shape[1])]
    o_ref[...] += jnp.concatenate(rows, axis=0)


def _ada(c, w, b):
    bsz, d = c.shape
    n = w.shape[1]
    rows = 256
    return pl.pallas_call(
        _ada_kernel,
        out_shape=jax.ShapeDtypeStruct((bsz, n), F32),
        grid=(d // rows,),
        in_specs=[pl.BlockSpec((rows, bsz), lambda j: (j, 0)),
                  pl.BlockSpec((rows, n), lambda j: (j, 0)),
                  pl.BlockSpec((1, n), lambda j: (0, 0))],
        out_specs=pl.BlockSpec((bsz, n), lambda j: (0, 0)),
        compiler_params=_cparams(1),
        name="ada",
    )(c.T, w, b.reshape(1, n))


def _log_bucket_starts():
    nb = N_BUCKETS // 2
    max_exact = nb // 2
    m = nb - max_exact
    ratio = MAX_DISTANCE // max_exact
    starts = []
    for k in range(1, m):
        n = max_exact
        while n ** m < max_exact ** m * ratio ** k:
            n += 1
        starts.append(n)
    return tuple(starts)


LOG_BUCKET_STARTS = _log_bucket_starts()


def _bias_kernel(rb_ref, o_ref, *, t):
    h = pl.program_id(0)
    kind = pl.program_id(1)
    nb = N_BUCKETS // 2
    max_exact = nb // 2
    kpos = lax.broadcasted_iota(jnp.int32, (t, t), 0)
    qpos = lax.broadcasted_iota(jnp.int32, (t, t), 1)
    rel = kpos - qpos - jnp.where(kind == 0, t, 0)
    n = jnp.abs(rel)

    def table(first):
        val = jnp.full((t, t), rb_ref[first + nb - 1, h], F32)
        for k in range(len(LOG_BUCKET_STARTS) - 1, -1, -1):
            val = jnp.where(n < LOG_BUCKET_STARTS[k], rb_ref[first + max_exact + k, h], val)
        for j in range(max_exact - 1, -1, -1):
            val = jnp.where(n == j, rb_ref[first + j, h], val)
        return val

    far = rb_ref[nb - 1, h]

    @pl.when(kind == 0)
    def _():
        o_ref[...] = (table(0) - far) * LOG2E

    @pl.when(kind == 1)
    def _():
        bias = jnp.where(rel > 0, table(nb), table(0))
        shift = CHUNK.bit_length() - 1
        hidden = (kpos >> shift) > (qpos >> shift)
        o_ref[...] = jnp.where(hidden, MASKED, (bias - far) * LOG2E)


def _bias_tiles(rel_bias, t):
    return pl.pallas_call(
        functools.partial(_bias_kernel, t=t),
        out_shape=jax.ShapeDtypeStruct((N_HEADS, 2, t, t), F32),
        grid=(N_HEADS, 2),
        in_specs=[pl.BlockSpec(memory_space=pltpu.SMEM)],
        out_specs=pl.BlockSpec((None, None, t, t), lambda h, j: (h, j, 0, 0)),
        compiler_params=_cparams(2),
        name="bias_tiles",
    )(rel_bias)


def _group_rms(xc, ones_blockdiag, gain):
    ssq = jnp.dot((xc * xc).astype(BF16), ones_blockdiag, preferred_element_type=F32)
    return xc * lax.rsqrt(ssq * (1.0 / HEAD_DIM) + RMS_EPS) * gain


def _split_bf16(a):
    hi = a.astype(BF16)
    return hi, (a - hi.astype(F32)).astype(BF16)


def _pool_fold_kernel(pw_ref, ps_ref, wbp_ref, o_ref):
    a_hi, a_lo = _split_bf16(pw_ref[...] * ps_ref[...])
    b_hi, b_lo = _split_bf16(wbp_ref[...])
    dot = functools.partial(jnp.dot, preferred_element_type=F32)
    o_ref[...] = (dot(a_hi, b_hi) + dot(a_hi, b_lo) + dot(a_lo, b_hi)).astype(o_ref.dtype)


def _pool_fold(pool_w, pool_scale, w_bp):
    g, c, _ = pool_w.shape
    d = w_bp.shape[1]
    return pl.pallas_call(
        _pool_fold_kernel,
        out_shape=jax.ShapeDtypeStruct((g * c, d), BF16),
        grid=(g,),
        in_specs=[pl.BlockSpec((None, c, c), lambda i: (i, 0, 0)),
                  pl.BlockSpec((None, 1, c), lambda i: (i, 0, 0)),
                  pl.BlockSpec((c, d), lambda i: (i, 0))],
        out_specs=pl.BlockSpec((c, d), lambda i: (i, 0)),
        compiler_params=_cparams(1),
        name="pool_fold",
    )(pool_w, pool_scale.reshape(g, 1, c), w_bp)


def _inproj_kernel(x_ref, mod_ref, g1_ref, win_ref, gq_ref, gk_ref, ones_ref, wpool_ref,
                   q_ref, k_ref, vt_ref, sga_ref, pc_ref, ext_ref, v_ref,
                   *, ts, tiles_per_batch):
    tb = pl.program_id(0) % tiles_per_batch
    x = x_ref[...]
    y = x * lax.rsqrt(jnp.mean(x * x, axis=-1, keepdims=True) + RMS_EPS)
    h = y * (g1_ref[...] * (1.0 + mod_ref[1:2, :])) + mod_ref[0:1, :]
    hb = h.astype(BF16)

    def proj(c0, c1):
        return jnp.dot(hb, win_ref[:, c0:c1], preferred_element_type=F32)

    ones_bd = ones_ref[...]
    q_ref[...] = _group_rms(proj(0, D_ATTN), ones_bd, gq_ref[...]).astype(BF16)
    k_ref[...] = _group_rms(proj(D_ATTN, 2 * D_ATTN), ones_bd, gk_ref[...]).astype(BF16)
    v_ref[...] = proj(2 * D_ATTN, 3 * D_ATTN)
    vt = v_ref[...].T.astype(BF16)
    for hd in range(N_HEADS):
        vt_ref[hd, 0:D_HEAD_V, :] = vt[hd * D_HEAD_V:(hd + 1) * D_HEAD_V, :]
        vt_ref[hd, D_HEAD_V:V_ROWS, :] = jnp.ones((V_ROWS - D_HEAD_V, ts), BF16)
    c_u = 3 * D_ATTN
    c_ga = c_u + D_POOL
    c_gp = c_ga + x.shape[1]
    sga_ref[...] = jax.nn.sigmoid(proj(c_ga, c_gp)).astype(BF16)

    u = proj(c_u, c_ga)

    @pl.when(tb == 0)
    def _():
        ext_ref[0:POOL_HALO, :] = jnp.zeros((POOL_HALO, D_POOL), F32)

    ext_ref[POOL_HALO:POOL_HALO + ts, :] = u
    row = lax.broadcasted_iota(jnp.int32, (ts, 1), 0) + tb * ts
    ys = []
    for g, w in enumerate(POOL_WINDOWS):
        c0 = g * POOL_GROUP_DIM
        ug = u[:, c0:c0 + POOL_GROUP_DIM]
        acc = ug
        for d in range(1, w):
            acc = acc + ext_ref[POOL_HALO - d:POOL_HALO - d + ts, c0:c0 + POOL_GROUP_DIM]
        cnt = jnp.minimum(row + 1, w).astype(F32)
        ys.append((acc / cnt - ug).astype(BF16))
    ypool = jnp.dot(jnp.concatenate(ys, axis=1), wpool_ref[...], preferred_element_type=F32)
    pc_ref[...] = (jax.nn.sigmoid(proj(c_gp, c_gp + x.shape[1])) * ypool).astype(BF16)
    ext_ref[0:POOL_HALO, :] = u[ts - POOL_HALO:ts, :]


def _inproj(x2, mod3, g1, w_in, gq, gk, pool_w, pool_scale, w_bp, *, bsz, seq, ts):
    n, d = x2.shape
    d_in = w_in.shape[1]
    tiles_per_batch = seq // ts
    win_b = w_in.astype(BF16)
    idx = jnp.arange(D_ATTN) // HEAD_DIM
    ones_bd = (idx[:, None] == idx[None, :]).astype(BF16)
    kern = functools.partial(_inproj_kernel, ts=ts, tiles_per_batch=tiles_per_batch)
    tok = lambda i: (i, 0)
    out_shape = (jax.ShapeDtypeStruct((n, D_ATTN), BF16),
                 jax.ShapeDtypeStruct((n, D_ATTN), BF16),
                 jax.ShapeDtypeStruct((bsz, N_HEADS, V_ROWS, seq), BF16),
                 jax.ShapeDtypeStruct((n, d), BF16),
                 jax.ShapeDtypeStruct((n, d), BF16))
    return pl.pallas_call(
        kern,
        out_shape=out_shape,
        grid=(n // ts,),
        in_specs=[pl.BlockSpec((ts, d), tok),
                  pl.BlockSpec((None, 6, d), lambda i: (i // tiles_per_batch, 0, 0)),
                  _const_spec((1, d)),
                  _const_spec((d, d_in)),
                  _const_spec((1, D_ATTN)),
                  _const_spec((1, D_ATTN)),
                  _const_spec((D_ATTN, D_ATTN)),
                  _const_spec((D_POOL, d))],
        out_specs=(pl.BlockSpec((ts, D_ATTN), tok),
                   pl.BlockSpec((ts, D_ATTN), tok),
                   pl.BlockSpec((None, N_HEADS, V_ROWS, ts),
                                lambda i: (i // tiles_per_batch, 0, 0, i % tiles_per_batch)),
                   pl.BlockSpec((ts, d), tok),
                   pl.BlockSpec((ts, d), tok)),
        scratch_shapes=[pltpu.VMEM((POOL_HALO + ts, D_POOL), F32), pltpu.VMEM((ts, D_ATTN), F32)],
        compiler_params=_cparams(1),
        name="inproj",
    )(x2, mod3, g1.reshape(1, d), win_b, gq.reshape(1, D_ATTN), gk.reshape(1, D_ATTN), ones_bd,
      _pool_fold(pool_w, pool_scale, w_bp))


def _attn_kernel(q_ref, qn_ref, k_ref, vt_ref, bias_ref, lq1_ref, lk1_ref, lq2_ref, lk2_ref, subg_ref,
                 *rest, t, lambda_init, n_cast):
    cast_in = rest[:n_cast]
    o_ref, clear_ref = rest[n_cast:n_cast + 2]
    cast_out = rest[n_cast + 2:2 * n_cast + 2]
    s_ref, mt_ref, m_ref, acc_ref = rest[2 * n_cast + 2:]
    i = pl.program_id(2)
    last = pl.num_programs(2) - 1
    clear_ref[...] = jnp.zeros(clear_ref.shape, clear_ref.dtype)
    for src, dst in zip(cast_in, cast_out):
        dst[...] = src[...].astype(dst.dtype)

    def split_maps(q):
        lane = lax.broadcasted_iota(jnp.int32, q.shape, 1)
        zero = jnp.zeros_like(q)
        return jnp.where(lane < HEAD_DIM, q, zero), jnp.where(lane >= HEAD_DIM, q, zero)

    q_now = split_maps(q_ref[...])

    m_ref[...] = jnp.full(m_ref.shape, MASKED, F32)
    acc_ref[...] = jnp.zeros(acc_ref.shape, F32)

    def scores(j, bias, slot, qm=q_now):
        kt = k_ref[pl.ds(pl.multiple_of(j * t, t), t), :]
        for mp in range(2):
            s = lax.dot_general(kt, qm[mp], (((1,), (1,)), ((), ())), preferred_element_type=F32)
            if bias is not None:
                s = s + bias
            s_ref[slot, mp] = s
            mt_ref[slot, mp] = jnp.max(s, axis=0, keepdims=True)

    def accumulate(j, slot):
        vt = vt_ref[:, pl.ds(pl.multiple_of(j * t, t), t)]
        for mp in range(2):
            m_old = m_ref[mp]
            m_new = jnp.maximum(m_old, mt_ref[slot, mp])
            p = jnp.exp2(s_ref[slot, mp] - m_new).astype(BF16)
            acc_ref[mp] = (jnp.exp2(m_old - m_new) * acc_ref[mp]
                           + jnp.dot(vt, p, preferred_element_type=F32))
            m_ref[mp] = m_new

    def next_diagonal():
        nxt = jnp.minimum(i + 1, last)
        scores(nxt, bias_ref[1], 2, split_maps(qn_ref[...]))

    @pl.when(i == 0)
    def _():
        scores(0, bias_ref[1], 0)

    @pl.when(i >= 1)
    def _():
        scores(i - 1, bias_ref[0], 1)
        accumulate(i, 2)

    n_far = jnp.maximum(i - 1, 0)

    def pair(kk):
        j = i - 1 - 2 * kk
        scores(j - 1, None, 0)
        accumulate(j, 1)
        scores(j - 2, None, 1)
        accumulate(j - 1, 0)

    def oct_body(oo, carry):
        for u in range(4):
            pair(4 * oo + u)
        return carry

    n_pairs = n_far // 2
    lax.fori_loop(0, n_pairs // 4, oct_body, 0)

    @pl.when(n_pairs % 4 >= 2)
    def _():
        pair(n_pairs // 4 * 4)
        pair(n_pairs // 4 * 4 + 1)

    @pl.when(n_pairs % 2 == 1)
    def _():
        pair(n_pairs - 1)

    def finish(slot):
        next_diagonal()
        accumulate(0, slot)
        lam = (jnp.exp(jnp.sum(lq1_ref[...] * lk1_ref[...], axis=1, keepdims=True))
               - jnp.exp(jnp.sum(lq2_ref[...] * lk2_ref[...], axis=1, keepdims=True)) + lambda_init)
        o1 = acc_ref[0, 0:D_HEAD_V, :] / acc_ref[0, D_HEAD_V:D_HEAD_V + 1, :]
        o2 = acc_ref[1, 0:D_HEAD_V, :] / acc_ref[1, D_HEAD_V:D_HEAD_V + 1, :]
        ot = o1 - lam * o2
        ot = ot * lax.rsqrt(jnp.mean(ot * ot, axis=0, keepdims=True) + RMS_EPS)
        ot = ot * subg_ref[...] * (1.0 - lambda_init)
        o_ref[...] = ot.T.astype(BF16)

    @pl.when(n_far % 2 == 1)
    def _():
        scores(0, None, 0)
        accumulate(1, 1)
        finish(0)

    @pl.when(jnp.logical_and(i >= 1, n_far % 2 == 0))
    def _():
        finish(1)

    @pl.when(i == 0)
    def _():
        finish(0)


def _attention(q, k, vt, bias, lq1, lk1, lq2, lk2, subln_g, *, t, lambda_init, clear_shape, to_bf16):
    bsz, seq, _ = q.shape
    nq = seq // t
    assert t + 1 >= LOG_BUCKET_STARTS[-1], "keys two tiles back must all fall in the last distance bucket"
    steps = bsz * N_HEADS * nq
    step_id = lambda b, h, i: ((b * N_HEADS + h) * nq + i, 0)
    clear_rows = clear_shape[0] // steps
    assert clear_rows * steps == clear_shape[0] and clear_rows % SUBLANES == 0, (clear_shape, steps)
    cast_specs = []
    for a in to_bf16:
        rows = a.shape[0] // steps
        assert rows * steps == a.shape[0] and rows % (2 * SUBLANES) == 0, (a.shape, steps)
        cast_specs.append(pl.BlockSpec((rows, a.shape[1]), step_id))
    kern = functools.partial(_attn_kernel, t=t, lambda_init=lambda_init, n_cast=len(to_bf16))
    vec = lambda a: a.reshape(1, HEAD_DIM)
    return pl.pallas_call(
        kern,
        out_shape=(jax.ShapeDtypeStruct((bsz, seq, D_ATTN), BF16),
                   jax.ShapeDtypeStruct(clear_shape, F32),
                   *[jax.ShapeDtypeStruct(a.shape, BF16) for a in to_bf16]),
        grid=(bsz, N_HEADS, seq // t),
        in_specs=[pl.BlockSpec((None, t, D_HEAD_V), lambda b, h, i: (b, i, h)),
                  pl.BlockSpec((None, t, D_HEAD_V), lambda b, h, i: (b, jnp.minimum(i + 1, nq - 1), h)),
                  pl.BlockSpec((None, seq, D_HEAD_V), lambda b, h, i: (b, 0, h)),
                  pl.BlockSpec((None, None, V_ROWS, seq), lambda b, h, i: (b, h, 0, 0)),
                  pl.BlockSpec((None, 2, t, t), lambda b, h, i: (h, 0, 0, 0)),
                  _const_spec((1, HEAD_DIM)), _const_spec((1, HEAD_DIM)),
                  _const_spec((1, HEAD_DIM)), _const_spec((1, HEAD_DIM)),
                  _const_spec((D_HEAD_V, 1)), *cast_specs],
        out_specs=(pl.BlockSpec((None, t, D_HEAD_V), lambda b, h, i: (b, i, h)),
                   pl.BlockSpec((clear_rows, clear_shape[1]), step_id), *cast_specs),
        scratch_shapes=[pltpu.VMEM((3, 2, t, t), F32),
                        pltpu.VMEM((3, 2, 1, t), F32),
                        pltpu.VMEM((2, 1, t), F32),
                        pltpu.VMEM((2, V_ROWS, t), F32)],
        compiler_params=_cparams(3),
        name="attn",
    )(q, q, k, vt, bias, vec(lq1), vec(lk1), vec(lq2), vec(lk2), subln_g.reshape(D_HEAD_V, 1),
      *to_bf16)


def _first_max(rows):
    best = rows[0]
    for r in rows[1:]:
        best = jnp.maximum(best, r)
    idx = jnp.full(best.shape, len(rows) - 1, jnp.int32)
    for j in range(len(rows) - 2, -1, -1):
        idx = jnp.where(rows[j] == best, j, idx)
    return best, idx


def _post_kernel(o_ref, sga_ref, pc_ref, x_ref, mod_ref, g2_ref, wba_ref, wout_ref, wr_ref, br_ref,
                 tri_ref, x1_ref, h2p_ref, route_ref, cnt_ref, carry_ref):
    @pl.when(pl.program_id(0) == 0)
    def _():
        carry_ref[...] = jnp.zeros(carry_ref.shape, F32)

    ya = jnp.dot(o_ref[...], wba_ref[...], preferred_element_type=F32)
    merged = sga_ref[...].astype(F32) * ya + pc_ref[...].astype(F32)
    z = jnp.dot(merged.astype(BF16), wout_ref[...], preferred_element_type=F32)
    x1 = x_ref[...] + mod_ref[2:3, :] * z
    x1_ref[...] = x1
    y = x1 * lax.rsqrt(jnp.mean(x1 * x1, axis=-1, keepdims=True) + RMS_EPS)
    h2 = y * (g2_ref[...] * (1.0 + mod_ref[4:5, :])) + mod_ref[3:4, :]
    hi = h2.astype(BF16)
    lo = (h2 - hi.astype(F32)).astype(BF16)

    nt = (((1,), (1,)), ((), ()))
    a = lax.dot_general(wr_ref[...], hi, nt, preferred_element_type=F32)
    b = lax.dot_general(wr_ref[0:ROUTER_ROWS, :], lo, nt, preferred_element_type=F32)
    logits = a[0:ROUTER_ROWS] + a[ROUTER_ROWS:2 * ROUTER_ROWS] + b + br_ref[...]

    gl = [logits[g:g + 1, :] for g in range(N_EXPERT_GROUPS)]
    gmax, gidx = _first_max(gl)
    gsum = gl[0] * 0.0
    for r in gl:
        gsum = gsum + jnp.exp(r - gmax)
    g_val = 1.0 / gsum
    es = []
    for r in range(EXPERTS_PER_GROUP):
        sel = jnp.zeros_like(gmax)
        for g in range(N_EXPERT_GROUPS):
            row = EXPERT_ROW0 + g * EXPERTS_PER_GROUP + r
            sel = jnp.where(gidx == g, logits[row:row + 1, :], sel)
        es.append(sel)
    e1, i1 = _first_max(es)
    rest = [jnp.where(i1 == r, -jnp.inf, es[r]) for r in range(EXPERTS_PER_GROUP)]
    e2, i2 = _first_max(rest)
    r21 = jnp.exp(e2 - e1)
    w1 = g_val / (1.0 + r21)
    w2 = g_val * r21 / (1.0 + r21)

    first = i1 < i2
    e_lo = jnp.where(first, i1, i2)
    e_hi = jnp.where(first, i2, i1)
    pair = jnp.zeros_like(e_lo)
    for p in range(1, PAIRS_PER_GROUP):
        pair = jnp.where(jnp.logical_and(e_lo == PAIR_LO[p], e_hi == PAIR_HI[p]), p, pair)
    bucket = gidx * PAIRS_PER_GROUP + pair
    w_lo = jnp.where(first, w1, w2)
    w_hi = jnp.where(first, w2, w1)

    ts = bucket.shape[1]
    brow = lax.broadcasted_iota(jnp.int32, (ROUTER_ROWS, ts), 0)
    onehot = brow == bucket
    prefix = jnp.dot(jnp.where(onehot, 1.0, 0.0).astype(BF16), tri_ref[...],
                     preferred_element_type=F32)
    carry = carry_ref[...]
    rank = jnp.sum(jnp.where(onehot, prefix + carry, 0.0), axis=0, keepdims=True) - 1.0
    carry = carry + prefix[:, ts - 1:ts]
    carry_ref[...] = carry
    cnt_ref[...] = jnp.broadcast_to(carry, cnt_ref.shape).astype(jnp.int32)
    chunks = ts // LANES
    rank_i = rank.astype(jnp.int32)
    for j in range(chunks):
        route_ref[j:j + 1, :] = bucket[:, j * LANES:(j + 1) * LANES]
        route_ref[chunks + j:chunks + j + 1, :] = rank_i[:, j * LANES:(j + 1) * LANES]

    arow = lax.broadcasted_iota(jnp.int32, (LANES, ts), 0)
    aux_t = jnp.where(arow == 0, w_lo, jnp.where(arow == 1, w_hi, 0.0))
    d = h2.shape[1]
    h2p_ref[:, 0:d] = h2
    h2p_ref[:, d:d + LANES] = aux_t.T


def _post(o2, sga, pc, x2, mod3, g2, w_ba, w_out, wg_r, bg_r, we_r, be_r, *, seq, ts):
    n, d = x2.shape
    tiles_per_batch = seq // ts
    steps = n // ts
    wr = jnp.zeros((ROUTER_ROWS, d), F32)
    wr = wr.at[0:N_EXPERT_GROUPS].set(wg_r.T).at[EXPERT_ROW0:EXPERT_ROW0 + N_EXPERTS].set(we_r.T)
    wr_hi = wr.astype(BF16)
    wr_lo = (wr - wr_hi.astype(F32)).astype(BF16)
    br = jnp.zeros((ROUTER_ROWS, 1), F32)
    br = br.at[0:N_EXPERT_GROUPS, 0].set(bg_r).at[EXPERT_ROW0:EXPERT_ROW0 + N_EXPERTS, 0].set(be_r)
    tok = lambda i: (i, 0)
    tidx = jnp.arange(ts)
    tri = (tidx[:, None] <= tidx[None, :]).astype(BF16)
    return pl.pallas_call(
        _post_kernel,
        out_shape=(jax.ShapeDtypeStruct((n, d), F32),
                   jax.ShapeDtypeStruct((n, d + LANES), F32),
                   jax.ShapeDtypeStruct((steps, 2 * (ts // LANES), LANES), jnp.int32),
                   jax.ShapeDtypeStruct((ROUTER_ROWS, LANES), jnp.int32)),
        grid=(steps,),
        in_specs=[pl.BlockSpec((ts, D_ATTN), tok),
                  pl.BlockSpec((ts, d), tok),
                  pl.BlockSpec((ts, d), tok),
                  pl.BlockSpec((ts, d), tok),
                  pl.BlockSpec((None, 6, d), lambda i: (i // tiles_per_batch, 0, 0)),
                  _const_spec((1, d)),
                  _const_spec((D_ATTN, d)),
                  _const_spec((d, d)),
                  _const_spec((2 * ROUTER_ROWS, d)),
                  _const_spec((ROUTER_ROWS, 1)),
                  _const_spec((ts, ts))],
        out_specs=(pl.BlockSpec((ts, d), tok), pl.BlockSpec((ts, d + LANES), tok),
                   pl.BlockSpec((None, 2 * (ts // LANES), LANES), lambda i: (i, 0, 0)),
                   pl.BlockSpec((ROUTER_ROWS, LANES), lambda i: (0, 0))),
        scratch_shapes=[pltpu.VMEM((ROUTER_ROWS, 1), F32)],
        compiler_params=_cparams(1),
        name="post",
    )(o2, sga, pc, x2, mod3, g2.reshape(1, d), w_ba.astype(BF16), w_out.astype(BF16),
      jnp.concatenate([wr_hi, wr_lo], axis=0), br, tri)


def _group_sublane(row):
    return lax.shift_right_logical(row, SUBLANES.bit_length() - 1), row & (SUBLANES - 1)


def _row_copy(src, src_row, dst, dst_row, sem):
    sg, ss = src_row
    dg, ds = dst_row
    return pltpu.make_async_copy(src.at[sg, pl.ds(ss, 1), :], dst.at[dg, pl.ds(ds, 1), :], sem)


def _dispatch_kernel(pos_ref, h_ref, init_ref, hs_ref, sem, *, ts):
    del init_ref
    base = pl.program_id(0) * ts

    def start(g, carry):
        for u in range(SUBLANES):
            p = pos_ref[base + g * SUBLANES + u]
            _row_copy(h_ref, (g, u), hs_ref, _group_sublane(p), sem).start(priority=u % 2)
        return carry

    def wait(g, carry):
        for u in range(SUBLANES):
            _row_copy(h_ref, (0, 0), hs_ref, (0, 0), sem).wait()
        return carry

    lax.fori_loop(0, ts // SUBLANES, start, 0)
    lax.fori_loop(0, ts // SUBLANES, wait, 0)


def _dispatch(pos, h2p, cleared, *, ts):
    n, w = h2p.shape
    n_rows = cleared.shape[0]
    return pl.pallas_call(
        functools.partial(_dispatch_kernel, ts=ts),
        out_shape=jax.ShapeDtypeStruct((n_rows // SUBLANES, SUBLANES, w), h2p.dtype),
        grid_spec=pltpu.PrefetchScalarGridSpec(
            num_scalar_prefetch=1,
            grid=(n // ts,),
            in_specs=[pl.BlockSpec((ts // SUBLANES, SUBLANES, w), lambda i, pos: (i, 0, 0)),
                      pl.BlockSpec(memory_space=pl.ANY)],
            out_specs=pl.BlockSpec(memory_space=pl.ANY),
            scratch_shapes=[pltpu.SemaphoreType.DMA]),
        input_output_aliases={2: 0},
        compiler_params=_cparams(1),
        name="dispatch",
    )(pos, h2p.reshape(n // SUBLANES, SUBLANES, w),
      cleared.reshape(n_rows // SUBLANES, SUBLANES, w)).reshape(n_rows, w)


def _experts_kernel(ea_ref, eb_ref, valid_ref, hs_ref, wga_ref, wua_ref, wda_ref, wgb_ref, wub_ref,
                    wdb_ref, ys_ref):
    del ea_ref, eb_ref
    rows = valid_ref[pl.program_id(0)]
    tm = hs_ref.shape[0]
    d = hs_ref.shape[1] - LANES

    def run(m):
        h = hs_ref[0:m, 0:d].astype(BF16)
        aux = hs_ref[0:m, d:d + LANES]

        def expert(wg_ref, wu_ref, wd_ref):
            a = jnp.dot(h, wg_ref[...], preferred_element_type=F32)
            b = jnp.dot(h, wu_ref[...], preferred_element_type=F32)
            hid = (a * jax.nn.sigmoid(a)) * b
            return jnp.dot(hid.astype(BF16), wd_ref[...], preferred_element_type=F32)

        ys_ref[0:m, :] = (aux[:, 0:1] * expert(wga_ref, wua_ref, wda_ref)
                          + aux[:, 1:2] * expert(wgb_ref, wub_ref, wdb_ref))
        if m < tm:
            ys_ref[m:tm, :] = jnp.zeros((tm - m, ys_ref.shape[1]), ys_ref.dtype)

    @pl.when(rows > tm // 2)
    def _():
        run(tm)

    @pl.when(jnp.logical_and(rows > 0, rows <= tm // 2))
    def _():
        run(tm // 2)

    @pl.when(rows == 0)
    def _():
        ys_ref[...] = jnp.zeros(ys_ref.shape, ys_ref.dtype)


def _experts(tile_ea, tile_eb, tile_valid, hs, w_gate, w_up, w_down, *, tm):
    n_rows, w = hs.shape
    _, d, de = w_gate.shape
    wg, wu, wd = w_gate, w_up, w_down
    ea = lambda t, ea_r, eb_r, v_r: (ea_r[t], 0, 0)
    eb = lambda t, ea_r, eb_r, v_r: (eb_r[t], 0, 0)
    row = lambda t, ea_r, eb_r, v_r: (t, 0)
    return pl.pallas_call(
        _experts_kernel,
        out_shape=jax.ShapeDtypeStruct((n_rows, d), F32),
        grid_spec=pltpu.PrefetchScalarGridSpec(
            num_scalar_prefetch=3,
            grid=(n_rows // tm,),
            in_specs=[pl.BlockSpec((tm, w), row),
                      pl.BlockSpec((None, d, de), ea), pl.BlockSpec((None, d, de), ea),
                      pl.BlockSpec((None, de, d), ea),
                      pl.BlockSpec((None, d, de), eb), pl.BlockSpec((None, d, de), eb),
                      pl.BlockSpec((None, de, d), eb)],
            out_specs=pl.BlockSpec((tm, d), row)),
        compiler_params=_cparams(1),
        name="experts",
    )(tile_ea, tile_eb, tile_valid, hs, wg, wu, wd, wg, wu, wd)


def _combine_kernel(pos_ref, ys_ref, x1_ref, mod_ref, out_ref, ybuf_ref, sem, *, ts):
    i = pl.program_id(0)
    groups = ts // SUBLANES

    def gather(tile, slot):
        base = tile * ts

        def start(g, carry):
            for u in range(SUBLANES):
                p = pos_ref[base + g * SUBLANES + u]
                _row_copy(ys_ref, _group_sublane(p), ybuf_ref.at[slot], (g, u),
                          sem.at[slot]).start(priority=u % 2)
            return carry

        lax.fori_loop(0, groups, start, 0)

    def wait_all(slot):
        def wait(g, carry):
            for u in range(SUBLANES):
                _row_copy(ys_ref, (0, 0), ybuf_ref.at[slot], (0, 0), sem.at[slot]).wait()
            return carry

        lax.fori_loop(0, groups, wait, 0)

    def step(slot):
        @pl.when(i + 1 < pl.num_programs(0))
        def _():
            gather(i + 1, 1 - slot)

        wait_all(slot)
        y = ybuf_ref[slot].reshape(ts, ybuf_ref.shape[-1])
        out_ref[...] = x1_ref[...] + mod_ref[5:6, :] * y

    @pl.when(i == 0)
    def _():
        gather(0, 0)

    @pl.when(i % 2 == 0)
    def _():
        step(0)

    @pl.when(i % 2 == 1)
    def _():
        step(1)


def _combine(pos, ys, x1, mod3, *, seq, ts):
    n, d = x1.shape
    tiles_per_batch = seq // ts
    return pl.pallas_call(
        functools.partial(_combine_kernel, ts=ts),
        out_shape=jax.ShapeDtypeStruct((n, d), F32),
        grid_spec=pltpu.PrefetchScalarGridSpec(
            num_scalar_prefetch=1,
            grid=(n // ts,),
            in_specs=[pl.BlockSpec(memory_space=pl.ANY),
                      pl.BlockSpec((ts, d), lambda i, pos: (i, 0)),
                      pl.BlockSpec((None, 6, d), lambda i, pos: (i // tiles_per_batch, 0, 0))],
            out_specs=pl.BlockSpec((ts, d), lambda i, pos: (i, 0)),
            scratch_shapes=[pltpu.VMEM((2, ts // SUBLANES, SUBLANES, d), F32),
                            pltpu.SemaphoreType.DMA((2,))]),
        compiler_params=_cparams(1),
        name="combine",
    )(pos, ys.reshape(ys.shape[0] // SUBLANES, SUBLANES, d), x1, mod3)


def _pos_kernel(start_ref, route_ref, pos_ref):
    chunks = pos_ref.shape[1]
    bucket = route_ref[:, 0:chunks, :]
    pos = route_ref[:, chunks:2 * chunks, :]
    for b in range(N_ROUTE_BUCKETS):
        pos = pos + jnp.where(bucket == b, start_ref[b], 0)
    pos_ref[...] = pos


def _route_tables(route, counts, n_tiles, tm):
    steps, rows, _ = route.shape
    cnt = counts[:N_ROUTE_BUCKETS, 0]
    padded = (cnt + tm - 1) // tm * tm
    end = jnp.cumsum(padded)
    pos = pl.pallas_call(
        _pos_kernel,
        out_shape=jax.ShapeDtypeStruct((steps, rows // 2, LANES), jnp.int32),
        in_specs=[pl.BlockSpec(memory_space=pltpu.SMEM), pl.BlockSpec(memory_space=pltpu.VMEM)],
        out_specs=pl.BlockSpec(memory_space=pltpu.VMEM),
        name="pos",
    )(end - padded, route).reshape(-1)
    tiles_used = end[-1] // tm
    tile = jnp.arange(n_tiles, dtype=jnp.int32)
    valid = tile < tiles_used
    first_row = jnp.minimum(tile, tiles_used - 1) * tm
    tile_bucket = jnp.sum((end[None, :] <= first_row[:, None]).astype(jnp.int32), axis=1)
    tile_bucket = jnp.minimum(tile_bucket, N_ROUTE_BUCKETS - 1)
    group = tile_bucket // PAIRS_PER_GROUP
    pair = tile_bucket % PAIRS_PER_GROUP
    tile_ea = group * EXPERTS_PER_GROUP + jnp.asarray(PAIR_LO, jnp.int32)[pair]
    tile_eb = group * EXPERTS_PER_GROUP + jnp.asarray(PAIR_HI, jnp.int32)[pair]
    real_end = (end - padded + cnt)[tile_bucket]
    tile_rows = jnp.where(valid, jnp.clip(real_end - tile * tm, 0, tm), 0)
    return pos.astype(jnp.int32), tile_ea, tile_eb, tile_rows.astype(jnp.int32)


def _tile(seq, pref):
    t = min(pref, seq)
    assert seq % t == 0 and t % LANES == 0, (seq, t)
    return t


def kernel(x, c, rel_bias, ada_w, ada_b, norm1_g, w_in, q_norm_g, k_norm_g, lambda_q1, lambda_k1,
           lambda_q2, lambda_k2, subln_g, w_branch_attn, pool_w, pool_scale, w_branch_pool, w_out,
           norm2_g, router_group_w, router_group_b, router_expert_w, router_expert_b,
           expert_w_gate, expert_w_up, expert_w_down):
    bsz, seq, d = x.shape
    n = bsz * seq
    ts = _tile(seq, TOKEN_TILE)
    t_attn = _tile(seq, ATTN_TILE)
    tr = _tile(seq, ROW_COPY_TILE)
    tm = MOE_TILE
    n_tiles = -(-(n + N_ROUTE_BUCKETS * (tm - 1)) // tm)
    attn_steps = bsz * N_HEADS * (seq // t_attn)
    while (n_tiles * tm) % (attn_steps * SUBLANES):
        n_tiles += 1
    bias = _bias_tiles(rel_bias, t_attn)
    x2 = x.reshape(n, d)
    for l in range(ada_w.shape[0]):
        lambda_init = 0.8 - 0.6 * math.exp(-0.3 * l)
        mod3 = _ada(c, ada_w[l], ada_b[l]).reshape(bsz, 6, d)
        gq = jnp.tile(q_norm_g[l], D_ATTN // HEAD_DIM) * (HEAD_DIM ** -0.5 * LOG2E)
        gk = jnp.tile(k_norm_g[l], D_ATTN // HEAD_DIM)
        q, k, vt, sga, pc = _inproj(x2, mod3, norm1_g[l], w_in[l], gq, gk, pool_w[l], pool_scale[l],
                                    w_branch_pool[l], bsz=bsz, seq=seq, ts=_tile(seq, INPROJ_TILE))
        experts_f32 = (expert_w_gate[l], expert_w_up[l], expert_w_down[l])
        o, cleared, *experts_bf16 = _attention(
            q.reshape(bsz, seq, D_ATTN), k.reshape(bsz, seq, D_ATTN), vt, bias,
            lambda_q1[l], lambda_k1[l], lambda_q2[l], lambda_k2[l], subln_g[l],
            t=t_attn, lambda_init=lambda_init, clear_shape=(n_tiles * tm, d + LANES),
            to_bf16=[w.reshape(-1, w.shape[-1]) for w in experts_f32])
        wg_b, wu_b, wd_b = [wb.reshape(w.shape) for wb, w in zip(experts_bf16, experts_f32)]
        x1, h2p, route, counts = _post(
            o.reshape(n, D_ATTN), sga, pc, x2, mod3, norm2_g[l], w_branch_attn[l], w_out[l],
            router_group_w[l], router_group_b[l], router_expert_w[l], router_expert_b[l],
            seq=seq, ts=ts)
        pos, tile_ea, tile_eb, tile_valid = _route_tables(route, counts, n_tiles, tm)
        hs = _dispatch(pos, h2p, cleared, ts=tr)
        ys = _experts(tile_ea, tile_eb, tile_valid, hs, wg_b, wu_b, wd_b, tm=tm)
        x2 = _combine(pos, ys, x1, mod3, seq=seq, ts=tr)
    return x2.reshape(bsz, seq, d)
```

```python
import functools
import math

import jax
import jax.numpy as jnp
from jax import lax
from jax.experimental import pallas as pl
from jax.experimental.pallas import tpu as pltpu

F32 = jnp.float32
BF16 = jnp.bfloat16

CHUNK = 64
N_HEADS = 4
HEAD_DIM = 64
D_HEAD_V = 2 * HEAD_DIM
V_ROWS = D_HEAD_V + 16
D_ATTN = N_HEADS * D_HEAD_V
POOL_WINDOWS = (2, 4, 8, 16)
POOL_GROUP_DIM = 128
D_POOL = len(POOL_WINDOWS) * POOL_GROUP_DIM
POOL_HALO = 16
N_BUCKETS = 32
MAX_DISTANCE = 128
N_EXPERT_GROUPS = 4
EXPERTS_PER_GROUP = 4
N_EXPERTS = N_EXPERT_GROUPS * EXPERTS_PER_GROUP
PAIRS_PER_GROUP = 6
PAIR_LO = (0, 0, 1, 1, 0, 2)
PAIR_HI = (1, 2, 2, 3, 3, 3)
N_ROUTE_BUCKETS = N_EXPERT_GROUPS * PAIRS_PER_GROUP
RMS_EPS = 1e-6
LOG2E = math.log2(math.e)
MASKED = -1e30

LANES = 128
SUBLANES = 8
ROUTER_ROWS = 32
EXPERT_ROW0 = 8

VMEM_LIMIT = 56 * 1024 * 1024
TOKEN_TILE = 512
INPROJ_TILE = 1024
ATTN_TILE = 512
MOE_TILE = 512
ROW_COPY_TILE = 1024


def _cparams(n_axes):
    return pltpu.CompilerParams(dimension_semantics=("arbitrary",) * n_axes,
                                vmem_limit_bytes=VMEM_LIMIT)


def _const_spec(shape):
    nd = len(shape)
    return pl.BlockSpec(shape, lambda *_: (0,) * nd, pipeline_mode=pl.Buffered(1))


def _ada_kernel(ct_ref, w_ref, b_ref, o_ref):
    @pl.when(pl.program_id(0) == 0)
    def _():
        o_ref[...] = jnp.broadcast_to(b_ref[...], o_ref.shape)

    ct = ct_ref[...]
    s = ct * jax.nn.sigmoid(ct)
    w = w_ref[...]
    rows = [jnp.sum(w * s[:, b:b + 1], axis=0, keepdims=True) for b in range(ct.shape[1])]
    o_ref[...] += jnp.concatenate(rows, axis=0)


def _ada(c, w, b):
    bsz, d = c.shape
    n = w.shape[1]
    rows = 256
    return pl.pallas_call(
        _ada_kernel,
        out_shape=jax.ShapeDtypeStruct((bsz, n), F32),
        grid=(d // rows,),
        in_specs=[pl.BlockSpec((rows, bsz), lambda j: (j, 0)),
                  pl.BlockSpec((rows, n), lambda j: (j, 0)),
                  pl.BlockSpec((1, n), lambda j: (0, 0))],
        out_specs=pl.BlockSpec((bsz, n), lambda j: (0, 0)),
        compiler_params=_cparams(1),
        name="ada",
    )(c.T, w, b.reshape(1, n))


def _log_bucket_starts():
    nb = N_BUCKETS // 2
    max_exact = nb // 2
    m = nb - max_exact
    ratio = MAX_DISTANCE // max_exact
    starts = []
    for k in range(1, m):
        n = max_exact
        while n ** m < max_exact ** m * ratio ** k:
            n += 1
        starts.append(n)
    return tuple(starts)


LOG_BUCKET_STARTS = _log_bucket_starts()


def _bias_kernel(rb_ref, o_ref, *, t):
    h = pl.program_id(0)
    kind = pl.program_id(1)
    nb = N_BUCKETS // 2
    max_exact = nb // 2
    kpos = lax.broadcasted_iota(jnp.int32, (t, t), 0)
    qpos = lax.broadcasted_iota(jnp.int32, (t, t), 1)
    rel = kpos - qpos - jnp.where(kind == 0, t, 0)
    n = jnp.abs(rel)

    def table(first):
        val = jnp.full((t, t), rb_ref[first + nb - 1, h], F32)
        for k in range(len(LOG_BUCKET_STARTS) - 1, -1, -1):
            val = jnp.where(n < LOG_BUCKET_STARTS[k], rb_ref[first + max_exact + k, h], val)
        for j in range(max_exact - 1, -1, -1):
            val = jnp.where(n == j, rb_ref[first + j, h], val)
        return val

    far = rb_ref[nb - 1, h]

    @pl.when(kind == 0)
    def _():
        o_ref[...] = (table(0) - far) * LOG2E

    @pl.when(kind == 1)
    def _():
        bias = jnp.where(rel > 0, table(nb), table(0))
        shift = CHUNK.bit_length() - 1
        hidden = (kpos >> shift) > (qpos >> shift)
        o_ref[...] = jnp.where(hidden, MASKED, (bias - far) * LOG2E)


def _bias_tiles(rel_bias, t):
    return pl.pallas_call(
        functools.partial(_bias_kernel, t=t),
        out_shape=jax.ShapeDtypeStruct((N_HEADS, 2, t, t), F32),
        grid=(N_HEADS, 2),
        in_specs=[pl.BlockSpec(memory_space=pltpu.SMEM)],
        out_specs=pl.BlockSpec((None, None, t, t), lambda h, j: (h, j, 0, 0)),
        compiler_params=_cparams(2),
        name="bias_tiles",
    )(rel_bias)


def _group_rms(xc, ones_blockdiag, gain):
    ssq = jnp.dot((xc * xc).astype(BF16), ones_blockdiag, preferred_element_type=F32)
    return xc * lax.rsqrt(ssq * (1.0 / HEAD_DIM) + RMS_EPS) * gain


def _split_bf16(a):
    hi = a.astype(BF16)
    return hi, (a - hi.astype(F32)).astype(BF16)


def _pool_fold_kernel(pw_ref, ps_ref, wbp_ref, o_ref):
    a_hi, a_lo = _split_bf16(pw_ref[...] * ps_ref[...])
    b_hi, b_lo = _split_bf16(wbp_ref[...])
    dot = functools.partial(jnp.dot, preferred_element_type=F32)
    o_ref[...] = (dot(a_hi, b_hi) + dot(a_hi, b_lo) + dot(a_lo, b_hi)).astype(o_ref.dtype)


def _pool_fold(pool_w, pool_scale, w_bp):
    g, c, _ = pool_w.shape
    d = w_bp.shape[1]
    return pl.pallas_call(
        _pool_fold_kernel,
        out_shape=jax.ShapeDtypeStruct((g * c, d), BF16),
        grid=(g,),
        in_specs=[pl.BlockSpec((None, c, c), lambda i: (i, 0, 0)),
                  pl.BlockSpec((None, 1, c), lambda i: (i, 0, 0)),
                  pl.BlockSpec((c, d), lambda i: (i, 0))],
        out_specs=pl.BlockSpec((c, d), lambda i: (i, 0)),
        compiler_params=_cparams(1),
        name="pool_fold",
    )(pool_w, pool_scale.reshape(g, 1, c), w_bp)


def _inproj_kernel(x_ref, mod_ref, g1_ref, win_ref, gq_ref, gk_ref, ones_ref, wpool_ref,
                   q_ref, k_ref, vt_ref, sga_ref, pc_ref, ext_ref, v_ref,
                   *, ts, tiles_per_batch):
    tb = pl.program_id(0) % tiles_per_batch
    x = x_ref[...]
    y = x * lax.rsqrt(jnp.mean(x * x, axis=-1, keepdims=True) + RMS_EPS)
    h = y * (g1_ref[...] * (1.0 + mod_ref[1:2, :])) + mod_ref[0:1, :]
    hb = h.astype(BF16)

    def proj(c0, c1):
        return jnp.dot(hb, win_ref[:, c0:c1], preferred_element_type=F32)

    ones_bd = ones_ref[...]
    q_ref[...] = _group_rms(proj(0, D_ATTN), ones_bd, gq_ref[...]).astype(BF16)
    k_ref[...] = _group_rms(proj(D_ATTN, 2 * D_ATTN), ones_bd, gk_ref[...]).astype(BF16)
    v_ref[...] = proj(2 * D_ATTN, 3 * D_ATTN)
    vt = v_ref[...].T.astype(BF16)
    for hd in range(N_HEADS):
        vt_ref[hd, 0:D_HEAD_V, :] = vt[hd * D_HEAD_V:(hd + 1) * D_HEAD_V, :]
        vt_ref[hd, D_HEAD_V:V_ROWS, :] = jnp.ones((V_ROWS - D_HEAD_V, ts), BF16)
    c_u = 3 * D_ATTN
    c_ga = c_u + D_POOL
    c_gp = c_ga + x.shape[1]
    sga_ref[...] = jax.nn.sigmoid(proj(c_ga, c_gp)).astype(BF16)

    u = proj(c_u, c_ga)

    @pl.when(tb == 0)
    def _():
        ext_ref[0:POOL_HALO, :] = jnp.zeros((POOL_HALO, D_POOL), F32)

    ext_ref[POOL_HALO:POOL_HALO + ts, :] = u
    row = lax.broadcasted_iota(jnp.int32, (ts, 1), 0) + tb * ts
    ys = []
    for g, w in enumerate(POOL_WINDOWS):
        c0 = g * POOL_GROUP_DIM
        ug = u[:, c0:c0 + POOL_GROUP_DIM]
        acc = ug
        for d in range(1, w):
            acc = acc + ext_ref[POOL_HALO - d:POOL_HALO - d + ts, c0:c0 + POOL_GROUP_DIM]
        cnt = jnp.minimum(row + 1, w).astype(F32)
        ys.append((acc / cnt - ug).astype(BF16))
    ypool = jnp.dot(jnp.concatenate(ys, axis=1), wpool_ref[...], preferred_element_type=F32)
    pc_ref[...] = (jax.nn.sigmoid(proj(c_gp, c_gp + x.shape[1])) * ypool).astype(BF16)
    ext_ref[0:POOL_HALO, :] = u[ts - POOL_HALO:ts, :]


def _inproj(x2, mod3, g1, w_in, gq, gk, pool_w, pool_scale, w_bp, *, bsz, seq, ts):
    n, d = x2.shape
    d_in = w_in.shape[1]
    tiles_per_batch = seq // ts
    win_b = w_in.astype(BF16)
    idx = jnp.arange(D_ATTN) // HEAD_DIM
    ones_bd = (idx[:, None] == idx[None, :]).astype(BF16)
    kern = functools.partial(_inproj_kernel, ts=ts, tiles_per_batch=tiles_per_batch)
    tok = lambda i: (i, 0)
    out_shape = (jax.ShapeDtypeStruct((n, D_ATTN), BF16),
                 jax.ShapeDtypeStruct((n, D_ATTN), BF16),
                 jax.ShapeDtypeStruct((bsz, N_HEADS, V_ROWS, seq), BF16),
                 jax.ShapeDtypeStruct((n, d), BF16),
                 jax.ShapeDtypeStruct((n, d), BF16))
    return pl.pallas_call(
        kern,
        out_shape=out_shape,
        grid=(n // ts,),
        in_specs=[pl.BlockSpec((ts, d), tok),
                  pl.BlockSpec((None, 6, d), lambda i: (i // tiles_per_batch, 0, 0)),
                  _const_spec((1, d)),
                  _const_spec((d, d_in)),
                  _const_spec((1, D_ATTN)),
                  _const_spec((1, D_ATTN)),
                  _const_spec((D_ATTN, D_ATTN)),
                  _const_spec((D_POOL, d))],
        out_specs=(pl.BlockSpec((ts, D_ATTN), tok),
                   pl.BlockSpec((ts, D_ATTN), tok),
                   pl.BlockSpec((None, N_HEADS, V_ROWS, ts),
                                lambda i: (i // tiles_per_batch, 0, 0, i % tiles_per_batch)),
                   pl.BlockSpec((ts, d), tok),
                   pl.BlockSpec((ts, d), tok)),
        scratch_shapes=[pltpu.VMEM((POOL_HALO + ts, D_POOL), F32), pltpu.VMEM((ts, D_ATTN), F32)],
        compiler_params=_cparams(1),
        name="inproj",
    )(x2, mod3, g1.reshape(1, d), win_b, gq.reshape(1, D_ATTN), gk.reshape(1, D_ATTN), ones_bd,
      _pool_fold(pool_w, pool_scale, w_bp))


def _attn_kernel(q_ref, qn_ref, k_ref, vt_ref, bias_ref, lq1_ref, lk1_ref, lq2_ref, lk2_ref, subg_ref,
                 *rest, t, lambda_init, cast_groups):
    n_in, n_out = sum(cast_groups), len(cast_groups)
    cast_in = rest[:n_in]
    o_ref, clear_ref = rest[n_in:n_in + 2]
    cast_out = rest[n_in + 2:n_in + 2 + n_out]
    s_ref, mt_ref, m_ref, acc_ref = rest[n_in + 2 + n_out:]
    i = pl.program_id(2)
    last = pl.num_programs(2) - 1
    clear_ref[...] = jnp.zeros(clear_ref.shape, clear_ref.dtype)
    first = 0
    for size, dst in zip(cast_groups, cast_out):
        col = 0
        for src in cast_in[first:first + size]:
            dst[:, col:col + src.shape[1]] = src[...].astype(dst.dtype)
            col += src.shape[1]
        first += size

    def split_maps(q):
        lane = lax.broadcasted_iota(jnp.int32, q.shape, 1)
        zero = jnp.zeros_like(q)
        return jnp.where(lane < HEAD_DIM, q, zero), jnp.where(lane >= HEAD_DIM, q, zero)

    q_now = split_maps(q_ref[...])

    m_ref[...] = jnp.full(m_ref.shape, MASKED, F32)
    acc_ref[...] = jnp.zeros(acc_ref.shape, F32)

    def scores(j, bias, slot, qm=q_now):
        kt = k_ref[pl.ds(pl.multiple_of(j * t, t), t), :]
        for mp in range(2):
            s = lax.dot_general(kt, qm[mp], (((1,), (1,)), ((), ())), preferred_element_type=F32)
            if bias is not None:
                s = s + bias
            s_ref[slot, mp] = s
            mt_ref[slot, mp] = jnp.max(s, axis=0, keepdims=True)

    def accumulate(j, slot):
        vt = vt_ref[:, pl.ds(pl.multiple_of(j * t, t), t)]
        for mp in range(2):
            m_old = m_ref[mp]
            m_new = jnp.maximum(m_old, mt_ref[slot, mp])
            p = jnp.exp2(s_ref[slot, mp] - m_new).astype(BF16)
            acc_ref[mp] = (jnp.exp2(m_old - m_new) * acc_ref[mp]
                           + jnp.dot(vt, p, preferred_element_type=F32))
            m_ref[mp] = m_new

    def next_diagonal():
        nxt = jnp.minimum(i + 1, last)
        scores(nxt, bias_ref[1], 2, split_maps(qn_ref[...]))

    @pl.when(i == 0)
    def _():
        scores(0, bias_ref[1], 0)

    @pl.when(i >= 1)
    def _():
        scores(i - 1, bias_ref[0], 1)
        accumulate(i, 2)

    n_far = jnp.maximum(i - 1, 0)

    def pair(kk):
        j = i - 1 - 2 * kk
        scores(j - 1, None, 0)
        accumulate(j, 1)
        scores(j - 2, None, 1)
        accumulate(j - 1, 0)

    def oct_body(oo, carry):
        for u in range(4):
            pair(4 * oo + u)
        return carry

    n_pairs = n_far // 2
    lax.fori_loop(0, n_pairs // 4, oct_body, 0)

    @pl.when(n_pairs % 4 >= 2)
    def _():
        pair(n_pairs // 4 * 4)
        pair(n_pairs // 4 * 4 + 1)

    @pl.when(n_pairs % 2 == 1)
    def _():
        pair(n_pairs - 1)

    def finish(slot):
        next_diagonal()
        accumulate(0, slot)
        lam = (jnp.exp(jnp.sum(lq1_ref[...] * lk1_ref[...], axis=1, keepdims=True))
               - jnp.exp(jnp.sum(lq2_ref[...] * lk2_ref[...], axis=1, keepdims=True)) + lambda_init)
        o1 = acc_ref[0, 0:D_HEAD_V, :] / acc_ref[0, D_HEAD_V:D_HEAD_V + 1, :]
        o2 = acc_ref[1, 0:D_HEAD_V, :] / acc_ref[1, D_HEAD_V:D_HEAD_V + 1, :]
        ot = o1 - lam * o2
        ot = ot * lax.rsqrt(jnp.mean(ot * ot, axis=0, keepdims=True) + RMS_EPS)
        ot = ot * subg_ref[...] * (1.0 - lambda_init)
        o_ref[...] = ot.T.astype(BF16)

    @pl.when(n_far % 2 == 1)
    def _():
        scores(0, None, 0)
        accumulate(1, 1)
        finish(0)

    @pl.when(jnp.logical_and(i >= 1, n_far % 2 == 0))
    def _():
        finish(1)

    @pl.when(i == 0)
    def _():
        finish(0)


def _attention(q, k, vt, bias, lq1, lk1, lq2, lk2, subln_g, *, t, lambda_init, clear_shape, to_bf16):
    bsz, seq, _ = q.shape
    nq = seq // t
    assert t + 1 >= LOG_BUCKET_STARTS[-1], "keys two tiles back must all fall in the last distance bucket"
    steps = bsz * N_HEADS * nq
    step_id = lambda b, h, i: ((b * N_HEADS + h) * nq + i, 0)
    clear_rows = clear_shape[0] // steps
    assert clear_rows * steps == clear_shape[0] and clear_rows % SUBLANES == 0, (clear_shape, steps)
    cast_in_specs, cast_out_specs, cast_out_shapes = [], [], []
    for group in to_bf16:
        rows = group[0].shape[0] // steps
        assert rows * steps == group[0].shape[0] and rows % (2 * SUBLANES) == 0, (group[0].shape, steps)
        cols = sum(a.shape[1] for a in group)
        cast_in_specs += [pl.BlockSpec((rows, a.shape[1]), step_id) for a in group]
        cast_out_specs.append(pl.BlockSpec((rows, cols), step_id))
        cast_out_shapes.append(jax.ShapeDtypeStruct((group[0].shape[0], cols), BF16))
    kern = functools.partial(_attn_kernel, t=t, lambda_init=lambda_init,
                             cast_groups=tuple(len(g) for g in to_bf16))
    vec = lambda a: a.reshape(1, HEAD_DIM)
    return pl.pallas_call(
        kern,
        out_shape=(jax.ShapeDtypeStruct((bsz, seq, D_ATTN), BF16),
                   jax.ShapeDtypeStruct(clear_shape, F32),
                   *cast_out_shapes),
        grid=(bsz, N_HEADS, seq // t),
        in_specs=[pl.BlockSpec((None, t, D_HEAD_V), lambda b, h, i: (b, i, h)),
                  pl.BlockSpec((None, t, D_HEAD_V), lambda b, h, i: (b, jnp.minimum(i + 1, nq - 1), h)),
                  pl.BlockSpec((None, seq, D_HEAD_V), lambda b, h, i: (b, 0, h)),
                  pl.BlockSpec((None, None, V_ROWS, seq), lambda b, h, i: (b, h, 0, 0)),
                  pl.BlockSpec((None, 2, t, t), lambda b, h, i: (h, 0, 0, 0)),
                  _const_spec((1, HEAD_DIM)), _const_spec((1, HEAD_DIM)),
                  _const_spec((1, HEAD_DIM)), _const_spec((1, HEAD_DIM)),
                  _const_spec((D_HEAD_V, 1)), *cast_in_specs],
        out_specs=(pl.BlockSpec((None, t, D_HEAD_V), lambda b, h, i: (b, i, h)),
                   pl.BlockSpec((clear_rows, clear_shape[1]), step_id), *cast_out_specs),
        scratch_shapes=[pltpu.VMEM((3, 2, t, t), F32),
                        pltpu.VMEM((3, 2, 1, t), F32),
                        pltpu.VMEM((2, 1, t), F32),
                        pltpu.VMEM((2, V_ROWS, t), F32)],
        compiler_params=_cparams(3),
        name="attn",
    )(q, q, k, vt, bias, vec(lq1), vec(lk1), vec(lq2), vec(lk2), subln_g.reshape(D_HEAD_V, 1),
      *[a for group in to_bf16 for a in group])


def _first_max(rows):
    best = rows[0]
    for r in rows[1:]:
        best = jnp.maximum(best, r)
    idx = jnp.full(best.shape, len(rows) - 1, jnp.int32)
    for j in range(len(rows) - 2, -1, -1):
        idx = jnp.where(rows[j] == best, j, idx)
    return best, idx


def _post_kernel(o_ref, sga_ref, pc_ref, x_ref, mod_ref, g2_ref, wba_ref, wout_ref, wr_ref, br_ref,
                 tri_ref, x1_ref, h2p_ref, route_ref, cnt_ref, carry_ref):
    @pl.when(pl.program_id(0) == 0)
    def _():
        carry_ref[...] = jnp.zeros(carry_ref.shape, F32)

    ya = jnp.dot(o_ref[...], wba_ref[...], preferred_element_type=F32)
    merged = sga_ref[...].astype(F32) * ya + pc_ref[...].astype(F32)
    z = jnp.dot(merged.astype(BF16), wout_ref[...], preferred_element_type=F32)
    x1 = x_ref[...] + mod_ref[2:3, :] * z
    x1_ref[...] = x1
    y = x1 * lax.rsqrt(jnp.mean(x1 * x1, axis=-1, keepdims=True) + RMS_EPS)
    h2 = y * (g2_ref[...] * (1.0 + mod_ref[4:5, :])) + mod_ref[3:4, :]
    hi = h2.astype(BF16)
    lo = (h2 - hi.astype(F32)).astype(BF16)

    nt = (((1,), (1,)), ((), ()))
    a = lax.dot_general(wr_ref[...], hi, nt, preferred_element_type=F32)
    b = lax.dot_general(wr_ref[0:ROUTER_ROWS, :], lo, nt, preferred_element_type=F32)
    logits = a[0:ROUTER_ROWS] + a[ROUTER_ROWS:2 * ROUTER_ROWS] + b + br_ref[...]

    gl = [logits[g:g + 1, :] for g in range(N_EXPERT_GROUPS)]
    gmax, gidx = _first_max(gl)
    gsum = gl[0] * 0.0
    for r in gl:
        gsum = gsum + jnp.exp(r - gmax)
    g_val = 1.0 / gsum
    es = []
    for r in range(EXPERTS_PER_GROUP):
        sel = jnp.zeros_like(gmax)
        for g in range(N_EXPERT_GROUPS):
            row = EXPERT_ROW0 + g * EXPERTS_PER_GROUP + r
            sel = jnp.where(gidx == g, logits[row:row + 1, :], sel)
        es.append(sel)
    e1, i1 = _first_max(es)
    rest = [jnp.where(i1 == r, -jnp.inf, es[r]) for r in range(EXPERTS_PER_GROUP)]
    e2, i2 = _first_max(rest)
    r21 = jnp.exp(e2 - e1)
    w1 = g_val / (1.0 + r21)
    w2 = g_val * r21 / (1.0 + r21)

    first = i1 < i2
    e_lo = jnp.where(first, i1, i2)
    e_hi = jnp.where(first, i2, i1)
    pair = jnp.zeros_like(e_lo)
    for p in range(1, PAIRS_PER_GROUP):
        pair = jnp.where(jnp.logical_and(e_lo == PAIR_LO[p], e_hi == PAIR_HI[p]), p, pair)
    bucket = gidx * PAIRS_PER_GROUP + pair
    w_lo = jnp.where(first, w1, w2)
    w_hi = jnp.where(first, w2, w1)

    ts = bucket.shape[1]
    brow = lax.broadcasted_iota(jnp.int32, (ROUTER_ROWS, ts), 0)
    onehot = brow == bucket
    prefix = jnp.dot(jnp.where(onehot, 1.0, 0.0).astype(BF16), tri_ref[...],
                     preferred_element_type=F32)
    carry = carry_ref[...]
    rank = jnp.sum(jnp.where(onehot, prefix + carry, 0.0), axis=0, keepdims=True) - 1.0
    carry = carry + prefix[:, ts - 1:ts]
    carry_ref[...] = carry
    cnt_ref[...] = jnp.broadcast_to(carry, cnt_ref.shape).astype(jnp.int32)
    chunks = ts // LANES
    rank_i = rank.astype(jnp.int32)
    for j in range(chunks):
        route_ref[j:j + 1, :] = bucket[:, j * LANES:(j + 1) * LANES]
        route_ref[chunks + j:chunks + j + 1, :] = rank_i[:, j * LANES:(j + 1) * LANES]

    arow = lax.broadcasted_iota(jnp.int32, (LANES, ts), 0)
    aux_t = jnp.where(arow == 0, w_lo, jnp.where(arow == 1, w_hi, 0.0))
    d = h2.shape[1]
    h2p_ref[:, 0:d] = h2
    h2p_ref[:, d:d + LANES] = aux_t.T


def _post(o2, sga, pc, x2, mod3, g2, w_ba, w_out, wg_r, bg_r, we_r, be_r, *, seq, ts):
    n, d = x2.shape
    tiles_per_batch = seq // ts
    steps = n // ts
    wr = jnp.zeros((ROUTER_ROWS, d), F32)
    wr = wr.at[0:N_EXPERT_GROUPS].set(wg_r.T).at[EXPERT_ROW0:EXPERT_ROW0 + N_EXPERTS].set(we_r.T)
    wr_hi = wr.astype(BF16)
    wr_lo = (wr - wr_hi.astype(F32)).astype(BF16)
    br = jnp.zeros((ROUTER_ROWS, 1), F32)
    br = br.at[0:N_EXPERT_GROUPS, 0].set(bg_r).at[EXPERT_ROW0:EXPERT_ROW0 + N_EXPERTS, 0].set(be_r)
    tok = lambda i: (i, 0)
    tidx = jnp.arange(ts)
    tri = (tidx[:, None] <= tidx[None, :]).astype(BF16)
    return pl.pallas_call(
        _post_kernel,
        out_shape=(jax.ShapeDtypeStruct((n, d), F32),
                   jax.ShapeDtypeStruct((n, d + LANES), F32),
                   jax.ShapeDtypeStruct((steps, 2 * (ts // LANES), LANES), jnp.int32),
                   jax.ShapeDtypeStruct((ROUTER_ROWS, LANES), jnp.int32)),
        grid=(steps,),
        in_specs=[pl.BlockSpec((ts, D_ATTN), tok),
                  pl.BlockSpec((ts, d), tok),
                  pl.BlockSpec((ts, d), tok),
                  pl.BlockSpec((ts, d), tok),
                  pl.BlockSpec((None, 6, d), lambda i: (i // tiles_per_batch, 0, 0)),
                  _const_spec((1, d)),
                  _const_spec((D_ATTN, d)),
                  _const_spec((d, d)),
                  _const_spec((2 * ROUTER_ROWS, d)),
                  _const_spec((ROUTER_ROWS, 1)),
                  _const_spec((ts, ts))],
        out_specs=(pl.BlockSpec((ts, d), tok), pl.BlockSpec((ts, d + LANES), tok),
                   pl.BlockSpec((None, 2 * (ts // LANES), LANES), lambda i: (i, 0, 0)),
                   pl.BlockSpec((ROUTER_ROWS, LANES), lambda i: (0, 0))),
        scratch_shapes=[pltpu.VMEM((ROUTER_ROWS, 1), F32)],
        compiler_params=_cparams(1),
        name="post",
    )(o2, sga, pc, x2, mod3, g2.reshape(1, d), w_ba.astype(BF16), w_out.astype(BF16),
      jnp.concatenate([wr_hi, wr_lo], axis=0), br, tri)


def _group_sublane(row):
    return lax.shift_right_logical(row, SUBLANES.bit_length() - 1), row & (SUBLANES - 1)


def _row_copy(src, src_row, dst, dst_row, sem):
    sg, ss = src_row
    dg, ds = dst_row
    return pltpu.make_async_copy(src.at[sg, pl.ds(ss, 1), :], dst.at[dg, pl.ds(ds, 1), :], sem)


def _dispatch_kernel(pos_ref, h_ref, init_ref, hs_ref, sem, *, ts):
    del init_ref
    base = pl.program_id(0) * ts

    def start(g, carry):
        for u in range(SUBLANES):
            p = pos_ref[base + g * SUBLANES + u]
            _row_copy(h_ref, (g, u), hs_ref, _group_sublane(p), sem).start(priority=u % 2)
        return carry

    def wait(g, carry):
        for u in range(SUBLANES):
            _row_copy(h_ref, (0, 0), hs_ref, (0, 0), sem).wait()
        return carry

    lax.fori_loop(0, ts // SUBLANES, start, 0)
    lax.fori_loop(0, ts // SUBLANES, wait, 0)


def _dispatch(pos, h2p, cleared, *, ts):
    n, w = h2p.shape
    n_rows = cleared.shape[0]
    return pl.pallas_call(
        functools.partial(_dispatch_kernel, ts=ts),
        out_shape=jax.ShapeDtypeStruct((n_rows // SUBLANES, SUBLANES, w), h2p.dtype),
        grid_spec=pltpu.PrefetchScalarGridSpec(
            num_scalar_prefetch=1,
            grid=(n // ts,),
            in_specs=[pl.BlockSpec((ts // SUBLANES, SUBLANES, w), lambda i, pos: (i, 0, 0)),
                      pl.BlockSpec(memory_space=pl.ANY)],
            out_specs=pl.BlockSpec(memory_space=pl.ANY),
            scratch_shapes=[pltpu.SemaphoreType.DMA]),
        input_output_aliases={2: 0},
        compiler_params=_cparams(1),
        name="dispatch",
    )(pos, h2p.reshape(n // SUBLANES, SUBLANES, w),
      cleared.reshape(n_rows // SUBLANES, SUBLANES, w)).reshape(n_rows, w)


def _experts_kernel(ea_ref, eb_ref, valid_ref, hs_ref, wgua_ref, wda_ref, wgub_ref, wdb_ref, ys_ref):
    del ea_ref, eb_ref
    rows = valid_ref[pl.program_id(0)]
    tm = hs_ref.shape[0]
    d = hs_ref.shape[1] - LANES

    def run(m):
        h = hs_ref[0:m, 0:d].astype(BF16)
        aux = hs_ref[0:m, d:d + LANES]

        def expert(wgu_ref, wd_ref):
            gu = jnp.dot(h, wgu_ref[...], preferred_element_type=F32)
            de = gu.shape[1] // 2
            a, b = gu[:, :de], gu[:, de:]
            hid = (a * jax.nn.sigmoid(a)) * b
            return jnp.dot(hid.astype(BF16), wd_ref[...], preferred_element_type=F32)

        ys_ref[0:m, :] = (aux[:, 0:1] * expert(wgua_ref, wda_ref)
                          + aux[:, 1:2] * expert(wgub_ref, wdb_ref))
        if m < tm:
            ys_ref[m:tm, :] = jnp.zeros((tm - m, ys_ref.shape[1]), ys_ref.dtype)

    @pl.when(rows > tm // 2)
    def _():
        run(tm)

    @pl.when(jnp.logical_and(rows > 0, rows <= tm // 2))
    def _():
        run(tm // 2)

    @pl.when(rows == 0)
    def _():
        ys_ref[...] = jnp.zeros(ys_ref.shape, ys_ref.dtype)


def _experts(tile_ea, tile_eb, tile_valid, hs, w_gate_up, w_down, *, tm):
    n_rows, w = hs.shape
    _, de, d = w_down.shape
    ea = lambda t, ea_r, eb_r, v_r: (ea_r[t], 0, 0)
    eb = lambda t, ea_r, eb_r, v_r: (eb_r[t], 0, 0)
    row = lambda t, ea_r, eb_r, v_r: (t, 0)
    return pl.pallas_call(
        _experts_kernel,
        out_shape=jax.ShapeDtypeStruct((n_rows, d), F32),
        grid_spec=pltpu.PrefetchScalarGridSpec(
            num_scalar_prefetch=3,
            grid=(n_rows // tm,),
            in_specs=[pl.BlockSpec((tm, w), row),
                      pl.BlockSpec((None, d, 2 * de), ea), pl.BlockSpec((None, de, d), ea),
                      pl.BlockSpec((None, d, 2 * de), eb), pl.BlockSpec((None, de, d), eb)],
            out_specs=pl.BlockSpec((tm, d), row)),
        compiler_params=_cparams(1),
        name="experts",
    )(tile_ea, tile_eb, tile_valid, hs, w_gate_up, w_down, w_gate_up, w_down)


def _combine_kernel(pos_ref, ys_ref, x1_ref, mod_ref, out_ref, ybuf_ref, sem, *, ts):
    i = pl.program_id(0)
    groups = ts // SUBLANES

    def gather(tile, slot):
        base = tile * ts

        def start(g, carry):
            for u in range(SUBLANES):
                p = pos_ref[base + g * SUBLANES + u]
                _row_copy(ys_ref, _group_sublane(p), ybuf_ref.at[slot], (g, u),
                          sem.at[slot]).start(priority=u % 2)
            return carry

        lax.fori_loop(0, groups, start, 0)

    def wait_all(slot):
        def wait(g, carry):
            for u in range(SUBLANES):
                _row_copy(ys_ref, (0, 0), ybuf_ref.at[slot], (0, 0), sem.at[slot]).wait()
            return carry

        lax.fori_loop(0, groups, wait, 0)

    def step(slot):
        @pl.when(i + 1 < pl.num_programs(0))
        def _():
            gather(i + 1, 1 - slot)

        wait_all(slot)
        y = ybuf_ref[slot].reshape(ts, ybuf_ref.shape[-1])
        out_ref[...] = x1_ref[...] + mod_ref[5:6, :] * y

    @pl.when(i == 0)
    def _():
        gather(0, 0)

    @pl.when(i % 2 == 0)
    def _():
        step(0)

    @pl.when(i % 2 == 1)
    def _():
        step(1)


def _combine(pos, ys, x1, mod3, *, seq, ts):
    n, d = x1.shape
    tiles_per_batch = seq // ts
    return pl.pallas_call(
        functools.partial(_combine_kernel, ts=ts),
        out_shape=jax.ShapeDtypeStruct((n, d), F32),
        grid_spec=pltpu.PrefetchScalarGridSpec(
            num_scalar_prefetch=1,
            grid=(n // ts,),
            in_specs=[pl.BlockSpec(memory_space=pl.ANY),
                      pl.BlockSpec((ts, d), lambda i, pos: (i, 0)),
                      pl.BlockSpec((None, 6, d), lambda i, pos: (i // tiles_per_batch, 0, 0))],
            out_specs=pl.BlockSpec((ts, d), lambda i, pos: (i, 0)),
            scratch_shapes=[pltpu.VMEM((2, ts // SUBLANES, SUBLANES, d), F32),
                            pltpu.SemaphoreType.DMA((2,))]),
        compiler_params=_cparams(1),
        name="combine",
    )(pos, ys.reshape(ys.shape[0] // SUBLANES, SUBLANES, d), x1, mod3)


def _pos_kernel(start_ref, route_ref, pos_ref):
    chunks = pos_ref.shape[1]
    bucket = route_ref[:, 0:chunks, :]
    pos = route_ref[:, chunks:2 * chunks, :]
    for b in range(N_ROUTE_BUCKETS):
        pos = pos + jnp.where(bucket == b, start_ref[b], 0)
    pos_ref[...] = pos


def _route_tables(route, counts, n_tiles, tm):
    steps, rows, _ = route.shape
    cnt = counts[:N_ROUTE_BUCKETS, 0]
    padded = (cnt + tm - 1) // tm * tm
    end = jnp.cumsum(padded)
    pos = pl.pallas_call(
        _pos_kernel,
        out_shape=jax.ShapeDtypeStruct((steps, rows // 2, LANES), jnp.int32),
        in_specs=[pl.BlockSpec(memory_space=pltpu.SMEM), pl.BlockSpec(memory_space=pltpu.VMEM)],
        out_specs=pl.BlockSpec(memory_space=pltpu.VMEM),
        name="pos",
    )(end - padded, route).reshape(-1)
    tiles_used = end[-1] // tm
    tile = jnp.arange(n_tiles, dtype=jnp.int32)
    valid = tile < tiles_used
    first_row = jnp.minimum(tile, tiles_used - 1) * tm
    tile_bucket = jnp.sum((end[None, :] <= first_row[:, None]).astype(jnp.int32), axis=1)
    tile_bucket = jnp.minimum(tile_bucket, N_ROUTE_BUCKETS - 1)
    group = tile_bucket // PAIRS_PER_GROUP
    pair = tile_bucket % PAIRS_PER_GROUP
    tile_ea = group * EXPERTS_PER_GROUP + jnp.asarray(PAIR_LO, jnp.int32)[pair]
    tile_eb = group * EXPERTS_PER_GROUP + jnp.asarray(PAIR_HI, jnp.int32)[pair]
    real_end = (end - padded + cnt)[tile_bucket]
    tile_rows = jnp.where(valid, jnp.clip(real_end - tile * tm, 0, tm), 0)
    return pos.astype(jnp.int32), tile_ea, tile_eb, tile_rows.astype(jnp.int32)


def _tile(seq, pref):
    t = min(pref, seq)
    assert seq % t == 0 and t % LANES == 0, (seq, t)
    return t


def kernel(x, c, rel_bias, ada_w, ada_b, norm1_g, w_in, q_norm_g, k_norm_g, lambda_q1, lambda_k1,
           lambda_q2, lambda_k2, subln_g, w_branch_attn, pool_w, pool_scale, w_branch_pool, w_out,
           norm2_g, router_group_w, router_group_b, router_expert_w, router_expert_b,
           expert_w_gate, expert_w_up, expert_w_down):
    bsz, seq, d = x.shape
    n = bsz * seq
    ts = _tile(seq, TOKEN_TILE)
    t_attn = _tile(seq, ATTN_TILE)
    tr = _tile(seq, ROW_COPY_TILE)
    tm = MOE_TILE
    n_tiles = -(-(n + N_ROUTE_BUCKETS * (tm - 1)) // tm)
    attn_steps = bsz * N_HEADS * (seq // t_attn)
    while (n_tiles * tm) % (attn_steps * SUBLANES):
        n_tiles += 1
    bias = _bias_tiles(rel_bias, t_attn)
    x2 = x.reshape(n, d)
    for l in range(ada_w.shape[0]):
        lambda_init = 0.8 - 0.6 * math.exp(-0.3 * l)
        mod3 = _ada(c, ada_w[l], ada_b[l]).reshape(bsz, 6, d)
        gq = jnp.tile(q_norm_g[l], D_ATTN // HEAD_DIM) * (HEAD_DIM ** -0.5 * LOG2E)
        gk = jnp.tile(k_norm_g[l], D_ATTN // HEAD_DIM)
        q, k, vt, sga, pc = _inproj(x2, mod3, norm1_g[l], w_in[l], gq, gk, pool_w[l], pool_scale[l],
                                    w_branch_pool[l], bsz=bsz, seq=seq, ts=_tile(seq, INPROJ_TILE))
        flat = lambda w: w.reshape(-1, w.shape[-1])
        o, cleared, wgu_b, wd_b = _attention(
            q.reshape(bsz, seq, D_ATTN), k.reshape(bsz, seq, D_ATTN), vt, bias,
            lambda_q1[l], lambda_k1[l], lambda_q2[l], lambda_k2[l], subln_g[l],
            t=t_attn, lambda_init=lambda_init, clear_shape=(n_tiles * tm, d + LANES),
            to_bf16=[[flat(expert_w_gate[l]), flat(expert_w_up[l])], [flat(expert_w_down[l])]])
        n_exp, _, de = expert_w_gate[l].shape
        wgu_b = wgu_b.reshape(n_exp, d, 2 * de)
        wd_b = wd_b.reshape(n_exp, de, d)
        x1, h2p, route, counts = _post(
            o.reshape(n, D_ATTN), sga, pc, x2, mod3, norm2_g[l], w_branch_attn[l], w_out[l],
            router_group_w[l], router_group_b[l], router_expert_w[l], router_expert_b[l],
            seq=seq, ts=ts)
        pos, tile_ea, tile_eb, tile_valid = _route_tables(route, counts, n_tiles, tm)
        hs = _dispatch(pos, h2p, cleared, ts=tr)
        ys = _experts(tile_ea, tile_eb, tile_valid, hs, wgu_b, wd_b, tm=tm)
        x2 = _combine(pos, ys, x1, mod3, seq=seq, ts=tr)
    return x2.reshape(bsz, seq, d)
```

```python
import functools
import math

import jax
import jax.numpy as jnp
from jax import lax
from jax.experimental import pallas as pl
from jax.experimental.pallas import tpu as pltpu

F32 = jnp.float32
BF16 = jnp.bfloat16

CHUNK = 64
N_HEADS = 4
HEAD_DIM = 64
D_HEAD_V = 2 * HEAD_DIM
V_ROWS = D_HEAD_V + 16
D_ATTN = N_HEADS * D_HEAD_V
POOL_WINDOWS = (2, 4, 8, 16)
POOL_GROUP_DIM = 128
D_POOL = len(POOL_WINDOWS) * POOL_GROUP_DIM
POOL_HALO = 16
N_BUCKETS = 32
MAX_DISTANCE = 128
N_EXPERT_GROUPS = 4
EXPERTS_PER_GROUP = 4
N_EXPERTS = N_EXPERT_GROUPS * EXPERTS_PER_GROUP
PAIRS_PER_GROUP = 6
PAIR_LO = (0, 0, 1, 1, 0, 2)
PAIR_HI = (1, 2, 2, 3, 3, 3)
N_ROUTE_BUCKETS = N_EXPERT_GROUPS * PAIRS_PER_GROUP
RMS_EPS = 1e-6
LOG2E = math.log2(math.e)
MASKED = -1e30

LANES = 128
SUBLANES = 8
ROUTER_ROWS = 32
EXPERT_ROW0 = 8

VMEM_LIMIT = 56 * 1024 * 1024
TOKEN_TILE = 1024
INPROJ_TILE = 1024
ATTN_TILE = 512
MOE_TILE = 512
ROW_COPY_TILE = 1024


def _cparams(n_axes):
    return pltpu.CompilerParams(dimension_semantics=("arbitrary",) * n_axes,
                                vmem_limit_bytes=VMEM_LIMIT)


def _const_spec(shape):
    nd = len(shape)
    return pl.BlockSpec(shape, lambda *_: (0,) * nd, pipeline_mode=pl.Buffered(1))


def _ada_kernel(ct_ref, w_ref, b_ref, o_ref):
    @pl.when(pl.program_id(0) == 0)
    def _():
        o_ref[...] = jnp.broadcast_to(b_ref[...], o_ref.shape)

    ct = ct_ref[...]
    s = ct * jax.nn.sigmoid(ct)
    w = w_ref[...]
    rows = [jnp.sum(w * s[:, b:b + 1], axis=0, keepdims=True) for b in range(ct.shape[1])]
    o_ref[...] += jnp.concatenate(rows, axis=0)


def _ada(c, w, b):
    bsz, d = c.shape
    n = w.shape[1]
    rows = 256
    return pl.pallas_call(
        _ada_kernel,
        out_shape=jax.ShapeDtypeStruct((bsz, n), F32),
        grid=(d // rows,),
        in_specs=[pl.BlockSpec((rows, bsz), lambda j: (j, 0)),
                  pl.BlockSpec((rows, n), lambda j: (j, 0)),
                  pl.BlockSpec((1, n), lambda j: (0, 0))],
        out_specs=pl.BlockSpec((bsz, n), lambda j: (0, 0)),
        compiler_params=_cparams(1),
        name="ada",
    )(c.T, w, b.reshape(1, n))


def _log_bucket_starts():
    nb = N_BUCKETS // 2
    max_exact = nb // 2
    m = nb - max_exact
    ratio = MAX_DISTANCE // max_exact
    starts = []
    for k in range(1, m):
        n = max_exact
        while n ** m < max_exact ** m * ratio ** k:
            n += 1
        starts.append(n)
    return tuple(starts)


LOG_BUCKET_STARTS = _log_bucket_starts()


def _bias_kernel(rb_ref, o_ref, *, t):
    h = pl.program_id(0)
    kind = pl.program_id(1)
    nb = N_BUCKETS // 2
    max_exact = nb // 2
    kpos = lax.broadcasted_iota(jnp.int32, (t, t), 0)
    qpos = lax.broadcasted_iota(jnp.int32, (t, t), 1)
    rel = kpos - qpos - jnp.where(kind == 0, t, 0)
    n = jnp.abs(rel)

    def table(first):
        val = jnp.full((t, t), rb_ref[first + nb - 1, h], F32)
        for k in range(len(LOG_BUCKET_STARTS) - 1, -1, -1):
            val = jnp.where(n < LOG_BUCKET_STARTS[k], rb_ref[first + max_exact + k, h], val)
        for j in range(max_exact - 1, -1, -1):
            val = jnp.where(n == j, rb_ref[first + j, h], val)
        return val

    far = rb_ref[nb - 1, h]

    @pl.when(kind == 0)
    def _():
        o_ref[...] = (table(0) - far) * LOG2E

    @pl.when(kind == 1)
    def _():
        bias = jnp.where(rel > 0, table(nb), table(0))
        shift = CHUNK.bit_length() - 1
        hidden = (kpos >> shift) > (qpos >> shift)
        o_ref[...] = jnp.where(hidden, MASKED, (bias - far) * LOG2E)


def _bias_tiles(rel_bias, t):
    return pl.pallas_call(
        functools.partial(_bias_kernel, t=t),
        out_shape=jax.ShapeDtypeStruct((N_HEADS, 2, t, t), F32),
        grid=(N_HEADS, 2),
        in_specs=[pl.BlockSpec(memory_space=pltpu.SMEM)],
        out_specs=pl.BlockSpec((None, None, t, t), lambda h, j: (h, j, 0, 0)),
        compiler_params=_cparams(2),
        name="bias_tiles",
    )(rel_bias)


def _group_rms(xc, ones_blockdiag, gain):
    ssq = jnp.dot((xc * xc).astype(BF16), ones_blockdiag, preferred_element_type=F32)
    return xc * lax.rsqrt(ssq * (1.0 / HEAD_DIM) + RMS_EPS) * gain


def _split_bf16(a):
    hi = a.astype(BF16)
    return hi, (a - hi.astype(F32)).astype(BF16)


def _pool_fold_kernel(pw_ref, ps_ref, wbp_ref, o_ref):
    a_hi, a_lo = _split_bf16(pw_ref[...] * ps_ref[...])
    b_hi, b_lo = _split_bf16(wbp_ref[...])
    dot = functools.partial(jnp.dot, preferred_element_type=F32)
    o_ref[...] = (dot(a_hi, b_hi) + dot(a_hi, b_lo) + dot(a_lo, b_hi)).astype(o_ref.dtype)


def _pool_fold(pool_w, pool_scale, w_bp):
    g, c, _ = pool_w.shape
    d = w_bp.shape[1]
    return pl.pallas_call(
        _pool_fold_kernel,
        out_shape=jax.ShapeDtypeStruct((g * c, d), BF16),
        grid=(g,),
        in_specs=[pl.BlockSpec((None, c, c), lambda i: (i, 0, 0)),
                  pl.BlockSpec((None, 1, c), lambda i: (i, 0, 0)),
                  pl.BlockSpec((c, d), lambda i: (i, 0))],
        out_specs=pl.BlockSpec((c, d), lambda i: (i, 0)),
        compiler_params=_cparams(1),
        name="pool_fold",
    )(pool_w, pool_scale.reshape(g, 1, c), w_bp)


def _inproj_kernel(x_ref, mod_ref, g1_ref, win_ref, gq_ref, gk_ref, ones_ref, wpool_ref,
                   q_ref, k_ref, vt_ref, sga_ref, pc_ref, ext_ref, v_ref,
                   *, ts, tiles_per_batch):
    tb = pl.program_id(0) % tiles_per_batch
    x = x_ref[...]
    y = x * lax.rsqrt(jnp.mean(x * x, axis=-1, keepdims=True) + RMS_EPS)
    h = y * (g1_ref[...] * (1.0 + mod_ref[1:2, :])) + mod_ref[0:1, :]
    hb = h.astype(BF16)

    def proj(c0, c1):
        return jnp.dot(hb, win_ref[:, c0:c1], preferred_element_type=F32)

    ones_bd = ones_ref[...]
    q_ref[...] = _group_rms(proj(0, D_ATTN), ones_bd, gq_ref[...]).astype(BF16)
    k_ref[...] = _group_rms(proj(D_ATTN, 2 * D_ATTN), ones_bd, gk_ref[...]).astype(BF16)
    v_ref[...] = proj(2 * D_ATTN, 3 * D_ATTN)
    vt = v_ref[...].T.astype(BF16)
    for hd in range(N_HEADS):
        vt_ref[hd, 0:D_HEAD_V, :] = vt[hd * D_HEAD_V:(hd + 1) * D_HEAD_V, :]
        vt_ref[hd, D_HEAD_V:V_ROWS, :] = jnp.ones((V_ROWS - D_HEAD_V, ts), BF16)
    c_u = 3 * D_ATTN
    c_ga = c_u + D_POOL
    c_gp = c_ga + x.shape[1]
    sga_ref[...] = jax.nn.sigmoid(proj(c_ga, c_gp)).astype(BF16)

    u = proj(c_u, c_ga)

    @pl.when(tb == 0)
    def _():
        ext_ref[0:POOL_HALO, :] = jnp.zeros((POOL_HALO, D_POOL), F32)

    ext_ref[POOL_HALO:POOL_HALO + ts, :] = u
    row = lax.broadcasted_iota(jnp.int32, (ts, 1), 0) + tb * ts
    ys = []
    for g, w in enumerate(POOL_WINDOWS):
        c0 = g * POOL_GROUP_DIM
        ug = u[:, c0:c0 + POOL_GROUP_DIM]
        acc = ug
        for d in range(1, w):
            acc = acc + ext_ref[POOL_HALO - d:POOL_HALO - d + ts, c0:c0 + POOL_GROUP_DIM]
        cnt = jnp.minimum(row + 1, w).astype(F32)
        ys.append((acc / cnt - ug).astype(BF16))
    ypool = jnp.dot(jnp.concatenate(ys, axis=1), wpool_ref[...], preferred_element_type=F32)
    pc_ref[...] = (jax.nn.sigmoid(proj(c_gp, c_gp + x.shape[1])) * ypool).astype(BF16)
    ext_ref[0:POOL_HALO, :] = u[ts - POOL_HALO:ts, :]


def _inproj(x2, mod3, g1, w_in, gq, gk, pool_w, pool_scale, w_bp, *, bsz, seq, ts):
    n, d = x2.shape
    d_in = w_in.shape[1]
    tiles_per_batch = seq // ts
    win_b = w_in.astype(BF16)
    idx = jnp.arange(D_ATTN) // HEAD_DIM
    ones_bd = (idx[:, None] == idx[None, :]).astype(BF16)
    kern = functools.partial(_inproj_kernel, ts=ts, tiles_per_batch=tiles_per_batch)
    tok = lambda i: (i, 0)
    out_shape = (jax.ShapeDtypeStruct((n, D_ATTN), BF16),
                 jax.ShapeDtypeStruct((n, D_ATTN), BF16),
                 jax.ShapeDtypeStruct((bsz, N_HEADS, V_ROWS, seq), BF16),
                 jax.ShapeDtypeStruct((n, d), BF16),
                 jax.ShapeDtypeStruct((n, d), BF16))
    return pl.pallas_call(
        kern,
        out_shape=out_shape,
        grid=(n // ts,),
        in_specs=[pl.BlockSpec((ts, d), tok),
                  pl.BlockSpec((None, 6, d), lambda i: (i // tiles_per_batch, 0, 0)),
                  _const_spec((1, d)),
                  _const_spec((d, d_in)),
                  _const_spec((1, D_ATTN)),
                  _const_spec((1, D_ATTN)),
                  _const_spec((D_ATTN, D_ATTN)),
                  _const_spec((D_POOL, d))],
        out_specs=(pl.BlockSpec((ts, D_ATTN), tok),
                   pl.BlockSpec((ts, D_ATTN), tok),
                   pl.BlockSpec((None, N_HEADS, V_ROWS, ts),
                                lambda i: (i // tiles_per_batch, 0, 0, i % tiles_per_batch)),
                   pl.BlockSpec((ts, d), tok),
                   pl.BlockSpec((ts, d), tok)),
        scratch_shapes=[pltpu.VMEM((POOL_HALO + ts, D_POOL), F32), pltpu.VMEM((ts, D_ATTN), F32)],
        compiler_params=_cparams(1),
        name="inproj",
    )(x2, mod3, g1.reshape(1, d), win_b, gq.reshape(1, D_ATTN), gk.reshape(1, D_ATTN), ones_bd,
      _pool_fold(pool_w, pool_scale, w_bp))


def _attn_kernel(q_ref, qn_ref, k_ref, vt_ref, bias_ref, lq1_ref, lk1_ref, lq2_ref, lk2_ref, subg_ref,
                 *rest, t, lambda_init, cast_groups):
    n_in, n_out = sum(cast_groups), len(cast_groups)
    cast_in = rest[:n_in]
    o_ref, clear_ref = rest[n_in:n_in + 2]
    cast_out = rest[n_in + 2:n_in + 2 + n_out]
    s_ref, mt_ref, m_ref, acc_ref = rest[n_in + 2 + n_out:]
    i = pl.program_id(2)
    last = pl.num_programs(2) - 1
    clear_ref[...] = jnp.zeros(clear_ref.shape, clear_ref.dtype)
    first = 0
    for size, dst in zip(cast_groups, cast_out):
        col = 0
        for src in cast_in[first:first + size]:
            dst[:, col:col + src.shape[1]] = src[...].astype(dst.dtype)
            col += src.shape[1]
        first += size

    def split_maps(q):
        lane = lax.broadcasted_iota(jnp.int32, q.shape, 1)
        zero = jnp.zeros_like(q)
        return jnp.where(lane < HEAD_DIM, q, zero), jnp.where(lane >= HEAD_DIM, q, zero)

    q_now = split_maps(q_ref[...])

    m_ref[...] = jnp.full(m_ref.shape, MASKED, F32)
    acc_ref[...] = jnp.zeros(acc_ref.shape, F32)

    def scores(j, bias, slot, qm=q_now):
        kt = k_ref[pl.ds(pl.multiple_of(j * t, t), t), :]
        for mp in range(2):
            s = lax.dot_general(kt, qm[mp], (((1,), (1,)), ((), ())), preferred_element_type=F32)
            if bias is not None:
                s = s + bias
            s_ref[slot, mp] = s
            mt_ref[slot, mp] = jnp.max(s, axis=0, keepdims=True)

    def accumulate(j, slot):
        vt = vt_ref[:, pl.ds(pl.multiple_of(j * t, t), t)]
        for mp in range(2):
            m_old = m_ref[mp]
            m_new = jnp.maximum(m_old, mt_ref[slot, mp])
            p = jnp.exp2(s_ref[slot, mp] - m_new).astype(BF16)
            acc_ref[mp] = (jnp.exp2(m_old - m_new) * acc_ref[mp]
                           + jnp.dot(vt, p, preferred_element_type=F32))
            m_ref[mp] = m_new

    def next_diagonal():
        nxt = jnp.minimum(i + 1, last)
        scores(nxt, bias_ref[1], 2, split_maps(qn_ref[...]))

    @pl.when(i == 0)
    def _():
        scores(0, bias_ref[1], 0)

    @pl.when(i >= 1)
    def _():
        scores(i - 1, bias_ref[0], 1)
        accumulate(i, 2)

    n_far = jnp.maximum(i - 1, 0)

    def pair(kk):
        j = i - 1 - 2 * kk
        scores(j - 1, None, 0)
        accumulate(j, 1)
        scores(j - 2, None, 1)
        accumulate(j - 1, 0)

    def oct_body(oo, carry):
        for u in range(4):
            pair(4 * oo + u)
        return carry

    n_pairs = n_far // 2
    lax.fori_loop(0, n_pairs // 4, oct_body, 0)

    @pl.when(n_pairs % 4 >= 2)
    def _():
        pair(n_pairs // 4 * 4)
        pair(n_pairs // 4 * 4 + 1)

    @pl.when(n_pairs % 2 == 1)
    def _():
        pair(n_pairs - 1)

    def finish(slot):
        next_diagonal()
        accumulate(0, slot)
        lam = (jnp.exp(jnp.sum(lq1_ref[...] * lk1_ref[...], axis=1, keepdims=True))
               - jnp.exp(jnp.sum(lq2_ref[...] * lk2_ref[...], axis=1, keepdims=True)) + lambda_init)
        o1 = acc_ref[0, 0:D_HEAD_V, :] / acc_ref[0, D_HEAD_V:D_HEAD_V + 1, :]
        o2 = acc_ref[1, 0:D_HEAD_V, :] / acc_ref[1, D_HEAD_V:D_HEAD_V + 1, :]
        ot = o1 - lam * o2
        ot = ot * lax.rsqrt(jnp.mean(ot * ot, axis=0, keepdims=True) + RMS_EPS)
        ot = ot * subg_ref[...] * (1.0 - lambda_init)
        o_ref[...] = ot.T.astype(BF16)

    @pl.when(n_far % 2 == 1)
    def _():
        scores(0, None, 0)
        accumulate(1, 1)
        finish(0)

    @pl.when(jnp.logical_and(i >= 1, n_far % 2 == 0))
    def _():
        finish(1)

    @pl.when(i == 0)
    def _():
        finish(0)


def _attention(q, k, vt, bias, lq1, lk1, lq2, lk2, subln_g, *, t, lambda_init, clear_shape, to_bf16):
    bsz, seq, _ = q.shape
    nq = seq // t
    assert t + 1 >= LOG_BUCKET_STARTS[-1], "keys two tiles back must all fall in the last distance bucket"
    steps = bsz * N_HEADS * nq
    step_id = lambda b, h, i: ((b * N_HEADS + h) * nq + i, 0)
    clear_rows = clear_shape[0] // steps
    assert clear_rows * steps == clear_shape[0] and clear_rows % SUBLANES == 0, (clear_shape, steps)
    cast_in_specs, cast_out_specs, cast_out_shapes = [], [], []
    for group in to_bf16:
        rows = group[0].shape[0] // steps
        assert rows * steps == group[0].shape[0] and rows % (2 * SUBLANES) == 0, (group[0].shape, steps)
        cols = sum(a.shape[1] for a in group)
        cast_in_specs += [pl.BlockSpec((rows, a.shape[1]), step_id) for a in group]
        cast_out_specs.append(pl.BlockSpec((rows, cols), step_id))
        cast_out_shapes.append(jax.ShapeDtypeStruct((group[0].shape[0], cols), BF16))
    kern = functools.partial(_attn_kernel, t=t, lambda_init=lambda_init,
                             cast_groups=tuple(len(g) for g in to_bf16))
    vec = lambda a: a.reshape(1, HEAD_DIM)
    return pl.pallas_call(
        kern,
        out_shape=(jax.ShapeDtypeStruct((bsz, seq, D_ATTN), BF16),
                   jax.ShapeDtypeStruct(clear_shape, F32),
                   *cast_out_shapes),
        grid=(bsz, N_HEADS, seq // t),
        in_specs=[pl.BlockSpec((None, t, D_HEAD_V), lambda b, h, i: (b, i, h)),
                  pl.BlockSpec((None, t, D_HEAD_V), lambda b, h, i: (b, jnp.minimum(i + 1, nq - 1), h)),
                  pl.BlockSpec((None, seq, D_HEAD_V), lambda b, h, i: (b, 0, h)),
                  pl.BlockSpec((None, None, V_ROWS, seq), lambda b, h, i: (b, h, 0, 0)),
                  pl.BlockSpec((None, 2, t, t), lambda b, h, i: (h, 0, 0, 0)),
                  _const_spec((1, HEAD_DIM)), _const_spec((1, HEAD_DIM)),
                  _const_spec((1, HEAD_DIM)), _const_spec((1, HEAD_DIM)),
                  _const_spec((D_HEAD_V, 1)), *cast_in_specs],
        out_specs=(pl.BlockSpec((None, t, D_HEAD_V), lambda b, h, i: (b, i, h)),
                   pl.BlockSpec((clear_rows, clear_shape[1]), step_id), *cast_out_specs),
        scratch_shapes=[pltpu.VMEM((3, 2, t, t), F32),
                        pltpu.VMEM((3, 2, 1, t), F32),
                        pltpu.VMEM((2, 1, t), F32),
                        pltpu.VMEM((2, V_ROWS, t), F32)],
        compiler_params=_cparams(3),
        name="attn",
    )(q, q, k, vt, bias, vec(lq1), vec(lk1), vec(lq2), vec(lk2), subln_g.reshape(D_HEAD_V, 1),
      *[a for group in to_bf16 for a in group])


def _first_max(rows):
    best = rows[0]
    for r in rows[1:]:
        best = jnp.maximum(best, r)
    idx = jnp.full(best.shape, len(rows) - 1, jnp.int32)
    for j in range(len(rows) - 2, -1, -1):
        idx = jnp.where(rows[j] == best, j, idx)
    return best, idx


def _post_kernel(o_ref, sga_ref, pc_ref, x_ref, mod_ref, g2_ref, wba_ref, wout_ref, wr_ref, br_ref,
                 tri_ref, x1_ref, h2p_ref, route_ref, cnt_ref, carry_ref):
    @pl.when(pl.program_id(0) == 0)
    def _():
        carry_ref[...] = jnp.zeros(carry_ref.shape, F32)

    ya = jnp.dot(o_ref[...], wba_ref[...], preferred_element_type=F32)
    merged = sga_ref[...].astype(F32) * ya + pc_ref[...].astype(F32)
    z = jnp.dot(merged.astype(BF16), wout_ref[...], preferred_element_type=F32)
    x1 = x_ref[...] + mod_ref[2:3, :] * z
    x1_ref[...] = x1
    y = x1 * lax.rsqrt(jnp.mean(x1 * x1, axis=-1, keepdims=True) + RMS_EPS)
    h2 = y * (g2_ref[...] * (1.0 + mod_ref[4:5, :])) + mod_ref[3:4, :]
    hi = h2.astype(BF16)
    lo = (h2 - hi.astype(F32)).astype(BF16)

    nt = (((1,), (1,)), ((), ()))
    a = lax.dot_general(wr_ref[...], hi, nt, preferred_element_type=F32)
    b = lax.dot_general(wr_ref[0:ROUTER_ROWS, :], lo, nt, preferred_element_type=F32)
    logits = a[0:ROUTER_ROWS] + a[ROUTER_ROWS:2 * ROUTER_ROWS] + b + br_ref[...]

    gl = [logits[g:g + 1, :] for g in range(N_EXPERT_GROUPS)]
    gmax, gidx = _first_max(gl)
    gsum = gl[0] * 0.0
    for r in gl:
        gsum = gsum + jnp.exp(r - gmax)
    g_val = 1.0 / gsum
    es = []
    for r in range(EXPERTS_PER_GROUP):
        sel = jnp.zeros_like(gmax)
        for g in range(N_EXPERT_GROUPS):
            row = EXPERT_ROW0 + g * EXPERTS_PER_GROUP + r
            sel = jnp.where(gidx == g, logits[row:row + 1, :], sel)
        es.append(sel)
    e1, i1 = _first_max(es)
    rest = [jnp.where(i1 == r, -jnp.inf, es[r]) for r in range(EXPERTS_PER_GROUP)]
    e2, i2 = _first_max(rest)
    r21 = jnp.exp(e2 - e1)
    w1 = g_val / (1.0 + r21)
    w2 = g_val * r21 / (1.0 + r21)

    first = i1 < i2
    e_lo = jnp.where(first, i1, i2)
    e_hi = jnp.where(first, i2, i1)
    pair = jnp.zeros_like(e_lo)
    for p in range(1, PAIRS_PER_GROUP):
        pair = jnp.where(jnp.logical_and(e_lo == PAIR_LO[p], e_hi == PAIR_HI[p]), p, pair)
    bucket = gidx * PAIRS_PER_GROUP + pair
    w_lo = jnp.where(first, w1, w2)
    w_hi = jnp.where(first, w2, w1)

    ts = bucket.shape[1]
    brow = lax.broadcasted_iota(jnp.int32, (ROUTER_ROWS, ts), 0)
    onehot = brow == bucket
    prefix = jnp.dot(jnp.where(onehot, 1.0, 0.0).astype(BF16), tri_ref[...],
                     preferred_element_type=F32)
    carry = carry_ref[...]
    rank = jnp.sum(jnp.where(onehot, prefix + carry, 0.0), axis=0, keepdims=True) - 1.0
    carry = carry + prefix[:, ts - 1:ts]
    carry_ref[...] = carry
    cnt_ref[...] = jnp.broadcast_to(carry, cnt_ref.shape).astype(jnp.int32)
    chunks = ts // LANES
    rank_i = rank.astype(jnp.int32)
    for j in range(chunks):
        route_ref[j:j + 1, :] = bucket[:, j * LANES:(j + 1) * LANES]
        route_ref[chunks + j:chunks + j + 1, :] = rank_i[:, j * LANES:(j + 1) * LANES]

    arow = lax.broadcasted_iota(jnp.int32, (LANES, ts), 0)
    aux_t = jnp.where(arow == 0, w_lo, jnp.where(arow == 1, w_hi, 0.0))
    d = h2.shape[1]
    h2p_ref[:, 0:d] = h2
    h2p_ref[:, d:d + LANES] = aux_t.T


def _post(o2, sga, pc, x2, mod3, g2, w_ba, w_out, wg_r, bg_r, we_r, be_r, *, seq, ts):
    n, d = x2.shape
    tiles_per_batch = seq // ts
    steps = n // ts
    wr = jnp.zeros((ROUTER_ROWS, d), F32)
    wr = wr.at[0:N_EXPERT_GROUPS].set(wg_r.T).at[EXPERT_ROW0:EXPERT_ROW0 + N_EXPERTS].set(we_r.T)
    wr_hi = wr.astype(BF16)
    wr_lo = (wr - wr_hi.astype(F32)).astype(BF16)
    br = jnp.zeros((ROUTER_ROWS, 1), F32)
    br = br.at[0:N_EXPERT_GROUPS, 0].set(bg_r).at[EXPERT_ROW0:EXPERT_ROW0 + N_EXPERTS, 0].set(be_r)
    tok = lambda i: (i, 0)
    tidx = jnp.arange(ts)
    tri = (tidx[:, None] <= tidx[None, :]).astype(BF16)
    return pl.pallas_call(
        _post_kernel,
        out_shape=(jax.ShapeDtypeStruct((n, d), F32),
                   jax.ShapeDtypeStruct((n, d + LANES), F32),
                   jax.ShapeDtypeStruct((steps, 2 * (ts // LANES), LANES), jnp.int32),
                   jax.ShapeDtypeStruct((ROUTER_ROWS, LANES), jnp.int32)),
        grid=(steps,),
        in_specs=[pl.BlockSpec((ts, D_ATTN), tok),
                  pl.BlockSpec((ts, d), tok),
                  pl.BlockSpec((ts, d), tok),
                  pl.BlockSpec((ts, d), tok),
                  pl.BlockSpec((None, 6, d), lambda i: (i // tiles_per_batch, 0, 0)),
                  _const_spec((1, d)),
                  _const_spec((D_ATTN, d)),
                  _const_spec((d, d)),
                  _const_spec((2 * ROUTER_ROWS, d)),
                  _const_spec((ROUTER_ROWS, 1)),
                  _const_spec((ts, ts))],
        out_specs=(pl.BlockSpec((ts, d), tok), pl.BlockSpec((ts, d + LANES), tok),
                   pl.BlockSpec((None, 2 * (ts // LANES), LANES), lambda i: (i, 0, 0)),
                   pl.BlockSpec((ROUTER_ROWS, LANES), lambda i: (0, 0))),
        scratch_shapes=[pltpu.VMEM((ROUTER_ROWS, 1), F32)],
        compiler_params=_cparams(1),
        name="post",
    )(o2, sga, pc, x2, mod3, g2.reshape(1, d), w_ba.astype(BF16), w_out.astype(BF16),
      jnp.concatenate([wr_hi, wr_lo], axis=0), br, tri)


def _group_sublane(row):
    return lax.shift_right_logical(row, SUBLANES.bit_length() - 1), row & (SUBLANES - 1)


def _row_copy(src, src_row, dst, dst_row, sem):
    sg, ss = src_row
    dg, ds = dst_row
    return pltpu.make_async_copy(src.at[sg, pl.ds(ss, 1), :], dst.at[dg, pl.ds(ds, 1), :], sem)


def _dispatch_kernel(pos_ref, h_ref, init_ref, hs_ref, sem, *, ts):
    del init_ref
    base = pl.program_id(0) * ts

    def start(g, carry):
        for u in range(SUBLANES):
            p = pos_ref[base + g * SUBLANES + u]
            _row_copy(h_ref, (g, u), hs_ref, _group_sublane(p), sem).start(priority=u % 2)
        return carry

    def wait(g, carry):
        for u in range(SUBLANES):
            _row_copy(h_ref, (0, 0), hs_ref, (0, 0), sem).wait()
        return carry

    lax.fori_loop(0, ts // SUBLANES, start, 0)
    lax.fori_loop(0, ts // SUBLANES, wait, 0)


def _dispatch(pos, h2p, cleared, *, ts):
    n, w = h2p.shape
    n_rows = cleared.shape[0]
    return pl.pallas_call(
        functools.partial(_dispatch_kernel, ts=ts),
        out_shape=jax.ShapeDtypeStruct((n_rows // SUBLANES, SUBLANES, w), h2p.dtype),
        grid_spec=pltpu.PrefetchScalarGridSpec(
            num_scalar_prefetch=1,
            grid=(n // ts,),
            in_specs=[pl.BlockSpec((ts // SUBLANES, SUBLANES, w), lambda i, pos: (i, 0, 0)),
                      pl.BlockSpec(memory_space=pl.ANY)],
            out_specs=pl.BlockSpec(memory_space=pl.ANY),
            scratch_shapes=[pltpu.SemaphoreType.DMA]),
        input_output_aliases={2: 0},
        compiler_params=_cparams(1),
        name="dispatch",
    )(pos, h2p.reshape(n // SUBLANES, SUBLANES, w),
      cleared.reshape(n_rows // SUBLANES, SUBLANES, w)).reshape(n_rows, w)


def _experts_kernel(ea_ref, eb_ref, valid_ref, hs_ref, wgua_ref, wda_ref, wgub_ref, wdb_ref, ys_ref):
    del ea_ref, eb_ref
    rows = valid_ref[pl.program_id(0)]
    tm = hs_ref.shape[0]
    d = hs_ref.shape[1] - LANES

    def run(m):
        h = hs_ref[0:m, 0:d].astype(BF16)
        aux = hs_ref[0:m, d:d + LANES]

        def expert(wgu_ref, wd_ref):
            gu = jnp.dot(h, wgu_ref[...], preferred_element_type=F32)
            de = gu.shape[1] // 2
            a, b = gu[:, :de], gu[:, de:]
            hid = (a * jax.nn.sigmoid(a)) * b
            return jnp.dot(hid.astype(BF16), wd_ref[...], preferred_element_type=F32)

        ys_ref[0:m, :] = (aux[:, 0:1] * expert(wgua_ref, wda_ref)
                          + aux[:, 1:2] * expert(wgub_ref, wdb_ref))
        if m < tm:
            ys_ref[m:tm, :] = jnp.zeros((tm - m, ys_ref.shape[1]), ys_ref.dtype)

    @pl.when(rows > tm // 2)
    def _():
        run(tm)

    @pl.when(jnp.logical_and(rows > 0, rows <= tm // 2))
    def _():
        run(tm // 2)

    @pl.when(rows == 0)
    def _():
        ys_ref[...] = jnp.zeros(ys_ref.shape, ys_ref.dtype)


def _experts(tile_ea, tile_eb, tile_valid, hs, w_gate_up, w_down, *, tm):
    n_rows, w = hs.shape
    _, de, d = w_down.shape
    ea = lambda t, ea_r, eb_r, v_r: (ea_r[t], 0, 0)
    eb = lambda t, ea_r, eb_r, v_r: (eb_r[t], 0, 0)
    row = lambda t, ea_r, eb_r, v_r: (t, 0)
    return pl.pallas_call(
        _experts_kernel,
        out_shape=jax.ShapeDtypeStruct((n_rows, d), F32),
        grid_spec=pltpu.PrefetchScalarGridSpec(
            num_scalar_prefetch=3,
            grid=(n_rows // tm,),
            in_specs=[pl.BlockSpec((tm, w), row),
                      pl.BlockSpec((None, d, 2 * de), ea), pl.BlockSpec((None, de, d), ea),
                      pl.BlockSpec((None, d, 2 * de), eb), pl.BlockSpec((None, de, d), eb)],
            out_specs=pl.BlockSpec((tm, d), row)),
        compiler_params=_cparams(1),
        name="experts",
    )(tile_ea, tile_eb, tile_valid, hs, w_gate_up, w_down, w_gate_up, w_down)


def _combine_kernel(pos_ref, ys_ref, x1_ref, mod_ref, out_ref, ybuf_ref, sem, *, ts):
    i = pl.program_id(0)
    groups = ts // SUBLANES

    def gather(tile, slot):
        base = tile * ts

        def start(g, carry):
            for u in range(SUBLANES):
                p = pos_ref[base + g * SUBLANES + u]
                _row_copy(ys_ref, _group_sublane(p), ybuf_ref.at[slot], (g, u),
                          sem.at[slot]).start(priority=u % 2)
            return carry

        lax.fori_loop(0, groups, start, 0)

    def wait_all(slot):
        def wait(g, carry):
            for u in range(SUBLANES):
                _row_copy(ys_ref, (0, 0), ybuf_ref.at[slot], (0, 0), sem.at[slot]).wait()
            return carry

        lax.fori_loop(0, groups, wait, 0)

    def step(slot):
        @pl.when(i + 1 < pl.num_programs(0))
        def _():
            gather(i + 1, 1 - slot)

        wait_all(slot)
        y = ybuf_ref[slot].reshape(ts, ybuf_ref.shape[-1])
        out_ref[...] = x1_ref[...] + mod_ref[5:6, :] * y

    @pl.when(i == 0)
    def _():
        gather(0, 0)

    @pl.when(i % 2 == 0)
    def _():
        step(0)

    @pl.when(i % 2 == 1)
    def _():
        step(1)


def _combine(pos, ys, x1, mod3, *, seq, ts):
    n, d = x1.shape
    tiles_per_batch = seq // ts
    return pl.pallas_call(
        functools.partial(_combine_kernel, ts=ts),
        out_shape=jax.ShapeDtypeStruct((n, d), F32),
        grid_spec=pltpu.PrefetchScalarGridSpec(
            num_scalar_prefetch=1,
            grid=(n // ts,),
            in_specs=[pl.BlockSpec(memory_space=pl.ANY),
                      pl.BlockSpec((ts, d), lambda i, pos: (i, 0)),
                      pl.BlockSpec((None, 6, d), lambda i, pos: (i // tiles_per_batch, 0, 0))],
            out_specs=pl.BlockSpec((ts, d), lambda i, pos: (i, 0)),
            scratch_shapes=[pltpu.VMEM((2, ts // SUBLANES, SUBLANES, d), F32),
                            pltpu.SemaphoreType.DMA((2,))]),
        compiler_params=_cparams(1),
        name="combine",
    )(pos, ys.reshape(ys.shape[0] // SUBLANES, SUBLANES, d), x1, mod3)


def _pos_kernel(start_ref, route_ref, pos_ref):
    chunks = pos_ref.shape[1]
    bucket = route_ref[:, 0:chunks, :]
    pos = route_ref[:, chunks:2 * chunks, :]
    for b in range(N_ROUTE_BUCKETS):
        pos = pos + jnp.where(bucket == b, start_ref[b], 0)
    pos_ref[...] = pos


def _route_tables(route, counts, n_tiles, tm):
    steps, rows, _ = route.shape
    cnt = counts[:N_ROUTE_BUCKETS, 0]
    padded = (cnt + tm - 1) // tm * tm
    end = jnp.cumsum(padded)
    pos = pl.pallas_call(
        _pos_kernel,
        out_shape=jax.ShapeDtypeStruct((steps, rows // 2, LANES), jnp.int32),
        in_specs=[pl.BlockSpec(memory_space=pltpu.SMEM), pl.BlockSpec(memory_space=pltpu.VMEM)],
        out_specs=pl.BlockSpec(memory_space=pltpu.VMEM),
        name="pos",
    )(end - padded, route).reshape(-1)
    tiles_used = end[-1] // tm
    tile = jnp.arange(n_tiles, dtype=jnp.int32)
    valid = tile < tiles_used
    first_row = jnp.minimum(tile, tiles_used - 1) * tm
    tile_bucket = jnp.sum((end[None, :] <= first_row[:, None]).astype(jnp.int32), axis=1)
    tile_bucket = jnp.minimum(tile_bucket, N_ROUTE_BUCKETS - 1)
    group = tile_bucket // PAIRS_PER_GROUP
    pair = tile_bucket % PAIRS_PER_GROUP
    tile_ea = group * EXPERTS_PER_GROUP + jnp.asarray(PAIR_LO, jnp.int32)[pair]
    tile_eb = group * EXPERTS_PER_GROUP + jnp.asarray(PAIR_HI, jnp.int32)[pair]
    real_end = (end - padded + cnt)[tile_bucket]
    tile_rows = jnp.where(valid, jnp.clip(real_end - tile * tm, 0, tm), 0)
    return pos.astype(jnp.int32), tile_ea, tile_eb, tile_rows.astype(jnp.int32)


def _tile(seq, pref):
    t = min(pref, seq)
    assert seq % t == 0 and t % LANES == 0, (seq, t)
    return t


def kernel(x, c, rel_bias, ada_w, ada_b, norm1_g, w_in, q_norm_g, k_norm_g, lambda_q1, lambda_k1,
           lambda_q2, lambda_k2, subln_g, w_branch_attn, pool_w, pool_scale, w_branch_pool, w_out,
           norm2_g, router_group_w, router_group_b, router_expert_w, router_expert_b,
           expert_w_gate, expert_w_up, expert_w_down):
    bsz, seq, d = x.shape
    n = bsz * seq
    ts = _tile(seq, TOKEN_TILE)
    t_attn = _tile(seq, ATTN_TILE)
    tr = _tile(seq, ROW_COPY_TILE)
    tm = MOE_TILE
    n_tiles = -(-(n + N_ROUTE_BUCKETS * (tm - 1)) // tm)
    attn_steps = bsz * N_HEADS * (seq // t_attn)
    while (n_tiles * tm) % (attn_steps * SUBLANES):
        n_tiles += 1
    bias = _bias_tiles(rel_bias, t_attn)
    x2 = x.reshape(n, d)
    for l in range(ada_w.shape[0]):
        lambda_init = 0.8 - 0.6 * math.exp(-0.3 * l)
        mod3 = _ada(c, ada_w[l], ada_b[l]).reshape(bsz, 6, d)
        gq = jnp.tile(q_norm_g[l], D_ATTN // HEAD_DIM) * (HEAD_DIM ** -0.5 * LOG2E)
        gk = jnp.tile(k_norm_g[l], D_ATTN // HEAD_DIM)
        q, k, vt, sga, pc = _inproj(x2, mod3, norm1_g[l], w_in[l], gq, gk, pool_w[l], pool_scale[l],
                                    w_branch_pool[l], bsz=bsz, seq=seq, ts=_tile(seq, INPROJ_TILE))
        flat = lambda w: w.reshape(-1, w.shape[-1])
        o, cleared, wgu_b, wd_b = _attention(
            q.reshape(bsz, seq, D_ATTN), k.reshape(bsz, seq, D_ATTN), vt, bias,
            lambda_q1[l], lambda_k1[l], lambda_q2[l], lambda_k2[l], subln_g[l],
            t=t_attn, lambda_init=lambda_init, clear_shape=(n_tiles * tm, d + LANES),
            to_bf16=[[flat(expert_w_gate[l]), flat(expert_w_up[l])], [flat(expert_w_down[l])]])
        n_exp, _, de = expert_w_gate[l].shape
        wgu_b = wgu_b.reshape(n_exp, d, 2 * de)
        wd_b = wd_b.reshape(n_exp, de, d)
        x1, h2p, route, counts = _post(
            o.reshape(n, D_ATTN), sga, pc, x2, mod3, norm2_g[l], w_branch_attn[l], w_out[l],
            router_group_w[l], router_group_b[l], router_expert_w[l], router_expert_b[l],
            seq=seq, ts=ts)
        pos, tile_ea, tile_eb, tile_valid = _route_tables(route, counts, n_tiles, tm)
        hs = _dispatch(pos, h2p, cleared, ts=tr)
        ys = _experts(tile_ea, tile_eb, tile_valid, hs, wgu_b, wd_b, tm=tm)
        x2 = _combine(pos, ys, x1, mod3, seq=seq, ts=tr)
    return x2.reshape(bsz, seq, d)
```

```python
import functools
import math

import jax
import jax.numpy as jnp
from jax import lax
from jax.experimental import pallas as pl
from jax.experimental.pallas import tpu as pltpu

F32 = jnp.float32
BF16 = jnp.bfloat16

CHUNK = 64
N_HEADS = 4
HEAD_DIM = 64
D_HEAD_V = 2 * HEAD_DIM
V_ROWS = D_HEAD_V + 16
D_ATTN = N_HEADS * D_HEAD_V
POOL_WINDOWS = (2, 4, 8, 16)
POOL_GROUP_DIM = 128
D_POOL = len(POOL_WINDOWS) * POOL_GROUP_DIM
POOL_HALO = 16
N_BUCKETS = 32
MAX_DISTANCE = 128
N_EXPERT_GROUPS = 4
EXPERTS_PER_GROUP = 4
N_EXPERTS = N_EXPERT_GROUPS * EXPERTS_PER_GROUP
PAIRS_PER_GROUP = 6
PAIR_LO = (0, 0, 1, 1, 0, 2)
PAIR_HI = (1, 2, 2, 3, 3, 3)
N_ROUTE_BUCKETS = N_EXPERT_GROUPS * PAIRS_PER_GROUP
RMS_EPS = 1e-6
LOG2E = math.log2(math.e)
MASKED = -1e30

LANES = 128
SUBLANES = 8
ROUTER_ROWS = 32
EXPERT_ROW0 = 8

VMEM_LIMIT = 56 * 1024 * 1024
TOKEN_TILE = 1024
INPROJ_TILE = 1024
ATTN_TILE = 512
MOE_TILE = 512
ROW_COPY_TILE = 1024


def _cparams(n_axes):
    return pltpu.CompilerParams(dimension_semantics=("arbitrary",) * n_axes,
                                vmem_limit_bytes=VMEM_LIMIT)


def _const_spec(shape):
    nd = len(shape)
    return pl.BlockSpec(shape, lambda *_: (0,) * nd, pipeline_mode=pl.Buffered(1))


def _ada_kernel(ct_ref, w_ref, b_ref, o_ref):
    @pl.when(pl.program_id(0) == 0)
    def _():
        o_ref[...] = jnp.broadcast_to(b_ref[...], o_ref.shape)

    ct = ct_ref[...]
    s = ct * jax.nn.sigmoid(ct)
    w = w_ref[...]
    rows = [jnp.sum(w * s[:, b:b + 1], axis=0, keepdims=True) for b in range(ct.shape[1])]
    o_ref[...] += jnp.concatenate(rows, axis=0)


def _ada(c, w, b):
    bsz, d = c.shape
    n = w.shape[1]
    rows = 256
    return pl.pallas_call(
        _ada_kernel,
        out_shape=jax.ShapeDtypeStruct((bsz, n), F32),
        grid=(d // rows,),
        in_specs=[pl.BlockSpec((rows, bsz), lambda j: (j, 0)),
                  pl.BlockSpec((rows, n), lambda j: (j, 0)),
                  pl.BlockSpec((1, n), lambda j: (0, 0))],
        out_specs=pl.BlockSpec((bsz, n), lambda j: (0, 0)),
        compiler_params=_cparams(1),
        name="ada",
    )(c.T, w, b.reshape(1, n))


def _log_bucket_starts():
    nb = N_BUCKETS // 2
    max_exact = nb // 2
    m = nb - max_exact
    ratio = MAX_DISTANCE // max_exact
    starts = []
    for k in range(1, m):
        n = max_exact
        while n ** m < max_exact ** m * ratio ** k:
            n += 1
        starts.append(n)
    return tuple(starts)


LOG_BUCKET_STARTS = _log_bucket_starts()


def _bias_kernel(rb_ref, o_ref, *, t):
    h = pl.program_id(0)
    kind = pl.program_id(1)
    nb = N_BUCKETS // 2
    max_exact = nb // 2
    kpos = lax.broadcasted_iota(jnp.int32, (t, t), 0)
    qpos = lax.broadcasted_iota(jnp.int32, (t, t), 1)
    rel = kpos - qpos - jnp.where(kind == 0, t, 0)
    n = jnp.abs(rel)

    def table(first):
        val = jnp.full((t, t), rb_ref[first + nb - 1, h], F32)
        for k in range(len(LOG_BUCKET_STARTS) - 1, -1, -1):
            val = jnp.where(n < LOG_BUCKET_STARTS[k], rb_ref[first + max_exact + k, h], val)
        for j in range(max_exact - 1, -1, -1):
            val = jnp.where(n == j, rb_ref[first + j, h], val)
        return val

    far = rb_ref[nb - 1, h]

    @pl.when(kind == 0)
    def _():
        o_ref[...] = (table(0) - far) * LOG2E

    @pl.when(kind == 1)
    def _():
        bias = jnp.where(rel > 0, table(nb), table(0))
        shift = CHUNK.bit_length() - 1
        hidden = (kpos >> shift) > (qpos >> shift)
        o_ref[...] = jnp.where(hidden, MASKED, (bias - far) * LOG2E)


def _bias_tiles(rel_bias, t):
    return pl.pallas_call(
        functools.partial(_bias_kernel, t=t),
        out_shape=jax.ShapeDtypeStruct((N_HEADS, 2, t, t), F32),
        grid=(N_HEADS, 2),
        in_specs=[pl.BlockSpec(memory_space=pltpu.SMEM)],
        out_specs=pl.BlockSpec((None, None, t, t), lambda h, j: (h, j, 0, 0)),
        compiler_params=_cparams(2),
        name="bias_tiles",
    )(rel_bias)


def _group_rms(xc, ones_blockdiag, gain):
    ssq = jnp.dot((xc * xc).astype(BF16), ones_blockdiag, preferred_element_type=F32)
    return xc * lax.rsqrt(ssq * (1.0 / HEAD_DIM) + RMS_EPS) * gain


def _split_bf16(a):
    hi = a.astype(BF16)
    return hi, (a - hi.astype(F32)).astype(BF16)


def _pool_fold_kernel(pw_ref, ps_ref, wbp_ref, o_ref):
    a_hi, a_lo = _split_bf16(pw_ref[...] * ps_ref[...])
    b_hi, b_lo = _split_bf16(wbp_ref[...])
    dot = functools.partial(jnp.dot, preferred_element_type=F32)
    o_ref[...] = (dot(a_hi, b_hi) + dot(a_hi, b_lo) + dot(a_lo, b_hi)).astype(o_ref.dtype)


def _pool_fold(pool_w, pool_scale, w_bp):
    g, c, _ = pool_w.shape
    d = w_bp.shape[1]
    return pl.pallas_call(
        _pool_fold_kernel,
        out_shape=jax.ShapeDtypeStruct((g * c, d), BF16),
        grid=(g,),
        in_specs=[pl.BlockSpec((None, c, c), lambda i: (i, 0, 0)),
                  pl.BlockSpec((None, 1, c), lambda i: (i, 0, 0)),
                  pl.BlockSpec((c, d), lambda i: (i, 0))],
        out_specs=pl.BlockSpec((c, d), lambda i: (i, 0)),
        compiler_params=_cparams(1),
        name="pool_fold",
    )(pool_w, pool_scale.reshape(g, 1, c), w_bp)


def _inproj_kernel(x_ref, mod_ref, g1_ref, win_ref, gq_ref, gk_ref, ones_ref, wpool_ref,
                   q_ref, k_ref, vt_ref, sga_ref, pc_ref, ext_ref, v_ref,
                   *, ts, tiles_per_batch):
    tb = pl.program_id(0) % tiles_per_batch

    x = x_ref[...]
    y = x * lax.rsqrt(jnp.mean(x * x, axis=-1, keepdims=True) + RMS_EPS)
    h = y * (g1_ref[...] * (1.0 + mod_ref[1:2, :])) + mod_ref[0:1, :]
    hb = h.astype(BF16)

    def proj(c0, c1):
        return jnp.dot(hb, win_ref[:, c0:c1], preferred_element_type=F32)

    ones_bd = ones_ref[...]
    q_ref[...] = _group_rms(proj(0, D_ATTN), ones_bd, gq_ref[...]).astype(BF16)
    k_ref[...] = _group_rms(proj(D_ATTN, 2 * D_ATTN), ones_bd, gk_ref[...]).astype(BF16)
    v_ref[...] = proj(2 * D_ATTN, 3 * D_ATTN)
    vt = v_ref[...].T.astype(BF16)
    for hd in range(N_HEADS):
        vt_ref[hd, 0:D_HEAD_V, :] = vt[hd * D_HEAD_V:(hd + 1) * D_HEAD_V, :]
        vt_ref[hd, D_HEAD_V:V_ROWS, :] = jnp.ones((V_ROWS - D_HEAD_V, ts), BF16)
    c_u = 3 * D_ATTN
    c_ga = c_u + D_POOL
    c_gp = c_ga + x.shape[1]
    sga_ref[...] = jax.nn.sigmoid(proj(c_ga, c_gp)).astype(BF16)

    u = proj(c_u, c_ga)

    @pl.when(tb == 0)
    def _():
        ext_ref[0:POOL_HALO, :] = jnp.zeros((POOL_HALO, D_POOL), F32)

    ext_ref[POOL_HALO:POOL_HALO + ts, :] = u
    row = lax.broadcasted_iota(jnp.int32, (ts, 1), 0) + tb * ts
    ys = []
    for g, w in enumerate(POOL_WINDOWS):
        c0 = g * POOL_GROUP_DIM
        ug = u[:, c0:c0 + POOL_GROUP_DIM]
        acc = ug
        for d in range(1, w):
            acc = acc + ext_ref[POOL_HALO - d:POOL_HALO - d + ts, c0:c0 + POOL_GROUP_DIM]
        cnt = jnp.minimum(row + 1, w).astype(F32)
        ys.append((acc / cnt - ug).astype(BF16))
    ypool = jnp.dot(jnp.concatenate(ys, axis=1), wpool_ref[...], preferred_element_type=F32)
    pc_ref[...] = (jax.nn.sigmoid(proj(c_gp, c_gp + x.shape[1])) * ypool).astype(BF16)
    ext_ref[0:POOL_HALO, :] = u[ts - POOL_HALO:ts, :]


def _inproj(x2, mod3, g1, w_in, gq, gk, pool_w, pool_scale, w_bp, *, bsz, seq, ts):
    n, d = x2.shape
    d_in = w_in.shape[1]
    tiles_per_batch = seq // ts
    win_b = w_in.astype(BF16)
    idx = jnp.arange(D_ATTN) // HEAD_DIM
    ones_bd = (idx[:, None] == idx[None, :]).astype(BF16)
    kern = functools.partial(_inproj_kernel, ts=ts, tiles_per_batch=tiles_per_batch)
    tok = lambda i: (i, 0)
    out_shape = (jax.ShapeDtypeStruct((n, D_ATTN), BF16),
                 jax.ShapeDtypeStruct((n, D_ATTN), BF16),
                 jax.ShapeDtypeStruct((bsz, N_HEADS, V_ROWS, seq), BF16),
                 jax.ShapeDtypeStruct((n, d), BF16),
                 jax.ShapeDtypeStruct((n, d), BF16))
    return pl.pallas_call(
        kern,
        out_shape=out_shape,
        grid=(n // ts,),
        in_specs=[pl.BlockSpec((ts, d), tok),
                  pl.BlockSpec((None, 6, d), lambda i: (i // tiles_per_batch, 0, 0)),
                  _const_spec((1, d)),
                  _const_spec((d, d_in)),
                  _const_spec((1, D_ATTN)),
                  _const_spec((1, D_ATTN)),
                  _const_spec((D_ATTN, D_ATTN)),
                  _const_spec((D_POOL, d))],
        out_specs=(pl.BlockSpec((ts, D_ATTN), tok),
                   pl.BlockSpec((ts, D_ATTN), tok),
                   pl.BlockSpec((None, N_HEADS, V_ROWS, ts),
                                lambda i: (i // tiles_per_batch, 0, 0, i % tiles_per_batch)),
                   pl.BlockSpec((ts, d), tok),
                   pl.BlockSpec((ts, d), tok)),
        scratch_shapes=[pltpu.VMEM((POOL_HALO + ts, D_POOL), F32), pltpu.VMEM((ts, D_ATTN), F32)],
        compiler_params=_cparams(1),
        name="inproj",
    )(x2, mod3, g1.reshape(1, d), win_b, gq.reshape(1, D_ATTN), gk.reshape(1, D_ATTN), ones_bd,
      _pool_fold(pool_w, pool_scale, w_bp))


def _attn_kernel(q_ref, qn_ref, k_ref, vt_ref, bias_ref, lq1_ref, lk1_ref, lq2_ref, lk2_ref, subg_ref,
                 *rest, t, lambda_init, cast_groups):
    n_in, n_out = sum(cast_groups), len(cast_groups)
    cast_in = rest[:n_in]
    o_ref, clear_ref = rest[n_in:n_in + 2]
    cast_out = rest[n_in + 2:n_in + 2 + n_out]
    s_ref, mt_ref, m_ref, acc_ref = rest[n_in + 2 + n_out:]
    i = pl.program_id(2)
    last = pl.num_programs(2) - 1
    clear_ref[...] = jnp.zeros(clear_ref.shape, clear_ref.dtype)
    first = 0
    for size, dst in zip(cast_groups, cast_out):
        col = 0
        for src in cast_in[first:first + size]:
            dst[:, col:col + src.shape[1]] = src[...].astype(dst.dtype)
            col += src.shape[1]
        first += size

    def split_maps(q):
        lane = lax.broadcasted_iota(jnp.int32, q.shape, 1)
        zero = jnp.zeros_like(q)
        return jnp.where(lane < HEAD_DIM, q, zero), jnp.where(lane >= HEAD_DIM, q, zero)

    q_now = split_maps(q_ref[...])

    m_ref[...] = jnp.full(m_ref.shape, MASKED, F32)
    acc_ref[...] = jnp.zeros(acc_ref.shape, F32)

    def scores(j, bias, slot, qm=q_now):
        kt = k_ref[pl.ds(pl.multiple_of(j * t, t), t), :]
        for mp in range(2):
            s = lax.dot_general(kt, qm[mp], (((1,), (1,)), ((), ())), preferred_element_type=F32)
            if bias is not None:
                s = s + bias
            s_ref[slot, mp] = s
            mt_ref[slot, mp] = jnp.max(s, axis=0, keepdims=True)

    def accumulate(j, slot):
        vt = vt_ref[:, pl.ds(pl.multiple_of(j * t, t), t)]
        for mp in range(2):
            m_old = m_ref[mp]
            m_new = jnp.maximum(m_old, mt_ref[slot, mp])
            p = jnp.exp2(s_ref[slot, mp] - m_new).astype(BF16)
            acc_ref[mp] = (jnp.exp2(m_old - m_new) * acc_ref[mp]
                           + jnp.dot(vt, p, preferred_element_type=F32))
            m_ref[mp] = m_new

    def next_diagonal():
        nxt = jnp.minimum(i + 1, last)
        scores(nxt, bias_ref[1], 2, split_maps(qn_ref[...]))

    @pl.when(i == 0)
    def _():
        scores(0, bias_ref[1], 0)

    @pl.when(i >= 1)
    def _():
        scores(i - 1, bias_ref[0], 1)
        accumulate(i, 2)

    n_far = jnp.maximum(i - 1, 0)

    def pair(kk):
        j = i - 1 - 2 * kk
        scores(j - 1, None, 0)
        accumulate(j, 1)
        scores(j - 2, None, 1)
        accumulate(j - 1, 0)

    def oct_body(oo, carry):
        for u in range(4):
            pair(4 * oo + u)
        return carry

    n_pairs = n_far // 2
    lax.fori_loop(0, n_pairs // 4, oct_body, 0)

    @pl.when(n_pairs % 4 >= 2)
    def _():
        pair(n_pairs // 4 * 4)
        pair(n_pairs // 4 * 4 + 1)

    @pl.when(n_pairs % 2 == 1)
    def _():
        pair(n_pairs - 1)

    def finish(slot):
        next_diagonal()
        accumulate(0, slot)
        lam = (jnp.exp(jnp.sum(lq1_ref[...] * lk1_ref[...], axis=1, keepdims=True))
               - jnp.exp(jnp.sum(lq2_ref[...] * lk2_ref[...], axis=1, keepdims=True)) + lambda_init)
        o1 = acc_ref[0, 0:D_HEAD_V, :] / acc_ref[0, D_HEAD_V:D_HEAD_V + 1, :]
        o2 = acc_ref[1, 0:D_HEAD_V, :] / acc_ref[1, D_HEAD_V:D_HEAD_V + 1, :]
        ot = o1 - lam * o2
        ot = ot * lax.rsqrt(jnp.mean(ot * ot, axis=0, keepdims=True) + RMS_EPS)
        ot = ot * subg_ref[...] * (1.0 - lambda_init)
        o_ref[...] = ot.T.astype(BF16)

    @pl.when(n_far % 2 == 1)
    def _():
        scores(0, None, 0)
        accumulate(1, 1)
        finish(0)

    @pl.when(jnp.logical_and(i >= 1, n_far % 2 == 0))
    def _():
        finish(1)

    @pl.when(i == 0)
    def _():
        finish(0)


def _attention(q, k, vt, bias, lq1, lk1, lq2, lk2, subln_g, *, t, lambda_init, clear_shape, to_bf16):
    bsz, seq, _ = q.shape
    nq = seq // t
    assert t + 1 >= LOG_BUCKET_STARTS[-1], "keys two tiles back must all fall in the last distance bucket"
    steps = bsz * N_HEADS * nq
    step_id = lambda b, h, i: ((b * N_HEADS + h) * nq + i, 0)
    clear_rows = clear_shape[0] // steps
    assert clear_rows * steps == clear_shape[0] and clear_rows % SUBLANES == 0, (clear_shape, steps)
    cast_in_specs, cast_out_specs, cast_out_shapes = [], [], []
    for group in to_bf16:
        rows = group[0].shape[0] // steps
        assert rows * steps == group[0].shape[0] and rows % (2 * SUBLANES) == 0, (group[0].shape, steps)
        cols = sum(a.shape[1] for a in group)
        cast_in_specs += [pl.BlockSpec((rows, a.shape[1]), step_id) for a in group]
        cast_out_specs.append(pl.BlockSpec((rows, cols), step_id))
        cast_out_shapes.append(jax.ShapeDtypeStruct((group[0].shape[0], cols), BF16))
    kern = functools.partial(_attn_kernel, t=t, lambda_init=lambda_init,
                             cast_groups=tuple(len(g) for g in to_bf16))
    vec = lambda a: a.reshape(1, HEAD_DIM)
    return pl.pallas_call(
        kern,
        out_shape=(jax.ShapeDtypeStruct((bsz, seq, D_ATTN), BF16),
                   jax.ShapeDtypeStruct(clear_shape, F32),
                   *cast_out_shapes),
        grid=(bsz, N_HEADS, seq // t),
        in_specs=[pl.BlockSpec((None, t, D_HEAD_V), lambda b, h, i: (b, i, h)),
                  pl.BlockSpec((None, t, D_HEAD_V), lambda b, h, i: (b, jnp.minimum(i + 1, nq - 1), h)),
                  pl.BlockSpec((None, seq, D_HEAD_V), lambda b, h, i: (b, 0, h)),
                  pl.BlockSpec((None, None, V_ROWS, seq), lambda b, h, i: (b, h, 0, 0)),
                  pl.BlockSpec((None, 2, t, t), lambda b, h, i: (h, 0, 0, 0)),
                  _const_spec((1, HEAD_DIM)), _const_spec((1, HEAD_DIM)),
                  _const_spec((1, HEAD_DIM)), _const_spec((1, HEAD_DIM)),
                  _const_spec((D_HEAD_V, 1)), *cast_in_specs],
        out_specs=(pl.BlockSpec((None, t, D_HEAD_V), lambda b, h, i: (b, i, h)),
                   pl.BlockSpec((clear_rows, clear_shape[1]), step_id), *cast_out_specs),
        scratch_shapes=[pltpu.VMEM((3, 2, t, t), F32),
                        pltpu.VMEM((3, 2, 1, t), F32),
                        pltpu.VMEM((2, 1, t), F32),
                        pltpu.VMEM((2, V_ROWS, t), F32)],
        compiler_params=_cparams(3),
        name="attn",
    )(q, q, k, vt, bias, vec(lq1), vec(lk1), vec(lq2), vec(lk2), subln_g.reshape(D_HEAD_V, 1),
      *[a for group in to_bf16 for a in group])


def _first_max(rows):
    best = rows[0]
    for r in rows[1:]:
        best = jnp.maximum(best, r)
    idx = jnp.full(best.shape, len(rows) - 1, jnp.int32)
    for j in range(len(rows) - 2, -1, -1):
        idx = jnp.where(rows[j] == best, j, idx)
    return best, idx


def _post_kernel(o_ref, sga_ref, pc_ref, x_ref, mod_ref, g2_ref, wba_ref, wout_ref, wr_ref, br_ref,
                 tri_ref, x1_ref, h2p_ref, route_ref, cnt_ref, carry_ref):
    @pl.when(pl.program_id(0) == 0)
    def _():
        carry_ref[...] = jnp.zeros(carry_ref.shape, F32)

    ya = jnp.dot(o_ref[...], wba_ref[...], preferred_element_type=F32)
    merged = sga_ref[...].astype(F32) * ya + pc_ref[...].astype(F32)
    z = jnp.dot(merged.astype(BF16), wout_ref[...], preferred_element_type=F32)
    x1 = x_ref[...] + mod_ref[2:3, :] * z
    x1_ref[...] = x1
    y = x1 * lax.rsqrt(jnp.mean(x1 * x1, axis=-1, keepdims=True) + RMS_EPS)
    h2 = y * (g2_ref[...] * (1.0 + mod_ref[4:5, :])) + mod_ref[3:4, :]
    hi = h2.astype(BF16)
    lo = (h2 - hi.astype(F32)).astype(BF16)

    nt = (((1,), (1,)), ((), ()))
    a = lax.dot_general(wr_ref[...], hi, nt, preferred_element_type=F32)
    b = lax.dot_general(wr_ref[0:ROUTER_ROWS, :], lo, nt, preferred_element_type=F32)
    logits = a[0:ROUTER_ROWS] + a[ROUTER_ROWS:2 * ROUTER_ROWS] + b + br_ref[...]

    gl = [logits[g:g + 1, :] for g in range(N_EXPERT_GROUPS)]
    gmax, gidx = _first_max(gl)
    gsum = gl[0] * 0.0
    for r in gl:
        gsum = gsum + jnp.exp(r - gmax)
    g_val = 1.0 / gsum
    es = []
    for r in range(EXPERTS_PER_GROUP):
        sel = jnp.zeros_like(gmax)
        for g in range(N_EXPERT_GROUPS):
            row = EXPERT_ROW0 + g * EXPERTS_PER_GROUP + r
            sel = jnp.where(gidx == g, logits[row:row + 1, :], sel)
        es.append(sel)
    e1, i1 = _first_max(es)
    rest = [jnp.where(i1 == r, -jnp.inf, es[r]) for r in range(EXPERTS_PER_GROUP)]
    e2, i2 = _first_max(rest)
    r21 = jnp.exp(e2 - e1)
    w1 = g_val / (1.0 + r21)
    w2 = g_val * r21 / (1.0 + r21)

    first = i1 < i2
    e_lo = jnp.where(first, i1, i2)
    e_hi = jnp.where(first, i2, i1)
    pair = jnp.zeros_like(e_lo)
    for p in range(1, PAIRS_PER_GROUP):
        pair = jnp.where(jnp.logical_and(e_lo == PAIR_LO[p], e_hi == PAIR_HI[p]), p, pair)
    bucket = gidx * PAIRS_PER_GROUP + pair
    w_lo = jnp.where(first, w1, w2)
    w_hi = jnp.where(first, w2, w1)

    ts = bucket.shape[1]
    brow = lax.broadcasted_iota(jnp.int32, (ROUTER_ROWS, ts), 0)
    onehot = brow == bucket
    prefix = jnp.dot(jnp.where(onehot, 1.0, 0.0).astype(BF16), tri_ref[...],
                     preferred_element_type=F32)
    carry = carry_ref[...]
    rank = jnp.sum(jnp.where(onehot, prefix + carry, 0.0), axis=0, keepdims=True) - 1.0
    carry = carry + prefix[:, ts - 1:ts]
    carry_ref[...] = carry
    cnt_ref[...] = jnp.broadcast_to(carry, cnt_ref.shape).astype(jnp.int32)
    chunks = ts // LANES
    rank_i = rank.astype(jnp.int32)
    for j in range(chunks):
        route_ref[j:j + 1, :] = bucket[:, j * LANES:(j + 1) * LANES]
        route_ref[chunks + j:chunks + j + 1, :] = rank_i[:, j * LANES:(j + 1) * LANES]

    arow = lax.broadcasted_iota(jnp.int32, (LANES, ts), 0)
    aux_t = jnp.where(arow == 0, w_lo, jnp.where(arow == 1, w_hi, 0.0))
    d = h2.shape[1]
    h2p_ref[:, 0:d] = h2
    h2p_ref[:, d:d + LANES] = aux_t.T


def _post(o2, sga, pc, x2, mod3, g2, w_ba, w_out, wg_r, bg_r, we_r, be_r, *, seq, ts):
    n, d = x2.shape
    tiles_per_batch = seq // ts
    steps = n // ts
    wr = jnp.zeros((ROUTER_ROWS, d), F32)
    wr = wr.at[0:N_EXPERT_GROUPS].set(wg_r.T).at[EXPERT_ROW0:EXPERT_ROW0 + N_EXPERTS].set(we_r.T)
    wr_hi = wr.astype(BF16)
    wr_lo = (wr - wr_hi.astype(F32)).astype(BF16)
    br = jnp.zeros((ROUTER_ROWS, 1), F32)
    br = br.at[0:N_EXPERT_GROUPS, 0].set(bg_r).at[EXPERT_ROW0:EXPERT_ROW0 + N_EXPERTS, 0].set(be_r)
    tok = lambda i: (i, 0)
    tidx = jnp.arange(ts)
    tri = (tidx[:, None] <= tidx[None, :]).astype(BF16)
    return pl.pallas_call(
        _post_kernel,
        out_shape=(jax.ShapeDtypeStruct((n, d), F32),
                   jax.ShapeDtypeStruct((n, d + LANES), F32),
                   jax.ShapeDtypeStruct((steps, 2 * (ts // LANES), LANES), jnp.int32),
                   jax.ShapeDtypeStruct((ROUTER_ROWS, LANES), jnp.int32)),
        grid=(steps,),
        in_specs=[pl.BlockSpec((ts, D_ATTN), tok),
                  pl.BlockSpec((ts, d), tok),
                  pl.BlockSpec((ts, d), tok),
                  pl.BlockSpec((ts, d), tok),
                  pl.BlockSpec((None, 6, d), lambda i: (i // tiles_per_batch, 0, 0)),
                  _const_spec((1, d)),
                  _const_spec((D_ATTN, d)),
                  _const_spec((d, d)),
                  _const_spec((2 * ROUTER_ROWS, d)),
                  _const_spec((ROUTER_ROWS, 1)),
                  _const_spec((ts, ts))],
        out_specs=(pl.BlockSpec((ts, d), tok), pl.BlockSpec((ts, d + LANES), tok),
                   pl.BlockSpec((None, 2 * (ts // LANES), LANES), lambda i: (i, 0, 0)),
                   pl.BlockSpec((ROUTER_ROWS, LANES), lambda i: (0, 0))),
        scratch_shapes=[pltpu.VMEM((ROUTER_ROWS, 1), F32)],
        compiler_params=_cparams(1),
        name="post",
    )(o2, sga, pc, x2, mod3, g2.reshape(1, d), w_ba.astype(BF16), w_out.astype(BF16),
      jnp.concatenate([wr_hi, wr_lo], axis=0), br, tri)


def _group_sublane(row):
    return lax.shift_right_logical(row, SUBLANES.bit_length() - 1), row & (SUBLANES - 1)


def _row_copy(src, src_row, dst, dst_row, sem):
    sg, ss = src_row
    dg, ds = dst_row
    return pltpu.make_async_copy(src.at[sg, pl.ds(ss, 1), :], dst.at[dg, pl.ds(ds, 1), :], sem)


def _dispatch_kernel(pos_ref, h_ref, init_ref, hs_ref, sem, *, ts):
    del init_ref
    base = pl.program_id(0) * ts

    def start(g, carry):
        for u in range(SUBLANES):
            p = pos_ref[base + g * SUBLANES + u]
            _row_copy(h_ref, (g, u), hs_ref, _group_sublane(p), sem).start(priority=u % 2)
        return carry

    def wait(g, carry):
        for u in range(SUBLANES):
            _row_copy(h_ref, (0, 0), hs_ref, (0, 0), sem).wait()
        return carry

    lax.fori_loop(0, ts // SUBLANES, start, 0)
    lax.fori_loop(0, ts // SUBLANES, wait, 0)


def _dispatch(pos, h2p, cleared, *, ts):
    n, w = h2p.shape
    n_rows = cleared.shape[0]
    return pl.pallas_call(
        functools.partial(_dispatch_kernel, ts=ts),
        out_shape=jax.ShapeDtypeStruct((n_rows // SUBLANES, SUBLANES, w), h2p.dtype),
        grid_spec=pltpu.PrefetchScalarGridSpec(
            num_scalar_prefetch=1,
            grid=(n // ts,),
            in_specs=[pl.BlockSpec((ts // SUBLANES, SUBLANES, w), lambda i, pos: (i, 0, 0)),
                      pl.BlockSpec(memory_space=pl.ANY)],
            out_specs=pl.BlockSpec(memory_space=pl.ANY),
            scratch_shapes=[pltpu.SemaphoreType.DMA]),
        input_output_aliases={2: 0},
        compiler_params=_cparams(1),
        name="dispatch",
    )(pos, h2p.reshape(n // SUBLANES, SUBLANES, w),
      cleared.reshape(n_rows // SUBLANES, SUBLANES, w)).reshape(n_rows, w)


def _experts_kernel(ea_ref, eb_ref, valid_ref, hs_ref, wgua_ref, wda_ref, wgub_ref, wdb_ref, ys_ref):
    del ea_ref, eb_ref
    rows = valid_ref[pl.program_id(0)]
    tm = hs_ref.shape[0]
    d = hs_ref.shape[1] - LANES

    def run(m):
        h = hs_ref[0:m, 0:d].astype(BF16)
        aux = hs_ref[0:m, d:d + LANES]

        def expert(wgu_ref, wd_ref):
            gu = jnp.dot(h, wgu_ref[...], preferred_element_type=F32)
            de = gu.shape[1] // 2
            a, b = gu[:, :de], gu[:, de:]
            hid = (a * jax.nn.sigmoid(a)) * b
            return jnp.dot(hid.astype(BF16), wd_ref[...], preferred_element_type=F32)

        ys_ref[0:m, :] = (aux[:, 0:1] * expert(wgua_ref, wda_ref)
                          + aux[:, 1:2] * expert(wgub_ref, wdb_ref))
        if m < tm:
            ys_ref[m:tm, :] = jnp.zeros((tm - m, ys_ref.shape[1]), ys_ref.dtype)

    @pl.when(rows > tm // 2)
    def _():
        run(tm)

    @pl.when(jnp.logical_and(rows > 0, rows <= tm // 2))
    def _():
        run(tm // 2)

    @pl.when(rows == 0)
    def _():
        ys_ref[...] = jnp.zeros(ys_ref.shape, ys_ref.dtype)


def _experts(tile_ea, tile_eb, tile_valid, hs, w_gate_up, w_down, *, tm):
    n_rows, w = hs.shape
    _, de, d = w_down.shape
    ea = lambda t, ea_r, eb_r, v_r: (ea_r[t], 0, 0)
    eb = lambda t, ea_r, eb_r, v_r: (eb_r[t], 0, 0)
    row = lambda t, ea_r, eb_r, v_r: (t, 0)
    return pl.pallas_call(
        _experts_kernel,
        out_shape=jax.ShapeDtypeStruct((n_rows, d), F32),
        grid_spec=pltpu.PrefetchScalarGridSpec(
            num_scalar_prefetch=3,
            grid=(n_rows // tm,),
            in_specs=[pl.BlockSpec((tm, w), row),
                      pl.BlockSpec((None, d, 2 * de), ea), pl.BlockSpec((None, de, d), ea),
                      pl.BlockSpec((None, d, 2 * de), eb), pl.BlockSpec((None, de, d), eb)],
            out_specs=pl.BlockSpec((tm, d), row)),
        compiler_params=_cparams(1),
        name="experts",
    )(tile_ea, tile_eb, tile_valid, hs, w_gate_up, w_down, w_gate_up, w_down)


def _combine_kernel(pos_ref, ys_ref, x1_ref, mod_ref, out_ref, ybuf_ref, sem, *, ts):
    i = pl.program_id(0)
    groups = ts // SUBLANES

    def gather(tile, slot):
        base = tile * ts

        def start(g, carry):
            for u in range(SUBLANES):
                p = pos_ref[base + g * SUBLANES + u]
                _row_copy(ys_ref, _group_sublane(p), ybuf_ref.at[slot], (g, u),
                          sem.at[slot]).start(priority=u % 2)
            return carry

        lax.fori_loop(0, groups, start, 0)

    def wait_all(slot):
        def wait(g, carry):
            for u in range(SUBLANES):
                _row_copy(ys_ref, (0, 0), ybuf_ref.at[slot], (0, 0), sem.at[slot]).wait()
            return carry

        lax.fori_loop(0, groups, wait, 0)

    def step(slot):
        @pl.when(i + 1 < pl.num_programs(0))
        def _():
            gather(i + 1, 1 - slot)

        wait_all(slot)
        y = ybuf_ref[slot].reshape(ts, ybuf_ref.shape[-1])
        out_ref[...] = x1_ref[...] + mod_ref[5:6, :] * y

    @pl.when(i == 0)
    def _():
        gather(0, 0)

    @pl.when(i % 2 == 0)
    def _():
        step(0)

    @pl.when(i % 2 == 1)
    def _():
        step(1)


def _combine(pos, ys, x1, mod3, *, seq, ts):
    n, d = x1.shape
    tiles_per_batch = seq // ts
    return pl.pallas_call(
        functools.partial(_combine_kernel, ts=ts),
        out_shape=jax.ShapeDtypeStruct((n, d), F32),
        grid_spec=pltpu.PrefetchScalarGridSpec(
            num_scalar_prefetch=1,
            grid=(n // ts,),
            in_specs=[pl.BlockSpec(memory_space=pl.ANY),
                      pl.BlockSpec((ts, d), lambda i, pos: (i, 0)),
                      pl.BlockSpec((None, 6, d), lambda i, pos: (i // tiles_per_batch, 0, 0))],
            out_specs=pl.BlockSpec((ts, d), lambda i, pos: (i, 0)),
            scratch_shapes=[pltpu.VMEM((2, ts // SUBLANES, SUBLANES, d), F32),
                            pltpu.SemaphoreType.DMA((2,))]),
        compiler_params=_cparams(1),
        name="combine",
    )(pos, ys.reshape(ys.shape[0] // SUBLANES, SUBLANES, d), x1, mod3)


def _pos_kernel(start_ref, route_ref, pos_ref):
    chunks = pos_ref.shape[1]
    bucket = route_ref[:, 0:chunks, :]
    pos = route_ref[:, chunks:2 * chunks, :]
    for b in range(N_ROUTE_BUCKETS):
        pos = pos + jnp.where(bucket == b, start_ref[b], 0)
    pos_ref[...] = pos


def _route_tables(route, counts, n_tiles, tm):
    steps, rows, _ = route.shape
    cnt = counts[:N_ROUTE_BUCKETS, 0]
    padded = (cnt + tm - 1) // tm * tm
    end = jnp.cumsum(padded)
    pos = pl.pallas_call(
        _pos_kernel,
        out_shape=jax.ShapeDtypeStruct((steps, rows // 2, LANES), jnp.int32),
        in_specs=[pl.BlockSpec(memory_space=pltpu.SMEM), pl.BlockSpec(memory_space=pltpu.VMEM)],
        out_specs=pl.BlockSpec(memory_space=pltpu.VMEM),
        name="pos",
    )(end - padded, route).reshape(-1)
    tiles_used = end[-1] // tm
    tile = jnp.arange(n_tiles, dtype=jnp.int32)
    valid = tile < tiles_used
    first_row = jnp.minimum(tile, tiles_used - 1) * tm
    tile_bucket = jnp.sum((end[None, :] <= first_row[:, None]).astype(jnp.int32), axis=1)
    tile_bucket = jnp.minimum(tile_bucket, N_ROUTE_BUCKETS - 1)
    group = tile_bucket // PAIRS_PER_GROUP
    pair = tile_bucket % PAIRS_PER_GROUP
    tile_ea = group * EXPERTS_PER_GROUP + jnp.asarray(PAIR_LO, jnp.int32)[pair]
    tile_eb = group * EXPERTS_PER_GROUP + jnp.asarray(PAIR_HI, jnp.int32)[pair]
    real_end = (end - padded + cnt)[tile_bucket]
    tile_rows = jnp.where(valid, jnp.clip(real_end - tile * tm, 0, tm), 0)
    return pos.astype(jnp.int32), tile_ea, tile_eb, tile_rows.astype(jnp.int32)


def _tile(seq, pref):
    t = min(pref, seq)
    assert seq % t == 0 and t % LANES == 0, (seq, t)
    return t


def kernel(x, c, rel_bias, ada_w, ada_b, norm1_g, w_in, q_norm_g, k_norm_g, lambda_q1, lambda_k1,
           lambda_q2, lambda_k2, subln_g, w_branch_attn, pool_w, pool_scale, w_branch_pool, w_out,
           norm2_g, router_group_w, router_group_b, router_expert_w, router_expert_b,
           expert_w_gate, expert_w_up, expert_w_down):
    bsz, seq, d = x.shape
    n = bsz * seq
    ts = _tile(seq, TOKEN_TILE)
    t_attn = _tile(seq, ATTN_TILE)
    tr = _tile(seq, ROW_COPY_TILE)
    tm = MOE_TILE
    n_tiles = -(-(n + N_ROUTE_BUCKETS * (tm - 1)) // tm)
    attn_steps = bsz * N_HEADS * (seq // t_attn)
    while (n_tiles * tm) % (attn_steps * SUBLANES):
        n_tiles += 1
    bias = _bias_tiles(rel_bias, t_attn)
    x2 = x.reshape(n, d)
    for l in range(ada_w.shape[0]):
        lambda_init = 0.8 - 0.6 * math.exp(-0.3 * l)
        mod3 = _ada(c, ada_w[l], ada_b[l]).reshape(bsz, 6, d)
        gq = jnp.tile(q_norm_g[l], D_ATTN // HEAD_DIM) * (HEAD_DIM ** -0.5 * LOG2E)
        gk = jnp.tile(k_norm_g[l], D_ATTN // HEAD_DIM)
        q, k, vt, sga, pc = _inproj(x2, mod3, norm1_g[l], w_in[l], gq, gk, pool_w[l], pool_scale[l],
                                    w_branch_pool[l], bsz=bsz, seq=seq, ts=_tile(seq, INPROJ_TILE))
        flat = lambda w: w.reshape(-1, w.shape[-1])
        o, cleared, wgu_b, wd_b = _attention(
            q.reshape(bsz, seq, D_ATTN), k.reshape(bsz, seq, D_ATTN), vt, bias,
            lambda_q1[l], lambda_k1[l], lambda_q2[l], lambda_k2[l], subln_g[l],
            t=t_attn, lambda_init=lambda_init, clear_shape=(n_tiles * tm, d + LANES),
            to_bf16=[[flat(expert_w_gate[l]), flat(expert_w_up[l])], [flat(expert_w_down[l])]])
        n_exp, _, de = expert_w_gate[l].shape
        wgu_b = wgu_b.reshape(n_exp, d, 2 * de)
        wd_b = wd_b.reshape(n_exp, de, d)
        x1, h2p, route, counts = _post(
            o.reshape(n, D_ATTN), sga, pc, x2, mod3, norm2_g[l], w_branch_attn[l], w_out[l],
            router_group_w[l], router_group_b[l], router_expert_w[l], router_expert_b[l],
            seq=seq, ts=ts)
        pos, tile_ea, tile_eb, tile_valid = _route_tables(route, counts, n_tiles, tm)
        hs = _dispatch(pos, h2p, cleared, ts=tr)
        ys = _experts(tile_ea, tile_eb, tile_valid, hs, wgu_b, wd_b, tm=tm)
        x2 = _combine(pos, ys, x1, mod3, seq=seq, ts=tr)
    return x2.reshape(bsz, seq, d)
```

```python
import functools
import math

import jax
import jax.numpy as jnp
from jax import lax
from jax.experimental import pallas as pl
from jax.experimental.pallas import tpu as pltpu

F32 = jnp.float32
BF16 = jnp.bfloat16

CHUNK = 64
N_HEADS = 4
HEAD_DIM = 64
D_HEAD_V = 2 * HEAD_DIM
V_ROWS = D_HEAD_V + 16
D_ATTN = N_HEADS * D_HEAD_V
POOL_WINDOWS = (2, 4, 8, 16)
POOL_GROUP_DIM = 128
D_POOL = len(POOL_WINDOWS) * POOL_GROUP_DIM
POOL_HALO = 16
N_BUCKETS = 32
MAX_DISTANCE = 128
N_EXPERT_GROUPS = 4
EXPERTS_PER_GROUP = 4
N_EXPERTS = N_EXPERT_GROUPS * EXPERTS_PER_GROUP
PAIRS_PER_GROUP = 6
PAIR_LO = (0, 0, 1, 1, 0, 2)
PAIR_HI = (1, 2, 2, 3, 3, 3)
N_ROUTE_BUCKETS = N_EXPERT_GROUPS * PAIRS_PER_GROUP
RMS_EPS = 1e-6
LOG2E = math.log2(math.e)
MASKED = -1e30

LANES = 128
SUBLANES = 8
ROUTER_ROWS = 32
EXPERT_ROW0 = 8

VMEM_LIMIT = 56 * 1024 * 1024
TOKEN_TILE = 1024
INPROJ_TILE = 1024
ATTN_TILE = 512
MOE_TILE = 512
ROW_COPY_TILE = 1024
EXPERT_ROW_STEPS = 4


def _cparams(n_axes):
    return pltpu.CompilerParams(dimension_semantics=("arbitrary",) * n_axes,
                                vmem_limit_bytes=VMEM_LIMIT)


def _const_spec(shape):
    nd = len(shape)
    return pl.BlockSpec(shape, lambda *_: (0,) * nd, pipeline_mode=pl.Buffered(1))


def _ada_kernel(ct_ref, w_ref, b_ref, o_ref):
    @pl.when(pl.program_id(0) == 0)
    def _():
        o_ref[...] = jnp.broadcast_to(b_ref[...], o_ref.shape)

    ct = ct_ref[...]
    s = ct * jax.nn.sigmoid(ct)
    w = w_ref[...]
    rows = [jnp.sum(w * s[:, b:b + 1], axis=0, keepdims=True) for b in range(ct.shape[1])]
    o_ref[...] += jnp.concatenate(rows, axis=0)


def _ada(c, w, b):
    bsz, d = c.shape
    n = w.shape[1]
    rows = 256
    return pl.pallas_call(
        _ada_kernel,
        out_shape=jax.ShapeDtypeStruct((bsz, n), F32),
        grid=(d // rows,),
        in_specs=[pl.BlockSpec((rows, bsz), lambda j: (j, 0)),
                  pl.BlockSpec((rows, n), lambda j: (j, 0)),
                  pl.BlockSpec((1, n), lambda j: (0, 0))],
        out_specs=pl.BlockSpec((bsz, n), lambda j: (0, 0)),
        compiler_params=_cparams(1),
        name="ada",
    )(c.T, w, b.reshape(1, n))


def _log_bucket_starts():
    nb = N_BUCKETS // 2
    max_exact = nb // 2
    m = nb - max_exact
    ratio = MAX_DISTANCE // max_exact
    starts = []
    for k in range(1, m):
        n = max_exact
        while n ** m < max_exact ** m * ratio ** k:
            n += 1
        starts.append(n)
    return tuple(starts)


LOG_BUCKET_STARTS = _log_bucket_starts()


def _bias_kernel(rb_ref, o_ref, *, t):
    h = pl.program_id(0)
    kind = pl.program_id(1)
    nb = N_BUCKETS // 2
    max_exact = nb // 2
    kpos = lax.broadcasted_iota(jnp.int32, (t, t), 0)
    qpos = lax.broadcasted_iota(jnp.int32, (t, t), 1)
    rel = kpos - qpos - jnp.where(kind == 0, t, 0)
    n = jnp.abs(rel)

    def table(first):
        val = jnp.full((t, t), rb_ref[first + nb - 1, h], F32)
        for k in range(len(LOG_BUCKET_STARTS) - 1, -1, -1):
            val = jnp.where(n < LOG_BUCKET_STARTS[k], rb_ref[first + max_exact + k, h], val)
        for j in range(max_exact - 1, -1, -1):
            val = jnp.where(n == j, rb_ref[first + j, h], val)
        return val

    far = rb_ref[nb - 1, h]

    @pl.when(kind == 0)
    def _():
        o_ref[...] = (table(0) - far) * LOG2E

    @pl.when(kind == 1)
    def _():
        bias = jnp.where(rel > 0, table(nb), table(0))
        shift = CHUNK.bit_length() - 1
        hidden = (kpos >> shift) > (qpos >> shift)
        o_ref[...] = jnp.where(hidden, MASKED, (bias - far) * LOG2E)


def _bias_tiles(rel_bias, t):
    return pl.pallas_call(
        functools.partial(_bias_kernel, t=t),
        out_shape=jax.ShapeDtypeStruct((N_HEADS, 2, t, t), F32),
        grid=(N_HEADS, 2),
        in_specs=[pl.BlockSpec(memory_space=pltpu.SMEM)],
        out_specs=pl.BlockSpec((None, None, t, t), lambda h, j: (h, j, 0, 0)),
        compiler_params=_cparams(2),
        name="bias_tiles",
    )(rel_bias)


def _group_rms(xc, ones_blockdiag, gain):
    ssq = jnp.dot((xc * xc).astype(BF16), ones_blockdiag, preferred_element_type=F32)
    return xc * lax.rsqrt(ssq * (1.0 / HEAD_DIM) + RMS_EPS) * gain


def _split_bf16(a):
    hi = a.astype(BF16)
    return hi, (a - hi.astype(F32)).astype(BF16)


def _pool_fold_kernel(pw_ref, ps_ref, wbp_ref, o_ref):
    a_hi, a_lo = _split_bf16(pw_ref[...] * ps_ref[...])
    b_hi, b_lo = _split_bf16(wbp_ref[...])
    dot = functools.partial(jnp.dot, preferred_element_type=F32)
    o_ref[...] = (dot(a_hi, b_hi) + dot(a_hi, b_lo) + dot(a_lo, b_hi)).astype(o_ref.dtype)


def _pool_fold(pool_w, pool_scale, w_bp):
    g, c, _ = pool_w.shape
    d = w_bp.shape[1]
    return pl.pallas_call(
        _pool_fold_kernel,
        out_shape=jax.ShapeDtypeStruct((g * c, d), BF16),
        grid=(g,),
        in_specs=[pl.BlockSpec((None, c, c), lambda i: (i, 0, 0)),
                  pl.BlockSpec((None, 1, c), lambda i: (i, 0, 0)),
                  pl.BlockSpec((c, d), lambda i: (i, 0))],
        out_specs=pl.BlockSpec((c, d), lambda i: (i, 0)),
        compiler_params=_cparams(1),
        name="pool_fold",
    )(pool_w, pool_scale.reshape(g, 1, c), w_bp)


def _inproj_kernel(x_ref, mod_ref, g1_ref, win_ref, gq_ref, gk_ref, ones_ref, wpool_ref,
                   q_ref, k_ref, vt_ref, sga_ref, pc_ref, ext_ref, v_ref,
                   *, ts, tiles_per_batch):
    tb = pl.program_id(0) % tiles_per_batch

    x = x_ref[...]
    y = x * lax.rsqrt(jnp.mean(x * x, axis=-1, keepdims=True) + RMS_EPS)
    h = y * (g1_ref[...] * (1.0 + mod_ref[1:2, :])) + mod_ref[0:1, :]
    hb = h.astype(BF16)

    def proj(c0, c1):
        return jnp.dot(hb, win_ref[:, c0:c1], preferred_element_type=F32)

    ones_bd = ones_ref[...]
    q_ref[...] = _group_rms(proj(0, D_ATTN), ones_bd, gq_ref[...]).astype(BF16)
    k_ref[...] = _group_rms(proj(D_ATTN, 2 * D_ATTN), ones_bd, gk_ref[...]).astype(BF16)
    v_ref[...] = proj(2 * D_ATTN, 3 * D_ATTN)
    vt = v_ref[...].T.astype(BF16)
    for hd in range(N_HEADS):
        vt_ref[hd, 0:D_HEAD_V, :] = vt[hd * D_HEAD_V:(hd + 1) * D_HEAD_V, :]
        vt_ref[hd, D_HEAD_V:V_ROWS, :] = jnp.ones((V_ROWS - D_HEAD_V, ts), BF16)
    c_u = 3 * D_ATTN
    c_ga = c_u + D_POOL
    c_gp = c_ga + x.shape[1]
    sga_ref[...] = jax.nn.sigmoid(proj(c_ga, c_gp)).astype(BF16)

    u = proj(c_u, c_ga)

    @pl.when(tb == 0)
    def _():
        ext_ref[0:POOL_HALO, :] = jnp.zeros((POOL_HALO, D_POOL), F32)

    ext_ref[POOL_HALO:POOL_HALO + ts, :] = u
    row = lax.broadcasted_iota(jnp.int32, (ts, 1), 0) + tb * ts
    ys = []
    for g, w in enumerate(POOL_WINDOWS):
        c0 = g * POOL_GROUP_DIM
        ug = u[:, c0:c0 + POOL_GROUP_DIM]
        acc = ug
        for d in range(1, w):
            acc = acc + ext_ref[POOL_HALO - d:POOL_HALO - d + ts, c0:c0 + POOL_GROUP_DIM]
        cnt = jnp.minimum(row + 1, w).astype(F32)
        ys.append((acc / cnt - ug).astype(BF16))
    ypool = jnp.dot(jnp.concatenate(ys, axis=1), wpool_ref[...], preferred_element_type=F32)
    pc_ref[...] = (jax.nn.sigmoid(proj(c_gp, c_gp + x.shape[1])) * ypool).astype(BF16)
    ext_ref[0:POOL_HALO, :] = u[ts - POOL_HALO:ts, :]


def _inproj(x2, mod3, g1, w_in, gq, gk, pool_w, pool_scale, w_bp, *, bsz, seq, ts):
    n, d = x2.shape
    d_in = w_in.shape[1]
    tiles_per_batch = seq // ts
    win_b = w_in.astype(BF16)
    idx = jnp.arange(D_ATTN) // HEAD_DIM
    ones_bd = (idx[:, None] == idx[None, :]).astype(BF16)
    kern = functools.partial(_inproj_kernel, ts=ts, tiles_per_batch=tiles_per_batch)
    tok = lambda i: (i, 0)
    out_shape = (jax.ShapeDtypeStruct((n, D_ATTN), BF16),
                 jax.ShapeDtypeStruct((n, D_ATTN), BF16),
                 jax.ShapeDtypeStruct((bsz, N_HEADS, V_ROWS, seq), BF16),
                 jax.ShapeDtypeStruct((n, d), BF16),
                 jax.ShapeDtypeStruct((n, d), BF16))
    return pl.pallas_call(
        kern,
        out_shape=out_shape,
        grid=(n // ts,),
        in_specs=[pl.BlockSpec((ts, d), tok),
                  pl.BlockSpec((None, 6, d), lambda i: (i // tiles_per_batch, 0, 0)),
                  _const_spec((1, d)),
                  _const_spec((d, d_in)),
                  _const_spec((1, D_ATTN)),
                  _const_spec((1, D_ATTN)),
                  _const_spec((D_ATTN, D_ATTN)),
                  _const_spec((D_POOL, d))],
        out_specs=(pl.BlockSpec((ts, D_ATTN), tok),
                   pl.BlockSpec((ts, D_ATTN), tok),
                   pl.BlockSpec((None, N_HEADS, V_ROWS, ts),
                                lambda i: (i // tiles_per_batch, 0, 0, i % tiles_per_batch)),
                   pl.BlockSpec((ts, d), tok),
                   pl.BlockSpec((ts, d), tok)),
        scratch_shapes=[pltpu.VMEM((POOL_HALO + ts, D_POOL), F32), pltpu.VMEM((ts, D_ATTN), F32)],
        compiler_params=_cparams(1),
        name="inproj",
    )(x2, mod3, g1.reshape(1, d), win_b, gq.reshape(1, D_ATTN), gk.reshape(1, D_ATTN), ones_bd,
      _pool_fold(pool_w, pool_scale, w_bp))


def _attn_kernel(q_ref, qn_ref, k_ref, vt_ref, bias_ref, lq1_ref, lk1_ref, lq2_ref, lk2_ref, subg_ref,
                 *rest, t, lambda_init, cast_groups):
    n_in, n_out = sum(cast_groups), len(cast_groups)
    cast_in = rest[:n_in]
    o_ref, clear_ref = rest[n_in:n_in + 2]
    cast_out = rest[n_in + 2:n_in + 2 + n_out]
    s_ref, mt_ref, m_ref, acc_ref = rest[n_in + 2 + n_out:]
    i = pl.program_id(2)
    last = pl.num_programs(2) - 1
    clear_ref[...] = jnp.zeros(clear_ref.shape, clear_ref.dtype)
    first = 0
    for size, dst in zip(cast_groups, cast_out):
        col = 0
        for src in cast_in[first:first + size]:
            dst[:, col:col + src.shape[1]] = src[...].astype(dst.dtype)
            col += src.shape[1]
        first += size

    def split_maps(q):
        lane = lax.broadcasted_iota(jnp.int32, q.shape, 1)
        zero = jnp.zeros_like(q)
        return jnp.where(lane < HEAD_DIM, q, zero), jnp.where(lane >= HEAD_DIM, q, zero)

    q_now = split_maps(q_ref[...])

    m_ref[...] = jnp.full(m_ref.shape, MASKED, F32)
    acc_ref[...] = jnp.zeros(acc_ref.shape, F32)

    def scores(j, bias, slot, qm=q_now):
        kt = k_ref[pl.ds(pl.multiple_of(j * t, t), t), :]
        for mp in range(2):
            s = lax.dot_general(kt, qm[mp], (((1,), (1,)), ((), ())), preferred_element_type=F32)
            if bias is not None:
                s = s + bias
            s_ref[slot, mp] = s
            mt_ref[slot, mp] = jnp.max(s, axis=0, keepdims=True)

    def accumulate(j, slot):
        vt = vt_ref[:, pl.ds(pl.multiple_of(j * t, t), t)]
        for mp in range(2):
            m_old = m_ref[mp]
            m_new = jnp.maximum(m_old, mt_ref[slot, mp])
            p = jnp.exp2(s_ref[slot, mp] - m_new).astype(BF16)
            acc_ref[mp] = (jnp.exp2(m_old - m_new) * acc_ref[mp]
                           + jnp.dot(vt, p, preferred_element_type=F32))
            m_ref[mp] = m_new

    def next_diagonal():
        nxt = jnp.minimum(i + 1, last)
        scores(nxt, bias_ref[1], 2, split_maps(qn_ref[...]))

    @pl.when(i == 0)
    def _():
        scores(0, bias_ref[1], 0)

    @pl.when(i >= 1)
    def _():
        scores(i - 1, bias_ref[0], 1)
        accumulate(i, 2)

    n_far = jnp.maximum(i - 1, 0)

    def pair(kk):
        j = i - 1 - 2 * kk
        scores(j - 1, None, 0)
        accumulate(j, 1)
        scores(j - 2, None, 1)
        accumulate(j - 1, 0)

    def oct_body(oo, carry):
        for u in range(4):
            pair(4 * oo + u)
        return carry

    n_pairs = n_far // 2
    lax.fori_loop(0, n_pairs // 4, oct_body, 0)

    @pl.when(n_pairs % 4 >= 2)
    def _():
        pair(n_pairs // 4 * 4)
        pair(n_pairs // 4 * 4 + 1)

    @pl.when(n_pairs % 2 == 1)
    def _():
        pair(n_pairs - 1)

    def finish(slot):
        next_diagonal()
        accumulate(0, slot)
        lam = (jnp.exp(jnp.sum(lq1_ref[...] * lk1_ref[...], axis=1, keepdims=True))
               - jnp.exp(jnp.sum(lq2_ref[...] * lk2_ref[...], axis=1, keepdims=True)) + lambda_init)
        o1 = acc_ref[0, 0:D_HEAD_V, :] / acc_ref[0, D_HEAD_V:D_HEAD_V + 1, :]
        o2 = acc_ref[1, 0:D_HEAD_V, :] / acc_ref[1, D_HEAD_V:D_HEAD_V + 1, :]
        ot = o1 - lam * o2
        ot = ot * lax.rsqrt(jnp.mean(ot * ot, axis=0, keepdims=True) + RMS_EPS)
        ot = ot * subg_ref[...] * (1.0 - lambda_init)
        o_ref[...] = ot.T.astype(BF16)

    @pl.when(n_far % 2 == 1)
    def _():
        scores(0, None, 0)
        accumulate(1, 1)
        finish(0)

    @pl.when(jnp.logical_and(i >= 1, n_far % 2 == 0))
    def _():
        finish(1)

    @pl.when(i == 0)
    def _():
        finish(0)


def _attention(q, k, vt, bias, lq1, lk1, lq2, lk2, subln_g, *, t, lambda_init, clear_shape, to_bf16):
    bsz, seq, _ = q.shape
    nq = seq // t
    assert t + 1 >= LOG_BUCKET_STARTS[-1], "keys two tiles back must all fall in the last distance bucket"
    steps = bsz * N_HEADS * nq
    step_id = lambda b, h, i: ((b * N_HEADS + h) * nq + i, 0)
    clear_rows = clear_shape[0] // steps
    assert clear_rows * steps == clear_shape[0] and clear_rows % SUBLANES == 0, (clear_shape, steps)
    cast_in_specs, cast_out_specs, cast_out_shapes = [], [], []
    for group in to_bf16:
        rows = group[0].shape[0] // steps
        assert rows * steps == group[0].shape[0] and rows % (2 * SUBLANES) == 0, (group[0].shape, steps)
        cols = sum(a.shape[1] for a in group)
        cast_in_specs += [pl.BlockSpec((rows, a.shape[1]), step_id) for a in group]
        cast_out_specs.append(pl.BlockSpec((rows, cols), step_id))
        cast_out_shapes.append(jax.ShapeDtypeStruct((group[0].shape[0], cols), BF16))
    kern = functools.partial(_attn_kernel, t=t, lambda_init=lambda_init,
                             cast_groups=tuple(len(g) for g in to_bf16))
    vec = lambda a: a.reshape(1, HEAD_DIM)
    return pl.pallas_call(
        kern,
        out_shape=(jax.ShapeDtypeStruct((bsz, seq, D_ATTN), BF16),
                   jax.ShapeDtypeStruct(clear_shape, F32),
                   *cast_out_shapes),
        grid=(bsz, N_HEADS, seq // t),
        in_specs=[pl.BlockSpec((None, t, D_HEAD_V), lambda b, h, i: (b, i, h)),
                  pl.BlockSpec((None, t, D_HEAD_V), lambda b, h, i: (b, jnp.minimum(i + 1, nq - 1), h)),
                  pl.BlockSpec((None, seq, D_HEAD_V), lambda b, h, i: (b, 0, h)),
                  pl.BlockSpec((None, None, V_ROWS, seq), lambda b, h, i: (b, h, 0, 0)),
                  pl.BlockSpec((None, 2, t, t), lambda b, h, i: (h, 0, 0, 0)),
                  _const_spec((1, HEAD_DIM)), _const_spec((1, HEAD_DIM)),
                  _const_spec((1, HEAD_DIM)), _const_spec((1, HEAD_DIM)),
                  _const_spec((D_HEAD_V, 1)), *cast_in_specs],
        out_specs=(pl.BlockSpec((None, t, D_HEAD_V), lambda b, h, i: (b, i, h)),
                   pl.BlockSpec((clear_rows, clear_shape[1]), step_id), *cast_out_specs),
        scratch_shapes=[pltpu.VMEM((3, 2, t, t), F32),
                        pltpu.VMEM((3, 2, 1, t), F32),
                        pltpu.VMEM((2, 1, t), F32),
                        pltpu.VMEM((2, V_ROWS, t), F32)],
        compiler_params=_cparams(3),
        name="attn",
    )(q, q, k, vt, bias, vec(lq1), vec(lk1), vec(lq2), vec(lk2), subln_g.reshape(D_HEAD_V, 1),
      *[a for group in to_bf16 for a in group])


def _first_max(rows):
    best = rows[0]
    for r in rows[1:]:
        best = jnp.maximum(best, r)
    idx = jnp.full(best.shape, len(rows) - 1, jnp.int32)
    for j in range(len(rows) - 2, -1, -1):
        idx = jnp.where(rows[j] == best, j, idx)
    return best, idx


def _post_kernel(o_ref, sga_ref, pc_ref, x_ref, mod_ref, g2_ref, wba_ref, wout_ref, wr_ref, br_ref,
                 tri_ref, x1_ref, h2p_ref, route_ref, cnt_ref, carry_ref):
    @pl.when(pl.program_id(0) == 0)
    def _():
        carry_ref[...] = jnp.zeros(carry_ref.shape, F32)

    ya = jnp.dot(o_ref[...], wba_ref[...], preferred_element_type=F32)
    merged = sga_ref[...].astype(F32) * ya + pc_ref[...].astype(F32)
    z = jnp.dot(merged.astype(BF16), wout_ref[...], preferred_element_type=F32)
    x1 = x_ref[...] + mod_ref[2:3, :] * z
    x1_ref[...] = x1
    y = x1 * lax.rsqrt(jnp.mean(x1 * x1, axis=-1, keepdims=True) + RMS_EPS)
    h2 = y * (g2_ref[...] * (1.0 + mod_ref[4:5, :])) + mod_ref[3:4, :]
    hi = h2.astype(BF16)
    lo = (h2 - hi.astype(F32)).astype(BF16)

    nt = (((1,), (1,)), ((), ()))
    a = lax.dot_general(wr_ref[...], hi, nt, preferred_element_type=F32)
    b = lax.dot_general(wr_ref[0:ROUTER_ROWS, :], lo, nt, preferred_element_type=F32)
    logits = a[0:ROUTER_ROWS] + a[ROUTER_ROWS:2 * ROUTER_ROWS] + b + br_ref[...]

    gl = [logits[g:g + 1, :] for g in range(N_EXPERT_GROUPS)]
    gmax, gidx = _first_max(gl)
    gsum = gl[0] * 0.0
    for r in gl:
        gsum = gsum + jnp.exp(r - gmax)
    g_val = 1.0 / gsum
    es = []
    for r in range(EXPERTS_PER_GROUP):
        sel = jnp.zeros_like(gmax)
        for g in range(N_EXPERT_GROUPS):
            row = EXPERT_ROW0 + g * EXPERTS_PER_GROUP + r
            sel = jnp.where(gidx == g, logits[row:row + 1, :], sel)
        es.append(sel)
    e1, i1 = _first_max(es)
    rest = [jnp.where(i1 == r, -jnp.inf, es[r]) for r in range(EXPERTS_PER_GROUP)]
    e2, i2 = _first_max(rest)
    r21 = jnp.exp(e2 - e1)
    w1 = g_val / (1.0 + r21)
    w2 = g_val * r21 / (1.0 + r21)

    first = i1 < i2
    e_lo = jnp.where(first, i1, i2)
    e_hi = jnp.where(first, i2, i1)
    pair = jnp.zeros_like(e_lo)
    for p in range(1, PAIRS_PER_GROUP):
        pair = jnp.where(jnp.logical_and(e_lo == PAIR_LO[p], e_hi == PAIR_HI[p]), p, pair)
    bucket = gidx * PAIRS_PER_GROUP + pair
    w_lo = jnp.where(first, w1, w2)
    w_hi = jnp.where(first, w2, w1)

    ts = bucket.shape[1]
    brow = lax.broadcasted_iota(jnp.int32, (ROUTER_ROWS, ts), 0)
    onehot = brow == bucket
    prefix = jnp.dot(jnp.where(onehot, 1.0, 0.0).astype(BF16), tri_ref[...],
                     preferred_element_type=F32)
    carry = carry_ref[...]
    rank = jnp.sum(jnp.where(onehot, prefix + carry, 0.0), axis=0, keepdims=True) - 1.0
    carry = carry + prefix[:, ts - 1:ts]
    carry_ref[...] = carry
    cnt_ref[...] = jnp.broadcast_to(carry, cnt_ref.shape).astype(jnp.int32)
    chunks = ts // LANES
    rank_i = rank.astype(jnp.int32)
    for j in range(chunks):
        route_ref[j:j + 1, :] = bucket[:, j * LANES:(j + 1) * LANES]
        route_ref[chunks + j:chunks + j + 1, :] = rank_i[:, j * LANES:(j + 1) * LANES]

    arow = lax.broadcasted_iota(jnp.int32, (LANES, ts), 0)
    aux_t = jnp.where(arow == 0, w_lo, jnp.where(arow == 1, w_hi, 0.0))
    d = h2.shape[1]
    h2p_ref[:, 0:d] = h2
    h2p_ref[:, d:d + LANES] = aux_t.T


def _post(o2, sga, pc, x2, mod3, g2, w_ba, w_out, wg_r, bg_r, we_r, be_r, *, seq, ts):
    n, d = x2.shape
    tiles_per_batch = seq // ts
    steps = n // ts
    wr = jnp.zeros((ROUTER_ROWS, d), F32)
    wr = wr.at[0:N_EXPERT_GROUPS].set(wg_r.T).at[EXPERT_ROW0:EXPERT_ROW0 + N_EXPERTS].set(we_r.T)
    wr_hi = wr.astype(BF16)
    wr_lo = (wr - wr_hi.astype(F32)).astype(BF16)
    br = jnp.zeros((ROUTER_ROWS, 1), F32)
    br = br.at[0:N_EXPERT_GROUPS, 0].set(bg_r).at[EXPERT_ROW0:EXPERT_ROW0 + N_EXPERTS, 0].set(be_r)
    tok = lambda i: (i, 0)
    tidx = jnp.arange(ts)
    tri = (tidx[:, None] <= tidx[None, :]).astype(BF16)
    return pl.pallas_call(
        _post_kernel,
        out_shape=(jax.ShapeDtypeStruct((n, d), F32),
                   jax.ShapeDtypeStruct((n, d + LANES), F32),
                   jax.ShapeDtypeStruct((steps, 2 * (ts // LANES), LANES), jnp.int32),
                   jax.ShapeDtypeStruct((ROUTER_ROWS, LANES), jnp.int32)),
        grid=(steps,),
        in_specs=[pl.BlockSpec((ts, D_ATTN), tok),
                  pl.BlockSpec((ts, d), tok),
                  pl.BlockSpec((ts, d), tok),
                  pl.BlockSpec((ts, d), tok),
                  pl.BlockSpec((None, 6, d), lambda i: (i // tiles_per_batch, 0, 0)),
                  _const_spec((1, d)),
                  _const_spec((D_ATTN, d)),
                  _const_spec((d, d)),
                  _const_spec((2 * ROUTER_ROWS, d)),
                  _const_spec((ROUTER_ROWS, 1)),
                  _const_spec((ts, ts))],
        out_specs=(pl.BlockSpec((ts, d), tok), pl.BlockSpec((ts, d + LANES), tok),
                   pl.BlockSpec((None, 2 * (ts // LANES), LANES), lambda i: (i, 0, 0)),
                   pl.BlockSpec((ROUTER_ROWS, LANES), lambda i: (0, 0))),
        scratch_shapes=[pltpu.VMEM((ROUTER_ROWS, 1), F32)],
        compiler_params=_cparams(1),
        name="post",
    )(o2, sga, pc, x2, mod3, g2.reshape(1, d), w_ba.astype(BF16), w_out.astype(BF16),
      jnp.concatenate([wr_hi, wr_lo], axis=0), br, tri)


def _group_sublane(row):
    return lax.shift_right_logical(row, SUBLANES.bit_length() - 1), row & (SUBLANES - 1)


def _row_copy(src, src_row, dst, dst_row, sem):
    sg, ss = src_row
    dg, ds = dst_row
    return pltpu.make_async_copy(src.at[sg, pl.ds(ss, 1), :], dst.at[dg, pl.ds(ds, 1), :], sem)


def _dispatch_kernel(pos_ref, h_ref, init_ref, hs_ref, sem, *, ts):
    del init_ref
    base = pl.program_id(0) * ts

    def start(g, carry):
        for u in range(SUBLANES):
            p = pos_ref[base + g * SUBLANES + u]
            _row_copy(h_ref, (g, u), hs_ref, _group_sublane(p), sem).start(priority=u % 2)
        return carry

    def wait(g, carry):
        for u in range(SUBLANES):
            _row_copy(h_ref, (0, 0), hs_ref, (0, 0), sem).wait()
        return carry

    lax.fori_loop(0, ts // SUBLANES, start, 0)
    lax.fori_loop(0, ts // SUBLANES, wait, 0)


def _dispatch(pos, h2p, cleared, *, ts):
    n, w = h2p.shape
    n_rows = cleared.shape[0]
    return pl.pallas_call(
        functools.partial(_dispatch_kernel, ts=ts),
        out_shape=jax.ShapeDtypeStruct((n_rows // SUBLANES, SUBLANES, w), h2p.dtype),
        grid_spec=pltpu.PrefetchScalarGridSpec(
            num_scalar_prefetch=1,
            grid=(n // ts,),
            in_specs=[pl.BlockSpec((ts // SUBLANES, SUBLANES, w), lambda i, pos: (i, 0, 0)),
                      pl.BlockSpec(memory_space=pl.ANY)],
            out_specs=pl.BlockSpec(memory_space=pl.ANY),
            scratch_shapes=[pltpu.SemaphoreType.DMA]),
        input_output_aliases={2: 0},
        compiler_params=_cparams(1),
        name="dispatch",
    )(pos, h2p.reshape(n // SUBLANES, SUBLANES, w),
      cleared.reshape(n_rows // SUBLANES, SUBLANES, w)).reshape(n_rows, w)


def _experts_kernel(ea_ref, eb_ref, valid_ref, hs_ref, wgua_ref, wda_ref, wgub_ref, wdb_ref, ys_ref):
    del ea_ref, eb_ref
    rows = valid_ref[pl.program_id(0)]
    tm = hs_ref.shape[0]
    d = hs_ref.shape[1] - LANES

    def run(m):
        h = hs_ref[0:m, 0:d].astype(BF16)
        aux = hs_ref[0:m, d:d + LANES]

        def expert(wgu_ref, wd_ref):
            gu = jnp.dot(h, wgu_ref[...], preferred_element_type=F32)
            de = gu.shape[1] // 2
            a, b = gu[:, :de], gu[:, de:]
            hid = (a * jax.nn.sigmoid(a)) * b
            return jnp.dot(hid.astype(BF16), wd_ref[...], preferred_element_type=F32)

        ys_ref[0:m, :] = (aux[:, 0:1] * expert(wgua_ref, wda_ref)
                          + aux[:, 1:2] * expert(wgub_ref, wdb_ref))
        if m < tm:
            ys_ref[m:tm, :] = jnp.zeros((tm - m, ys_ref.shape[1]), ys_ref.dtype)

    quarter = tm // EXPERT_ROW_STEPS
    for s in range(1, EXPERT_ROW_STEPS + 1):
        @pl.when(jnp.logical_and(rows > (s - 1) * quarter, rows <= s * quarter))
        def _(s=s):
            run(s * quarter)

    @pl.when(rows == 0)
    def _():
        ys_ref[...] = jnp.zeros(ys_ref.shape, ys_ref.dtype)


def _experts(tile_ea, tile_eb, tile_valid, hs, w_gate_up, w_down, *, tm):
    n_rows, w = hs.shape
    _, de, d = w_down.shape
    ea = lambda t, ea_r, eb_r, v_r: (ea_r[t], 0, 0)
    eb = lambda t, ea_r, eb_r, v_r: (eb_r[t], 0, 0)
    row = lambda t, ea_r, eb_r, v_r: (t, 0)
    return pl.pallas_call(
        _experts_kernel,
        out_shape=jax.ShapeDtypeStruct((n_rows, d), F32),
        grid_spec=pltpu.PrefetchScalarGridSpec(
            num_scalar_prefetch=3,
            grid=(n_rows // tm,),
            in_specs=[pl.BlockSpec((tm, w), row),
                      pl.BlockSpec((None, d, 2 * de), ea), pl.BlockSpec((None, de, d), ea),
                      pl.BlockSpec((None, d, 2 * de), eb), pl.BlockSpec((None, de, d), eb)],
            out_specs=pl.BlockSpec((tm, d), row)),
        compiler_params=_cparams(1),
        name="experts",
    )(tile_ea, tile_eb, tile_valid, hs, w_gate_up, w_down, w_gate_up, w_down)


def _combine_kernel(pos_ref, ys_ref, x1_ref, mod_ref, out_ref, ybuf_ref, sem, *, ts):
    i = pl.program_id(0)
    groups = ts // SUBLANES

    def gather(tile, slot):
        base = tile * ts

        def start(g, carry):
            for u in range(SUBLANES):
                p = pos_ref[base + g * SUBLANES + u]
                _row_copy(ys_ref, _group_sublane(p), ybuf_ref.at[slot], (g, u),
                          sem.at[slot]).start(priority=u % 2)
            return carry

        lax.fori_loop(0, groups, start, 0)

    def wait_all(slot):
        def wait(g, carry):
            for u in range(SUBLANES):
                _row_copy(ys_ref, (0, 0), ybuf_ref.at[slot], (0, 0), sem.at[slot]).wait()
            return carry

        lax.fori_loop(0, groups, wait, 0)

    def step(slot):
        @pl.when(i + 1 < pl.num_programs(0))
        def _():
            gather(i + 1, 1 - slot)

        wait_all(slot)
        y = ybuf_ref[slot].reshape(ts, ybuf_ref.shape[-1])
        out_ref[...] = x1_ref[...] + mod_ref[5:6, :] * y

    @pl.when(i == 0)
    def _():
        gather(0, 0)

    @pl.when(i % 2 == 0)
    def _():
        step(0)

    @pl.when(i % 2 == 1)
    def _():
        step(1)


def _combine(pos, ys, x1, mod3, *, seq, ts):
    n, d = x1.shape
    tiles_per_batch = seq // ts
    return pl.pallas_call(
        functools.partial(_combine_kernel, ts=ts),
        out_shape=jax.ShapeDtypeStruct((n, d), F32),
        grid_spec=pltpu.PrefetchScalarGridSpec(
            num_scalar_prefetch=1,
            grid=(n // ts,),
            in_specs=[pl.BlockSpec(memory_space=pl.ANY),
                      pl.BlockSpec((ts, d), lambda i, pos: (i, 0)),
                      pl.BlockSpec((None, 6, d), lambda i, pos: (i // tiles_per_batch, 0, 0))],
            out_specs=pl.BlockSpec((ts, d), lambda i, pos: (i, 0)),
            scratch_shapes=[pltpu.VMEM((2, ts // SUBLANES, SUBLANES, d), F32),
                            pltpu.SemaphoreType.DMA((2,))]),
        compiler_params=_cparams(1),
        name="combine",
    )(pos, ys.reshape(ys.shape[0] // SUBLANES, SUBLANES, d), x1, mod3)


def _pos_kernel(start_ref, route_ref, pos_ref):
    chunks = pos_ref.shape[1]
    bucket = route_ref[:, 0:chunks, :]
    pos = route_ref[:, chunks:2 * chunks, :]
    for b in range(N_ROUTE_BUCKETS):
        pos = pos + jnp.where(bucket == b, start_ref[b], 0)
    pos_ref[...] = pos


def _route_tables(route, counts, n_tiles, tm):
    steps, rows, _ = route.shape
    cnt = counts[:N_ROUTE_BUCKETS, 0]
    padded = (cnt + tm - 1) // tm * tm
    end = jnp.cumsum(padded)
    pos = pl.pallas_call(
        _pos_kernel,
        out_shape=jax.ShapeDtypeStruct((steps, rows // 2, LANES), jnp.int32),
        in_specs=[pl.BlockSpec(memory_space=pltpu.SMEM), pl.BlockSpec(memory_space=pltpu.VMEM)],
        out_specs=pl.BlockSpec(memory_space=pltpu.VMEM),
        name="pos",
    )(end - padded, route).reshape(-1)
    tiles_used = end[-1] // tm
    tile = jnp.arange(n_tiles, dtype=jnp.int32)
    valid = tile < tiles_used
    first_row = jnp.minimum(tile, tiles_used - 1) * tm
    tile_bucket = jnp.sum((end[None, :] <= first_row[:, None]).astype(jnp.int32), axis=1)
    tile_bucket = jnp.minimum(tile_bucket, N_ROUTE_BUCKETS - 1)
    group = tile_bucket // PAIRS_PER_GROUP
    pair = tile_bucket % PAIRS_PER_GROUP
    tile_ea = group * EXPERTS_PER_GROUP + jnp.asarray(PAIR_LO, jnp.int32)[pair]
    tile_eb = group * EXPERTS_PER_GROUP + jnp.asarray(PAIR_HI, jnp.int32)[pair]
    real_end = (end - padded + cnt)[tile_bucket]
    tile_rows = jnp.where(valid, jnp.clip(real_end - tile * tm, 0, tm), 0)
    return pos.astype(jnp.int32), tile_ea, tile_eb, tile_rows.astype(jnp.int32)


def _tile(seq, pref):
    t = min(pref, seq)
    assert seq % t == 0 and t % LANES == 0, (seq, t)
    return t


def kernel(x, c, rel_bias, ada_w, ada_b, norm1_g, w_in, q_norm_g, k_norm_g, lambda_q1, lambda_k1,
           lambda_q2, lambda_k2, subln_g, w_branch_attn, pool_w, pool_scale, w_branch_pool, w_out,
           norm2_g, router_group_w, router_group_b, router_expert_w, router_expert_b,
           expert_w_gate, expert_w_up, expert_w_down):
    bsz, seq, d = x.shape
    n = bsz * seq
    ts = _tile(seq, TOKEN_TILE)
    t_attn = _tile(seq, ATTN_TILE)
    tr = _tile(seq, ROW_COPY_TILE)
    tm = MOE_TILE
    n_tiles = -(-(n + N_ROUTE_BUCKETS * (tm - 1)) // tm)
    attn_steps = bsz * N_HEADS * (seq // t_attn)
    while (n_tiles * tm) % (attn_steps * SUBLANES):
        n_tiles += 1
    bias = _bias_tiles(rel_bias, t_attn)
    x2 = x.reshape(n, d)
    for l in range(ada_w.shape[0]):
        lambda_init = 0.8 - 0.6 * math.exp(-0.3 * l)
        mod3 = _ada(c, ada_w[l], ada_b[l]).reshape(bsz, 6, d)
        gq = jnp.tile(q_norm_g[l], D_ATTN // HEAD_DIM) * (HEAD_DIM ** -0.5 * LOG2E)
        gk = jnp.tile(k_norm_g[l], D_ATTN // HEAD_DIM)
        q, k, vt, sga, pc = _inproj(x2, mod3, norm1_g[l], w_in[l], gq, gk, pool_w[l], pool_scale[l],
                                    w_branch_pool[l], bsz=bsz, seq=seq, ts=_tile(seq, INPROJ_TILE))
        flat = lambda w: w.reshape(-1, w.shape[-1])
        o, cleared, wgu_b, wd_b = _attention(
            q.reshape(bsz, seq, D_ATTN), k.reshape(bsz, seq, D_ATTN), vt, bias,
            lambda_q1[l], lambda_k1[l], lambda_q2[l], lambda_k2[l], subln_g[l],
            t=t_attn, lambda_init=lambda_init, clear_shape=(n_tiles * tm, d + LANES),
            to_bf16=[[flat(expert_w_gate[l]), flat(expert_w_up[l])], [flat(expert_w_down[l])]])
        n_exp, _, de = expert_w_gate[l].shape
        wgu_b = wgu_b.reshape(n_exp, d, 2 * de)
        wd_b = wd_b.reshape(n_exp, de, d)
        x1, h2p, route, counts = _post(
            o.reshape(n, D_ATTN), sga, pc, x2, mod3, norm2_g[l], w_branch_attn[l], w_out[l],
            router_group_w[l], router_group_b[l], router_expert_w[l], router_expert_b[l],
            seq=seq, ts=ts)
        pos, tile_ea, tile_eb, tile_valid = _route_tables(route, counts, n_tiles, tm)
        hs = _dispatch(pos, h2p, cleared, ts=tr)
        ys = _experts(tile_ea, tile_eb, tile_valid, hs, wgu_b, wd_b, tm=tm)
        x2 = _combine(pos, ys, x1, mod3, seq=seq, ts=tr)
    return x2.reshape(bsz, seq, d)
```

```python
import functools
import math

import jax
import jax.numpy as jnp
from jax import lax
from jax.experimental import pallas as pl
from jax.experimental.pallas import tpu as pltpu

F32 = jnp.float32
BF16 = jnp.bfloat16

CHUNK = 64
N_HEADS = 4
HEAD_DIM = 64
D_HEAD_V = 2 * HEAD_DIM
V_ROWS = D_HEAD_V + 16
D_ATTN = N_HEADS * D_HEAD_V
POOL_WINDOWS = (2, 4, 8, 16)
POOL_GROUP_DIM = 128
D_POOL = len(POOL_WINDOWS) * POOL_GROUP_DIM
POOL_HALO = 16
N_BUCKETS = 32
MAX_DISTANCE = 128
N_EXPERT_GROUPS = 4
EXPERTS_PER_GROUP = 4
N_EXPERTS = N_EXPERT_GROUPS * EXPERTS_PER_GROUP
PAIRS_PER_GROUP = 6
PAIR_LO = (0, 0, 1, 1, 0, 2)
PAIR_HI = (1, 2, 2, 3, 3, 3)
N_ROUTE_BUCKETS = N_EXPERT_GROUPS * PAIRS_PER_GROUP
RMS_EPS = 1e-6
LOG2E = math.log2(math.e)
MASKED = -1e30

LANES = 128
SUBLANES = 8
ROUTER_ROWS = 32
EXPERT_ROW0 = 8

VMEM_LIMIT = 56 * 1024 * 1024
TOKEN_TILE = 1024
INPROJ_TILE = 1024
ATTN_TILE = 512
MOE_TILE = 512
ROW_COPY_TILE = 1024


def _cparams(n_axes):
    return pltpu.CompilerParams(dimension_semantics=("arbitrary",) * n_axes,
                                vmem_limit_bytes=VMEM_LIMIT)


def _const_spec(shape):
    nd = len(shape)
    return pl.BlockSpec(shape, lambda *_: (0,) * nd, pipeline_mode=pl.Buffered(1))


def _ada_kernel(ct_ref, w_ref, b_ref, o_ref):
    @pl.when(pl.program_id(0) == 0)
    def _():
        o_ref[...] = jnp.broadcast_to(b_ref[...], o_ref.shape)

    ct = ct_ref[...]
    s = ct * jax.nn.sigmoid(ct)
    w = w_ref[...]
    rows = [jnp.sum(w * s[:, b:b + 1], axis=0, keepdims=True) for b in range(ct.shape[1])]
    o_ref[...] += jnp.concatenate(rows, axis=0)


def _ada(c, w, b):
    bsz, d = c.shape
    n = w.shape[1]
    rows = 256
    return pl.pallas_call(
        _ada_kernel,
        out_shape=jax.ShapeDtypeStruct((bsz, n), F32),
        grid=(d // rows,),
        in_specs=[pl.BlockSpec((rows, bsz), lambda j: (j, 0)),
                  pl.BlockSpec((rows, n), lambda j: (j, 0)),
                  pl.BlockSpec((1, n), lambda j: (0, 0))],
        out_specs=pl.BlockSpec((bsz, n), lambda j: (0, 0)),
        compiler_params=_cparams(1),
        name="ada",
    )(c.T, w, b.reshape(1, n))


def _log_bucket_starts():
    nb = N_BUCKETS // 2
    max_exact = nb // 2
    m = nb - max_exact
    ratio = MAX_DISTANCE // max_exact
    starts = []
    for k in range(1, m):
        n = max_exact
        while n ** m < max_exact ** m * ratio ** k:
            n += 1
        starts.append(n)
    return tuple(starts)


LOG_BUCKET_STARTS = _log_bucket_starts()


def _bias_kernel(rb_ref, o_ref, *, t):
    h = pl.program_id(0)
    kind = pl.program_id(1)
    nb = N_BUCKETS // 2
    max_exact = nb // 2
    kpos = lax.broadcasted_iota(jnp.int32, (t, t), 0)
    qpos = lax.broadcasted_iota(jnp.int32, (t, t), 1)
    rel = kpos - qpos - jnp.where(kind == 0, t, 0)
    n = jnp.abs(rel)

    def table(first):
        val = jnp.full((t, t), rb_ref[first + nb - 1, h], F32)
        for k in range(len(LOG_BUCKET_STARTS) - 1, -1, -1):
            val = jnp.where(n < LOG_BUCKET_STARTS[k], rb_ref[first + max_exact + k, h], val)
        for j in range(max_exact - 1, -1, -1):
            val = jnp.where(n == j, rb_ref[first + j, h], val)
        return val

    far = rb_ref[nb - 1, h]

    @pl.when(kind == 0)
    def _():
        o_ref[...] = (table(0) - far) * LOG2E

    @pl.when(kind == 1)
    def _():
        bias = jnp.where(rel > 0, table(nb), table(0))
        shift = CHUNK.bit_length() - 1
        hidden = (kpos >> shift) > (qpos >> shift)
        o_ref[...] = jnp.where(hidden, MASKED, (bias - far) * LOG2E)


def _bias_tiles(rel_bias, t):
    return pl.pallas_call(
        functools.partial(_bias_kernel, t=t),
        out_shape=jax.ShapeDtypeStruct((N_HEADS, 2, t, t), F32),
        grid=(N_HEADS, 2),
        in_specs=[pl.BlockSpec(memory_space=pltpu.SMEM)],
        out_specs=pl.BlockSpec((None, None, t, t), lambda h, j: (h, j, 0, 0)),
        compiler_params=_cparams(2),
        name="bias_tiles",
    )(rel_bias)


def _group_rms(xc, ones_blockdiag, gain):
    ssq = jnp.dot((xc * xc).astype(BF16), ones_blockdiag, preferred_element_type=F32)
    return xc * lax.rsqrt(ssq * (1.0 / HEAD_DIM) + RMS_EPS) * gain


def _split_bf16(a):
    hi = a.astype(BF16)
    return hi, (a - hi.astype(F32)).astype(BF16)


def _pool_fold_kernel(pw_ref, ps_ref, wbp_ref, o_ref):
    a_hi, a_lo = _split_bf16(pw_ref[...] * ps_ref[...])
    b_hi, b_lo = _split_bf16(wbp_ref[...])
    dot = functools.partial(jnp.dot, preferred_element_type=F32)
    o_ref[...] = (dot(a_hi, b_hi) + dot(a_hi, b_lo) + dot(a_lo, b_hi)).astype(o_ref.dtype)


def _pool_fold(pool_w, pool_scale, w_bp):
    g, c, _ = pool_w.shape
    d = w_bp.shape[1]
    return pl.pallas_call(
        _pool_fold_kernel,
        out_shape=jax.ShapeDtypeStruct((g * c, d), BF16),
        grid=(g,),
        in_specs=[pl.BlockSpec((None, c, c), lambda i: (i, 0, 0)),
                  pl.BlockSpec((None, 1, c), lambda i: (i, 0, 0)),
                  pl.BlockSpec((c, d), lambda i: (i, 0))],
        out_specs=pl.BlockSpec((c, d), lambda i: (i, 0)),
        compiler_params=_cparams(1),
        name="pool_fold",
    )(pool_w, pool_scale.reshape(g, 1, c), w_bp)


def _inproj_kernel(x_ref, mod_ref, g1_ref, win_ref, gq_ref, gk_ref, ones_ref, wpool_ref,
                   q_ref, k_ref, vt_ref, sga_ref, pc_ref, ext_ref, v_ref,
                   *, ts, tiles_per_batch):
    tb = pl.program_id(0) % tiles_per_batch

    x = x_ref[...]
    y = x * lax.rsqrt(jnp.mean(x * x, axis=-1, keepdims=True) + RMS_EPS)
    h = y * (g1_ref[...] * (1.0 + mod_ref[1:2, :])) + mod_ref[0:1, :]
    hb = h.astype(BF16)

    def proj(c0, c1):
        return jnp.dot(hb, win_ref[:, c0:c1], preferred_element_type=F32)

    ones_bd = ones_ref[...]
    q_ref[...] = _group_rms(proj(0, D_ATTN), ones_bd, gq_ref[...]).astype(BF16)
    k_ref[...] = _group_rms(proj(D_ATTN, 2 * D_ATTN), ones_bd, gk_ref[...]).astype(BF16)
    v_ref[...] = proj(2 * D_ATTN, 3 * D_ATTN)
    vt = v_ref[...].T.astype(BF16)
    for hd in range(N_HEADS):
        vt_ref[hd, 0:D_HEAD_V, :] = vt[hd * D_HEAD_V:(hd + 1) * D_HEAD_V, :]
        vt_ref[hd, D_HEAD_V:V_ROWS, :] = jnp.ones((V_ROWS - D_HEAD_V, ts), BF16)
    c_u = 3 * D_ATTN
    c_ga = c_u + D_POOL
    c_gp = c_ga + x.shape[1]
    sga_ref[...] = jax.nn.sigmoid(proj(c_ga, c_gp)).astype(BF16)

    u = proj(c_u, c_ga)

    @pl.when(tb == 0)
    def _():
        ext_ref[0:POOL_HALO, :] = jnp.zeros((POOL_HALO, D_POOL), F32)

    ext_ref[POOL_HALO:POOL_HALO + ts, :] = u
    row = lax.broadcasted_iota(jnp.int32, (ts, 1), 0) + tb * ts
    ys = []
    for g, w in enumerate(POOL_WINDOWS):
        c0 = g * POOL_GROUP_DIM
        ug = u[:, c0:c0 + POOL_GROUP_DIM]
        acc = ug
        for d in range(1, w):
            acc = acc + ext_ref[POOL_HALO - d:POOL_HALO - d + ts, c0:c0 + POOL_GROUP_DIM]
        cnt = jnp.minimum(row + 1, w).astype(F32)
        ys.append((acc / cnt - ug).astype(BF16))
    ypool = jnp.dot(jnp.concatenate(ys, axis=1), wpool_ref[...], preferred_element_type=F32)
    pc_ref[...] = (jax.nn.sigmoid(proj(c_gp, c_gp + x.shape[1])) * ypool).astype(BF16)
    ext_ref[0:POOL_HALO, :] = u[ts - POOL_HALO:ts, :]


def _inproj(x2, mod3, g1, w_in, gq, gk, pool_w, pool_scale, w_bp, *, bsz, seq, ts):
    n, d = x2.shape
    d_in = w_in.shape[1]
    tiles_per_batch = seq // ts
    win_b = w_in.astype(BF16)
    idx = jnp.arange(D_ATTN) // HEAD_DIM
    ones_bd = (idx[:, None] == idx[None, :]).astype(BF16)
    kern = functools.partial(_inproj_kernel, ts=ts, tiles_per_batch=tiles_per_batch)
    tok = lambda i: (i, 0)
    out_shape = (jax.ShapeDtypeStruct((n, D_ATTN), BF16),
                 jax.ShapeDtypeStruct((n, D_ATTN), BF16),
                 jax.ShapeDtypeStruct((bsz, N_HEADS, V_ROWS, seq), BF16),
                 jax.ShapeDtypeStruct((n, d), BF16),
                 jax.ShapeDtypeStruct((n, d), BF16))
    return pl.pallas_call(
        kern,
        out_shape=out_shape,
        grid=(n // ts,),
        in_specs=[pl.BlockSpec((ts, d), tok),
                  pl.BlockSpec((None, 6, d), lambda i: (i // tiles_per_batch, 0, 0)),
                  _const_spec((1, d)),
                  _const_spec((d, d_in)),
                  _const_spec((1, D_ATTN)),
                  _const_spec((1, D_ATTN)),
                  _const_spec((D_ATTN, D_ATTN)),
                  _const_spec((D_POOL, d))],
        out_specs=(pl.BlockSpec((ts, D_ATTN), tok),
                   pl.BlockSpec((ts, D_ATTN), tok),
                   pl.BlockSpec((None, N_HEADS, V_ROWS, ts),
                                lambda i: (i // tiles_per_batch, 0, 0, i % tiles_per_batch)),
                   pl.BlockSpec((ts, d), tok),
                   pl.BlockSpec((ts, d), tok)),
        scratch_shapes=[pltpu.VMEM((POOL_HALO + ts, D_POOL), F32), pltpu.VMEM((ts, D_ATTN), F32)],
        compiler_params=_cparams(1),
        name="inproj",
    )(x2, mod3, g1.reshape(1, d), win_b, gq.reshape(1, D_ATTN), gk.reshape(1, D_ATTN), ones_bd,
      _pool_fold(pool_w, pool_scale, w_bp))


def _attn_kernel(q_ref, qn_ref, k_ref, vt_ref, bias_ref, lq1_ref, lk1_ref, lq2_ref, lk2_ref, subg_ref,
                 *rest, t, lambda_init, cast_groups):
    n_in, n_out = sum(cast_groups), len(cast_groups)
    cast_in = rest[:n_in]
    o_ref, clear_ref = rest[n_in:n_in + 2]
    cast_out = rest[n_in + 2:n_in + 2 + n_out]
    s_ref, mt_ref, m_ref, acc_ref = rest[n_in + 2 + n_out:]
    i = pl.program_id(2)
    last = pl.num_programs(2) - 1
    clear_ref[...] = jnp.zeros(clear_ref.shape, clear_ref.dtype)
    first = 0
    for size, dst in zip(cast_groups, cast_out):
        col = 0
        for src in cast_in[first:first + size]:
            dst[:, col:col + src.shape[1]] = src[...].astype(dst.dtype)
            col += src.shape[1]
        first += size

    def split_maps(q):
        lane = lax.broadcasted_iota(jnp.int32, q.shape, 1)
        zero = jnp.zeros_like(q)
        return jnp.where(lane < HEAD_DIM, q, zero), jnp.where(lane >= HEAD_DIM, q, zero)

    q_now = split_maps(q_ref[...])

    m_ref[...] = jnp.full(m_ref.shape, MASKED, F32)
    acc_ref[...] = jnp.zeros(acc_ref.shape, F32)

    def scores(j, bias, slot, qm=q_now):
        kt = k_ref[pl.ds(pl.multiple_of(j * t, t), t), :]
        for mp in range(2):
            s = lax.dot_general(kt, qm[mp], (((1,), (1,)), ((), ())), preferred_element_type=F32)
            if bias is not None:
                s = s + bias
            s_ref[slot, mp] = s
            mt_ref[slot, mp] = jnp.max(s, axis=0, keepdims=True)

    def accumulate(j, slot):
        vt = vt_ref[:, pl.ds(pl.multiple_of(j * t, t), t)]
        for mp in range(2):
            m_old = m_ref[mp]
            m_new = jnp.maximum(m_old, mt_ref[slot, mp])
            p = jnp.exp2(s_ref[slot, mp] - m_new).astype(BF16)
            acc_ref[mp] = (jnp.exp2(m_old - m_new) * acc_ref[mp]
                           + jnp.dot(vt, p, preferred_element_type=F32))
            m_ref[mp] = m_new

    def next_diagonal():
        nxt = jnp.minimum(i + 1, last)
        scores(nxt, bias_ref[1], 2, split_maps(qn_ref[...]))

    @pl.when(i == 0)
    def _():
        scores(0, bias_ref[1], 0)

    @pl.when(i >= 1)
    def _():
        scores(i - 1, bias_ref[0], 1)
        accumulate(i, 2)

    n_far = jnp.maximum(i - 1, 0)

    def pair(kk):
        j = i - 1 - 2 * kk
        scores(j - 1, None, 0)
        accumulate(j, 1)
        scores(j - 2, None, 1)
        accumulate(j - 1, 0)

    def oct_body(oo, carry):
        for u in range(4):
            pair(4 * oo + u)
        return carry

    n_pairs = n_far // 2
    lax.fori_loop(0, n_pairs // 4, oct_body, 0)

    @pl.when(n_pairs % 4 >= 2)
    def _():
        pair(n_pairs // 4 * 4)
        pair(n_pairs // 4 * 4 + 1)

    @pl.when(n_pairs % 2 == 1)
    def _():
        pair(n_pairs - 1)

    def finish(slot):
        next_diagonal()
        accumulate(0, slot)
        lam = (jnp.exp(jnp.sum(lq1_ref[...] * lk1_ref[...], axis=1, keepdims=True))
               - jnp.exp(jnp.sum(lq2_ref[...] * lk2_ref[...], axis=1, keepdims=True)) + lambda_init)
        o1 = acc_ref[0, 0:D_HEAD_V, :] / acc_ref[0, D_HEAD_V:D_HEAD_V + 1, :]
        o2 = acc_ref[1, 0:D_HEAD_V, :] / acc_ref[1, D_HEAD_V:D_HEAD_V + 1, :]
        ot = o1 - lam * o2
        ot = ot * lax.rsqrt(jnp.mean(ot * ot, axis=0, keepdims=True) + RMS_EPS)
        ot = ot * subg_ref[...] * (1.0 - lambda_init)
        o_ref[...] = ot.T.astype(BF16)

    @pl.when(n_far % 2 == 1)
    def _():
        scores(0, None, 0)
        accumulate(1, 1)
        finish(0)

    @pl.when(jnp.logical_and(i >= 1, n_far % 2 == 0))
    def _():
        finish(1)

    @pl.when(i == 0)
    def _():
        finish(0)


def _attention(q, k, vt, bias, lq1, lk1, lq2, lk2, subln_g, *, t, lambda_init, clear_shape, to_bf16):
    bsz, seq, _ = q.shape
    nq = seq // t
    assert t + 1 >= LOG_BUCKET_STARTS[-1], "keys two tiles back must all fall in the last distance bucket"
    steps = bsz * N_HEADS * nq
    step_id = lambda b, h, i: ((b * N_HEADS + h) * nq + i, 0)
    clear_rows = clear_shape[0] // steps
    assert clear_rows * steps == clear_shape[0] and clear_rows % SUBLANES == 0, (clear_shape, steps)
    cast_in_specs, cast_out_specs, cast_out_shapes = [], [], []
    for group in to_bf16:
        rows = group[0].shape[0] // steps
        assert rows * steps == group[0].shape[0] and rows % (2 * SUBLANES) == 0, (group[0].shape, steps)
        cols = sum(a.shape[1] for a in group)
        cast_in_specs += [pl.BlockSpec((rows, a.shape[1]), step_id) for a in group]
        cast_out_specs.append(pl.BlockSpec((rows, cols), step_id))
        cast_out_shapes.append(jax.ShapeDtypeStruct((group[0].shape[0], cols), BF16))
    kern = functools.partial(_attn_kernel, t=t, lambda_init=lambda_init,
                             cast_groups=tuple(len(g) for g in to_bf16))
    vec = lambda a: a.reshape(1, HEAD_DIM)
    return pl.pallas_call(
        kern,
        out_shape=(jax.ShapeDtypeStruct((bsz, seq, D_ATTN), BF16),
                   jax.ShapeDtypeStruct(clear_shape, F32),
                   *cast_out_shapes),
        grid=(bsz, N_HEADS, seq // t),
        in_specs=[pl.BlockSpec((None, t, D_HEAD_V), lambda b, h, i: (b, i, h)),
                  pl.BlockSpec((None, t, D_HEAD_V), lambda b, h, i: (b, jnp.minimum(i + 1, nq - 1), h)),
                  pl.BlockSpec((None, seq, D_HEAD_V), lambda b, h, i: (b, 0, h)),
                  pl.BlockSpec((None, None, V_ROWS, seq), lambda b, h, i: (b, h, 0, 0)),
                  pl.BlockSpec((None, 2, t, t), lambda b, h, i: (h, 0, 0, 0)),
                  _const_spec((1, HEAD_DIM)), _const_spec((1, HEAD_DIM)),
                  _const_spec((1, HEAD_DIM)), _const_spec((1, HEAD_DIM)),
                  _const_spec((D_HEAD_V, 1)), *cast_in_specs],
        out_specs=(pl.BlockSpec((None, t, D_HEAD_V), lambda b, h, i: (b, i, h)),
                   pl.BlockSpec((clear_rows, clear_shape[1]), step_id), *cast_out_specs),
        scratch_shapes=[pltpu.VMEM((3, 2, t, t), F32),
                        pltpu.VMEM((3, 2, 1, t), F32),
                        pltpu.VMEM((2, 1, t), F32),
                        pltpu.VMEM((2, V_ROWS, t), F32)],
        compiler_params=_cparams(3),
        name="attn",
    )(q, q, k, vt, bias, vec(lq1), vec(lk1), vec(lq2), vec(lk2), subln_g.reshape(D_HEAD_V, 1),
      *[a for group in to_bf16 for a in group])


def _first_max(rows):
    best = rows[0]
    for r in rows[1:]:
        best = jnp.maximum(best, r)
    idx = jnp.full(best.shape, len(rows) - 1, jnp.int32)
    for j in range(len(rows) - 2, -1, -1):
        idx = jnp.where(rows[j] == best, j, idx)
    return best, idx


def _post_kernel(o_ref, sga_ref, pc_ref, x_ref, mod_ref, g2_ref, wba_ref, wout_ref, wr_ref, br_ref,
                 tri_ref, x1_ref, h2p_ref, route_ref, cnt_ref, carry_ref):
    @pl.when(pl.program_id(0) == 0)
    def _():
        carry_ref[...] = jnp.zeros(carry_ref.shape, F32)

    ya = jnp.dot(o_ref[...], wba_ref[...], preferred_element_type=F32)
    merged = sga_ref[...].astype(F32) * ya + pc_ref[...].astype(F32)
    z = jnp.dot(merged.astype(BF16), wout_ref[...], preferred_element_type=F32)
    x1 = x_ref[...] + mod_ref[2:3, :] * z
    x1_ref[...] = x1
    y = x1 * lax.rsqrt(jnp.mean(x1 * x1, axis=-1, keepdims=True) + RMS_EPS)
    h2 = y * (g2_ref[...] * (1.0 + mod_ref[4:5, :])) + mod_ref[3:4, :]
    hi = h2.astype(BF16)
    lo = (h2 - hi.astype(F32)).astype(BF16)

    nt = (((1,), (1,)), ((), ()))
    a = lax.dot_general(wr_ref[...], hi, nt, preferred_element_type=F32)
    b = lax.dot_general(wr_ref[0:ROUTER_ROWS, :], lo, nt, preferred_element_type=F32)
    logits = a[0:ROUTER_ROWS] + a[ROUTER_ROWS:2 * ROUTER_ROWS] + b + br_ref[...]

    gl = [logits[g:g + 1, :] for g in range(N_EXPERT_GROUPS)]
    gmax, gidx = _first_max(gl)
    gsum = gl[0] * 0.0
    for r in gl:
        gsum = gsum + jnp.exp(r - gmax)
    g_val = 1.0 / gsum
    es = []
    for r in range(EXPERTS_PER_GROUP):
        sel = jnp.zeros_like(gmax)
        for g in range(N_EXPERT_GROUPS):
            row = EXPERT_ROW0 + g * EXPERTS_PER_GROUP + r
            sel = jnp.where(gidx == g, logits[row:row + 1, :], sel)
        es.append(sel)
    e1, i1 = _first_max(es)
    rest = [jnp.where(i1 == r, -jnp.inf, es[r]) for r in range(EXPERTS_PER_GROUP)]
    e2, i2 = _first_max(rest)
    r21 = jnp.exp(e2 - e1)
    w1 = g_val / (1.0 + r21)
    w2 = g_val * r21 / (1.0 + r21)

    first = i1 < i2
    e_lo = jnp.where(first, i1, i2)
    e_hi = jnp.where(first, i2, i1)
    pair = jnp.zeros_like(e_lo)
    for p in range(1, PAIRS_PER_GROUP):
        pair = jnp.where(jnp.logical_and(e_lo == PAIR_LO[p], e_hi == PAIR_HI[p]), p, pair)
    bucket = gidx * PAIRS_PER_GROUP + pair
    w_lo = jnp.where(first, w1, w2)
    w_hi = jnp.where(first, w2, w1)

    ts = bucket.shape[1]
    brow = lax.broadcasted_iota(jnp.int32, (ROUTER_ROWS, ts), 0)
    onehot = brow == bucket
    prefix = jnp.dot(jnp.where(onehot, 1.0, 0.0).astype(BF16), tri_ref[...],
                     preferred_element_type=F32)
    carry = carry_ref[...]
    rank = jnp.sum(jnp.where(onehot, prefix + carry, 0.0), axis=0, keepdims=True) - 1.0
    carry = carry + prefix[:, ts - 1:ts]
    carry_ref[...] = carry
    cnt_ref[...] = jnp.broadcast_to(carry, cnt_ref.shape).astype(jnp.int32)
    chunks = ts // LANES
    rank_i = rank.astype(jnp.int32)
    for j in range(chunks):
        route_ref[j:j + 1, :] = bucket[:, j * LANES:(j + 1) * LANES]
        route_ref[chunks + j:chunks + j + 1, :] = rank_i[:, j * LANES:(j + 1) * LANES]

    arow = lax.broadcasted_iota(jnp.int32, (LANES, ts), 0)
    aux_t = jnp.where(arow == 0, w_lo, jnp.where(arow == 1, w_hi, 0.0))
    d = h2.shape[1]
    h2p_ref[:, 0:d] = h2
    h2p_ref[:, d:d + LANES] = aux_t.T


def _post(o2, sga, pc, x2, mod3, g2, w_ba, w_out, wg_r, bg_r, we_r, be_r, *, seq, ts):
    n, d = x2.shape
    tiles_per_batch = seq // ts
    steps = n // ts
    wr = jnp.zeros((ROUTER_ROWS, d), F32)
    wr = wr.at[0:N_EXPERT_GROUPS].set(wg_r.T).at[EXPERT_ROW0:EXPERT_ROW0 + N_EXPERTS].set(we_r.T)
    wr_hi = wr.astype(BF16)
    wr_lo = (wr - wr_hi.astype(F32)).astype(BF16)
    br = jnp.zeros((ROUTER_ROWS, 1), F32)
    br = br.at[0:N_EXPERT_GROUPS, 0].set(bg_r).at[EXPERT_ROW0:EXPERT_ROW0 + N_EXPERTS, 0].set(be_r)
    tok = lambda i: (i, 0)
    tidx = jnp.arange(ts)
    tri = (tidx[:, None] <= tidx[None, :]).astype(BF16)
    return pl.pallas_call(
        _post_kernel,
        out_shape=(jax.ShapeDtypeStruct((n, d), F32),
                   jax.ShapeDtypeStruct((n, d + LANES), F32),
                   jax.ShapeDtypeStruct((steps, 2 * (ts // LANES), LANES), jnp.int32),
                   jax.ShapeDtypeStruct((ROUTER_ROWS, LANES), jnp.int32)),
        grid=(steps,),
        in_specs=[pl.BlockSpec((ts, D_ATTN), tok),
                  pl.BlockSpec((ts, d), tok),
                  pl.BlockSpec((ts, d), tok),
                  pl.BlockSpec((ts, d), tok),
                  pl.BlockSpec((None, 6, d), lambda i: (i // tiles_per_batch, 0, 0)),
                  _const_spec((1, d)),
                  _const_spec((D_ATTN, d)),
                  _const_spec((d, d)),
                  _const_spec((2 * ROUTER_ROWS, d)),
                  _const_spec((ROUTER_ROWS, 1)),
                  _const_spec((ts, ts))],
        out_specs=(pl.BlockSpec((ts, d), tok), pl.BlockSpec((ts, d + LANES), tok),
                   pl.BlockSpec((None, 2 * (ts // LANES), LANES), lambda i: (i, 0, 0)),
                   pl.BlockSpec((ROUTER_ROWS, LANES), lambda i: (0, 0))),
        scratch_shapes=[pltpu.VMEM((ROUTER_ROWS, 1), F32)],
        compiler_params=_cparams(1),
        name="post",
    )(o2, sga, pc, x2, mod3, g2.reshape(1, d), w_ba.astype(BF16), w_out.astype(BF16),
      jnp.concatenate([wr_hi, wr_lo], axis=0), br, tri)


def _group_sublane(row):
    return lax.shift_right_logical(row, SUBLANES.bit_length() - 1), row & (SUBLANES - 1)


def _row_copy(src, src_row, dst, dst_row, sem):
    sg, ss = src_row
    dg, ds = dst_row
    return pltpu.make_async_copy(src.at[sg, pl.ds(ss, 1), :], dst.at[dg, pl.ds(ds, 1), :], sem)


def _dispatch_kernel(pos_ref, h_ref, init_ref, hs_ref, sem, *, ts):
    del init_ref
    base = pl.program_id(0) * ts

    def start(g, carry):
        for u in range(SUBLANES):
            p = pos_ref[base + g * SUBLANES + u]
            _row_copy(h_ref, (g, u), hs_ref, _group_sublane(p), sem).start(priority=u % 2)
        return carry

    def wait(g, carry):
        for u in range(SUBLANES):
            _row_copy(h_ref, (0, 0), hs_ref, (0, 0), sem).wait()
        return carry

    lax.fori_loop(0, ts // SUBLANES, start, 0)
    lax.fori_loop(0, ts // SUBLANES, wait, 0)


def _dispatch(pos, h2p, cleared, *, ts):
    n, w = h2p.shape
    n_rows = cleared.shape[0]
    return pl.pallas_call(
        functools.partial(_dispatch_kernel, ts=ts),
        out_shape=jax.ShapeDtypeStruct((n_rows // SUBLANES, SUBLANES, w), h2p.dtype),
        grid_spec=pltpu.PrefetchScalarGridSpec(
            num_scalar_prefetch=1,
            grid=(n // ts,),
            in_specs=[pl.BlockSpec((ts // SUBLANES, SUBLANES, w), lambda i, pos: (i, 0, 0)),
                      pl.BlockSpec(memory_space=pl.ANY)],
            out_specs=pl.BlockSpec(memory_space=pl.ANY),
            scratch_shapes=[pltpu.SemaphoreType.DMA]),
        input_output_aliases={2: 0},
        compiler_params=_cparams(1),
        name="dispatch",
    )(pos, h2p.reshape(n // SUBLANES, SUBLANES, w),
      cleared.reshape(n_rows // SUBLANES, SUBLANES, w)).reshape(n_rows, w)


def _experts_kernel(ea_ref, eb_ref, valid_ref, hs_ref, wgua_ref, wda_ref, wgub_ref, wdb_ref, ys_ref):
    del ea_ref, eb_ref
    rows = valid_ref[pl.program_id(0)]
    tm = hs_ref.shape[0]
    d = hs_ref.shape[1] - LANES

    def run(m):
        h = hs_ref[0:m, 0:d].astype(BF16)
        aux = hs_ref[0:m, d:d + LANES]

        def expert(wgu_ref, wd_ref):
            gu = jnp.dot(h, wgu_ref[...], preferred_element_type=F32)
            de = gu.shape[1] // 2
            a, b = gu[:, :de], gu[:, de:]
            hid = (a * jax.nn.sigmoid(a)) * b
            return jnp.dot(hid.astype(BF16), wd_ref[...], preferred_element_type=F32)

        ys_ref[0:m, :] = (aux[:, 0:1] * expert(wgua_ref, wda_ref)
                          + aux[:, 1:2] * expert(wgub_ref, wdb_ref))
        if m < tm:
            ys_ref[m:tm, :] = jnp.zeros((tm - m, ys_ref.shape[1]), ys_ref.dtype)

    @pl.when(rows > tm // 2)
    def _():
        run(tm)

    @pl.when(jnp.logical_and(rows > 0, rows <= tm // 2))
    def _():
        run(tm // 2)

    @pl.when(rows == 0)
    def _():
        ys_ref[...] = jnp.zeros(ys_ref.shape, ys_ref.dtype)


def _experts(tile_ea, tile_eb, tile_valid, hs, w_gate_up, w_down, *, tm):
    n_rows, w = hs.shape
    _, de, d = w_down.shape
    ea = lambda t, ea_r, eb_r, v_r: (ea_r[t], 0, 0)
    eb = lambda t, ea_r, eb_r, v_r: (eb_r[t], 0, 0)
    row = lambda t, ea_r, eb_r, v_r: (t, 0)
    used_row = lambda t, ea_r, eb_r, v_r: (jnp.where(v_r[t] > 0, t, 0), 0)
    return pl.pallas_call(
        _experts_kernel,
        out_shape=jax.ShapeDtypeStruct((n_rows, d), F32),
        grid_spec=pltpu.PrefetchScalarGridSpec(
            num_scalar_prefetch=3,
            grid=(n_rows // tm,),
            in_specs=[pl.BlockSpec((tm, w), used_row),
                      pl.BlockSpec((None, d, 2 * de), ea), pl.BlockSpec((None, de, d), ea),
                      pl.BlockSpec((None, d, 2 * de), eb), pl.BlockSpec((None, de, d), eb)],
            out_specs=pl.BlockSpec((tm, d), row)),
        compiler_params=_cparams(1),
        name="experts",
    )(tile_ea, tile_eb, tile_valid, hs, w_gate_up, w_down, w_gate_up, w_down)


def _combine_kernel(pos_ref, ys_ref, x1_ref, mod_ref, out_ref, ybuf_ref, sem, *, ts):
    i = pl.program_id(0)
    groups = ts // SUBLANES

    def gather(tile, slot):
        base = tile * ts

        def start(g, carry):
            for u in range(SUBLANES):
                p = pos_ref[base + g * SUBLANES + u]
                _row_copy(ys_ref, _group_sublane(p), ybuf_ref.at[slot], (g, u),
                          sem.at[slot]).start(priority=u % 2)
            return carry

        lax.fori_loop(0, groups, start, 0)

    def wait_all(slot):
        def wait(g, carry):
            for u in range(SUBLANES):
                _row_copy(ys_ref, (0, 0), ybuf_ref.at[slot], (0, 0), sem.at[slot]).wait()
            return carry

        lax.fori_loop(0, groups, wait, 0)

    def step(slot):
        @pl.when(i + 1 < pl.num_programs(0))
        def _():
            gather(i + 1, 1 - slot)

        wait_all(slot)
        y = ybuf_ref[slot].reshape(ts, ybuf_ref.shape[-1])
        out_ref[...] = x1_ref[...] + mod_ref[5:6, :] * y

    @pl.when(i == 0)
    def _():
        gather(0, 0)

    @pl.when(i % 2 == 0)
    def _():
        step(0)

    @pl.when(i % 2 == 1)
    def _():
        step(1)


def _combine(pos, ys, x1, mod3, *, seq, ts):
    n, d = x1.shape
    tiles_per_batch = seq // ts
    return pl.pallas_call(
        functools.partial(_combine_kernel, ts=ts),
        out_shape=jax.ShapeDtypeStruct((n, d), F32),
        grid_spec=pltpu.PrefetchScalarGridSpec(
            num_scalar_prefetch=1,
            grid=(n // ts,),
            in_specs=[pl.BlockSpec(memory_space=pl.ANY),
                      pl.BlockSpec((ts, d), lambda i, pos: (i, 0)),
                      pl.BlockSpec((None, 6, d), lambda i, pos: (i // tiles_per_batch, 0, 0))],
            out_specs=pl.BlockSpec((ts, d), lambda i, pos: (i, 0)),
            scratch_shapes=[pltpu.VMEM((2, ts // SUBLANES, SUBLANES, d), F32),
                            pltpu.SemaphoreType.DMA((2,))]),
        compiler_params=_cparams(1),
        name="combine",
    )(pos, ys.reshape(ys.shape[0] // SUBLANES, SUBLANES, d), x1, mod3)


def _pos_kernel(start_ref, route_ref, pos_ref):
    chunks = pos_ref.shape[1]
    bucket = route_ref[:, 0:chunks, :]
    pos = route_ref[:, chunks:2 * chunks, :]
    for b in range(N_ROUTE_BUCKETS):
        pos = pos + jnp.where(bucket == b, start_ref[b], 0)
    pos_ref[...] = pos


def _route_tables(route, counts, n_tiles, tm):
    steps, rows, _ = route.shape
    cnt = counts[:N_ROUTE_BUCKETS, 0]
    padded = (cnt + tm - 1) // tm * tm
    end = jnp.cumsum(padded)
    pos = pl.pallas_call(
        _pos_kernel,
        out_shape=jax.ShapeDtypeStruct((steps, rows // 2, LANES), jnp.int32),
        in_specs=[pl.BlockSpec(memory_space=pltpu.SMEM), pl.BlockSpec(memory_space=pltpu.VMEM)],
        out_specs=pl.BlockSpec(memory_space=pltpu.VMEM),
        name="pos",
    )(end - padded, route).reshape(-1)
    tiles_used = end[-1] // tm
    tile = jnp.arange(n_tiles, dtype=jnp.int32)
    valid = tile < tiles_used
    first_row = jnp.minimum(tile, tiles_used - 1) * tm
    tile_bucket = jnp.sum((end[None, :] <= first_row[:, None]).astype(jnp.int32), axis=1)
    tile_bucket = jnp.minimum(tile_bucket, N_ROUTE_BUCKETS - 1)
    group = tile_bucket // PAIRS_PER_GROUP
    pair = tile_bucket % PAIRS_PER_GROUP
    tile_ea = group * EXPERTS_PER_GROUP + jnp.asarray(PAIR_LO, jnp.int32)[pair]
    tile_eb = group * EXPERTS_PER_GROUP + jnp.asarray(PAIR_HI, jnp.int32)[pair]
    real_end = (end - padded + cnt)[tile_bucket]
    tile_rows = jnp.where(valid, jnp.clip(real_end - tile * tm, 0, tm), 0)
    return pos.astype(jnp.int32), tile_ea, tile_eb, tile_rows.astype(jnp.int32)


def _tile(seq, pref):
    t = min(pref, seq)
    assert seq % t == 0 and t % LANES == 0, (seq, t)
    return t


def kernel(x, c, rel_bias, ada_w, ada_b, norm1_g, w_in, q_norm_g, k_norm_g, lambda_q1, lambda_k1,
           lambda_q2, lambda_k2, subln_g, w_branch_attn, pool_w, pool_scale, w_branch_pool, w_out,
           norm2_g, router_group_w, router_group_b, router_expert_w, router_expert_b,
           expert_w_gate, expert_w_up, expert_w_down):
    bsz, seq, d = x.shape
    n = bsz * seq
    ts = _tile(seq, TOKEN_TILE)
    t_attn = _tile(seq, ATTN_TILE)
    tr = _tile(seq, ROW_COPY_TILE)
    tm = MOE_TILE
    n_tiles = -(-(n + N_ROUTE_BUCKETS * (tm - 1)) // tm)
    attn_steps = bsz * N_HEADS * (seq // t_attn)
    while (n_tiles * tm) % (attn_steps * SUBLANES):
        n_tiles += 1
    bias = _bias_tiles(rel_bias, t_attn)
    x2 = x.reshape(n, d)
    for l in range(ada_w.shape[0]):
        lambda_init = 0.8 - 0.6 * math.exp(-0.3 * l)
        mod3 = _ada(c, ada_w[l], ada_b[l]).reshape(bsz, 6, d)
        gq = jnp.tile(q_norm_g[l], D_ATTN // HEAD_DIM) * (HEAD_DIM ** -0.5 * LOG2E)
        gk = jnp.tile(k_norm_g[l], D_ATTN // HEAD_DIM)
        q, k, vt, sga, pc = _inproj(x2, mod3, norm1_g[l], w_in[l], gq, gk, pool_w[l], pool_scale[l],
                                    w_branch_pool[l], bsz=bsz, seq=seq, ts=_tile(seq, INPROJ_TILE))
        flat = lambda w: w.reshape(-1, w.shape[-1])
        o, cleared, wgu_b, wd_b = _attention(
            q.reshape(bsz, seq, D_ATTN), k.reshape(bsz, seq, D_ATTN), vt, bias,
            lambda_q1[l], lambda_k1[l], lambda_q2[l], lambda_k2[l], subln_g[l],
            t=t_attn, lambda_init=lambda_init, clear_shape=(n_tiles * tm, d + LANES),
            to_bf16=[[flat(expert_w_gate[l]), flat(expert_w_up[l])], [flat(expert_w_down[l])]])
        n_exp, _, de = expert_w_gate[l].shape
        wgu_b = wgu_b.reshape(n_exp, d, 2 * de)
        wd_b = wd_b.reshape(n_exp, de, d)
        x1, h2p, route, counts = _post(
            o.reshape(n, D_ATTN), sga, pc, x2, mod3, norm2_g[l], w_branch_attn[l], w_out[l],
            router_group_w[l], router_group_b[l], router_expert_w[l], router_expert_b[l],
            seq=seq, ts=ts)
        pos, tile_ea, tile_eb, tile_valid = _route_tables(route, counts, n_tiles, tm)
        hs = _dispatch(pos, h2p, cleared, ts=tr)
        ys = _experts(tile_ea, tile_eb, tile_valid, hs, wgu_b, wd_b, tm=tm)
        x2 = _combine(pos, ys, x1, mod3, seq=seq, ts=tr)
    return x2.reshape(bsz, seq, d)
```
